```python
import math
import jax
import jax.numpy as jnp
from jax import lax
import numpy as np

D_MODEL = 1024
BATCH = 16
SEQ = 4096
DEPTH = 1

CTX_LEN = 256
GRID_W = 64
H_M = 8
D_HM = 64
W_M = H_M * D_HM
H_R = 8
D_HR = 64
W_R = H_R * D_HR
MIX_W = W_M + W_R
W_LORA = 64
A_LORA = 64
G_LORA = 128
CONV_K = 3
CONV_CH = 2 * W_M + 3 * W_R
SECTION_WIDTHS = (W_M, W_M, W_R, W_R, W_R, W_M, W_M, 4 * H_M, W_LORA, W_LORA, A_LORA, G_LORA)
IN_COLS = sum(SECTION_WIDTHS)
SPLIT_POINTS = tuple(int(v) for v in np.cumsum(SECTION_WIDTHS)[:-1])
CHUNK = 64
N_GROUPS = 4
E_PER_GROUP = 8
N_EXPERTS = N_GROUPS * E_PER_GROUP
TOP_K_IN_GROUP = 2
D_EXPERT = 512
MOE_BLOCK = 256
ALPHA = (2.0 * DEPTH) ** 0.25
BETA = (8.0 * DEPTH) ** -0.25
DECAY_SCALE = math.exp(-0.5)
LN_EPS = 1e-6
GN_EPS = 64e-5

kernel_name = 'hybrid_mlstm_rwkv7_hmoe_dit'


def _layernorm(z, gain=None, bias=None):
    zf = z.astype(jnp.float32)
    mu = jnp.mean(zf, -1, keepdims=True)
    var = jnp.mean(jnp.square(zf - mu), -1, keepdims=True)
    y = (zf - mu) * lax.rsqrt(var + LN_EPS)
    if gain is not None:
        y = y * gain + bias
    return y.astype(z.dtype)


def _headnorm(z, eps):
    zf = z.astype(jnp.float32)
    mu = jnp.mean(zf, -1, keepdims=True)
    var = jnp.mean(jnp.square(zf - mu), -1, keepdims=True)
    return (zf - mu) * lax.rsqrt(var + eps)


def _modulate(z, shift, scale):
    return _layernorm(z) * (1 + scale) + shift


def _rev_segments(u):
    return jnp.concatenate([jnp.flip(u[:, :CTX_LEN], 1), jnp.flip(u[:, CTX_LEN:], 1)], axis=1)


def _dir_stack(u_fwd, u_bwd):
    return jnp.concatenate([u_fwd, _rev_segments(u_bwd)], axis=0)


def _dir_merge(y):
    b = y.shape[0] // 2
    return y[:b] + _rev_segments(y[b:])


def _conv_grid(u, w, rows):
    b, l, ch = u.shape
    img = u.reshape(b, rows, l // rows, ch)
    out = lax.conv_general_dilated(img, w[:, :, None, :].astype(u.dtype), (1, 1), 'SAME',
                                   dimension_numbers=('NHWC', 'HWIO', 'NHWC'), feature_group_count=ch)
    return out.reshape(b, l, ch)


def _conv_seq(u, w_row):
    ch = u.shape[-1]
    return lax.conv_general_dilated(u, w_row[:, None, :].astype(u.dtype), (1,), 'SAME',
                                    dimension_numbers=('NWC', 'WIO', 'NWC'), feature_group_count=ch)


def _mlstm_chunkwise(q, k, v, log_i, log_f):
    z, t, h, dh = q.shape
    nc = t // CHUNK
    def chunks(a):
        a = a.astype(jnp.float32).reshape((z, nc, CHUNK, h) + a.shape[3:])
        return jnp.swapaxes(jnp.swapaxes(a, 0, 1), 2, 3)
    qc, kc, vc, lic, lfc = chunks(q), chunks(k), chunks(v), chunks(log_i), chunks(log_f)
    tri = jnp.tril(jnp.ones((CHUNK, CHUNK), bool))

    def step(carry, inp):
        c_mat, n_vec, m = carry
        qb, kb, vb, li, lf = inp
        b = jnp.cumsum(lf, -1)
        dmat = jnp.where(tri, b[..., :, None] - b[..., None, :] + li[..., None, :], -jnp.inf)
        m_inter = b + m[..., None]
        m_t = jnp.maximum(m_inter, jnp.max(dmat, -1))
        s = jnp.einsum('zhtd,zhsd->zhts', qb, kb) * jnp.exp(dmat - m_t[..., None])
        inter = jnp.exp(m_inter - m_t)
        num = inter[..., None] * jnp.einsum('zhtd,zhde->zhte', qb, c_mat) + jnp.einsum('zhts,zhse->zhte', s, vb)
        den = inter * jnp.einsum('zhtd,zhd->zht', qb, n_vec) + jnp.sum(s, -1)
        h_out = num / jnp.maximum(jnp.abs(den), jnp.exp(-m_t))[..., None]
        b_last = b[..., -1]
        g = b_last[..., None] - b + li
        m_new = jnp.maximum(b_last + m, jnp.max(g, -1))
        wts = jnp.exp(g - m_new[..., None])
        decay = jnp.exp(b_last + m - m_new)
        c_mat = decay[..., None, None] * c_mat + jnp.einsum('zhs,zhsd,zhse->zhde', wts, kb, vb)
        n_vec = decay[..., None] * n_vec + jnp.einsum('zhs,zhsd->zhd', wts, kb)
        return (c_mat, n_vec, m_new), h_out

    init = (jnp.zeros((z, h, dh, dh), jnp.float32), jnp.zeros((z, h, dh), jnp.float32), jnp.zeros((z, h), jnp.float32))
    _, hs = lax.scan(step, init, (qc, kc, vc, lic, lfc))
    return jnp.swapaxes(jnp.swapaxes(hs, 2, 3), 0, 1).reshape(z, t, h, dh)


def _mlstm_group(mq, mk, mv, mo, gates, b_i, b_f, norm_w):
    b, t, _ = mq.shape
    q = jax.nn.silu(mq).reshape(b, t, H_M, D_HM)
    k = jax.nn.silu(mk).reshape(b, t, H_M, D_HM) * (D_HM ** -0.5)
    v = mv.reshape(b, t, H_M, D_HM)
    gt = gates.astype(jnp.float32).reshape(b, t, 4, H_M)
    li_f = gt[:, :, 0] + b_i[0]
    lf_f = jax.nn.log_sigmoid(gt[:, :, 1] + b_f[0])
    li_b = gt[:, :, 2] + b_i[1]
    lf_b = jax.nn.log_sigmoid(gt[:, :, 3] + b_f[1])
    h = _mlstm_chunkwise(_dir_stack(q, q), _dir_stack(k, k), _dir_stack(v, v),
                         _dir_stack(li_f, li_b), _dir_stack(lf_f, lf_b))
    h = _headnorm(_dir_merge(h), LN_EPS) * norm_w.reshape(H_M, D_HM)
    return jax.nn.sigmoid(mo) * h.reshape(b, t, W_M)


def _rwkv7_scan(r, w, k, v, kh, a):
    z, t, h, n = r.shape
    seq = tuple(jnp.moveaxis(s.astype(jnp.float32), 1, 0) for s in (r, w, k, v, kh, a))

    def step(state, inp):
        r_t, w_t, k_t, v_t, kh_t, a_t = inp
        removed = jnp.einsum('zhvk,zhk->zhv', state, kh_t)
        state = (state * w_t[:, :, None, :] - removed[..., None] * (kh_t * a_t)[:, :, None, :]
                 + v_t[..., None] * k_t[:, :, None, :])
        return state, jnp.einsum('zhvk,zhk->zhv', state, r_t)

    _, y = lax.scan(step, jnp.zeros((z, h, n, n), jnp.float32), seq)
    return jnp.moveaxis(y, 0, 1)


def _rwkv7_group(rr, rk, rv, lw_f, lw_b, la, lg, w0, w_b_mat, a0, a_b_mat, g_b_mat, kk, ka, bonus_w, gn_w, gn_b):
    b, t, _ = rr.shape
    shp = (b, t, H_R, D_HR)
    w_fwd = jnp.exp(-DECAY_SCALE * jax.nn.sigmoid((w0[0] + jnp.tanh(lw_f) @ w_b_mat[0]).astype(jnp.float32)))
    w_bwd = jnp.exp(-DECAY_SCALE * jax.nn.sigmoid((w0[1] + jnp.tanh(lw_b) @ w_b_mat[1]).astype(jnp.float32)))
    a = jax.nn.sigmoid(a0 + la @ a_b_mat)
    g = jax.nn.sigmoid(lg) @ g_b_mat
    kap = (rk * kk).reshape(shp).astype(jnp.float32)
    kh = kap / jnp.maximum(jnp.sqrt(jnp.sum(jnp.square(kap), -1, keepdims=True)), 1e-12)
    kmod = (rk * (1 + (a - 1) * ka)).reshape(shp)
    r = rr.reshape(shp)
    v = rv.reshape(shp)
    a_h = a.reshape(shp)
    y = _rwkv7_scan(_dir_stack(r, r), _dir_stack(w_fwd.reshape(shp), w_bwd.reshape(shp)), _dir_stack(kmod, kmod),
                    _dir_stack(v, v), _dir_stack(kh, kh), _dir_stack(a_h, a_h))
    y = _headnorm(_dir_merge(y), GN_EPS) * gn_w.reshape(H_R, D_HR) + gn_b.reshape(H_R, D_HR)
    bonus = jnp.sum(r * kmod * bonus_w, -1, keepdims=True) * v
    return (y + bonus).reshape(b, t, W_R) * g


def _mixer(h_ctx, h_lat, rows, w_in, conv_w, m_bias_i, m_bias_f, m_norm_w, r_w0, r_wB, r_a0, r_aB, r_gB,
           r_kk, r_ka, r_bonus, r_norm_w, r_norm_b):
    p_ctx = h_ctx @ w_in
    p_lat = h_lat @ w_in
    p_ctx = jnp.concatenate([_conv_seq(p_ctx[..., :CONV_CH], conv_w[CONV_K // 2]), p_ctx[..., CONV_CH:]], -1)
    p_lat = jnp.concatenate([_conv_grid(p_lat[..., :CONV_CH], conv_w, rows), p_lat[..., CONV_CH:]], -1)
    u = jnp.concatenate([p_ctx, p_lat], axis=1)
    mq, mk, rr, rk, rv, mv, mo, gates, lw_f, lw_b, la, lg = jnp.split(u, SPLIT_POINTS, axis=-1)
    out_m = _mlstm_group(mq, mk, mv, mo, gates, m_bias_i, m_bias_f, m_norm_w)
    out_r = _rwkv7_group(rr, rk, rv, lw_f, lw_b, la, lg, r_w0, r_wB, r_a0, r_aB, r_gB, r_kk, r_ka, r_bonus,
                         r_norm_w, r_norm_b)
    return jnp.concatenate([out_m.astype(out_r.dtype), out_r], -1)


def _hier_moe(h, rt_g, rt_g_b, rt_e, rt_e_b, ex_gate, ex_up, ex_down):
    b, t, d = h.shape
    tok = h.reshape(b * t, d)
    n = tok.shape[0]
    lg = (tok @ rt_g + rt_g_b).astype(jnp.float32)
    p_grp = jax.nn.softmax(lg, axis=-1)
    grp = jnp.argmax(lg, axis=-1).astype(jnp.int32)
    p_top = jnp.take_along_axis(p_grp, grp[:, None], axis=-1)
    le = (tok @ rt_e + rt_e_b).astype(jnp.float32).reshape(n, N_GROUPS, E_PER_GROUP)
    le_grp = jnp.take_along_axis(le, jnp.broadcast_to(grp[:, None, None], (n, 1, E_PER_GROUP)), axis=1)[:, 0]
    top_val, top_idx = lax.top_k(le_grp, TOP_K_IN_GROUP)
    wts = jax.nn.softmax(top_val, axis=-1) * p_top
    e_flat = (grp[:, None] * E_PER_GROUP + top_idx.astype(jnp.int32)).reshape(-1)
    w_flat = wts.reshape(-1)
    t_flat = jnp.repeat(jnp.arange(n, dtype=jnp.int32), TOP_K_IN_GROUP)
    order = jnp.argsort(e_flat)
    e_s, t_s, w_s = e_flat[order], t_flat[order], w_flat[order]
    counts = jnp.zeros((N_EXPERTS,), jnp.int32).at[e_flat].add(1)
    starts = jnp.cumsum(counts) - counts
    padded = (counts + MOE_BLOCK - 1) // MOE_BLOCK * MOE_BLOCK
    pad_ends = jnp.cumsum(padded)
    pad_starts = pad_ends - padded
    dest = pad_starts[e_s] + jnp.arange(e_s.shape[0], dtype=jnp.int32) - starts[e_s]
    n_pairs = n * TOP_K_IN_GROUP
    buf = (-(-n_pairs // MOE_BLOCK) + N_EXPERTS) * MOE_BLOCK
    tok_buf = jnp.full((buf,), n, jnp.int32).at[dest].set(t_s)
    w_buf = jnp.zeros((buf,), w_s.dtype).at[dest].set(w_s)
    n_blk = buf // MOE_BLOCK
    blk_exp = jnp.minimum(jnp.searchsorted(pad_ends, jnp.arange(n_blk, dtype=jnp.int32) * MOE_BLOCK, side='right'),
                          N_EXPERTS - 1)
    tok_pad = jnp.concatenate([tok, jnp.zeros((1, d), tok.dtype)], 0)

    def expert_block(args):
        idx, w, e = args
        xb = tok_pad[idx]
        hb = jax.nn.silu(xb @ ex_gate[e]) * (xb @ ex_up[e])
        return (hb @ ex_down[e]) * w[:, None]

    y = lax.map(expert_block, (tok_buf.reshape(n_blk, MOE_BLOCK), w_buf.reshape(n_blk, MOE_BLOCK), blk_exp))
    out = jnp.zeros((n + 1, d), y.dtype).at[tok_buf].add(y.reshape(buf, d))
    return out[:n].reshape(b, t, d).astype(h.dtype)


def setup_inputs(seed: int = 0) -> dict:
    key = jax.random.key(seed)
    ks = jax.random.split(key, 40)
    def nrm(k, shape, s):
        return jax.random.normal(k, shape, jnp.float32) * s
    L = DEPTH
    return {
        'x': nrm(ks[0], (BATCH, SEQ, D_MODEL), 1.0),
        'c': nrm(ks[1], (BATCH, D_MODEL), 1.0),
        'ctx': nrm(ks[2], (BATCH, CTX_LEN, D_MODEL), 1.0),
        'c_ctx': nrm(ks[3], (D_MODEL,), 1.0),
        'w_ada': nrm(ks[4], (L, D_MODEL, 6 * D_MODEL), D_MODEL ** -0.5),
        'b_ada': nrm(ks[5], (L, 6 * D_MODEL), 0.02),
        'w_in': nrm(ks[6], (L, D_MODEL, IN_COLS), D_MODEL ** -0.5),
        'conv_w': nrm(ks[7], (L, CONV_K, CONV_K, CONV_CH), 0.3),
        'm_bias_i': nrm(ks[8], (L, 2, H_M), 0.5),
        'm_bias_f': 3.0 + nrm(ks[9], (L, 2, H_M), 0.5),
        'm_norm_w': 1.0 + nrm(ks[10], (L, W_M), 0.05),
        'r_w0': nrm(ks[11], (L, 2, W_R), 0.5),
        'r_wB': nrm(ks[12], (L, 2, W_LORA, W_R), W_LORA ** -0.5),
        'r_a0': nrm(ks[13], (L, W_R), 0.5),
        'r_aB': nrm(ks[14], (L, A_LORA, W_R), A_LORA ** -0.5),
        'r_gB': nrm(ks[15], (L, G_LORA, W_R), G_LORA ** -0.5),
        'r_kk': 0.85 + nrm(ks[16], (L, W_R), 0.05),
        'r_ka': 1.0 + nrm(ks[17], (L, W_R), 0.05),
        'r_bonus': nrm(ks[18], (L, H_R, D_HR), 0.1),
        'r_norm_w': 1.0 + nrm(ks[19], (L, W_R), 0.05),
        'r_norm_b': nrm(ks[20], (L, W_R), 0.02),
        'w_out': nrm(ks[21], (L, MIX_W, D_MODEL), BETA * MIX_W ** -0.5),
        'ln1_g': 1.0 + nrm(ks[22], (L, D_MODEL), 0.05),
        'ln1_b': nrm(ks[23], (L, D_MODEL), 0.02),
        'ln2_g': 1.0 + nrm(ks[24], (L, D_MODEL), 0.05),
        'ln2_b': nrm(ks[25], (L, D_MODEL), 0.02),
        'rt_g': nrm(ks[26], (L, D_MODEL, N_GROUPS), D_MODEL ** -0.5),
        'rt_g_b': nrm(ks[27], (L, N_GROUPS), 0.01),
        'rt_e': nrm(ks[28], (L, D_MODEL, N_EXPERTS), D_MODEL ** -0.5),
        'rt_e_b': nrm(ks[29], (L, N_EXPERTS), 0.01),
        'ex_gate': nrm(ks[30], (L, N_EXPERTS, D_MODEL, D_EXPERT), D_MODEL ** -0.5),
        'ex_up': nrm(ks[31], (L, N_EXPERTS, D_MODEL, D_EXPERT), D_MODEL ** -0.5),
        'ex_down': nrm(ks[32], (L, N_EXPERTS, D_EXPERT, D_MODEL), BETA * D_EXPERT ** -0.5),
    }


def reference(x, c, ctx, c_ctx, w_ada, b_ada, w_in, conv_w, m_bias_i, m_bias_f, m_norm_w, r_w0, r_wB, r_a0, r_aB,
              r_gB, r_kk, r_ka, r_bonus, r_norm_w, r_norm_b, w_out, ln1_g, ln1_b, ln2_g, ln2_b, rt_g, rt_g_b, rt_e,
              rt_e_b, ex_gate, ex_up, ex_down):
    rows = x.shape[1] // GRID_W
    for l in range(DEPTH):
        mod = jax.nn.silu(c) @ w_ada[l] + b_ada[l]
        mod_c = jax.nn.silu(c_ctx) @ w_ada[l] + b_ada[l]
        sh1, sc1, g1, sh2, sc2, g2 = jnp.split(mod[:, None, :], 6, axis=-1)
        sh1c, sc1c, g1c, sh2c, sc2c, g2c = jnp.split(mod_c, 6)
        mixed = _mixer(_modulate(ctx, sh1c, sc1c), _modulate(x, sh1, sc1), rows, w_in[l], conv_w[l], m_bias_i[l],
                       m_bias_f[l], m_norm_w[l], r_w0[l], r_wB[l], r_a0[l], r_aB[l], r_gB[l], r_kk[l], r_ka[l],
                       r_bonus[l], r_norm_w[l], r_norm_b[l])
        x = _layernorm(ALPHA * x + g1 * (mixed[:, CTX_LEN:] @ w_out[l]), ln1_g[l], ln1_b[l])
        ffn = _hier_moe(_modulate(x, sh2, sc2), rt_g[l], rt_g_b[l], rt_e[l], rt_e_b[l], ex_gate[l], ex_up[l], ex_down[l])
        x = _layernorm(ALPHA * x + g2 * ffn, ln2_g[l], ln2_b[l])
        if l + 1 < DEPTH:
            ctx = _layernorm(ALPHA * ctx + g1c * (mixed[:, :CTX_LEN] @ w_out[l]), ln1_g[l], ln1_b[l])
            ffn_c = _hier_moe(_modulate(ctx, sh2c, sc2c), rt_g[l], rt_g_b[l], rt_e[l], rt_e_b[l], ex_gate[l],
                              ex_up[l], ex_down[l])
            ctx = _layernorm(ALPHA * ctx + g2c * ffn_c, ln2_g[l], ln2_b[l])
    return x
```

```python
import functools
import math

import jax
import jax.numpy as jnp
from jax import lax
import numpy as np
from jax.experimental import pallas as pl
from jax.experimental.pallas import tpu as pltpu

H_M = 8
D_HM = 64
W_M = H_M * D_HM
H_R = 8
D_HR = 64
W_R = H_R * D_HR
MIX_W = W_M + W_R
W_LORA = 64
A_LORA = 64
G_LORA = 128
CONV_K = 3
CONV_CH = 2 * W_M + 3 * W_R
SECTION_WIDTHS = (W_M, W_M, W_R, W_R, W_R, W_M, W_M, 4 * H_M, W_LORA, W_LORA, A_LORA, G_LORA)
IN_COLS = sum(SECTION_WIDTHS)
SPLIT_POINTS = tuple(int(v) for v in np.cumsum(SECTION_WIDTHS)[:-1])
CHUNK = 64
N_GROUPS = 4
E_PER_GROUP = 8
N_EXPERTS = N_GROUPS * E_PER_GROUP
TOP_K_IN_GROUP = 2
D_EXPERT = 512
MOE_BLOCK = 256
DEPTH = 1
ALPHA = (2.0 * DEPTH) ** 0.25
DECAY_SCALE = math.exp(-0.5)
LN_EPS = 1e-6
GN_EPS = 64e-5

ROW_TILE = 256
NONCONV_W = 1536
VMEM_LIMIT = 56 * 1024 * 1024


def _inproj_kernel(ctx_ref, x_ref, mod_ref, w_ref, oc_ref, on_ref):
    j = pl.program_id(1)

    def body(src_ref):
        z = src_ref[0]
        mu = jnp.mean(z, -1, keepdims=True)
        zc = z - mu
        var = jnp.mean(zc * zc, -1, keepdims=True)
        y = zc * lax.rsqrt(var + LN_EPS)
        h = (y * (1.0 + mod_ref[0, 1:2, :]) + mod_ref[0, 0:1, :]).astype(jnp.bfloat16)
        oc_ref[0] = jnp.dot(h, w_ref[:, :CONV_CH], preferred_element_type=jnp.float32)
        on_ref[0] = jnp.dot(h, w_ref[:, CONV_CH:], preferred_element_type=jnp.float32)

    @pl.when(j == 0)
    def _():
        body(ctx_ref)

    @pl.when(j > 0)
    def _():
        body(x_ref)


def _inproj(ctx, x, mod_all, w_pad):
    b, seq, d = x.shape
    nt = (ctx.shape[1] + seq) // ROW_TILE
    t_all = ctx.shape[1] + seq
    return pl.pallas_call(
        _inproj_kernel,
        grid=(b, nt),
        in_specs=[
            pl.BlockSpec((1, ROW_TILE, d), lambda i, j: (i, 0, 0)),
            pl.BlockSpec((1, ROW_TILE, d), lambda i, j: (i, jnp.maximum(j - 1, 0), 0)),
            pl.BlockSpec((1, 6, d), lambda i, j: (jnp.where(j == 0, b, i), 0, 0)),
            pl.BlockSpec((d, CONV_CH + NONCONV_W), lambda i, j: (0, 0)),
        ],
        out_specs=[
            pl.BlockSpec((1, ROW_TILE, CONV_CH), lambda i, j: (i, j, 0)),
            pl.BlockSpec((1, ROW_TILE, NONCONV_W), lambda i, j: (i, j, 0)),
        ],
        out_shape=[
            jax.ShapeDtypeStruct((b, t_all, CONV_CH), jnp.float32),
            jax.ShapeDtypeStruct((b, t_all, NONCONV_W), jnp.float32),
        ],
        compiler_params=pltpu.CompilerParams(
            dimension_semantics=("arbitrary", "arbitrary"), vmem_limit_bytes=VMEM_LIMIT),
        name="inproj",
    )(ctx, x, mod_all, w_pad)


def _pad_w_in(w_in):
    d = w_in.shape[0]
    sp = SPLIT_POINTS
    z = lambda n: jnp.zeros((d, n), w_in.dtype)
    gates = w_in[:, sp[6]:sp[7]]
    lw = w_in[:, sp[7]:sp[9]]
    la = w_in[:, sp[9]:sp[10]]
    lg = w_in[:, sp[10]:]
    return jnp.concatenate([w_in[:, :sp[6]], gates, z(96), lw, la, z(64), lg], axis=1).astype(jnp.bfloat16)


def _layernorm(z, gain=None, bias=None):
    zf = z.astype(jnp.float32)
    mu = jnp.mean(zf, -1, keepdims=True)
    var = jnp.mean(jnp.square(zf - mu), -1, keepdims=True)
    y = (zf - mu) * lax.rsqrt(var + LN_EPS)
    if gain is not None:
        y = y * gain + bias
    return y.astype(z.dtype)


def _headnorm(z, eps):
    zf = z.astype(jnp.float32)
    mu = jnp.mean(zf, -1, keepdims=True)
    var = jnp.mean(jnp.square(zf - mu), -1, keepdims=True)
    return (zf - mu) * lax.rsqrt(var + eps)


def _modulate(z, shift, scale):
    return _layernorm(z) * (1 + scale) + shift


def _rev_segments(u, ctx_len):
    return jnp.concatenate([jnp.flip(u[:, :ctx_len], 1), jnp.flip(u[:, ctx_len:], 1)], axis=1)


def _dir_stack(u_fwd, u_bwd, ctx_len):
    return jnp.concatenate([u_fwd, _rev_segments(u_bwd, ctx_len)], axis=0)


def _dir_merge(y, ctx_len):
    b = y.shape[0] // 2
    return y[:b] + _rev_segments(y[b:], ctx_len)


def _conv_grid(u, w, rows):
    b, l, ch = u.shape
    img = u.reshape(b, rows, l // rows, ch)
    out = lax.conv_general_dilated(img, w[:, :, None, :].astype(u.dtype), (1, 1), 'SAME',
                                   dimension_numbers=('NHWC', 'HWIO', 'NHWC'), feature_group_count=ch)
    return out.reshape(b, l, ch)


def _conv_seq(u, w_row):
    ch = u.shape[-1]
    return lax.conv_general_dilated(u, w_row[:, None, :].astype(u.dtype), (1,), 'SAME',
                                    dimension_numbers=('NWC', 'WIO', 'NWC'), feature_group_count=ch)


def _mlstm_chunkwise(q, k, v, log_i, log_f):
    z, t, h, dh = q.shape
    nc = t // CHUNK

    def chunks(a):
        a = a.astype(jnp.float32).reshape((z, nc, CHUNK, h) + a.shape[3:])
        return jnp.swapaxes(jnp.swapaxes(a, 0, 1), 2, 3)
    qc, kc, vc, lic, lfc = chunks(q), chunks(k), chunks(v), chunks(log_i), chunks(log_f)
    tri = jnp.tril(jnp.ones((CHUNK, CHUNK), bool))

    def step(carry, inp):
        c_mat, n_vec, m = carry
        qb, kb, vb, li, lf = inp
        b = jnp.cumsum(lf, -1)
        dmat = jnp.where(tri, b[..., :, None] - b[..., None, :] + li[..., None, :], -jnp.inf)
        m_inter = b + m[..., None]
        m_t = jnp.maximum(m_inter, jnp.max(dmat, -1))
        s = jnp.einsum('zhtd,zhsd->zhts', qb, kb) * jnp.exp(dmat - m_t[..., None])
        inter = jnp.exp(m_inter - m_t)
        num = inter[..., None] * jnp.einsum('zhtd,zhde->zhte', qb, c_mat) + jnp.einsum('zhts,zhse->zhte', s, vb)
        den = inter * jnp.einsum('zhtd,zhd->zht', qb, n_vec) + jnp.sum(s, -1)
        h_out = num / jnp.maximum(jnp.abs(den), jnp.exp(-m_t))[..., None]
        b_last = b[..., -1]
        g = b_last[..., None] - b + li
        m_new = jnp.maximum(b_last + m, jnp.max(g, -1))
        wts = jnp.exp(g - m_new[..., None])
        decay = jnp.exp(b_last + m - m_new)
        c_mat = decay[..., None, None] * c_mat + jnp.einsum('zhs,zhsd,zhse->zhde', wts, kb, vb)
        n_vec = decay[..., None] * n_vec + jnp.einsum('zhs,zhsd->zhd', wts, kb)
        return (c_mat, n_vec, m_new), h_out

    init = (jnp.zeros((z, h, dh, dh), jnp.float32), jnp.zeros((z, h, dh), jnp.float32), jnp.zeros((z, h), jnp.float32))
    _, hs = lax.scan(step, init, (qc, kc, vc, lic, lfc))
    return jnp.swapaxes(jnp.swapaxes(hs, 2, 3), 0, 1).reshape(z, t, h, dh)


def _mlstm_group(mq, mk, mv, mo, gates, b_i, b_f, norm_w, ctx_len):
    b, t, _ = mq.shape
    q = jax.nn.silu(mq).reshape(b, t, H_M, D_HM)
    k = jax.nn.silu(mk).reshape(b, t, H_M, D_HM) * (D_HM ** -0.5)
    v = mv.reshape(b, t, H_M, D_HM)
    gt = gates.astype(jnp.float32).reshape(b, t, 4, H_M)
    li_f = gt[:, :, 0] + b_i[0]
    lf_f = jax.nn.log_sigmoid(gt[:, :, 1] + b_f[0])
    li_b = gt[:, :, 2] + b_i[1]
    lf_b = jax.nn.log_sigmoid(gt[:, :, 3] + b_f[1])
    ds = functools.partial(_dir_stack, ctx_len=ctx_len)
    h = _mlstm_chunkwise(ds(q, q), ds(k, k), ds(v, v), ds(li_f, li_b), ds(lf_f, lf_b))
    h = _headnorm(_dir_merge(h, ctx_len), LN_EPS) * norm_w.reshape(H_M, D_HM)
    return jax.nn.sigmoid(mo) * h.reshape(b, t, W_M)


def _rwkv7_scan(r, w, k, v, kh, a):
    z, t, h, n = r.shape
    seq = tuple(jnp.moveaxis(s.astype(jnp.float32), 1, 0) for s in (r, w, k, v, kh, a))

    def step(state, inp):
        r_t, w_t, k_t, v_t, kh_t, a_t = inp
        removed = jnp.einsum('zhvk,zhk->zhv', state, kh_t)
        state = (state * w_t[:, :, None, :] - removed[..., None] * (kh_t * a_t)[:, :, None, :]
                 + v_t[..., None] * k_t[:, :, None, :])
        return state, jnp.einsum('zhvk,zhk->zhv', state, r_t)

    _, y = lax.scan(step, jnp.zeros((z, h, n, n), jnp.float32), seq)
    return jnp.moveaxis(y, 0, 1)


def _rwkv7_group(rr, rk, rv, lw_f, lw_b, la, lg, w0, w_b_mat, a0, a_b_mat, g_b_mat, kk, ka, bonus_w, gn_w, gn_b,
                 ctx_len):
    b, t, _ = rr.shape
    shp = (b, t, H_R, D_HR)
    w_fwd = jnp.exp(-DECAY_SCALE * jax.nn.sigmoid((w0[0] + jnp.tanh(lw_f) @ w_b_mat[0]).astype(jnp.float32)))
    w_bwd = jnp.exp(-DECAY_SCALE * jax.nn.sigmoid((w0[1] + jnp.tanh(lw_b) @ w_b_mat[1]).astype(jnp.float32)))
    a = jax.nn.sigmoid(a0 + la @ a_b_mat)
    g = jax.nn.sigmoid(lg) @ g_b_mat
    kap = (rk * kk).reshape(shp).astype(jnp.float32)
    kh = kap / jnp.maximum(jnp.sqrt(jnp.sum(jnp.square(kap), -1, keepdims=True)), 1e-12)
    kmod = (rk * (1 + (a - 1) * ka)).reshape(shp)
    r = rr.reshape(shp)
    v = rv.reshape(shp)
    a_h = a.reshape(shp)
    ds = functools.partial(_dir_stack, ctx_len=ctx_len)
    y = _rwkv7_scan(ds(r, r), ds(w_fwd.reshape(shp), w_bwd.reshape(shp)), ds(kmod, kmod),
                    ds(v, v), ds(kh, kh), ds(a_h, a_h))
    y = _headnorm(_dir_merge(y, ctx_len), GN_EPS) * gn_w.reshape(H_R, D_HR) + gn_b.reshape(H_R, D_HR)
    bonus = jnp.sum(r * kmod * bonus_w, -1, keepdims=True) * v
    return (y + bonus).reshape(b, t, W_R) * g


def _hier_moe(h, rt_g, rt_g_b, rt_e, rt_e_b, ex_gate, ex_up, ex_down):
    b, t, d = h.shape
    tok = h.reshape(b * t, d)
    n = tok.shape[0]
    lg = (tok @ rt_g + rt_g_b).astype(jnp.float32)
    p_grp = jax.nn.softmax(lg, axis=-1)
    grp = jnp.argmax(lg, axis=-1).astype(jnp.int32)
    p_top = jnp.take_along_axis(p_grp, grp[:, None], axis=-1)
    le = (tok @ rt_e + rt_e_b).astype(jnp.float32).reshape(n, N_GROUPS, E_PER_GROUP)
    le_grp = jnp.take_along_axis(le, jnp.broadcast_to(grp[:, None, None], (n, 1, E_PER_GROUP)), axis=1)[:, 0]
    top_val, top_idx = lax.top_k(le_grp, TOP_K_IN_GROUP)
    wts = jax.nn.softmax(top_val, axis=-1) * p_top
    e_flat = (grp[:, None] * E_PER_GROUP + top_idx.astype(jnp.int32)).reshape(-1)
    w_flat = wts.reshape(-1)
    t_flat = jnp.repeat(jnp.arange(n, dtype=jnp.int32), TOP_K_IN_GROUP)
    order = jnp.argsort(e_flat)
    e_s, t_s, w_s = e_flat[order], t_flat[order], w_flat[order]
    counts = jnp.zeros((N_EXPERTS,), jnp.int32).at[e_flat].add(1)
    starts = jnp.cumsum(counts) - counts
    padded = (counts + MOE_BLOCK - 1) // MOE_BLOCK * MOE_BLOCK
    pad_ends = jnp.cumsum(padded)
    pad_starts = pad_ends - padded
    dest = pad_starts[e_s] + jnp.arange(e_s.shape[0], dtype=jnp.int32) - starts[e_s]
    n_pairs = n * TOP_K_IN_GROUP
    buf = (-(-n_pairs // MOE_BLOCK) + N_EXPERTS) * MOE_BLOCK
    tok_buf = jnp.full((buf,), n, jnp.int32).at[dest].set(t_s)
    w_buf = jnp.zeros((buf,), w_s.dtype).at[dest].set(w_s)
    n_blk = buf // MOE_BLOCK
    blk_exp = jnp.minimum(jnp.searchsorted(pad_ends, jnp.arange(n_blk, dtype=jnp.int32) * MOE_BLOCK, side='right'),
                          N_EXPERTS - 1)
    tok_pad = jnp.concatenate([tok, jnp.zeros((1, d), tok.dtype)], 0)

    def expert_block(args):
        idx, w, e = args
        xb = tok_pad[idx]
        hb = jax.nn.silu(xb @ ex_gate[e]) * (xb @ ex_up[e])
        return (hb @ ex_down[e]) * w[:, None]

    y = lax.map(expert_block, (tok_buf.reshape(n_blk, MOE_BLOCK), w_buf.reshape(n_blk, MOE_BLOCK), blk_exp))
    out = jnp.zeros((n + 1, d), y.dtype).at[tok_buf].add(y.reshape(buf, d))
    return out[:n].reshape(b, t, d).astype(h.dtype)


def kernel(x, c, ctx, c_ctx, w_ada, b_ada, w_in, conv_w, m_bias_i, m_bias_f, m_norm_w, r_w0, r_wB, r_a0, r_aB,
           r_gB, r_kk, r_ka, r_bonus, r_norm_w, r_norm_b, w_out, ln1_g, ln1_b, ln2_g, ln2_b, rt_g, rt_g_b, rt_e,
           rt_e_b, ex_gate, ex_up, ex_down):
    assert w_ada.shape[0] == DEPTH
    bsz, seq, d = x.shape
    ctx_len = ctx.shape[1]
    assert ctx_len == ROW_TILE and seq % ROW_TILE == 0
    grid_w = 64
    rows = seq // grid_w
    l = 0
    c_all = jnp.concatenate([c, c_ctx[None, :]], 0)
    mod_all = jax.nn.silu(c_all) @ w_ada[l] + b_ada[l]
    mod = mod_all[:bsz]
    sh1, sc1, g1, sh2, sc2, g2 = jnp.split(mod[:, None, :], 6, axis=-1)

    pc, pn = _inproj(ctx, x, mod_all.reshape(bsz + 1, 6, d), _pad_w_in(w_in[l]))
    p_ctx_c, p_lat_c = pc[:, :ctx_len], pc[:, ctx_len:]
    uc = jnp.concatenate([_conv_seq(p_ctx_c, conv_w[l][CONV_K // 2]), _conv_grid(p_lat_c, conv_w[l], rows)], axis=1)
    mq, mk, rr, rk, rv = jnp.split(uc, 5, axis=-1)
    mv, mo = pn[..., :512], pn[..., 512:1024]
    gates = pn[..., 1024:1056]
    lw_f, lw_b = pn[..., 1152:1216], pn[..., 1216:1280]
    la = pn[..., 1280:1344]
    lg = pn[..., 1408:1536]
    out_m = _mlstm_group(mq, mk, mv, mo, gates, m_bias_i[l], m_bias_f[l], m_norm_w[l], ctx_len)
    out_r = _rwkv7_group(rr, rk, rv, lw_f, lw_b, la, lg, r_w0[l], r_wB[l], r_a0[l], r_aB[l], r_gB[l], r_kk[l],
                         r_ka[l], r_bonus[l], r_norm_w[l], r_norm_b[l], ctx_len)
    mixed = jnp.concatenate([out_m, out_r], -1)
    x = _layernorm(ALPHA * x + g1 * (mixed[:, ctx_len:] @ w_out[l]), ln1_g[l], ln1_b[l])
    ffn = _hier_moe(_modulate(x, sh2, sc2), rt_g[l], rt_g_b[l], rt_e[l], rt_e_b[l], ex_gate[l], ex_up[l], ex_down[l])
    x = _layernorm(ALPHA * x + g2 * ffn, ln2_g[l], ln2_b[l])
    return x
```

```python
import functools
import math

import jax
import jax.numpy as jnp
from jax import lax
import numpy as np
from jax.experimental import pallas as pl
from jax.experimental.pallas import tpu as pltpu

H_M = 8
D_HM = 64
W_M = H_M * D_HM
H_R = 8
D_HR = 64
W_R = H_R * D_HR
MIX_W = W_M + W_R
W_LORA = 64
A_LORA = 64
G_LORA = 128
CONV_K = 3
CONV_CH = 2 * W_M + 3 * W_R
SECTION_WIDTHS = (W_M, W_M, W_R, W_R, W_R, W_M, W_M, 4 * H_M, W_LORA, W_LORA, A_LORA, G_LORA)
IN_COLS = sum(SECTION_WIDTHS)
SPLIT_POINTS = tuple(int(v) for v in np.cumsum(SECTION_WIDTHS)[:-1])
CHUNK = 64
N_GROUPS = 4
E_PER_GROUP = 8
N_EXPERTS = N_GROUPS * E_PER_GROUP
TOP_K_IN_GROUP = 2
D_EXPERT = 512
MOE_BLOCK = 256
DEPTH = 1
ALPHA = (2.0 * DEPTH) ** 0.25
DECAY_SCALE = math.exp(-0.5)
LN_EPS = 1e-6
GN_EPS = 64e-5

ROW_TILE = 256
NONCONV_W = 1536
VMEM_LIMIT = 56 * 1024 * 1024


def _inproj_kernel(ctx_ref, x_ref, mod_ref, w_ref, oc_ref, on_ref):
    j = pl.program_id(1)

    def body(src_ref):
        z = src_ref[0]
        mu = jnp.mean(z, -1, keepdims=True)
        zc = z - mu
        var = jnp.mean(zc * zc, -1, keepdims=True)
        y = zc * lax.rsqrt(var + LN_EPS)
        h = (y * (1.0 + mod_ref[0, 1:2, :]) + mod_ref[0, 0:1, :]).astype(jnp.bfloat16)
        oc_ref[0] = jnp.dot(h, w_ref[:, :CONV_CH], preferred_element_type=jnp.float32)
        on_ref[0] = jnp.dot(h, w_ref[:, CONV_CH:], preferred_element_type=jnp.float32)

    @pl.when(j == 0)
    def _():
        body(ctx_ref)

    @pl.when(j > 0)
    def _():
        body(x_ref)


def _inproj(ctx, x, mod_all, w_pad):
    b, seq, d = x.shape
    nt = (ctx.shape[1] + seq) // ROW_TILE
    t_all = ctx.shape[1] + seq
    return pl.pallas_call(
        _inproj_kernel,
        grid=(b, nt),
        in_specs=[
            pl.BlockSpec((1, ROW_TILE, d), lambda i, j: (i, 0, 0)),
            pl.BlockSpec((1, ROW_TILE, d), lambda i, j: (i, jnp.maximum(j - 1, 0), 0)),
            pl.BlockSpec((1, 6, d), lambda i, j: (jnp.where(j == 0, b, i), 0, 0)),
            pl.BlockSpec((d, CONV_CH + NONCONV_W), lambda i, j: (0, 0)),
        ],
        out_specs=[
            pl.BlockSpec((1, ROW_TILE, CONV_CH), lambda i, j: (i, j, 0)),
            pl.BlockSpec((1, ROW_TILE, NONCONV_W), lambda i, j: (i, j, 0)),
        ],
        out_shape=[
            jax.ShapeDtypeStruct((b, t_all, CONV_CH), jnp.float32),
            jax.ShapeDtypeStruct((b, t_all, NONCONV_W), jnp.float32),
        ],
        compiler_params=pltpu.CompilerParams(
            dimension_semantics=("arbitrary", "arbitrary"), vmem_limit_bytes=VMEM_LIMIT),
        name="inproj",
    )(ctx, x, mod_all, w_pad)


def _pad_w_in(w_in):
    d = w_in.shape[0]
    sp = SPLIT_POINTS
    z = lambda n: jnp.zeros((d, n), w_in.dtype)
    gates = w_in[:, sp[6]:sp[7]]
    lw = w_in[:, sp[7]:sp[9]]
    la = w_in[:, sp[9]:sp[10]]
    lg = w_in[:, sp[10]:]
    return jnp.concatenate([w_in[:, :sp[6]], gates, z(96), lw, la, z(64), lg], axis=1).astype(jnp.bfloat16)


PAIR = 2 * D_HR
NPAIR = H_R // 2
_NN = (((1,), (0,)), ((), ()))
_NT = (((1,), (1,)), ((), ()))
_TN = (((0,), (0,)), ((), ()))


def _split2(a):
    hi = a.astype(jnp.bfloat16)
    lo = (a - hi.astype(jnp.float32)).astype(jnp.bfloat16)
    return hi, lo


def _dg(a, b, dims):
    return lax.dot_general(a, b, dims, preferred_element_type=jnp.float32)


def _dot3(a, b, dims=_NN):
    ah, al = _split2(a)
    bh, bl = _split2(b)
    return _dg(ah, bh, dims) + _dg(ah, bl, dims) + _dg(al, bh, dims)


def _dot_exact_lhs(a_bf16, b, dims=_NN):
    b1 = b.astype(jnp.bfloat16)
    r1 = b - b1.astype(jnp.float32)
    b2 = r1.astype(jnp.bfloat16)
    b3 = (r1 - b2.astype(jnp.float32)).astype(jnp.bfloat16)
    return _dg(a_bf16, b1, dims) + _dg(a_bf16, b2, dims) + _dg(a_bf16, b3, dims)


def _rwkv_pair_chunk(r, k, v, kh, ab, lw, s_mat, tri_incl, strict, incl, last_row, m0, m1):
    cw = _dot_exact_lhs(tri_incl, lw)
    e_pos = jnp.exp(cw)
    e_neg = jnp.exp(-cw)
    e_prev = jnp.exp(cw - lw)
    stack = lambda a: jnp.concatenate([a * m0, a * m1], axis=0)
    kt = stack(kh * e_prev)
    bt = stack(ab * e_neg)
    kk = stack(k * e_neg)
    rt = stack(r * e_pos)
    vs = stack(v)
    zero = jnp.zeros((), jnp.float32)
    a_bk = jnp.where(strict, _dot3(kt, bt, _NT), zero)
    a_kk = jnp.where(strict, _dot3(kt, kk, _NT), zero)
    a_rb = jnp.where(incl, _dot3(rt, bt, _NT), zero)
    a_rk = jnp.where(incl, _dot3(rt, kk, _NT), zero)
    p = -a_bk
    eye = (lax.broadcasted_iota(jnp.int32, (PAIR, PAIR), 0)
           == lax.broadcasted_iota(jnp.int32, (PAIR, PAIR), 1)).astype(jnp.float32)
    t_inv = eye + p
    for _ in range(5):
        p = _dot3(p, p)
        t_inv = t_inv + _dot3(t_inv, p)
    rhs = _dot3(kt, s_mat, _NT) + _dot3(a_kk, vs)
    u = -_dot3(t_inv, rhs)
    y = _dot3(rt, s_mat, _NT) + _dot3(a_rb, u) + _dot3(a_rk, vs)
    w_last = jnp.sum(jnp.where(last_row, e_pos, zero), axis=0, keepdims=True)
    s_new = (s_mat + _dot3(u, bt, _TN) + _dot3(vs, kk, _TN)) * w_last
    return y[:CHUNK] + y[CHUNK:], s_new


def _rwkv_kernel(rf, kf, vf, khf, abf, lwf, rb, kb, vb, khb, abb, lwb, yf_ref, yb_ref, s_ref, *, nctx):
    j = pl.program_id(1)

    @pl.when(j == 0)
    def _():
        s_ref[...] = jnp.zeros_like(s_ref)

    row = lax.broadcasted_iota(jnp.int32, (PAIR, PAIR), 0)
    col = lax.broadcasted_iota(jnp.int32, (PAIR, PAIR), 1)
    same = (row // CHUNK) == (col // CHUNK)
    r64 = lax.broadcasted_iota(jnp.int32, (CHUNK, CHUNK), 0)
    c64 = lax.broadcasted_iota(jnp.int32, (CHUNK, CHUNK), 1)
    rowl = lax.broadcasted_iota(jnp.int32, (CHUNK, PAIR), 0)
    lane = lax.broadcasted_iota(jnp.int32, (1, PAIR), 1)
    m0 = (lane < D_HR).astype(jnp.float32)
    m1 = 1.0 - m0
    dirs = (
        (rf, kf, vf, khf, abf, lwf, yf_ref, (c64 <= r64), same & (col < row), same & (col <= row), rowl == CHUNK - 1),
        (rb, kb, vb, khb, abb, lwb, yb_ref, (c64 >= r64), same & (col > row), same & (col >= row), rowl == 0),
    )
    for d, (r_ref, k_ref, v_ref, kh_ref, ab_ref, lw_ref, y_ref, tri, strict, incl, last_row) in enumerate(dirs):
        tri = tri.astype(jnp.float32).astype(jnp.bfloat16)
        for p in range(NPAIR):
            sl = slice(p * PAIR, (p + 1) * PAIR)
            y, s_new = _rwkv_pair_chunk(r_ref[0, :, sl], k_ref[0, :, sl], v_ref[0, :, sl], kh_ref[0, :, sl],
                                        ab_ref[0, :, sl], lw_ref[0, :, sl], s_ref[d, p], tri, strict, incl,
                                        last_row, m0, m1)
            s_ref[d, p] = s_new

            @pl.when(j >= nctx)
            def _():
                y_ref[0, :, sl] = y


def _rwkv_scan(r, k, v, kh, ab, lw_f, lw_b, ctx_len):
    b, t, w = r.shape
    nc = t // CHUNK
    nctx = ctx_len // CHUNK
    nlat = nc - nctx

    def fwd_map(i, j):
        return (i, j, 0)

    def bwd_map(i, j):
        return (i, jnp.where(j < nctx, nctx - 1 - j, nc - 1 - (j - nctx)), 0)

    blk = (1, CHUNK, w)
    in_specs = [pl.BlockSpec(blk, fwd_map)] * 6 + [pl.BlockSpec(blk, bwd_map)] * 6
    out_specs = [
        pl.BlockSpec(blk, lambda i, j: (i, jnp.maximum(j - nctx, 0), 0)),
        pl.BlockSpec(blk, lambda i, j: (i, nlat - 1 - jnp.maximum(j - nctx, 0), 0)),
    ]
    return pl.pallas_call(
        functools.partial(_rwkv_kernel, nctx=nctx),
        grid=(b, nc),
        in_specs=in_specs,
        out_specs=out_specs,
        out_shape=[jax.ShapeDtypeStruct((b, t - ctx_len, w), jnp.float32)] * 2,
        scratch_shapes=[pltpu.VMEM((2, NPAIR, PAIR, PAIR), jnp.float32)],
        compiler_params=pltpu.CompilerParams(
            dimension_semantics=("arbitrary", "arbitrary"), vmem_limit_bytes=VMEM_LIMIT),
        name="rwkv_scan",
    )(r, k, v, kh, ab, lw_f, r, k, v, kh, ab, lw_b)


def _col_stack(a, c0, c1):
    return jnp.concatenate([a[:, c0:c0 + 1], a[:, c1:c1 + 1]], axis=0)


def _half_bcast(v0, v1):
    return jnp.concatenate([jnp.broadcast_to(v0, (CHUNK, 1)), jnp.broadcast_to(v1, (CHUNK, 1))], axis=0)


def _mlstm_pair_chunk(q, k, v, b_col, li_col, c_row, c_mat, n_row, m_row, incl, last_sel, m0, m1):
    stack = lambda a: jnp.concatenate([a * m0, a * m1], axis=0)
    q_st, k_st, v_st = stack(q), stack(k), stack(v)
    lane = lax.broadcasted_iota(jnp.int32, (1, PAIR), 1)
    first = lane < D_HM
    neg_inf = jnp.full((), -jnp.inf, jnp.float32)
    m_h0, m_h1 = m_row[:, 0:1], m_row[:, D_HM:D_HM + 1]
    m_prev = _half_bcast(m_h0, m_h1)
    dmat = jnp.where(incl, b_col + c_row, neg_inf)
    m_inter = b_col + m_prev
    m_t = jnp.maximum(m_inter, jnp.max(dmat, axis=1, keepdims=True))
    s = _dot3(q_st, k_st, _NT) * jnp.exp(dmat - m_t)
    inter = jnp.exp(m_inter - m_t)
    num = inter * _dot3(q_st, c_mat) + _dot3(s, v_st)
    den = inter * jnp.sum(q_st * n_row, axis=1, keepdims=True) + jnp.sum(s, axis=1, keepdims=True)
    h_st = num / jnp.maximum(jnp.abs(den), jnp.exp(-m_t))
    h = h_st[:CHUNK] + h_st[CHUNK:]
    bl = jnp.where(last_sel, b_col, neg_inf)
    bl0 = jnp.max(bl[:CHUNK], axis=0, keepdims=True)
    bl1 = jnp.max(bl[CHUNK:], axis=0, keepdims=True)
    g = _half_bcast(bl0, bl1) - b_col + li_col
    mn0 = jnp.maximum(bl0 + m_h0, jnp.max(g[:CHUNK], axis=0, keepdims=True))
    mn1 = jnp.maximum(bl1 + m_h1, jnp.max(g[CHUNK:], axis=0, keepdims=True))
    wts = jnp.exp(g - _half_bcast(mn0, mn1))
    decay_row = jnp.where(first, jnp.exp(bl0 + m_h0 - mn0), jnp.exp(bl1 + m_h1 - mn1))
    wk = wts * k_st
    c_new = decay_row * c_mat + _dot3(wk, v_st, _TN)
    n_new = decay_row * n_row + jnp.sum(wk, axis=0, keepdims=True)
    m_new = jnp.where(first, mn0, mn1)
    return h, c_new, n_new, m_new


def _mlstm_kernel(qf, kf, vf, gf, qb, kb, vb, gb, bias_ref, hf_ref, hb_ref, c_ref, n_ref, m_ref, *, nctx):
    j = pl.program_id(1)

    @pl.when(j == 0)
    def _():
        c_ref[...] = jnp.zeros_like(c_ref)
        n_ref[...] = jnp.zeros_like(n_ref)
        m_ref[...] = jnp.zeros_like(m_ref)

    row = lax.broadcasted_iota(jnp.int32, (PAIR, PAIR), 0)
    col = lax.broadcasted_iota(jnp.int32, (PAIR, PAIR), 1)
    same = (row // CHUNK) == (col // CHUNK)
    r64 = lax.broadcasted_iota(jnp.int32, (CHUNK, CHUNK), 0)
    c64 = lax.broadcasted_iota(jnp.int32, (CHUNK, CHUNK), 1)
    rcol = lax.broadcasted_iota(jnp.int32, (PAIR, 1), 0) % CHUNK
    lane = lax.broadcasted_iota(jnp.int32, (1, PAIR), 1)
    m0 = (lane < D_HM).astype(jnp.float32)
    m1 = 1.0 - m0
    is_f = (lane % (2 * H_M)) >= H_M
    dirs = (
        (qf, kf, vf, gf, hf_ref, (c64 <= r64), same & (col <= row), rcol == CHUNK - 1),
        (qb, kb, vb, gb, hb_ref, (c64 >= r64), same & (col >= row), rcol == 0),
    )
    for d, (q_ref, k_ref, v_ref, g_ref, h_ref, tri, incl, last_sel) in enumerate(dirs):
        tri = tri.astype(jnp.float32).astype(jnp.bfloat16)
        gl = g_ref[0] + bias_ref[...]
        act = jnp.where(is_f, jax.nn.log_sigmoid(gl), gl)
        b_all = _dot_exact_lhs(tri, act)
        act_t = act.T
        b_t = b_all.T
        for p in range(NPAIR):
            sl = slice(p * PAIR, (p + 1) * PAIR)
            ci, cf = d * 2 * H_M + 2 * p, d * 2 * H_M + H_M + 2 * p
            b_col = _col_stack(b_all, cf, cf + 1)
            li_col = _col_stack(act, ci, ci + 1)
            c_row = jnp.concatenate([act_t[ci:ci + 1, :] - b_t[cf:cf + 1, :],
                                     act_t[ci + 1:ci + 2, :] - b_t[cf + 1:cf + 2, :]], axis=1)
            h, c_new, n_new, m_new = _mlstm_pair_chunk(
                q_ref[0, :, sl], k_ref[0, :, sl], v_ref[0, :, sl], b_col, li_col, c_row,
                c_ref[d, p], n_ref[d, p], m_ref[d, p], incl, last_sel, m0, m1)
            c_ref[d, p] = c_new
            n_ref[d, p] = n_new
            m_ref[d, p] = m_new

            @pl.when(j >= nctx)
            def _():
                h_ref[0, :, sl] = h


def _mlstm_scan(q, k, v, gates, bias_row, ctx_len):
    b, t, w = q.shape
    nc = t // CHUNK
    nctx = ctx_len // CHUNK
    nlat = nc - nctx

    def fwd_map(i, j):
        return (i, j, 0)

    def bwd_map(i, j):
        return (i, jnp.where(j < nctx, nctx - 1 - j, nc - 1 - (j - nctx)), 0)

    blk = (1, CHUNK, w)
    gblk = (1, CHUNK, 128)
    in_specs = ([pl.BlockSpec(blk, fwd_map)] * 3 + [pl.BlockSpec(gblk, fwd_map)]
                + [pl.BlockSpec(blk, bwd_map)] * 3 + [pl.BlockSpec(gblk, bwd_map)]
                + [pl.BlockSpec((1, 128), lambda i, j: (0, 0))])
    out_specs = [
        pl.BlockSpec(blk, lambda i, j: (i, jnp.maximum(j - nctx, 0), 0)),
        pl.BlockSpec(blk, lambda i, j: (i, nlat - 1 - jnp.maximum(j - nctx, 0), 0)),
    ]
    return pl.pallas_call(
        functools.partial(_mlstm_kernel, nctx=nctx),
        grid=(b, nc),
        in_specs=in_specs,
        out_specs=out_specs,
        out_shape=[jax.ShapeDtypeStruct((b, t - ctx_len, w), jnp.float32)] * 2,
        scratch_shapes=[pltpu.VMEM((2, NPAIR, PAIR, PAIR), jnp.float32),
                        pltpu.VMEM((2, NPAIR, 1, PAIR), jnp.float32),
                        pltpu.VMEM((2, NPAIR, 1, PAIR), jnp.float32)],
        compiler_params=pltpu.CompilerParams(
            dimension_semantics=("arbitrary", "arbitrary"), vmem_limit_bytes=VMEM_LIMIT),
        name="mlstm_scan",
    )(q, k, v, gates, q, k, v, gates, bias_row)


def _layernorm(z, gain=None, bias=None):
    zf = z.astype(jnp.float32)
    mu = jnp.mean(zf, -1, keepdims=True)
    var = jnp.mean(jnp.square(zf - mu), -1, keepdims=True)
    y = (zf - mu) * lax.rsqrt(var + LN_EPS)
    if gain is not None:
        y = y * gain + bias
    return y.astype(z.dtype)


def _headnorm(z, eps):
    zf = z.astype(jnp.float32)
    mu = jnp.mean(zf, -1, keepdims=True)
    var = jnp.mean(jnp.square(zf - mu), -1, keepdims=True)
    return (zf - mu) * lax.rsqrt(var + eps)


def _modulate(z, shift, scale):
    return _layernorm(z) * (1 + scale) + shift


def _rev_segments(u, ctx_len):
    return jnp.concatenate([jnp.flip(u[:, :ctx_len], 1), jnp.flip(u[:, ctx_len:], 1)], axis=1)


def _dir_stack(u_fwd, u_bwd, ctx_len):
    return jnp.concatenate([u_fwd, _rev_segments(u_bwd, ctx_len)], axis=0)


def _dir_merge(y, ctx_len):
    b = y.shape[0] // 2
    return y[:b] + _rev_segments(y[b:], ctx_len)


def _conv_grid(u, w, rows):
    b, l, ch = u.shape
    img = u.reshape(b, rows, l // rows, ch)
    out = lax.conv_general_dilated(img, w[:, :, None, :].astype(u.dtype), (1, 1), 'SAME',
                                   dimension_numbers=('NHWC', 'HWIO', 'NHWC'), feature_group_count=ch)
    return out.reshape(b, l, ch)


def _conv_seq(u, w_row):
    ch = u.shape[-1]
    return lax.conv_general_dilated(u, w_row[:, None, :].astype(u.dtype), (1,), 'SAME',
                                    dimension_numbers=('NWC', 'WIO', 'NWC'), feature_group_count=ch)


def _mlstm_chunkwise(q, k, v, log_i, log_f):
    z, t, h, dh = q.shape
    nc = t // CHUNK

    def chunks(a):
        a = a.astype(jnp.float32).reshape((z, nc, CHUNK, h) + a.shape[3:])
        return jnp.swapaxes(jnp.swapaxes(a, 0, 1), 2, 3)
    qc, kc, vc, lic, lfc = chunks(q), chunks(k), chunks(v), chunks(log_i), chunks(log_f)
    tri = jnp.tril(jnp.ones((CHUNK, CHUNK), bool))

    def step(carry, inp):
        c_mat, n_vec, m = carry
        qb, kb, vb, li, lf = inp
        b = jnp.cumsum(lf, -1)
        dmat = jnp.where(tri, b[..., :, None] - b[..., None, :] + li[..., None, :], -jnp.inf)
        m_inter = b + m[..., None]
        m_t = jnp.maximum(m_inter, jnp.max(dmat, -1))
        s = jnp.einsum('zhtd,zhsd->zhts', qb, kb) * jnp.exp(dmat - m_t[..., None])
        inter = jnp.exp(m_inter - m_t)
        num = inter[..., None] * jnp.einsum('zhtd,zhde->zhte', qb, c_mat) + jnp.einsum('zhts,zhse->zhte', s, vb)
        den = inter * jnp.einsum('zhtd,zhd->zht', qb, n_vec) + jnp.sum(s, -1)
        h_out = num / jnp.maximum(jnp.abs(den), jnp.exp(-m_t))[..., None]
        b_last = b[..., -1]
        g = b_last[..., None] - b + li
        m_new = jnp.maximum(b_last + m, jnp.max(g, -1))
        wts = jnp.exp(g - m_new[..., None])
        decay = jnp.exp(b_last + m - m_new)
        c_mat = decay[..., None, None] * c_mat + jnp.einsum('zhs,zhsd,zhse->zhde', wts, kb, vb)
        n_vec = decay[..., None] * n_vec + jnp.einsum('zhs,zhsd->zhd', wts, kb)
        return (c_mat, n_vec, m_new), h_out

    init = (jnp.zeros((z, h, dh, dh), jnp.float32), jnp.zeros((z, h, dh), jnp.float32), jnp.zeros((z, h), jnp.float32))
    _, hs = lax.scan(step, init, (qc, kc, vc, lic, lfc))
    return jnp.swapaxes(jnp.swapaxes(hs, 2, 3), 0, 1).reshape(z, t, h, dh)


def _mlstm_group(mq, mk, mv, mo, gates, b_i, b_f, norm_w, ctx_len):
    b, t, _ = mq.shape
    q = jax.nn.silu(mq)
    k = jax.nn.silu(mk) * (D_HM ** -0.5)
    bias_row = jnp.concatenate([b_i[0], b_f[0], b_i[1], b_f[1], jnp.zeros((128 - 4 * H_M,), jnp.float32)])[None, :]
    h_f, h_b = _mlstm_scan(q, k, mv, gates, bias_row, ctx_len)
    h = (h_f + h_b).reshape(b, t - ctx_len, H_M, D_HM)
    h = _headnorm(h, LN_EPS) * norm_w.reshape(H_M, D_HM)
    return jax.nn.sigmoid(mo[:, ctx_len:]) * h.reshape(b, t - ctx_len, W_M)


def _rwkv7_scan(r, w, k, v, kh, a):
    z, t, h, n = r.shape
    seq = tuple(jnp.moveaxis(s.astype(jnp.float32), 1, 0) for s in (r, w, k, v, kh, a))

    def step(state, inp):
        r_t, w_t, k_t, v_t, kh_t, a_t = inp
        removed = jnp.einsum('zhvk,zhk->zhv', state, kh_t)
        state = (state * w_t[:, :, None, :] - removed[..., None] * (kh_t * a_t)[:, :, None, :]
                 + v_t[..., None] * k_t[:, :, None, :])
        return state, jnp.einsum('zhvk,zhk->zhv', state, r_t)

    _, y = lax.scan(step, jnp.zeros((z, h, n, n), jnp.float32), seq)
    return jnp.moveaxis(y, 0, 1)


def _rwkv7_group(rr, rk, rv, lw_f, lw_b, la, lg, w0, w_b_mat, a0, a_b_mat, g_b_mat, kk, ka, bonus_w, gn_w, gn_b,
                 ctx_len):
    b, t, _ = rr.shape
    shp = (b, t, H_R, D_HR)
    logw_f = -DECAY_SCALE * jax.nn.sigmoid((w0[0] + jnp.tanh(lw_f) @ w_b_mat[0]).astype(jnp.float32))
    logw_b = -DECAY_SCALE * jax.nn.sigmoid((w0[1] + jnp.tanh(lw_b) @ w_b_mat[1]).astype(jnp.float32))
    a = jax.nn.sigmoid(a0 + la @ a_b_mat)
    g = jax.nn.sigmoid(lg) @ g_b_mat
    kap = (rk * kk).reshape(shp).astype(jnp.float32)
    kh = kap / jnp.maximum(jnp.sqrt(jnp.sum(jnp.square(kap), -1, keepdims=True)), 1e-12)
    kmod = (rk * (1 + (a - 1) * ka)).reshape(shp)
    r = rr.reshape(shp)
    v = rv.reshape(shp)
    a_h = a.reshape(shp)
    flat = lambda u: u.reshape(b, t, W_R)
    y_f, y_b = _rwkv_scan(rr, flat(kmod), rv, flat(kh), flat(kh * a_h), logw_f, logw_b, ctx_len)
    y = (y_f + y_b).reshape(b, t - ctx_len, H_R, D_HR)
    y = _headnorm(y, GN_EPS) * gn_w.reshape(H_R, D_HR) + gn_b.reshape(H_R, D_HR)
    bonus = (jnp.sum(r * kmod * bonus_w, -1, keepdims=True) * v)[:, ctx_len:]
    return (y + bonus).reshape(b, t - ctx_len, W_R) * g[:, ctx_len:]


def _hier_moe(h, rt_g, rt_g_b, rt_e, rt_e_b, ex_gate, ex_up, ex_down):
    b, t, d = h.shape
    tok = h.reshape(b * t, d)
    n = tok.shape[0]
    lg = (tok @ rt_g + rt_g_b).astype(jnp.float32)
    p_grp = jax.nn.softmax(lg, axis=-1)
    grp = jnp.argmax(lg, axis=-1).astype(jnp.int32)
    p_top = jnp.take_along_axis(p_grp, grp[:, None], axis=-1)
    le = (tok @ rt_e + rt_e_b).astype(jnp.float32).reshape(n, N_GROUPS, E_PER_GROUP)
    le_grp = jnp.take_along_axis(le, jnp.broadcast_to(grp[:, None, None], (n, 1, E_PER_GROUP)), axis=1)[:, 0]
    top_val, top_idx = lax.top_k(le_grp, TOP_K_IN_GROUP)
    wts = jax.nn.softmax(top_val, axis=-1) * p_top
    e_flat = (grp[:, None] * E_PER_GROUP + top_idx.astype(jnp.int32)).reshape(-1)
    w_flat = wts.reshape(-1)
    t_flat = jnp.repeat(jnp.arange(n, dtype=jnp.int32), TOP_K_IN_GROUP)
    order = jnp.argsort(e_flat)
    e_s, t_s, w_s = e_flat[order], t_flat[order], w_flat[order]
    counts = jnp.zeros((N_EXPERTS,), jnp.int32).at[e_flat].add(1)
    starts = jnp.cumsum(counts) - counts
    padded = (counts + MOE_BLOCK - 1) // MOE_BLOCK * MOE_BLOCK
    pad_ends = jnp.cumsum(padded)
    pad_starts = pad_ends - padded
    dest = pad_starts[e_s] + jnp.arange(e_s.shape[0], dtype=jnp.int32) - starts[e_s]
    n_pairs = n * TOP_K_IN_GROUP
    buf = (-(-n_pairs // MOE_BLOCK) + N_EXPERTS) * MOE_BLOCK
    tok_buf = jnp.full((buf,), n, jnp.int32).at[dest].set(t_s)
    w_buf = jnp.zeros((buf,), w_s.dtype).at[dest].set(w_s)
    n_blk = buf // MOE_BLOCK
    blk_exp = jnp.minimum(jnp.searchsorted(pad_ends, jnp.arange(n_blk, dtype=jnp.int32) * MOE_BLOCK, side='right'),
                          N_EXPERTS - 1)
    tok_pad = jnp.concatenate([tok, jnp.zeros((1, d), tok.dtype)], 0)

    def expert_block(args):
        idx, w, e = args
        xb = tok_pad[idx]
        hb = jax.nn.silu(xb @ ex_gate[e]) * (xb @ ex_up[e])
        return (hb @ ex_down[e]) * w[:, None]

    y = lax.map(expert_block, (tok_buf.reshape(n_blk, MOE_BLOCK), w_buf.reshape(n_blk, MOE_BLOCK), blk_exp))
    out = jnp.zeros((n + 1, d), y.dtype).at[tok_buf].add(y.reshape(buf, d))
    return out[:n].reshape(b, t, d).astype(h.dtype)


def kernel(x, c, ctx, c_ctx, w_ada, b_ada, w_in, conv_w, m_bias_i, m_bias_f, m_norm_w, r_w0, r_wB, r_a0, r_aB,
           r_gB, r_kk, r_ka, r_bonus, r_norm_w, r_norm_b, w_out, ln1_g, ln1_b, ln2_g, ln2_b, rt_g, rt_g_b, rt_e,
           rt_e_b, ex_gate, ex_up, ex_down):
    assert w_ada.shape[0] == DEPTH
    bsz, seq, d = x.shape
    ctx_len = ctx.shape[1]
    assert ctx_len == ROW_TILE and seq % ROW_TILE == 0
    grid_w = 64
    rows = seq // grid_w
    l = 0
    c_all = jnp.concatenate([c, c_ctx[None, :]], 0)
    mod_all = jax.nn.silu(c_all) @ w_ada[l] + b_ada[l]
    mod = mod_all[:bsz]
    sh1, sc1, g1, sh2, sc2, g2 = jnp.split(mod[:, None, :], 6, axis=-1)

    pc, pn = _inproj(ctx, x, mod_all.reshape(bsz + 1, 6, d), _pad_w_in(w_in[l]))
    p_ctx_c, p_lat_c = pc[:, :ctx_len], pc[:, ctx_len:]
    uc = jnp.concatenate([_conv_seq(p_ctx_c, conv_w[l][CONV_K // 2]), _conv_grid(p_lat_c, conv_w[l], rows)], axis=1)
    mq, mk, rr, rk, rv = jnp.split(uc, 5, axis=-1)
    mv, mo = pn[..., :512], pn[..., 512:1024]
    gates = pn[..., 1024:1152]
    lw_f, lw_b = pn[..., 1152:1216], pn[..., 1216:1280]
    la = pn[..., 1280:1344]
    lg = pn[..., 1408:1536]
    out_m = _mlstm_group(mq, mk, mv, mo, gates, m_bias_i[l], m_bias_f[l], m_norm_w[l], ctx_len)
    out_r = _rwkv7_group(rr, rk, rv, lw_f, lw_b, la, lg, r_w0[l], r_wB[l], r_a0[l], r_aB[l], r_gB[l], r_kk[l],
                         r_ka[l], r_bonus[l], r_norm_w[l], r_norm_b[l], ctx_len)
    mixed = jnp.concatenate([out_m, out_r], -1)
    x = _layernorm(ALPHA * x + g1 * (mixed @ w_out[l]), ln1_g[l], ln1_b[l])
    ffn = _hier_moe(_modulate(x, sh2, sc2), rt_g[l], rt_g_b[l], rt_e[l], rt_e_b[l], ex_gate[l], ex_up[l], ex_down[l])
    x = _layernorm(ALPHA * x + g2 * ffn, ln2_g[l], ln2_b[l])
    return x
```

```python
import functools
import math

import jax
import jax.numpy as jnp
from jax import lax
import numpy as np
from jax.experimental import pallas as pl
from jax.experimental.pallas import tpu as pltpu

H_M = 8
D_HM = 64
W_M = H_M * D_HM
H_R = 8
D_HR = 64
W_R = H_R * D_HR
MIX_W = W_M + W_R
W_LORA = 64
A_LORA = 64
G_LORA = 128
CONV_K = 3
CONV_CH = 2 * W_M + 3 * W_R
SECTION_WIDTHS = (W_M, W_M, W_R, W_R, W_R, W_M, W_M, 4 * H_M, W_LORA, W_LORA, A_LORA, G_LORA)
IN_COLS = sum(SECTION_WIDTHS)
SPLIT_POINTS = tuple(int(v) for v in np.cumsum(SECTION_WIDTHS)[:-1])
CHUNK = 64
N_GROUPS = 4
E_PER_GROUP = 8
N_EXPERTS = N_GROUPS * E_PER_GROUP
TOP_K_IN_GROUP = 2
D_EXPERT = 512
MOE_BLOCK = 256
DEPTH = 1
ALPHA = (2.0 * DEPTH) ** 0.25
DECAY_SCALE = math.exp(-0.5)
LN_EPS = 1e-6
GN_EPS = 64e-5

ROW_TILE = 256
NONCONV_W = 1536
VMEM_LIMIT = 56 * 1024 * 1024


def _inproj_kernel(ctx_ref, x_ref, mod_ref, w_ref, oc_ref, on_ref):
    j = pl.program_id(1)

    def body(src_ref):
        z = src_ref[0]
        mu = jnp.mean(z, -1, keepdims=True)
        zc = z - mu
        var = jnp.mean(zc * zc, -1, keepdims=True)
        y = zc * lax.rsqrt(var + LN_EPS)
        h = (y * (1.0 + mod_ref[0, 1:2, :]) + mod_ref[0, 0:1, :]).astype(jnp.bfloat16)
        oc_ref[0] = jnp.dot(h, w_ref[:, :CONV_CH], preferred_element_type=jnp.float32)
        on_ref[0] = jnp.dot(h, w_ref[:, CONV_CH:], preferred_element_type=jnp.float32)

    @pl.when(j == 0)
    def _():
        body(ctx_ref)

    @pl.when(j > 0)
    def _():
        body(x_ref)


def _inproj(ctx, x, mod_all, w_pad):
    b, seq, d = x.shape
    nt = (ctx.shape[1] + seq) // ROW_TILE
    t_all = ctx.shape[1] + seq
    return pl.pallas_call(
        _inproj_kernel,
        grid=(b, nt),
        in_specs=[
            pl.BlockSpec((1, ROW_TILE, d), lambda i, j: (i, 0, 0)),
            pl.BlockSpec((1, ROW_TILE, d), lambda i, j: (i, jnp.maximum(j - 1, 0), 0)),
            pl.BlockSpec((1, 6, d), lambda i, j: (jnp.where(j == 0, b, i), 0, 0)),
            pl.BlockSpec((d, CONV_CH + NONCONV_W), lambda i, j: (0, 0)),
        ],
        out_specs=[
            pl.BlockSpec((1, ROW_TILE, CONV_CH), lambda i, j: (i, j, 0)),
            pl.BlockSpec((1, ROW_TILE, NONCONV_W), lambda i, j: (i, j, 0)),
        ],
        out_shape=[
            jax.ShapeDtypeStruct((b, t_all, CONV_CH), jnp.float32),
            jax.ShapeDtypeStruct((b, t_all, NONCONV_W), jnp.float32),
        ],
        compiler_params=pltpu.CompilerParams(
            dimension_semantics=("arbitrary", "arbitrary"), vmem_limit_bytes=VMEM_LIMIT),
        name="inproj",
    )(ctx, x, mod_all, w_pad)


def _pad_w_in(w_in):
    d = w_in.shape[0]
    sp = SPLIT_POINTS
    z = lambda n: jnp.zeros((d, n), w_in.dtype)
    gates = w_in[:, sp[6]:sp[7]]
    lw = w_in[:, sp[7]:sp[9]]
    la = w_in[:, sp[9]:sp[10]]
    lg = w_in[:, sp[10]:]
    return jnp.concatenate([w_in[:, :sp[6]], gates, z(96), lw, la, z(64), lg], axis=1).astype(jnp.bfloat16)


PAIR = 2 * D_HR
NPAIR = H_R // 2
_NN = (((1,), (0,)), ((), ()))
_NT = (((1,), (1,)), ((), ()))
_TN = (((0,), (0,)), ((), ()))


def _split2(a):
    hi = a.astype(jnp.bfloat16)
    lo = (a - hi.astype(jnp.float32)).astype(jnp.bfloat16)
    return hi, lo


def _dg(a, b, dims):
    return lax.dot_general(a, b, dims, preferred_element_type=jnp.float32)


def _dot3(a, b, dims=_NN):
    ah, al = _split2(a)
    bh, bl = _split2(b)
    return _dg(ah, bh, dims) + _dg(ah, bl, dims) + _dg(al, bh, dims)


def _dot_exact_lhs(a_bf16, b, dims=_NN):
    b1 = b.astype(jnp.bfloat16)
    r1 = b - b1.astype(jnp.float32)
    b2 = r1.astype(jnp.bfloat16)
    b3 = (r1 - b2.astype(jnp.float32)).astype(jnp.bfloat16)
    return _dg(a_bf16, b1, dims) + _dg(a_bf16, b2, dims) + _dg(a_bf16, b3, dims)


def _rwkv_pair_chunk(r, k, v, kh, ab, lw, s_mat, tri_incl, strict, incl, last_row, m0, m1):
    cw = _dot_exact_lhs(tri_incl, lw)
    e_pos = jnp.exp(cw)
    e_neg = jnp.exp(-cw)
    e_prev = jnp.exp(cw - lw)
    stack = lambda a: jnp.concatenate([a * m0, a * m1], axis=0)
    kt = stack(kh * e_prev)
    bt = stack(ab * e_neg)
    kk = stack(k * e_neg)
    rt = stack(r * e_pos)
    vs = stack(v)
    zero = jnp.zeros((), jnp.float32)
    a_bk = jnp.where(strict, _dot3(kt, bt, _NT), zero)
    a_kk = jnp.where(strict, _dot3(kt, kk, _NT), zero)
    a_rb = jnp.where(incl, _dot3(rt, bt, _NT), zero)
    a_rk = jnp.where(incl, _dot3(rt, kk, _NT), zero)
    p = -a_bk
    eye = (lax.broadcasted_iota(jnp.int32, (PAIR, PAIR), 0)
           == lax.broadcasted_iota(jnp.int32, (PAIR, PAIR), 1)).astype(jnp.float32)
    t_inv = eye + p
    for _ in range(5):
        p = _dot3(p, p)
        t_inv = t_inv + _dot3(t_inv, p)
    rhs = _dot3(kt, s_mat, _NT) + _dot3(a_kk, vs)
    u = -_dot3(t_inv, rhs)
    y = _dot3(rt, s_mat, _NT) + _dot3(a_rb, u) + _dot3(a_rk, vs)
    w_last = jnp.sum(jnp.where(last_row, e_pos, zero), axis=0, keepdims=True)
    s_new = (s_mat + _dot3(u, bt, _TN) + _dot3(vs, kk, _TN)) * w_last
    return y[:CHUNK] + y[CHUNK:], s_new


def _rwkv_kernel(rf, kf, vf, khf, abf, lwf, rb, kb, vb, khb, abb, lwb, yf_ref, yb_ref, s_ref, *, nctx):
    j = pl.program_id(1)

    @pl.when(j == 0)
    def _():
        s_ref[...] = jnp.zeros_like(s_ref)

    row = lax.broadcasted_iota(jnp.int32, (PAIR, PAIR), 0)
    col = lax.broadcasted_iota(jnp.int32, (PAIR, PAIR), 1)
    same = (row // CHUNK) == (col // CHUNK)
    r64 = lax.broadcasted_iota(jnp.int32, (CHUNK, CHUNK), 0)
    c64 = lax.broadcasted_iota(jnp.int32, (CHUNK, CHUNK), 1)
    rowl = lax.broadcasted_iota(jnp.int32, (CHUNK, PAIR), 0)
    lane = lax.broadcasted_iota(jnp.int32, (1, PAIR), 1)
    m0 = (lane < D_HR).astype(jnp.float32)
    m1 = 1.0 - m0
    dirs = (
        (rf, kf, vf, khf, abf, lwf, yf_ref, (c64 <= r64), same & (col < row), same & (col <= row), rowl == CHUNK - 1),
        (rb, kb, vb, khb, abb, lwb, yb_ref, (c64 >= r64), same & (col > row), same & (col >= row), rowl == 0),
    )
    for d, (r_ref, k_ref, v_ref, kh_ref, ab_ref, lw_ref, y_ref, tri, strict, incl, last_row) in enumerate(dirs):
        tri = tri.astype(jnp.float32).astype(jnp.bfloat16)
        for p in range(NPAIR):
            sl = slice(p * PAIR, (p + 1) * PAIR)
            y, s_new = _rwkv_pair_chunk(r_ref[0, :, sl], k_ref[0, :, sl], v_ref[0, :, sl], kh_ref[0, :, sl],
                                        ab_ref[0, :, sl], lw_ref[0, :, sl], s_ref[d, p], tri, strict, incl,
                                        last_row, m0, m1)
            s_ref[d, p] = s_new

            @pl.when(j >= nctx)
            def _():
                y_ref[0, :, sl] = y


def _rwkv_scan(r, k, v, kh, ab, lw_f, lw_b, ctx_len):
    b, t, w = r.shape
    nc = t // CHUNK
    nctx = ctx_len // CHUNK
    nlat = nc - nctx

    def fwd_map(i, j):
        return (i, j, 0)

    def bwd_map(i, j):
        return (i, jnp.where(j < nctx, nctx - 1 - j, nc - 1 - (j - nctx)), 0)

    blk = (1, CHUNK, w)
    in_specs = [pl.BlockSpec(blk, fwd_map)] * 6 + [pl.BlockSpec(blk, bwd_map)] * 6
    out_specs = [
        pl.BlockSpec(blk, lambda i, j: (i, jnp.maximum(j - nctx, 0), 0)),
        pl.BlockSpec(blk, lambda i, j: (i, nlat - 1 - jnp.maximum(j - nctx, 0), 0)),
    ]
    return pl.pallas_call(
        functools.partial(_rwkv_kernel, nctx=nctx),
        grid=(b, nc),
        in_specs=in_specs,
        out_specs=out_specs,
        out_shape=[jax.ShapeDtypeStruct((b, t - ctx_len, w), jnp.float32)] * 2,
        scratch_shapes=[pltpu.VMEM((2, NPAIR, PAIR, PAIR), jnp.float32)],
        compiler_params=pltpu.CompilerParams(
            dimension_semantics=("arbitrary", "arbitrary"), vmem_limit_bytes=VMEM_LIMIT),
        name="rwkv_scan",
    )(r, k, v, kh, ab, lw_f, r, k, v, kh, ab, lw_b)


def _col_stack(a, c0, c1):
    return jnp.concatenate([a[:, c0:c0 + 1], a[:, c1:c1 + 1]], axis=0)


def _half_bcast(v0, v1):
    return jnp.concatenate([jnp.broadcast_to(v0, (CHUNK, 1)), jnp.broadcast_to(v1, (CHUNK, 1))], axis=0)


def _mlstm_pair_chunk(q, k, v, b_col, li_col, c_row, c_mat, n_row, m_row, incl, last_sel, m0, m1):
    stack = lambda a: jnp.concatenate([a * m0, a * m1], axis=0)
    q_st, k_st, v_st = stack(q), stack(k), stack(v)
    lane = lax.broadcasted_iota(jnp.int32, (1, PAIR), 1)
    first = lane < D_HM
    neg_inf = jnp.full((), -jnp.inf, jnp.float32)
    m_h0, m_h1 = m_row[:, 0:1], m_row[:, D_HM:D_HM + 1]
    m_prev = _half_bcast(m_h0, m_h1)
    dmat = jnp.where(incl, b_col + c_row, neg_inf)
    m_inter = b_col + m_prev
    m_t = jnp.maximum(m_inter, jnp.max(dmat, axis=1, keepdims=True))
    s = _dot3(q_st, k_st, _NT) * jnp.exp(dmat - m_t)
    inter = jnp.exp(m_inter - m_t)
    num = inter * _dot3(q_st, c_mat) + _dot3(s, v_st)
    den = inter * jnp.sum(q_st * n_row, axis=1, keepdims=True) + jnp.sum(s, axis=1, keepdims=True)
    h_st = num / jnp.maximum(jnp.abs(den), jnp.exp(-m_t))
    h = h_st[:CHUNK] + h_st[CHUNK:]
    bl = jnp.where(last_sel, b_col, neg_inf)
    bl0 = jnp.max(bl[:CHUNK], axis=0, keepdims=True)
    bl1 = jnp.max(bl[CHUNK:], axis=0, keepdims=True)
    g = _half_bcast(bl0, bl1) - b_col + li_col
    mn0 = jnp.maximum(bl0 + m_h0, jnp.max(g[:CHUNK], axis=0, keepdims=True))
    mn1 = jnp.maximum(bl1 + m_h1, jnp.max(g[CHUNK:], axis=0, keepdims=True))
    wts = jnp.exp(g - _half_bcast(mn0, mn1))
    decay_row = jnp.where(first, jnp.exp(bl0 + m_h0 - mn0), jnp.exp(bl1 + m_h1 - mn1))
    wk = wts * k_st
    c_new = decay_row * c_mat + _dot3(wk, v_st, _TN)
    n_new = decay_row * n_row + jnp.sum(wk, axis=0, keepdims=True)
    m_new = jnp.where(first, mn0, mn1)
    return h, c_new, n_new, m_new


def _mlstm_kernel(qf, kf, vf, gf, qb, kb, vb, gb, bias_ref, hf_ref, hb_ref, c_ref, n_ref, m_ref, *, nctx):
    j = pl.program_id(1)

    @pl.when(j == 0)
    def _():
        c_ref[...] = jnp.zeros_like(c_ref)
        n_ref[...] = jnp.zeros_like(n_ref)
        m_ref[...] = jnp.zeros_like(m_ref)

    row = lax.broadcasted_iota(jnp.int32, (PAIR, PAIR), 0)
    col = lax.broadcasted_iota(jnp.int32, (PAIR, PAIR), 1)
    same = (row // CHUNK) == (col // CHUNK)
    r64 = lax.broadcasted_iota(jnp.int32, (CHUNK, CHUNK), 0)
    c64 = lax.broadcasted_iota(jnp.int32, (CHUNK, CHUNK), 1)
    rcol = lax.broadcasted_iota(jnp.int32, (PAIR, 1), 0) % CHUNK
    lane = lax.broadcasted_iota(jnp.int32, (1, PAIR), 1)
    m0 = (lane < D_HM).astype(jnp.float32)
    m1 = 1.0 - m0
    is_f = (lane % (2 * H_M)) >= H_M
    dirs = (
        (qf, kf, vf, gf, hf_ref, (c64 <= r64), same & (col <= row), rcol == CHUNK - 1),
        (qb, kb, vb, gb, hb_ref, (c64 >= r64), same & (col >= row), rcol == 0),
    )
    for d, (q_ref, k_ref, v_ref, g_ref, h_ref, tri, incl, last_sel) in enumerate(dirs):
        tri = tri.astype(jnp.float32).astype(jnp.bfloat16)
        gl = g_ref[0] + bias_ref[...]
        act = jnp.where(is_f, jax.nn.log_sigmoid(gl), gl)
        b_all = _dot_exact_lhs(tri, act)
        act_t = act.T
        b_t = b_all.T
        for p in range(NPAIR):
            sl = slice(p * PAIR, (p + 1) * PAIR)
            ci, cf = d * 2 * H_M + 2 * p, d * 2 * H_M + H_M + 2 * p
            b_col = _col_stack(b_all, cf, cf + 1)
            li_col = _col_stack(act, ci, ci + 1)
            c_row = jnp.concatenate([act_t[ci:ci + 1, :] - b_t[cf:cf + 1, :],
                                     act_t[ci + 1:ci + 2, :] - b_t[cf + 1:cf + 2, :]], axis=1)
            h, c_new, n_new, m_new = _mlstm_pair_chunk(
                q_ref[0, :, sl], k_ref[0, :, sl], v_ref[0, :, sl], b_col, li_col, c_row,
                c_ref[d, p], n_ref[d, p], m_ref[d, p], incl, last_sel, m0, m1)
            c_ref[d, p] = c_new
            n_ref[d, p] = n_new
            m_ref[d, p] = m_new

            @pl.when(j >= nctx)
            def _():
                h_ref[0, :, sl] = h


def _mlstm_scan(q, k, v, gates, bias_row, ctx_len):
    b, t, w = q.shape
    nc = t // CHUNK
    nctx = ctx_len // CHUNK
    nlat = nc - nctx

    def fwd_map(i, j):
        return (i, j, 0)

    def bwd_map(i, j):
        return (i, jnp.where(j < nctx, nctx - 1 - j, nc - 1 - (j - nctx)), 0)

    blk = (1, CHUNK, w)
    gblk = (1, CHUNK, 128)
    in_specs = ([pl.BlockSpec(blk, fwd_map)] * 3 + [pl.BlockSpec(gblk, fwd_map)]
                + [pl.BlockSpec(blk, bwd_map)] * 3 + [pl.BlockSpec(gblk, bwd_map)]
                + [pl.BlockSpec((1, 128), lambda i, j: (0, 0))])
    out_specs = [
        pl.BlockSpec(blk, lambda i, j: (i, jnp.maximum(j - nctx, 0), 0)),
        pl.BlockSpec(blk, lambda i, j: (i, nlat - 1 - jnp.maximum(j - nctx, 0), 0)),
    ]
    return pl.pallas_call(
        functools.partial(_mlstm_kernel, nctx=nctx),
        grid=(b, nc),
        in_specs=in_specs,
        out_specs=out_specs,
        out_shape=[jax.ShapeDtypeStruct((b, t - ctx_len, w), jnp.float32)] * 2,
        scratch_shapes=[pltpu.VMEM((2, NPAIR, PAIR, PAIR), jnp.float32),
                        pltpu.VMEM((2, NPAIR, 1, PAIR), jnp.float32),
                        pltpu.VMEM((2, NPAIR, 1, PAIR), jnp.float32)],
        compiler_params=pltpu.CompilerParams(
            dimension_semantics=("arbitrary", "arbitrary"), vmem_limit_bytes=VMEM_LIMIT),
        name="mlstm_scan",
    )(q, k, v, gates, q, k, v, gates, bias_row)


ROUTE_W = 128
N_ROUTE = N_GROUPS + N_EXPERTS


def _ln_rows(z, eps):
    mu = jnp.mean(z, -1, keepdims=True)
    zc = z - mu
    var = jnp.mean(zc * zc, -1, keepdims=True)
    return zc * lax.rsqrt(var + eps)


def _headnorm_mxu(z, avg_bf16, eps):
    mu = _dot_exact_lhs_rhs(z, avg_bf16)
    zc = z - mu
    var = _dot_exact_lhs_rhs(zc * zc, avg_bf16)
    return zc * lax.rsqrt(var + eps)


def _dot_exact_lhs_rhs(a, b_bf16):
    a1 = a.astype(jnp.bfloat16)
    r1 = a - a1.astype(jnp.float32)
    a2 = r1.astype(jnp.bfloat16)
    a3 = (r1 - a2.astype(jnp.float32)).astype(jnp.bfloat16)
    return _dg(a1, b_bf16, _NN) + _dg(a2, b_bf16, _NN) + _dg(a3, b_bf16, _NN)


def _outproj_kernel(hf, hb, mo, yf, yb, gg, bonus, x_ref, mod_ref, wout_ref, hpar_ref, ln_ref, rtw_ref, rtb_ref,
                    x1_ref, tok_ref, route_ref, ridx_ref, cnt_ref, carry_ref):
    i, j = pl.program_id(0), pl.program_id(1)

    @pl.when((i == 0) & (j == 0))
    def _():
        carry_ref[...] = jnp.zeros_like(carry_ref)

    hrow = lax.broadcasted_iota(jnp.int32, (W_M, W_M), 0) // D_HM
    hcol = lax.broadcasted_iota(jnp.int32, (W_M, W_M), 1) // D_HM
    avg = jnp.where(hrow == hcol, 1.0 / D_HM, 0.0).astype(jnp.bfloat16)
    out_m = jax.nn.sigmoid(mo[0]) * (_headnorm_mxu(hf[0] + hb[0], avg, LN_EPS) * hpar_ref[0:1, :])
    y = _headnorm_mxu(yf[0] + yb[0], avg, GN_EPS) * hpar_ref[1:2, :] + hpar_ref[2:3, :]
    out_r = (y + bonus[0]) * gg[0]
    proj = (jnp.dot(out_m.astype(jnp.bfloat16), wout_ref[:W_M, :], preferred_element_type=jnp.float32)
            + jnp.dot(out_r.astype(jnp.bfloat16), wout_ref[W_M:, :], preferred_element_type=jnp.float32))
    g1, sh2, sc2 = mod_ref[0, 2:3, :], mod_ref[0, 3:4, :], mod_ref[0, 4:5, :]
    x1 = _ln_rows(ALPHA * x_ref[0] + g1 * proj, LN_EPS) * ln_ref[0:1, :] + ln_ref[1:2, :]
    x1_ref[0] = x1
    tok = _ln_rows(x1, LN_EPS) * (1.0 + sc2) + sh2
    tok_ref[0] = tok

    logits = _dot3(tok, rtw_ref[...]) + rtb_ref[...]
    lane = lax.broadcasted_iota(jnp.int32, (ROW_TILE, ROUTE_W), 1)
    neg_inf = jnp.full((), -jnp.inf, jnp.float32)
    big = jnp.int32(ROUTE_W)
    is_g = lane < N_GROUPS
    lg = jnp.where(is_g, logits, neg_inf)
    gmax = jnp.max(lg, axis=1, keepdims=True)
    grp = jnp.min(jnp.where(lg == gmax, lane, big), axis=1, keepdims=True)
    p_top = 1.0 / jnp.sum(jnp.where(is_g, jnp.exp(lg - gmax), 0.0), axis=1, keepdims=True)
    in_grp = (lane >= N_GROUPS) & (lane < N_ROUTE) & ((lane - N_GROUPS) // E_PER_GROUP == grp)
    le = jnp.where(in_grp, logits, neg_inf)
    v1 = jnp.max(le, axis=1, keepdims=True)
    i1 = jnp.min(jnp.where(le == v1, lane, big), axis=1, keepdims=True)
    le2 = jnp.where(lane == i1, neg_inf, le)
    v2 = jnp.max(le2, axis=1, keepdims=True)
    i2 = jnp.min(jnp.where(le2 == v2, lane, big), axis=1, keepdims=True)
    e21 = jnp.exp(v2 - v1)
    w1 = (1.0 / (1.0 + e21)) * p_top
    w2 = (e21 / (1.0 + e21)) * p_top

    sel1, sel2 = lane == i1, lane == i2
    onehot = jnp.where(sel1 | sel2, 1.0, 0.0)
    tr = lax.broadcasted_iota(jnp.int32, (ROW_TILE, ROW_TILE), 0)
    tc = lax.broadcasted_iota(jnp.int32, (ROW_TILE, ROW_TILE), 1)
    strict = jnp.where(tc < tr, 1.0, 0.0).astype(jnp.bfloat16)
    before = _dg(strict, onehot.astype(jnp.bfloat16), _NN) + carry_ref[...]
    r1 = jnp.sum(jnp.where(sel1, before, 0.0), axis=1, keepdims=True)
    r2 = jnp.sum(jnp.where(sel2, before, 0.0), axis=1, keepdims=True)
    carry_ref[...] = carry_ref[...] + jnp.sum(onehot, axis=0, keepdims=True)
    e1 = (i1 - N_GROUPS).astype(jnp.float32)
    e2 = (i2 - N_GROUPS).astype(jnp.float32)
    route = jnp.where(lane == 0, e1, jnp.where(lane == 1, e2, jnp.where(lane == 2, r1, jnp.where(
        lane == 3, r2, jnp.where(lane == 4, w1, jnp.where(lane == 5, w2, 0.0))))))
    route_ref[0] = route
    ridx_ref[0] = route.T[0:8, :].astype(jnp.int32)
    cnt_ref[...] = jnp.broadcast_to(carry_ref[...], cnt_ref.shape)


def _outproj(h_f, h_b, pn, y_f, y_b, gg, bonus, x, mod3, wout_bf16, hpar, lnpar, rtw, rtb, ctx_len):
    b, seq, d = x.shape
    nt = seq // ROW_TILE
    off = ctx_len // ROW_TILE
    lat = lambda w: pl.BlockSpec((1, ROW_TILE, w), lambda i, j: (i, j, 0))
    full = lambda shp: pl.BlockSpec(shp, lambda i, j: tuple(0 for _ in shp))
    in_specs = [
        lat(W_M), lat(W_M),
        pl.BlockSpec((1, ROW_TILE, W_M), lambda i, j: (i, j + off, 1)),
        lat(W_R), lat(W_R),
        pl.BlockSpec((1, ROW_TILE, W_R), lambda i, j: (i, j + off, 0)),
        pl.BlockSpec((1, ROW_TILE, W_R), lambda i, j: (i, j + off, 0)),
        lat(d),
        pl.BlockSpec((1, 6, d), lambda i, j: (i, 0, 0)),
        full((MIX_W, d)), full((8, W_M)), full((8, d)), full((d, ROUTE_W)), full((1, ROUTE_W)),
    ]
    out_specs = [
        lat(d), lat(d), lat(ROUTE_W),
        pl.BlockSpec((1, 8, ROW_TILE), lambda i, j: (i * nt + j, 0, 0)),
        full((8, ROUTE_W)),
    ]
    out_shape = [
        jax.ShapeDtypeStruct((b, seq, d), jnp.float32),
        jax.ShapeDtypeStruct((b, seq, d), jnp.float32),
        jax.ShapeDtypeStruct((b, seq, ROUTE_W), jnp.float32),
        jax.ShapeDtypeStruct((b * nt, 8, ROW_TILE), jnp.int32),
        jax.ShapeDtypeStruct((8, ROUTE_W), jnp.float32),
    ]
    return pl.pallas_call(
        _outproj_kernel,
        grid=(b, nt),
        in_specs=in_specs,
        out_specs=out_specs,
        out_shape=out_shape,
        scratch_shapes=[pltpu.VMEM((1, ROUTE_W), jnp.float32)],
        compiler_params=pltpu.CompilerParams(
            dimension_semantics=("arbitrary", "arbitrary"), vmem_limit_bytes=VMEM_LIMIT),
        name="outproj_router",
    )(h_f, h_b, pn, y_f, y_b, gg, bonus, x, mod3, wout_bf16, hpar, lnpar, rtw, rtb)


def _pair_dest(ridx_ref, starts_ref, r, slot):
    return starts_ref[ridx_ref[0, slot, r]] + ridx_ref[0, 2 + slot, r]


def _scatter_kernel(starts_ref, ridx_ref, tok_ref, xs_in_ref, xs_ref, sem):
    del xs_in_ref

    def row_copy(r, slot):
        dst = _pair_dest(ridx_ref, starts_ref, r, slot)
        return pltpu.make_async_copy(tok_ref.at[pl.ds(r, 1), :], xs_ref.at[pl.ds(dst, 1), :], sem)

    def issue(r, c):
        row_copy(r, 0).start()
        row_copy(r, 1).start()
        return c

    def drain(r, c):
        pltpu.make_async_copy(tok_ref.at[pl.ds(0, 1), :], xs_ref.at[pl.ds(0, 1), :], sem).wait()
        return c

    lax.fori_loop(0, ROW_TILE, issue, 0)
    lax.fori_loop(0, 2 * ROW_TILE, drain, 0)


def _moe_scatter(starts, ridx, tok2d, xs_zero):
    n, d = tok2d.shape
    nt = n // ROW_TILE
    return pl.pallas_call(
        _scatter_kernel,
        grid_spec=pltpu.PrefetchScalarGridSpec(
            num_scalar_prefetch=1,
            grid=(nt,),
            in_specs=[
                pl.BlockSpec((1, 8, ROW_TILE), lambda i, s: (i, 0, 0), memory_space=pltpu.SMEM),
                pl.BlockSpec((ROW_TILE, d), lambda i, s: (i, 0)),
                pl.BlockSpec(memory_space=pl.ANY),
            ],
            out_specs=pl.BlockSpec(memory_space=pl.ANY),
            scratch_shapes=[pltpu.SemaphoreType.DMA(())],
        ),
        out_shape=jax.ShapeDtypeStruct(xs_zero.shape, xs_zero.dtype),
        input_output_aliases={3: 0},
        compiler_params=pltpu.CompilerParams(dimension_semantics=("arbitrary",), vmem_limit_bytes=VMEM_LIMIT),
        name="moe_scatter",
    )(starts, ridx, tok2d, xs_zero)


def _expert_kernel(blk_exp_ref, nused_ref, xs_ref, wg_ref, wu_ref, wd_ref, ys_ref):
    i = pl.program_id(0)

    @pl.when(i < nused_ref[0])
    def _():
        xb = xs_ref[...].astype(jnp.bfloat16)
        hg = jnp.dot(xb, wg_ref[0], preferred_element_type=jnp.float32)
        hu = jnp.dot(xb, wu_ref[0], preferred_element_type=jnp.float32)
        hb = (jax.nn.silu(hg) * hu).astype(jnp.bfloat16)
        ys_ref[...] = jnp.dot(hb, wd_ref[0], preferred_element_type=jnp.float32)

    @pl.when(i >= nused_ref[0])
    def _():
        ys_ref[...] = jnp.zeros_like(ys_ref)


def _moe_experts(blk_exp, nused, xs, wg, wu, wd):
    nrow, d = xs.shape
    de = wg.shape[2]
    return pl.pallas_call(
        _expert_kernel,
        grid_spec=pltpu.PrefetchScalarGridSpec(
            num_scalar_prefetch=2,
            grid=(nrow // MOE_BLOCK,),
            in_specs=[
                pl.BlockSpec((MOE_BLOCK, d), lambda i, be, nu: (i, 0)),
                pl.BlockSpec((1, d, de), lambda i, be, nu: (be[i], 0, 0)),
                pl.BlockSpec((1, d, de), lambda i, be, nu: (be[i], 0, 0)),
                pl.BlockSpec((1, de, d), lambda i, be, nu: (be[i], 0, 0)),
            ],
            out_specs=pl.BlockSpec((MOE_BLOCK, d), lambda i, be, nu: (i, 0)),
        ),
        out_shape=jax.ShapeDtypeStruct((nrow, d), jnp.float32),
        compiler_params=pltpu.CompilerParams(dimension_semantics=("arbitrary",), vmem_limit_bytes=VMEM_LIMIT),
        name="moe_experts",
    )(blk_exp, nused, xs, wg, wu, wd)


def _combine_kernel(starts_ref, ridx_ref, route_ref, x1_ref, mod_ref, ln_ref, ys_ref, out_ref, ybuf, sem):
    def row_copy(r, slot):
        src = _pair_dest(ridx_ref, starts_ref, r, slot)
        return pltpu.make_async_copy(ys_ref.at[pl.ds(src, 1), :], ybuf.at[slot, pl.ds(r, 1), :], sem)

    def issue(r, c):
        row_copy(r, 0).start()
        row_copy(r, 1).start()
        return c

    def drain(r, c):
        pltpu.make_async_copy(ys_ref.at[pl.ds(0, 1), :], ybuf.at[0, pl.ds(0, 1), :], sem).wait()
        return c

    lax.fori_loop(0, ROW_TILE, issue, 0)
    lax.fori_loop(0, 2 * ROW_TILE, drain, 0)
    route = route_ref[0]
    ffn = ybuf[0] * route[:, 4:5] + ybuf[1] * route[:, 5:6]
    g2 = mod_ref[0, 5:6, :]
    out_ref[0] = _ln_rows(ALPHA * x1_ref[0] + g2 * ffn, LN_EPS) * ln_ref[2:3, :] + ln_ref[3:4, :]


def _moe_combine(starts, ridx, route, x1, mod3, lnpar, ys):
    b, seq, d = x1.shape
    nt = seq // ROW_TILE
    return pl.pallas_call(
        _combine_kernel,
        grid_spec=pltpu.PrefetchScalarGridSpec(
            num_scalar_prefetch=1,
            grid=(b, nt),
            in_specs=[
                pl.BlockSpec((1, 8, ROW_TILE), lambda i, j, s: (i * nt + j, 0, 0), memory_space=pltpu.SMEM),
                pl.BlockSpec((1, ROW_TILE, ROUTE_W), lambda i, j, s: (i, j, 0)),
                pl.BlockSpec((1, ROW_TILE, d), lambda i, j, s: (i, j, 0)),
                pl.BlockSpec((1, 6, d), lambda i, j, s: (i, 0, 0)),
                pl.BlockSpec((8, d), lambda i, j, s: (0, 0)),
                pl.BlockSpec(memory_space=pl.ANY),
            ],
            out_specs=pl.BlockSpec((1, ROW_TILE, d), lambda i, j, s: (i, j, 0)),
            scratch_shapes=[pltpu.VMEM((2, ROW_TILE, d), jnp.float32), pltpu.SemaphoreType.DMA(())],
        ),
        out_shape=jax.ShapeDtypeStruct((b, seq, d), jnp.float32),
        compiler_params=pltpu.CompilerParams(dimension_semantics=("arbitrary", "arbitrary"),
                                             vmem_limit_bytes=VMEM_LIMIT),
        name="moe_combine",
    )(starts, ridx, route, x1, mod3, lnpar, ys)


def _moe_plan(cnt, n_pairs):
    counts = cnt[0, N_GROUPS:N_ROUTE].astype(jnp.int32)
    padded = (counts + MOE_BLOCK - 1) // MOE_BLOCK * MOE_BLOCK
    pad_ends = jnp.cumsum(padded)
    pad_starts = pad_ends - padded
    n_blk = -(-n_pairs // MOE_BLOCK) + N_EXPERTS
    blk_exp = jnp.minimum(jnp.searchsorted(pad_ends, jnp.arange(n_blk, dtype=jnp.int32) * MOE_BLOCK, side='right'),
                          N_EXPERTS - 1).astype(jnp.int32)
    nused = (pad_ends[-1:] // MOE_BLOCK).astype(jnp.int32)
    return pad_starts.astype(jnp.int32), blk_exp, nused, n_blk


def _layernorm(z, gain=None, bias=None):
    zf = z.astype(jnp.float32)
    mu = jnp.mean(zf, -1, keepdims=True)
    var = jnp.mean(jnp.square(zf - mu), -1, keepdims=True)
    y = (zf - mu) * lax.rsqrt(var + LN_EPS)
    if gain is not None:
        y = y * gain + bias
    return y.astype(z.dtype)


def _headnorm(z, eps):
    zf = z.astype(jnp.float32)
    mu = jnp.mean(zf, -1, keepdims=True)
    var = jnp.mean(jnp.square(zf - mu), -1, keepdims=True)
    return (zf - mu) * lax.rsqrt(var + eps)


def _modulate(z, shift, scale):
    return _layernorm(z) * (1 + scale) + shift


def _rev_segments(u, ctx_len):
    return jnp.concatenate([jnp.flip(u[:, :ctx_len], 1), jnp.flip(u[:, ctx_len:], 1)], axis=1)


def _dir_stack(u_fwd, u_bwd, ctx_len):
    return jnp.concatenate([u_fwd, _rev_segments(u_bwd, ctx_len)], axis=0)


def _dir_merge(y, ctx_len):
    b = y.shape[0] // 2
    return y[:b] + _rev_segments(y[b:], ctx_len)


def _conv_grid(u, w, rows):
    b, l, ch = u.shape
    img = u.reshape(b, rows, l // rows, ch)
    out = lax.conv_general_dilated(img, w[:, :, None, :].astype(u.dtype), (1, 1), 'SAME',
                                   dimension_numbers=('NHWC', 'HWIO', 'NHWC'), feature_group_count=ch)
    return out.reshape(b, l, ch)


def _conv_seq(u, w_row):
    ch = u.shape[-1]
    return lax.conv_general_dilated(u, w_row[:, None, :].astype(u.dtype), (1,), 'SAME',
                                    dimension_numbers=('NWC', 'WIO', 'NWC'), feature_group_count=ch)


def _mlstm_chunkwise(q, k, v, log_i, log_f):
    z, t, h, dh = q.shape
    nc = t // CHUNK

    def chunks(a):
        a = a.astype(jnp.float32).reshape((z, nc, CHUNK, h) + a.shape[3:])
        return jnp.swapaxes(jnp.swapaxes(a, 0, 1), 2, 3)
    qc, kc, vc, lic, lfc = chunks(q), chunks(k), chunks(v), chunks(log_i), chunks(log_f)
    tri = jnp.tril(jnp.ones((CHUNK, CHUNK), bool))

    def step(carry, inp):
        c_mat, n_vec, m = carry
        qb, kb, vb, li, lf = inp
        b = jnp.cumsum(lf, -1)
        dmat = jnp.where(tri, b[..., :, None] - b[..., None, :] + li[..., None, :], -jnp.inf)
        m_inter = b + m[..., None]
        m_t = jnp.maximum(m_inter, jnp.max(dmat, -1))
        s = jnp.einsum('zhtd,zhsd->zhts', qb, kb) * jnp.exp(dmat - m_t[..., None])
        inter = jnp.exp(m_inter - m_t)
        num = inter[..., None] * jnp.einsum('zhtd,zhde->zhte', qb, c_mat) + jnp.einsum('zhts,zhse->zhte', s, vb)
        den = inter * jnp.einsum('zhtd,zhd->zht', qb, n_vec) + jnp.sum(s, -1)
        h_out = num / jnp.maximum(jnp.abs(den), jnp.exp(-m_t))[..., None]
        b_last = b[..., -1]
        g = b_last[..., None] - b + li
        m_new = jnp.maximum(b_last + m, jnp.max(g, -1))
        wts = jnp.exp(g - m_new[..., None])
        decay = jnp.exp(b_last + m - m_new)
        c_mat = decay[..., None, None] * c_mat + jnp.einsum('zhs,zhsd,zhse->zhde', wts, kb, vb)
        n_vec = decay[..., None] * n_vec + jnp.einsum('zhs,zhsd->zhd', wts, kb)
        return (c_mat, n_vec, m_new), h_out

    init = (jnp.zeros((z, h, dh, dh), jnp.float32), jnp.zeros((z, h, dh), jnp.float32), jnp.zeros((z, h), jnp.float32))
    _, hs = lax.scan(step, init, (qc, kc, vc, lic, lfc))
    return jnp.swapaxes(jnp.swapaxes(hs, 2, 3), 0, 1).reshape(z, t, h, dh)


def _mlstm_group(mq, mk, mv, mo, gates, b_i, b_f, norm_w, ctx_len):
    b, t, _ = mq.shape
    q = jax.nn.silu(mq)
    k = jax.nn.silu(mk) * (D_HM ** -0.5)
    bias_row = jnp.concatenate([b_i[0], b_f[0], b_i[1], b_f[1], jnp.zeros((128 - 4 * H_M,), jnp.float32)])[None, :]
    h_f, h_b = _mlstm_scan(q, k, mv, gates, bias_row, ctx_len)
    h = (h_f + h_b).reshape(b, t - ctx_len, H_M, D_HM)
    h = _headnorm(h, LN_EPS) * norm_w.reshape(H_M, D_HM)
    return jax.nn.sigmoid(mo[:, ctx_len:]) * h.reshape(b, t - ctx_len, W_M)


def _rwkv7_scan(r, w, k, v, kh, a):
    z, t, h, n = r.shape
    seq = tuple(jnp.moveaxis(s.astype(jnp.float32), 1, 0) for s in (r, w, k, v, kh, a))

    def step(state, inp):
        r_t, w_t, k_t, v_t, kh_t, a_t = inp
        removed = jnp.einsum('zhvk,zhk->zhv', state, kh_t)
        state = (state * w_t[:, :, None, :] - removed[..., None] * (kh_t * a_t)[:, :, None, :]
                 + v_t[..., None] * k_t[:, :, None, :])
        return state, jnp.einsum('zhvk,zhk->zhv', state, r_t)

    _, y = lax.scan(step, jnp.zeros((z, h, n, n), jnp.float32), seq)
    return jnp.moveaxis(y, 0, 1)


def _rwkv7_group(rr, rk, rv, lw_f, lw_b, la, lg, w0, w_b_mat, a0, a_b_mat, g_b_mat, kk, ka, bonus_w, gn_w, gn_b,
                 ctx_len):
    b, t, _ = rr.shape
    shp = (b, t, H_R, D_HR)
    logw_f = -DECAY_SCALE * jax.nn.sigmoid((w0[0] + jnp.tanh(lw_f) @ w_b_mat[0]).astype(jnp.float32))
    logw_b = -DECAY_SCALE * jax.nn.sigmoid((w0[1] + jnp.tanh(lw_b) @ w_b_mat[1]).astype(jnp.float32))
    a = jax.nn.sigmoid(a0 + la @ a_b_mat)
    g = jax.nn.sigmoid(lg) @ g_b_mat
    kap = (rk * kk).reshape(shp).astype(jnp.float32)
    kh = kap / jnp.maximum(jnp.sqrt(jnp.sum(jnp.square(kap), -1, keepdims=True)), 1e-12)
    kmod = (rk * (1 + (a - 1) * ka)).reshape(shp)
    r = rr.reshape(shp)
    v = rv.reshape(shp)
    a_h = a.reshape(shp)
    flat = lambda u: u.reshape(b, t, W_R)
    y_f, y_b = _rwkv_scan(rr, flat(kmod), rv, flat(kh), flat(kh * a_h), logw_f, logw_b, ctx_len)
    y = (y_f + y_b).reshape(b, t - ctx_len, H_R, D_HR)
    y = _headnorm(y, GN_EPS) * gn_w.reshape(H_R, D_HR) + gn_b.reshape(H_R, D_HR)
    bonus = (jnp.sum(r * kmod * bonus_w, -1, keepdims=True) * v)[:, ctx_len:]
    return (y + bonus).reshape(b, t - ctx_len, W_R) * g[:, ctx_len:]


def _hier_moe(h, rt_g, rt_g_b, rt_e, rt_e_b, ex_gate, ex_up, ex_down):
    b, t, d = h.shape
    tok = h.reshape(b * t, d)
    n = tok.shape[0]
    lg = (tok @ rt_g + rt_g_b).astype(jnp.float32)
    p_grp = jax.nn.softmax(lg, axis=-1)
    grp = jnp.argmax(lg, axis=-1).astype(jnp.int32)
    p_top = jnp.take_along_axis(p_grp, grp[:, None], axis=-1)
    le = (tok @ rt_e + rt_e_b).astype(jnp.float32).reshape(n, N_GROUPS, E_PER_GROUP)
    le_grp = jnp.take_along_axis(le, jnp.broadcast_to(grp[:, None, None], (n, 1, E_PER_GROUP)), axis=1)[:, 0]
    top_val, top_idx = lax.top_k(le_grp, TOP_K_IN_GROUP)
    wts = jax.nn.softmax(top_val, axis=-1) * p_top
    e_flat = (grp[:, None] * E_PER_GROUP + top_idx.astype(jnp.int32)).reshape(-1)
    w_flat = wts.reshape(-1)
    t_flat = jnp.repeat(jnp.arange(n, dtype=jnp.int32), TOP_K_IN_GROUP)
    order = jnp.argsort(e_flat)
    e_s, t_s, w_s = e_flat[order], t_flat[order], w_flat[order]
    counts = jnp.zeros((N_EXPERTS,), jnp.int32).at[e_flat].add(1)
    starts = jnp.cumsum(counts) - counts
    padded = (counts + MOE_BLOCK - 1) // MOE_BLOCK * MOE_BLOCK
    pad_ends = jnp.cumsum(padded)
    pad_starts = pad_ends - padded
    dest = pad_starts[e_s] + jnp.arange(e_s.shape[0], dtype=jnp.int32) - starts[e_s]
    n_pairs = n * TOP_K_IN_GROUP
    buf = (-(-n_pairs // MOE_BLOCK) + N_EXPERTS) * MOE_BLOCK
    tok_buf = jnp.full((buf,), n, jnp.int32).at[dest].set(t_s)
    w_buf = jnp.zeros((buf,), w_s.dtype).at[dest].set(w_s)
    n_blk = buf // MOE_BLOCK
    blk_exp = jnp.minimum(jnp.searchsorted(pad_ends, jnp.arange(n_blk, dtype=jnp.int32) * MOE_BLOCK, side='right'),
                          N_EXPERTS - 1)
    tok_pad = jnp.concatenate([tok, jnp.zeros((1, d), tok.dtype)], 0)

    def expert_block(args):
        idx, w, e = args
        xb = tok_pad[idx]
        hb = jax.nn.silu(xb @ ex_gate[e]) * (xb @ ex_up[e])
        return (hb @ ex_down[e]) * w[:, None]

    y = lax.map(expert_block, (tok_buf.reshape(n_blk, MOE_BLOCK), w_buf.reshape(n_blk, MOE_BLOCK), blk_exp))
    out = jnp.zeros((n + 1, d), y.dtype).at[tok_buf].add(y.reshape(buf, d))
    return out[:n].reshape(b, t, d).astype(h.dtype)


def kernel(x, c, ctx, c_ctx, w_ada, b_ada, w_in, conv_w, m_bias_i, m_bias_f, m_norm_w, r_w0, r_wB, r_a0, r_aB,
           r_gB, r_kk, r_ka, r_bonus, r_norm_w, r_norm_b, w_out, ln1_g, ln1_b, ln2_g, ln2_b, rt_g, rt_g_b, rt_e,
           rt_e_b, ex_gate, ex_up, ex_down):
    assert w_ada.shape[0] == DEPTH
    bsz, seq, d = x.shape
    ctx_len = ctx.shape[1]
    assert ctx_len == ROW_TILE and seq % ROW_TILE == 0
    grid_w = 64
    rows = seq // grid_w
    l = 0
    c_all = jnp.concatenate([c, c_ctx[None, :]], 0)
    mod_all = jax.nn.silu(c_all) @ w_ada[l] + b_ada[l]
    mod = mod_all[:bsz]
    sh1, sc1, g1, sh2, sc2, g2 = jnp.split(mod[:, None, :], 6, axis=-1)

    pc, pn = _inproj(ctx, x, mod_all.reshape(bsz + 1, 6, d), _pad_w_in(w_in[l]))
    p_ctx_c, p_lat_c = pc[:, :ctx_len], pc[:, ctx_len:]
    uc = jnp.concatenate([_conv_seq(p_ctx_c, conv_w[l][CONV_K // 2]), _conv_grid(p_lat_c, conv_w[l], rows)], axis=1)
    mq, mk, rr, rk, rv = jnp.split(uc, 5, axis=-1)
    mv, mo = pn[..., :512], pn[..., 512:1024]
    gates = pn[..., 1024:1152]
    lw_f, lw_b = pn[..., 1152:1216], pn[..., 1216:1280]
    la = pn[..., 1280:1344]
    lg = pn[..., 1408:1536]
    t_all = ctx_len + seq
    q = jax.nn.silu(mq)
    k = jax.nn.silu(mk) * (D_HM ** -0.5)
    bias_row = jnp.concatenate([m_bias_i[l][0], m_bias_f[l][0], m_bias_i[l][1], m_bias_f[l][1],
                                jnp.zeros((128 - 4 * H_M,), jnp.float32)])[None, :]
    h_f, h_b = _mlstm_scan(q, k, mv, gates, bias_row, ctx_len)
    shp = (bsz, t_all, H_R, D_HR)
    logw_f = -DECAY_SCALE * jax.nn.sigmoid(r_w0[l][0] + jnp.tanh(lw_f) @ r_wB[l][0])
    logw_b = -DECAY_SCALE * jax.nn.sigmoid(r_w0[l][1] + jnp.tanh(lw_b) @ r_wB[l][1])
    a = jax.nn.sigmoid(r_a0[l] + la @ r_aB[l])
    gg = jax.nn.sigmoid(lg) @ r_gB[l]
    kap = (rk * r_kk[l]).reshape(shp)
    kh = (kap / jnp.maximum(jnp.sqrt(jnp.sum(jnp.square(kap), -1, keepdims=True)), 1e-12)).reshape(bsz, t_all, W_R)
    kmod = rk * (1 + (a - 1) * r_ka[l])
    bonus = (jnp.sum((rr * kmod).reshape(shp) * r_bonus[l], -1, keepdims=True) * rv.reshape(shp)).reshape(
        bsz, t_all, W_R)
    y_f, y_b = _rwkv_scan(rr, kmod, rv, kh, kh * a, logw_f, logw_b, ctx_len)
    mod3 = mod_all.reshape(bsz + 1, 6, d)
    hpar = jnp.zeros((8, W_M), jnp.float32).at[0].set(m_norm_w[l]).at[1].set(r_norm_w[l]).at[2].set(r_norm_b[l])
    lnpar = jnp.zeros((8, d), jnp.float32).at[0].set(ln1_g[l]).at[1].set(ln1_b[l]).at[2].set(ln2_g[l]).at[3].set(
        ln2_b[l])
    rtw = jnp.concatenate([rt_g[l], rt_e[l], jnp.zeros((d, ROUTE_W - N_ROUTE), jnp.float32)], axis=1)
    rtb = jnp.concatenate([rt_g_b[l], rt_e_b[l], jnp.zeros((ROUTE_W - N_ROUTE,), jnp.float32)])[None, :]
    x1, tok, route, ridx, cnt = _outproj(h_f, h_b, pn, y_f, y_b, gg, bonus, x, mod3, w_out[l].astype(jnp.bfloat16),
                                         hpar, lnpar, rtw, rtb, ctx_len)
    n_tok = bsz * seq
    starts, blk_exp, nused, n_blk = _moe_plan(cnt, n_tok * TOP_K_IN_GROUP)
    xs = _moe_scatter(starts, ridx, tok.reshape(n_tok, d), jnp.zeros((n_blk * MOE_BLOCK, d), jnp.float32))
    ys = _moe_experts(blk_exp, nused, xs, ex_gate[l].astype(jnp.bfloat16), ex_up[l].astype(jnp.bfloat16),
                      ex_down[l].astype(jnp.bfloat16))
    return _moe_combine(starts, ridx, route, x1, mod3, lnpar, ys)
```

```python
import functools
import math

import jax
import jax.numpy as jnp
from jax import lax
import numpy as np
from jax.experimental import pallas as pl
from jax.experimental.pallas import tpu as pltpu

H_M = 8
D_HM = 64
W_M = H_M * D_HM
H_R = 8
D_HR = 64
W_R = H_R * D_HR
MIX_W = W_M + W_R
W_LORA = 64
A_LORA = 64
G_LORA = 128
CONV_K = 3
CONV_CH = 2 * W_M + 3 * W_R
SECTION_WIDTHS = (W_M, W_M, W_R, W_R, W_R, W_M, W_M, 4 * H_M, W_LORA, W_LORA, A_LORA, G_LORA)
IN_COLS = sum(SECTION_WIDTHS)
SPLIT_POINTS = tuple(int(v) for v in np.cumsum(SECTION_WIDTHS)[:-1])
CHUNK = 64
N_GROUPS = 4
E_PER_GROUP = 8
N_EXPERTS = N_GROUPS * E_PER_GROUP
TOP_K_IN_GROUP = 2
D_EXPERT = 512
MOE_BLOCK = 256
DEPTH = 1
ALPHA = (2.0 * DEPTH) ** 0.25
DECAY_SCALE = math.exp(-0.5)
LN_EPS = 1e-6
GN_EPS = 64e-5

ROW_TILE = 256
NONCONV_W = 1536
VMEM_LIMIT = 56 * 1024 * 1024


def _inproj_kernel(ctx_ref, x_ref, mod_ref, w_ref, oc_ref, on_ref):
    j = pl.program_id(1)

    def body(src_ref):
        z = src_ref[0]
        mu = jnp.mean(z, -1, keepdims=True)
        zc = z - mu
        var = jnp.mean(zc * zc, -1, keepdims=True)
        y = zc * lax.rsqrt(var + LN_EPS)
        h = (y * (1.0 + mod_ref[0, 1:2, :]) + mod_ref[0, 0:1, :]).astype(jnp.bfloat16)
        oc_ref[0] = jnp.dot(h, w_ref[:, :CONV_CH], preferred_element_type=jnp.float32)
        on_ref[0] = jnp.dot(h, w_ref[:, CONV_CH:], preferred_element_type=jnp.float32)

    @pl.when(j == 0)
    def _():
        body(ctx_ref)

    @pl.when(j > 0)
    def _():
        body(x_ref)


def _inproj(ctx, x, mod_all, w_pad):
    b, seq, d = x.shape
    nt = (ctx.shape[1] + seq) // ROW_TILE
    t_all = ctx.shape[1] + seq
    return pl.pallas_call(
        _inproj_kernel,
        grid=(b, nt),
        in_specs=[
            pl.BlockSpec((1, ROW_TILE, d), lambda i, j: (i, 0, 0)),
            pl.BlockSpec((1, ROW_TILE, d), lambda i, j: (i, jnp.maximum(j - 1, 0), 0)),
            pl.BlockSpec((1, 6, d), lambda i, j: (jnp.where(j == 0, b, i), 0, 0)),
            pl.BlockSpec((d, CONV_CH + NONCONV_W), lambda i, j: (0, 0)),
        ],
        out_specs=[
            pl.BlockSpec((1, ROW_TILE, CONV_CH), lambda i, j: (i, j, 0)),
            pl.BlockSpec((1, ROW_TILE, NONCONV_W), lambda i, j: (i, j, 0)),
        ],
        out_shape=[
            jax.ShapeDtypeStruct((b, t_all, CONV_CH), jnp.float32),
            jax.ShapeDtypeStruct((b, t_all, NONCONV_W), jnp.float32),
        ],
        compiler_params=pltpu.CompilerParams(
            dimension_semantics=("arbitrary", "arbitrary"), vmem_limit_bytes=VMEM_LIMIT),
        name="inproj",
    )(ctx, x, mod_all, w_pad)


def _pad_w_in(w_in):
    d = w_in.shape[0]
    sp = SPLIT_POINTS
    z = lambda n: jnp.zeros((d, n), w_in.dtype)
    gates = w_in[:, sp[6]:sp[7]]
    lw = w_in[:, sp[7]:sp[9]]
    la = w_in[:, sp[9]:sp[10]]
    lg = w_in[:, sp[10]:]
    return jnp.concatenate([w_in[:, :sp[6]], gates, z(96), lw, la, z(64), lg], axis=1).astype(jnp.bfloat16)


PAIR = 2 * D_HR
NPAIR = H_R // 2
_NN = (((1,), (0,)), ((), ()))
_NT = (((1,), (1,)), ((), ()))
_TN = (((0,), (0,)), ((), ()))


def _split2(a):
    hi = a.astype(jnp.bfloat16)
    lo = (a - hi.astype(jnp.float32)).astype(jnp.bfloat16)
    return hi, lo


def _split3(a):
    a1 = a.astype(jnp.bfloat16)
    r1 = a - a1.astype(jnp.float32)
    a2 = r1.astype(jnp.bfloat16)
    a3 = (r1 - a2.astype(jnp.float32)).astype(jnp.bfloat16)
    return a1, a2, a3


def _dg(a, b, dims):
    return lax.dot_general(a, b, dims, preferred_element_type=jnp.float32)


def _dot3(a, b, dims=_NN):
    ah, al = _split2(a)
    bh, bl = _split2(b)
    return _dg(ah, bh, dims) + _dg(ah, bl, dims) + _dg(al, bh, dims)


def _dot_exact_lhs(a_bf16, b, dims=_NN):
    b1, b2, b3 = _split3(b)
    return _dg(a_bf16, b1, dims) + _dg(a_bf16, b2, dims) + _dg(a_bf16, b3, dims)


def _dot_exact_rhs(a, b_bf16):
    a1, a2, a3 = _split3(a)
    return _dg(a1, b_bf16, _NN) + _dg(a2, b_bf16, _NN) + _dg(a3, b_bf16, _NN)


def _head_block_diag(width, head, value):
    hrow = lax.broadcasted_iota(jnp.int32, (width, width), 0) // head
    hcol = lax.broadcasted_iota(jnp.int32, (width, width), 1) // head
    return jnp.where(hrow == hcol, value, 0.0).astype(jnp.bfloat16)


def _ada_kernel(c_ref, w_ref, b_ref, o_ref):
    o_ref[...] = _dot3(jax.nn.silu(c_ref[...]), w_ref[...]) + b_ref[...]


def _ada(c_pad, w, bias):
    m, d = c_pad.shape
    n = w.shape[1]
    return pl.pallas_call(
        _ada_kernel,
        grid=(n // d,),
        in_specs=[pl.BlockSpec((m, d), lambda j: (0, 0)),
                  pl.BlockSpec((d, d), lambda j: (0, j)),
                  pl.BlockSpec((1, d), lambda j: (0, j))],
        out_specs=pl.BlockSpec((m, d), lambda j: (0, j)),
        out_shape=jax.ShapeDtypeStruct((m, n), jnp.float32),
        compiler_params=pltpu.CompilerParams(dimension_semantics=("arbitrary",), vmem_limit_bytes=VMEM_LIMIT),
        name="adaln",
    )(c_pad, w, bias)


GRID_W = 64
HALO = GRID_W
XOFF = 8


def _prep_kernel(top_ref, main_ref, bot_ref, pn_ref, cw_ref, rpar_ref, wbf_ref, wbb_ref, ab_ref, gb_ref,
                 q_ref, k_ref, r_ref, km_ref, v_ref, kh_ref, abo_ref, lwf_ref, lwb_ref, gg_ref, bon_ref, xbuf):
    j = pl.program_id(1)
    nt = pl.num_programs(1)
    is_ctx = j == 0
    top_ok = j >= 2
    bot_ok = (j >= 1) & (j < nt - 1)
    l_idx = lax.broadcasted_iota(jnp.int32, (ROW_TILE, 1), 0)
    col = jnp.where(is_ctx, l_idx, l_idx % GRID_W)
    left_ok = col != 0
    right_ok = col != jnp.where(is_ctx, ROW_TILE - 1, GRID_W - 1)
    vert = jnp.where(is_ctx, 0.0, 1.0)
    xbuf[0:XOFF, :] = jnp.zeros((XOFF, W_M), jnp.float32)
    xbuf[XOFF + 2 * HALO + ROW_TILE:, :] = jnp.zeros((XOFF, W_M), jnp.float32)
    sec = {}
    for s in range(5):
        sl = slice(s * W_M, (s + 1) * W_M)
        xbuf[XOFF:XOFF + HALO, :] = jnp.where(top_ok, top_ref[0, :, sl], 0.0)
        xbuf[XOFF + HALO:XOFF + HALO + ROW_TILE, :] = main_ref[0, :, sl]
        xbuf[XOFF + HALO + ROW_TILE:XOFF + 2 * HALO + ROW_TILE, :] = jnp.where(bot_ok, bot_ref[0, :, sl], 0.0)
        acc = None
        for dr in range(3):
            base = XOFF + HALO + GRID_W * (dr - 1)
            scale = 1.0 if dr == 1 else vert
            left = jnp.where(left_ok, xbuf[base - 1:base - 1 + ROW_TILE, :], 0.0)
            mid = xbuf[base:base + ROW_TILE, :]
            right = jnp.where(right_ok, xbuf[base + 1:base + 1 + ROW_TILE, :], 0.0)
            term = (left * cw_ref[3 * dr:3 * dr + 1, sl] + mid * cw_ref[3 * dr + 1:3 * dr + 2, sl]
                    + right * cw_ref[3 * dr + 2:3 * dr + 3, sl]) * scale
            acc = term if acc is None else acc + term
        sec[s] = acc
    q_ref[0] = jax.nn.silu(sec[0])
    k_ref[0] = jax.nn.silu(sec[1]) * (D_HM ** -0.5)
    rr, rk, rv = sec[2], sec[3], sec[4]
    r_ref[0] = rr
    v_ref[0] = rv
    lw = jnp.tanh(pn_ref[0, :, 1152:1280])
    lwf_ref[0] = -DECAY_SCALE * jax.nn.sigmoid(rpar_ref[0:1, :] + _dot3(lw, wbf_ref[...]))
    lwb_ref[0] = -DECAY_SCALE * jax.nn.sigmoid(rpar_ref[1:2, :] + _dot3(lw, wbb_ref[...]))
    a = jax.nn.sigmoid(rpar_ref[2:3, :] + _dot3(pn_ref[0, :, 1280:1408], ab_ref[...]))
    gg_ref[0] = _dot3(jax.nn.sigmoid(pn_ref[0, :, 1408:1536]), gb_ref[...])
    ones_bd = _head_block_diag(W_R, D_HR, 1.0)
    kap = rk * rpar_ref[3:4, :]
    norm = jnp.sqrt(_dot_exact_rhs(kap * kap, ones_bd))
    kh = kap / jnp.maximum(norm, 1e-12)
    kmod = rk * (1.0 + (a - 1.0) * rpar_ref[4:5, :])
    kh_ref[0] = kh
    abo_ref[0] = kh * a
    km_ref[0] = kmod
    bon_ref[0] = _dot_exact_rhs(rr * kmod * rpar_ref[5:6, :], ones_bd) * rv


def _prep(pc, pn, cw, rpar, wbf, wbb, ab, gb):
    b, t, _ = pc.shape
    nt = t // ROW_TILE
    per = ROW_TILE // HALO
    nh = t // HALO
    w = W_M
    full = lambda shp: pl.BlockSpec(shp, lambda i, j: tuple(0 for _ in shp))
    in_specs = [
        pl.BlockSpec((1, HALO, CONV_CH), lambda i, j: (i, jnp.maximum(j * per - 1, 0), 0)),
        pl.BlockSpec((1, ROW_TILE, CONV_CH), lambda i, j: (i, j, 0)),
        pl.BlockSpec((1, HALO, CONV_CH), lambda i, j: (i, jnp.minimum(j * per + per, nh - 1), 0)),
        pl.BlockSpec((1, ROW_TILE, NONCONV_W), lambda i, j: (i, j, 0)),
        full((16, CONV_CH)), full((8, w)), full((128, w)), full((128, w)), full((128, w)), full((128, w)),
    ]
    out = pl.BlockSpec((1, ROW_TILE, w), lambda i, j: (i, j, 0))
    return pl.pallas_call(
        _prep_kernel,
        grid=(b, nt),
        in_specs=in_specs,
        out_specs=[out] * 11,
        out_shape=[jax.ShapeDtypeStruct((b, t, w), jnp.float32)] * 11,
        scratch_shapes=[pltpu.VMEM((2 * XOFF + 2 * HALO + ROW_TILE, w), jnp.float32)],
        compiler_params=pltpu.CompilerParams(
            dimension_semantics=("arbitrary", "arbitrary"), vmem_limit_bytes=VMEM_LIMIT),
        name="conv_prep",
    )(pc, pc, pc, pn, cw, rpar, wbf, wbb, ab, gb)


def _rwkv_pair_chunk(r, k, v, kh, ab, lw, s_mat, tri_incl, strict, incl, last_row, m0, m1):
    cw = _dot_exact_lhs(tri_incl, lw)
    e_pos = jnp.exp(cw)
    e_neg = jnp.exp(-cw)
    e_prev = jnp.exp(cw - lw)
    stack = lambda a: jnp.concatenate([a * m0, a * m1], axis=0)
    kt = stack(kh * e_prev)
    bt = stack(ab * e_neg)
    kk = stack(k * e_neg)
    rt = stack(r * e_pos)
    vs = stack(v)
    zero = jnp.zeros((), jnp.float32)
    a_bk = jnp.where(strict, _dot3(kt, bt, _NT), zero)
    a_kk = jnp.where(strict, _dot3(kt, kk, _NT), zero)
    a_rb = jnp.where(incl, _dot3(rt, bt, _NT), zero)
    a_rk = jnp.where(incl, _dot3(rt, kk, _NT), zero)
    p = -a_bk
    eye = (lax.broadcasted_iota(jnp.int32, (PAIR, PAIR), 0)
           == lax.broadcasted_iota(jnp.int32, (PAIR, PAIR), 1)).astype(jnp.float32)
    t_inv = eye + p
    for _ in range(5):
        p = _dot3(p, p)
        t_inv = t_inv + _dot3(t_inv, p)
    rhs = _dot3(kt, s_mat, _NT) + _dot3(a_kk, vs)
    u = -_dot3(t_inv, rhs)
    y = _dot3(rt, s_mat, _NT) + _dot3(a_rb, u) + _dot3(a_rk, vs)
    w_last = jnp.sum(jnp.where(last_row, e_pos, zero), axis=0, keepdims=True)
    s_new = (s_mat + _dot3(u, bt, _TN) + _dot3(vs, kk, _TN)) * w_last
    return y[:CHUNK] + y[CHUNK:], s_new


def _rwkv_kernel(rf, kf, vf, khf, abf, lwf, rb, kb, vb, khb, abb, lwb, yf_ref, yb_ref, s_ref, *, nctx):
    j = pl.program_id(1)

    @pl.when(j == 0)
    def _():
        s_ref[...] = jnp.zeros_like(s_ref)

    row = lax.broadcasted_iota(jnp.int32, (PAIR, PAIR), 0)
    col = lax.broadcasted_iota(jnp.int32, (PAIR, PAIR), 1)
    same = (row // CHUNK) == (col // CHUNK)
    r64 = lax.broadcasted_iota(jnp.int32, (CHUNK, CHUNK), 0)
    c64 = lax.broadcasted_iota(jnp.int32, (CHUNK, CHUNK), 1)
    rowl = lax.broadcasted_iota(jnp.int32, (CHUNK, PAIR), 0)
    lane = lax.broadcasted_iota(jnp.int32, (1, PAIR), 1)
    m0 = (lane < D_HR).astype(jnp.float32)
    m1 = 1.0 - m0
    dirs = (
        (rf, kf, vf, khf, abf, lwf, yf_ref, (c64 <= r64), same & (col < row), same & (col <= row), rowl == CHUNK - 1),
        (rb, kb, vb, khb, abb, lwb, yb_ref, (c64 >= r64), same & (col > row), same & (col >= row), rowl == 0),
    )
    for d, (r_ref, k_ref, v_ref, kh_ref, ab_ref, lw_ref, y_ref, tri, strict, incl, last_row) in enumerate(dirs):
        tri = tri.astype(jnp.float32).astype(jnp.bfloat16)
        for p in range(NPAIR):
            sl = slice(p * PAIR, (p + 1) * PAIR)
            y, s_new = _rwkv_pair_chunk(r_ref[0, :, sl], k_ref[0, :, sl], v_ref[0, :, sl], kh_ref[0, :, sl],
                                        ab_ref[0, :, sl], lw_ref[0, :, sl], s_ref[d, p], tri, strict, incl,
                                        last_row, m0, m1)
            s_ref[d, p] = s_new

            @pl.when(j >= nctx)
            def _():
                y_ref[0, :, sl] = y


def _rwkv_scan(r, k, v, kh, ab, lw_f, lw_b, ctx_len):
    b, t, w = r.shape
    nc = t // CHUNK
    nctx = ctx_len // CHUNK
    nlat = nc - nctx

    def fwd_map(i, j):
        return (i, j, 0)

    def bwd_map(i, j):
        return (i, jnp.where(j < nctx, nctx - 1 - j, nc - 1 - (j - nctx)), 0)

    blk = (1, CHUNK, w)
    in_specs = [pl.BlockSpec(blk, fwd_map)] * 6 + [pl.BlockSpec(blk, bwd_map)] * 6
    out_specs = [
        pl.BlockSpec(blk, lambda i, j: (i, jnp.maximum(j - nctx, 0), 0)),
        pl.BlockSpec(blk, lambda i, j: (i, nlat - 1 - jnp.maximum(j - nctx, 0), 0)),
    ]
    return pl.pallas_call(
        functools.partial(_rwkv_kernel, nctx=nctx),
        grid=(b, nc),
        in_specs=in_specs,
        out_specs=out_specs,
        out_shape=[jax.ShapeDtypeStruct((b, t - ctx_len, w), jnp.float32)] * 2,
        scratch_shapes=[pltpu.VMEM((2, NPAIR, PAIR, PAIR), jnp.float32)],
        compiler_params=pltpu.CompilerParams(
            dimension_semantics=("arbitrary", "arbitrary"), vmem_limit_bytes=VMEM_LIMIT),
        name="rwkv_scan",
    )(r, k, v, kh, ab, lw_f, r, k, v, kh, ab, lw_b)


def _col_stack(a, c0, c1):
    return jnp.concatenate([a[:, c0:c0 + 1], a[:, c1:c1 + 1]], axis=0)


def _half_bcast(v0, v1):
    return jnp.concatenate([jnp.broadcast_to(v0, (CHUNK, 1)), jnp.broadcast_to(v1, (CHUNK, 1))], axis=0)


def _mlstm_pair_chunk(q, k, v, b_col, li_col, c_row, c_mat, n_row, m_row, incl, last_sel, m0, m1):
    stack = lambda a: jnp.concatenate([a * m0, a * m1], axis=0)
    q_st, k_st, v_st = stack(q), stack(k), stack(v)
    lane = lax.broadcasted_iota(jnp.int32, (1, PAIR), 1)
    first = lane < D_HM
    neg_inf = jnp.full((), -jnp.inf, jnp.float32)
    m_h0, m_h1 = m_row[:, 0:1], m_row[:, D_HM:D_HM + 1]
    m_prev = _half_bcast(m_h0, m_h1)
    dmat = jnp.where(incl, b_col + c_row, neg_inf)
    m_inter = b_col + m_prev
    m_t = jnp.maximum(m_inter, jnp.max(dmat, axis=1, keepdims=True))
    s = _dot3(q_st, k_st, _NT) * jnp.exp(dmat - m_t)
    inter = jnp.exp(m_inter - m_t)
    num = inter * _dot3(q_st, c_mat) + _dot3(s, v_st)
    den = inter * jnp.sum(q_st * n_row, axis=1, keepdims=True) + jnp.sum(s, axis=1, keepdims=True)
    h_st = num / jnp.maximum(jnp.abs(den), jnp.exp(-m_t))
    h = h_st[:CHUNK] + h_st[CHUNK:]
    bl = jnp.where(last_sel, b_col, neg_inf)
    bl0 = jnp.max(bl[:CHUNK], axis=0, keepdims=True)
    bl1 = jnp.max(bl[CHUNK:], axis=0, keepdims=True)
    g = _half_bcast(bl0, bl1) - b_col + li_col
    mn0 = jnp.maximum(bl0 + m_h0, jnp.max(g[:CHUNK], axis=0, keepdims=True))
    mn1 = jnp.maximum(bl1 + m_h1, jnp.max(g[CHUNK:], axis=0, keepdims=True))
    wts = jnp.exp(g - _half_bcast(mn0, mn1))
    decay_row = jnp.where(first, jnp.exp(bl0 + m_h0 - mn0), jnp.exp(bl1 + m_h1 - mn1))
    wk = wts * k_st
    c_new = decay_row * c_mat + _dot3(wk, v_st, _TN)
    n_new = decay_row * n_row + jnp.sum(wk, axis=0, keepdims=True)
    m_new = jnp.where(first, mn0, mn1)
    return h, c_new, n_new, m_new


def _mlstm_kernel(qf, kf, vf, gf, qb, kb, vb, gb, bias_ref, hf_ref, hb_ref, c_ref, n_ref, m_ref, *, nctx):
    j = pl.program_id(1)

    @pl.when(j == 0)
    def _():
        c_ref[...] = jnp.zeros_like(c_ref)
        n_ref[...] = jnp.zeros_like(n_ref)
        m_ref[...] = jnp.zeros_like(m_ref)

    row = lax.broadcasted_iota(jnp.int32, (PAIR, PAIR), 0)
    col = lax.broadcasted_iota(jnp.int32, (PAIR, PAIR), 1)
    same = (row // CHUNK) == (col // CHUNK)
    r64 = lax.broadcasted_iota(jnp.int32, (CHUNK, CHUNK), 0)
    c64 = lax.broadcasted_iota(jnp.int32, (CHUNK, CHUNK), 1)
    rcol = lax.broadcasted_iota(jnp.int32, (PAIR, 1), 0) % CHUNK
    lane = lax.broadcasted_iota(jnp.int32, (1, PAIR), 1)
    m0 = (lane < D_HM).astype(jnp.float32)
    m1 = 1.0 - m0
    is_f = (lane % (2 * H_M)) >= H_M
    dirs = (
        (qf, kf, vf, gf, hf_ref, (c64 <= r64), same & (col <= row), rcol == CHUNK - 1),
        (qb, kb, vb, gb, hb_ref, (c64 >= r64), same & (col >= row), rcol == 0),
    )
    for d, (q_ref, k_ref, v_ref, g_ref, h_ref, tri, incl, last_sel) in enumerate(dirs):
        tri = tri.astype(jnp.float32).astype(jnp.bfloat16)
        gl = g_ref[0] + bias_ref[...]
        act = jnp.where(is_f, jax.nn.log_sigmoid(gl), gl)
        b_all = _dot_exact_lhs(tri, act)
        act_t = act.T
        b_t = b_all.T
        for p in range(NPAIR):
            sl = slice(p * PAIR, (p + 1) * PAIR)
            ci, cf = d * 2 * H_M + 2 * p, d * 2 * H_M + H_M + 2 * p
            b_col = _col_stack(b_all, cf, cf + 1)
            li_col = _col_stack(act, ci, ci + 1)
            c_row = jnp.concatenate([act_t[ci:ci + 1, :] - b_t[cf:cf + 1, :],
                                     act_t[ci + 1:ci + 2, :] - b_t[cf + 1:cf + 2, :]], axis=1)
            h, c_new, n_new, m_new = _mlstm_pair_chunk(
                q_ref[0, :, sl], k_ref[0, :, sl], v_ref[0, :, sl], b_col, li_col, c_row,
                c_ref[d, p], n_ref[d, p], m_ref[d, p], incl, last_sel, m0, m1)
            c_ref[d, p] = c_new
            n_ref[d, p] = n_new
            m_ref[d, p] = m_new

            @pl.when(j >= nctx)
            def _():
                h_ref[0, :, sl] = h


def _mlstm_scan(q, k, pn, bias_row, ctx_len):
    v = gates = pn
    b, t, w = q.shape
    nc = t // CHUNK
    nctx = ctx_len // CHUNK
    nlat = nc - nctx

    def fwd_map(i, j):
        return (i, j, 0)

    def bwd_map(i, j):
        return (i, jnp.where(j < nctx, nctx - 1 - j, nc - 1 - (j - nctx)), 0)

    blk = (1, CHUNK, w)
    gblk = (1, CHUNK, 128)
    gate_blk = (2 * W_M) // 128
    gate_of = lambda m: (lambda i, j: m(i, j)[:2] + (gate_blk,))
    in_specs = ([pl.BlockSpec(blk, fwd_map)] * 3 + [pl.BlockSpec(gblk, gate_of(fwd_map))]
                + [pl.BlockSpec(blk, bwd_map)] * 3 + [pl.BlockSpec(gblk, gate_of(bwd_map))]
                + [pl.BlockSpec((1, 128), lambda i, j: (0, 0))])
    out_specs = [
        pl.BlockSpec(blk, lambda i, j: (i, jnp.maximum(j - nctx, 0), 0)),
        pl.BlockSpec(blk, lambda i, j: (i, nlat - 1 - jnp.maximum(j - nctx, 0), 0)),
    ]
    return pl.pallas_call(
        functools.partial(_mlstm_kernel, nctx=nctx),
        grid=(b, nc),
        in_specs=in_specs,
        out_specs=out_specs,
        out_shape=[jax.ShapeDtypeStruct((b, t - ctx_len, w), jnp.float32)] * 2,
        scratch_shapes=[pltpu.VMEM((2, NPAIR, PAIR, PAIR), jnp.float32),
                        pltpu.VMEM((2, NPAIR, 1, PAIR), jnp.float32),
                        pltpu.VMEM((2, NPAIR, 1, PAIR), jnp.float32)],
        compiler_params=pltpu.CompilerParams(
            dimension_semantics=("arbitrary", "arbitrary"), vmem_limit_bytes=VMEM_LIMIT),
        name="mlstm_scan",
    )(q, k, v, gates, q, k, v, gates, bias_row)


ROUTE_W = 128
N_ROUTE = N_GROUPS + N_EXPERTS


def _ln_rows(z, eps):
    mu = jnp.mean(z, -1, keepdims=True)
    zc = z - mu
    var = jnp.mean(zc * zc, -1, keepdims=True)
    return zc * lax.rsqrt(var + eps)


def _headnorm_mxu(z, avg_bf16, eps):
    mu = _dot_exact_rhs(z, avg_bf16)
    zc = z - mu
    var = _dot_exact_rhs(zc * zc, avg_bf16)
    return zc * lax.rsqrt(var + eps)


def _outproj_kernel(hf, hb, mo, yf, yb, gg, bonus, x_ref, mod_ref, wout_ref, hpar_ref, ln_ref, rtw_ref, rtb_ref,
                    x1_ref, tok_ref, route_ref, ridx_ref, cnt_ref, carry_ref):
    i, j = pl.program_id(0), pl.program_id(1)

    @pl.when((i == 0) & (j == 0))
    def _():
        carry_ref[...] = jnp.zeros_like(carry_ref)

    avg = _head_block_diag(W_M, D_HM, 1.0 / D_HM)
    out_m = jax.nn.sigmoid(mo[0]) * (_headnorm_mxu(hf[0] + hb[0], avg, LN_EPS) * hpar_ref[0:1, :])
    y = _headnorm_mxu(yf[0] + yb[0], avg, GN_EPS) * hpar_ref[1:2, :] + hpar_ref[2:3, :]
    out_r = (y + bonus[0]) * gg[0]
    proj = (jnp.dot(out_m.astype(jnp.bfloat16), wout_ref[:W_M, :], preferred_element_type=jnp.float32)
            + jnp.dot(out_r.astype(jnp.bfloat16), wout_ref[W_M:, :], preferred_element_type=jnp.float32))
    g1, sh2, sc2 = mod_ref[0, 2:3, :], mod_ref[0, 3:4, :], mod_ref[0, 4:5, :]
    x1 = _ln_rows(ALPHA * x_ref[0] + g1 * proj, LN_EPS) * ln_ref[0:1, :] + ln_ref[1:2, :]
    x1_ref[0] = x1
    tok = _ln_rows(x1, LN_EPS) * (1.0 + sc2) + sh2
    tok_ref[0] = tok

    logits = _dot3(tok, rtw_ref[...]) + rtb_ref[...]
    lane = lax.broadcasted_iota(jnp.int32, (ROW_TILE, ROUTE_W), 1)
    neg_inf = jnp.full((), -jnp.inf, jnp.float32)
    big = jnp.int32(ROUTE_W)
    is_g = lane < N_GROUPS
    lg = jnp.where(is_g, logits, neg_inf)
    gmax = jnp.max(lg, axis=1, keepdims=True)
    grp = jnp.min(jnp.where(lg == gmax, lane, big), axis=1, keepdims=True)
    p_top = 1.0 / jnp.sum(jnp.where(is_g, jnp.exp(lg - gmax), 0.0), axis=1, keepdims=True)
    in_grp = (lane >= N_GROUPS) & (lane < N_ROUTE) & ((lane - N_GROUPS) // E_PER_GROUP == grp)
    le = jnp.where(in_grp, logits, neg_inf)
    v1 = jnp.max(le, axis=1, keepdims=True)
    i1 = jnp.min(jnp.where(le == v1, lane, big), axis=1, keepdims=True)
    le2 = jnp.where(lane == i1, neg_inf, le)
    v2 = jnp.max(le2, axis=1, keepdims=True)
    i2 = jnp.min(jnp.where(le2 == v2, lane, big), axis=1, keepdims=True)
    e21 = jnp.exp(v2 - v1)
    w1 = (1.0 / (1.0 + e21)) * p_top
    w2 = (e21 / (1.0 + e21)) * p_top

    sel1, sel2 = lane == i1, lane == i2
    onehot = jnp.where(sel1 | sel2, 1.0, 0.0)
    tr = lax.broadcasted_iota(jnp.int32, (ROW_TILE, ROW_TILE), 0)
    tc = lax.broadcasted_iota(jnp.int32, (ROW_TILE, ROW_TILE), 1)
    strict = jnp.where(tc < tr, 1.0, 0.0).astype(jnp.bfloat16)
    before = _dg(strict, onehot.astype(jnp.bfloat16), _NN) + carry_ref[...]
    r1 = jnp.sum(jnp.where(sel1, before, 0.0), axis=1, keepdims=True)
    r2 = jnp.sum(jnp.where(sel2, before, 0.0), axis=1, keepdims=True)
    carry_ref[...] = carry_ref[...] + jnp.sum(onehot, axis=0, keepdims=True)
    e1 = (i1 - N_GROUPS).astype(jnp.float32)
    e2 = (i2 - N_GROUPS).astype(jnp.float32)
    route = jnp.where(lane == 0, e1, jnp.where(lane == 1, e2, jnp.where(lane == 2, r1, jnp.where(
        lane == 3, r2, jnp.where(lane == 4, w1, jnp.where(lane == 5, w2, 0.0))))))
    route_ref[0] = route
    ridx_ref[0] = route.T[0:8, :].astype(jnp.int32)
    cnt_ref[...] = jnp.broadcast_to(carry_ref[...], cnt_ref.shape)


def _outproj(h_f, h_b, pn, y_f, y_b, gg, bonus, x, mod3, wout_bf16, hpar, lnpar, rtw, rtb, ctx_len):
    b, seq, d = x.shape
    nt = seq // ROW_TILE
    off = ctx_len // ROW_TILE
    lat = lambda w: pl.BlockSpec((1, ROW_TILE, w), lambda i, j: (i, j, 0))
    full = lambda shp: pl.BlockSpec(shp, lambda i, j: tuple(0 for _ in shp))
    in_specs = [
        lat(W_M), lat(W_M),
        pl.BlockSpec((1, ROW_TILE, W_M), lambda i, j: (i, j + off, 1)),
        lat(W_R), lat(W_R),
        pl.BlockSpec((1, ROW_TILE, W_R), lambda i, j: (i, j + off, 0)),
        pl.BlockSpec((1, ROW_TILE, W_R), lambda i, j: (i, j + off, 0)),
        lat(d),
        pl.BlockSpec((1, 6, d), lambda i, j: (i, 0, 0)),
        full((MIX_W, d)), full((8, W_M)), full((8, d)), full((d, ROUTE_W)), full((1, ROUTE_W)),
    ]
    out_specs = [
        lat(d), lat(d), lat(ROUTE_W),
        pl.BlockSpec((1, 8, ROW_TILE), lambda i, j: (i * nt + j, 0, 0)),
        full((8, ROUTE_W)),
    ]
    out_shape = [
        jax.ShapeDtypeStruct((b, seq, d), jnp.float32),
        jax.ShapeDtypeStruct((b, seq, d), jnp.float32),
        jax.ShapeDtypeStruct((b, seq, ROUTE_W), jnp.float32),
        jax.ShapeDtypeStruct((b * nt, 8, ROW_TILE), jnp.int32),
        jax.ShapeDtypeStruct((8, ROUTE_W), jnp.float32),
    ]
    return pl.pallas_call(
        _outproj_kernel,
        grid=(b, nt),
        in_specs=in_specs,
        out_specs=out_specs,
        out_shape=out_shape,
        scratch_shapes=[pltpu.VMEM((1, ROUTE_W), jnp.float32)],
        compiler_params=pltpu.CompilerParams(
            dimension_semantics=("arbitrary", "arbitrary"), vmem_limit_bytes=VMEM_LIMIT),
        name="outproj_router",
    )(h_f, h_b, pn, y_f, y_b, gg, bonus, x, mod3, wout_bf16, hpar, lnpar, rtw, rtb)


def _pair_dest(ridx_ref, starts_ref, r, slot):
    return starts_ref[ridx_ref[0, slot, r]] + ridx_ref[0, 2 + slot, r]


def _scatter_kernel(starts_ref, ridx_ref, tok_ref, xs_in_ref, xs_ref, sem):
    del xs_in_ref

    def row_copy(r, slot):
        dst = _pair_dest(ridx_ref, starts_ref, r, slot)
        return pltpu.make_async_copy(tok_ref.at[pl.ds(r, 1), :], xs_ref.at[pl.ds(dst, 1), :], sem)

    def issue(r, c):
        row_copy(r, 0).start()
        row_copy(r, 1).start()
        return c

    def drain(r, c):
        pltpu.make_async_copy(tok_ref.at[pl.ds(0, 1), :], xs_ref.at[pl.ds(0, 1), :], sem).wait()
        return c

    lax.fori_loop(0, ROW_TILE, issue, 0)
    lax.fori_loop(0, 2 * ROW_TILE, drain, 0)


def _moe_scatter(starts, ridx, tok2d, xs_zero):
    n, d = tok2d.shape
    nt = n // ROW_TILE
    return pl.pallas_call(
        _scatter_kernel,
        grid_spec=pltpu.PrefetchScalarGridSpec(
            num_scalar_prefetch=1,
            grid=(nt,),
            in_specs=[
                pl.BlockSpec((1, 8, ROW_TILE), lambda i, s: (i, 0, 0), memory_space=pltpu.SMEM),
                pl.BlockSpec((ROW_TILE, d), lambda i, s: (i, 0)),
                pl.BlockSpec(memory_space=pl.ANY),
            ],
            out_specs=pl.BlockSpec(memory_space=pl.ANY),
            scratch_shapes=[pltpu.SemaphoreType.DMA(())],
        ),
        out_shape=jax.ShapeDtypeStruct(xs_zero.shape, xs_zero.dtype),
        input_output_aliases={3: 0},
        compiler_params=pltpu.CompilerParams(dimension_semantics=("arbitrary",), vmem_limit_bytes=VMEM_LIMIT),
        name="moe_scatter",
    )(starts, ridx, tok2d, xs_zero)


def _expert_kernel(blk_exp_ref, nused_ref, xs_ref, wg_ref, wu_ref, wd_ref, ys_ref):
    i = pl.program_id(0)

    @pl.when(i < nused_ref[0])
    def _():
        xb = xs_ref[...].astype(jnp.bfloat16)
        hg = jnp.dot(xb, wg_ref[0], preferred_element_type=jnp.float32)
        hu = jnp.dot(xb, wu_ref[0], preferred_element_type=jnp.float32)
        hb = (jax.nn.silu(hg) * hu).astype(jnp.bfloat16)
        ys_ref[...] = jnp.dot(hb, wd_ref[0], preferred_element_type=jnp.float32)

    @pl.when(i >= nused_ref[0])
    def _():
        ys_ref[...] = jnp.zeros_like(ys_ref)


def _moe_experts(blk_exp, nused, xs, wg, wu, wd):
    nrow, d = xs.shape
    de = wg.shape[2]
    return pl.pallas_call(
        _expert_kernel,
        grid_spec=pltpu.PrefetchScalarGridSpec(
            num_scalar_prefetch=2,
            grid=(nrow // MOE_BLOCK,),
            in_specs=[
                pl.BlockSpec((MOE_BLOCK, d), lambda i, be, nu: (i, 0)),
                pl.BlockSpec((1, d, de), lambda i, be, nu: (be[i], 0, 0)),
                pl.BlockSpec((1, d, de), lambda i, be, nu: (be[i], 0, 0)),
                pl.BlockSpec((1, de, d), lambda i, be, nu: (be[i], 0, 0)),
            ],
            out_specs=pl.BlockSpec((MOE_BLOCK, d), lambda i, be, nu: (i, 0)),
        ),
        out_shape=jax.ShapeDtypeStruct((nrow, d), jnp.float32),
        compiler_params=pltpu.CompilerParams(dimension_semantics=("arbitrary",), vmem_limit_bytes=VMEM_LIMIT),
        name="moe_experts",
    )(blk_exp, nused, xs, wg, wu, wd)


def _combine_kernel(starts_ref, ridx_ref, route_ref, x1_ref, mod_ref, ln_ref, ys_ref, out_ref, ybuf, sem):
    def row_copy(r, slot):
        src = _pair_dest(ridx_ref, starts_ref, r, slot)
        return pltpu.make_async_copy(ys_ref.at[pl.ds(src, 1), :], ybuf.at[slot, pl.ds(r, 1), :], sem)

    def issue(r, c):
        row_copy(r, 0).start()
        row_copy(r, 1).start()
        return c

    def drain(r, c):
        pltpu.make_async_copy(ys_ref.at[pl.ds(0, 1), :], ybuf.at[0, pl.ds(0, 1), :], sem).wait()
        return c

    lax.fori_loop(0, ROW_TILE, issue, 0)
    lax.fori_loop(0, 2 * ROW_TILE, drain, 0)
    route = route_ref[0]
    ffn = ybuf[0] * route[:, 4:5] + ybuf[1] * route[:, 5:6]
    g2 = mod_ref[0, 5:6, :]
    out_ref[0] = _ln_rows(ALPHA * x1_ref[0] + g2 * ffn, LN_EPS) * ln_ref[2:3, :] + ln_ref[3:4, :]


def _moe_combine(starts, ridx, route, x1, mod3, lnpar, ys):
    b, seq, d = x1.shape
    nt = seq // ROW_TILE
    return pl.pallas_call(
        _combine_kernel,
        grid_spec=pltpu.PrefetchScalarGridSpec(
            num_scalar_prefetch=1,
            grid=(b, nt),
            in_specs=[
                pl.BlockSpec((1, 8, ROW_TILE), lambda i, j, s: (i * nt + j, 0, 0), memory_space=pltpu.SMEM),
                pl.BlockSpec((1, ROW_TILE, ROUTE_W), lambda i, j, s: (i, j, 0)),
                pl.BlockSpec((1, ROW_TILE, d), lambda i, j, s: (i, j, 0)),
                pl.BlockSpec((1, 6, d), lambda i, j, s: (i, 0, 0)),
                pl.BlockSpec((8, d), lambda i, j, s: (0, 0)),
                pl.BlockSpec(memory_space=pl.ANY),
            ],
            out_specs=pl.BlockSpec((1, ROW_TILE, d), lambda i, j, s: (i, j, 0)),
            scratch_shapes=[pltpu.VMEM((2, ROW_TILE, d), jnp.float32), pltpu.SemaphoreType.DMA(())],
        ),
        out_shape=jax.ShapeDtypeStruct((b, seq, d), jnp.float32),
        compiler_params=pltpu.CompilerParams(dimension_semantics=("arbitrary", "arbitrary"),
                                             vmem_limit_bytes=VMEM_LIMIT),
        name="moe_combine",
    )(starts, ridx, route, x1, mod3, lnpar, ys)


def _moe_plan(cnt, n_pairs):
    counts = cnt[0, N_GROUPS:N_ROUTE].astype(jnp.int32)
    padded = (counts + MOE_BLOCK - 1) // MOE_BLOCK * MOE_BLOCK
    pad_ends = jnp.cumsum(padded)
    pad_starts = pad_ends - padded
    n_blk = -(-n_pairs // MOE_BLOCK) + N_EXPERTS
    blk_exp = jnp.minimum(jnp.searchsorted(pad_ends, jnp.arange(n_blk, dtype=jnp.int32) * MOE_BLOCK, side='right'),
                          N_EXPERTS - 1).astype(jnp.int32)
    nused = (pad_ends[-1:] // MOE_BLOCK).astype(jnp.int32)
    return pad_starts.astype(jnp.int32), blk_exp, nused, n_blk


def _layernorm(z, gain=None, bias=None):
    zf = z.astype(jnp.float32)
    mu = jnp.mean(zf, -1, keepdims=True)
    var = jnp.mean(jnp.square(zf - mu), -1, keepdims=True)
    y = (zf - mu) * lax.rsqrt(var + LN_EPS)
    if gain is not None:
        y = y * gain + bias
    return y.astype(z.dtype)


def _headnorm(z, eps):
    zf = z.astype(jnp.float32)
    mu = jnp.mean(zf, -1, keepdims=True)
    var = jnp.mean(jnp.square(zf - mu), -1, keepdims=True)
    return (zf - mu) * lax.rsqrt(var + eps)


def _modulate(z, shift, scale):
    return _layernorm(z) * (1 + scale) + shift


def _rev_segments(u, ctx_len):
    return jnp.concatenate([jnp.flip(u[:, :ctx_len], 1), jnp.flip(u[:, ctx_len:], 1)], axis=1)


def _dir_stack(u_fwd, u_bwd, ctx_len):
    return jnp.concatenate([u_fwd, _rev_segments(u_bwd, ctx_len)], axis=0)


def _dir_merge(y, ctx_len):
    b = y.shape[0] // 2
    return y[:b] + _rev_segments(y[b:], ctx_len)


def _conv_grid(u, w, rows):
    b, l, ch = u.shape
    img = u.reshape(b, rows, l // rows, ch)
    out = lax.conv_general_dilated(img, w[:, :, None, :].astype(u.dtype), (1, 1), 'SAME',
                                   dimension_numbers=('NHWC', 'HWIO', 'NHWC'), feature_group_count=ch)
    return out.reshape(b, l, ch)


def _conv_seq(u, w_row):
    ch = u.shape[-1]
    return lax.conv_general_dilated(u, w_row[:, None, :].astype(u.dtype), (1,), 'SAME',
                                    dimension_numbers=('NWC', 'WIO', 'NWC'), feature_group_count=ch)


def _mlstm_chunkwise(q, k, v, log_i, log_f):
    z, t, h, dh = q.shape
    nc = t // CHUNK

    def chunks(a):
        a = a.astype(jnp.float32).reshape((z, nc, CHUNK, h) + a.shape[3:])
        return jnp.swapaxes(jnp.swapaxes(a, 0, 1), 2, 3)
    qc, kc, vc, lic, lfc = chunks(q), chunks(k), chunks(v), chunks(log_i), chunks(log_f)
    tri = jnp.tril(jnp.ones((CHUNK, CHUNK), bool))

    def step(carry, inp):
        c_mat, n_vec, m = carry
        qb, kb, vb, li, lf = inp
        b = jnp.cumsum(lf, -1)
        dmat = jnp.where(tri, b[..., :, None] - b[..., None, :] + li[..., None, :], -jnp.inf)
        m_inter = b + m[..., None]
        m_t = jnp.maximum(m_inter, jnp.max(dmat, -1))
        s = jnp.einsum('zhtd,zhsd->zhts', qb, kb) * jnp.exp(dmat - m_t[..., None])
        inter = jnp.exp(m_inter - m_t)
        num = inter[..., None] * jnp.einsum('zhtd,zhde->zhte', qb, c_mat) + jnp.einsum('zhts,zhse->zhte', s, vb)
        den = inter * jnp.einsum('zhtd,zhd->zht', qb, n_vec) + jnp.sum(s, -1)
        h_out = num / jnp.maximum(jnp.abs(den), jnp.exp(-m_t))[..., None]
        b_last = b[..., -1]
        g = b_last[..., None] - b + li
        m_new = jnp.maximum(b_last + m, jnp.max(g, -1))
        wts = jnp.exp(g - m_new[..., None])
        decay = jnp.exp(b_last + m - m_new)
        c_mat = decay[..., None, None] * c_mat + jnp.einsum('zhs,zhsd,zhse->zhde', wts, kb, vb)
        n_vec = decay[..., None] * n_vec + jnp.einsum('zhs,zhsd->zhd', wts, kb)
        return (c_mat, n_vec, m_new), h_out

    init = (jnp.zeros((z, h, dh, dh), jnp.float32), jnp.zeros((z, h, dh), jnp.float32), jnp.zeros((z, h), jnp.float32))
    _, hs = lax.scan(step, init, (qc, kc, vc, lic, lfc))
    return jnp.swapaxes(jnp.swapaxes(hs, 2, 3), 0, 1).reshape(z, t, h, dh)


def _mlstm_group(mq, mk, mv, mo, gates, b_i, b_f, norm_w, ctx_len):
    b, t, _ = mq.shape
    q = jax.nn.silu(mq)
    k = jax.nn.silu(mk) * (D_HM ** -0.5)
    bias_row = jnp.concatenate([b_i[0], b_f[0], b_i[1], b_f[1], jnp.zeros((128 - 4 * H_M,), jnp.float32)])[None, :]
    h_f, h_b = _mlstm_scan(q, k, mv, gates, bias_row, ctx_len)
    h = (h_f + h_b).reshape(b, t - ctx_len, H_M, D_HM)
    h = _headnorm(h, LN_EPS) * norm_w.reshape(H_M, D_HM)
    return jax.nn.sigmoid(mo[:, ctx_len:]) * h.reshape(b, t - ctx_len, W_M)


def _rwkv7_scan(r, w, k, v, kh, a):
    z, t, h, n = r.shape
    seq = tuple(jnp.moveaxis(s.astype(jnp.float32), 1, 0) for s in (r, w, k, v, kh, a))

    def step(state, inp):
        r_t, w_t, k_t, v_t, kh_t, a_t = inp
        removed = jnp.einsum('zhvk,zhk->zhv', state, kh_t)
        state = (state * w_t[:, :, None, :] - removed[..., None] * (kh_t * a_t)[:, :, None, :]
                 + v_t[..., None] * k_t[:, :, None, :])
        return state, jnp.einsum('zhvk,zhk->zhv', state, r_t)

    _, y = lax.scan(step, jnp.zeros((z, h, n, n), jnp.float32), seq)
    return jnp.moveaxis(y, 0, 1)


def _rwkv7_group(rr, rk, rv, lw_f, lw_b, la, lg, w0, w_b_mat, a0, a_b_mat, g_b_mat, kk, ka, bonus_w, gn_w, gn_b,
                 ctx_len):
    b, t, _ = rr.shape
    shp = (b, t, H_R, D_HR)
    logw_f = -DECAY_SCALE * jax.nn.sigmoid((w0[0] + jnp.tanh(lw_f) @ w_b_mat[0]).astype(jnp.float32))
    logw_b = -DECAY_SCALE * jax.nn.sigmoid((w0[1] + jnp.tanh(lw_b) @ w_b_mat[1]).astype(jnp.float32))
    a = jax.nn.sigmoid(a0 + la @ a_b_mat)
    g = jax.nn.sigmoid(lg) @ g_b_mat
    kap = (rk * kk).reshape(shp).astype(jnp.float32)
    kh = kap / jnp.maximum(jnp.sqrt(jnp.sum(jnp.square(kap), -1, keepdims=True)), 1e-12)
    kmod = (rk * (1 + (a - 1) * ka)).reshape(shp)
    r = rr.reshape(shp)
    v = rv.reshape(shp)
    a_h = a.reshape(shp)
    flat = lambda u: u.reshape(b, t, W_R)
    y_f, y_b = _rwkv_scan(rr, flat(kmod), rv, flat(kh), flat(kh * a_h), logw_f, logw_b, ctx_len)
    y = (y_f + y_b).reshape(b, t - ctx_len, H_R, D_HR)
    y = _headnorm(y, GN_EPS) * gn_w.reshape(H_R, D_HR) + gn_b.reshape(H_R, D_HR)
    bonus = (jnp.sum(r * kmod * bonus_w, -1, keepdims=True) * v)[:, ctx_len:]
    return (y + bonus).reshape(b, t - ctx_len, W_R) * g[:, ctx_len:]


def _hier_moe(h, rt_g, rt_g_b, rt_e, rt_e_b, ex_gate, ex_up, ex_down):
    b, t, d = h.shape
    tok = h.reshape(b * t, d)
    n = tok.shape[0]
    lg = (tok @ rt_g + rt_g_b).astype(jnp.float32)
    p_grp = jax.nn.softmax(lg, axis=-1)
    grp = jnp.argmax(lg, axis=-1).astype(jnp.int32)
    p_top = jnp.take_along_axis(p_grp, grp[:, None], axis=-1)
    le = (tok @ rt_e + rt_e_b).astype(jnp.float32).reshape(n, N_GROUPS, E_PER_GROUP)
    le_grp = jnp.take_along_axis(le, jnp.broadcast_to(grp[:, None, None], (n, 1, E_PER_GROUP)), axis=1)[:, 0]
    top_val, top_idx = lax.top_k(le_grp, TOP_K_IN_GROUP)
    wts = jax.nn.softmax(top_val, axis=-1) * p_top
    e_flat = (grp[:, None] * E_PER_GROUP + top_idx.astype(jnp.int32)).reshape(-1)
    w_flat = wts.reshape(-1)
    t_flat = jnp.repeat(jnp.arange(n, dtype=jnp.int32), TOP_K_IN_GROUP)
    order = jnp.argsort(e_flat)
    e_s, t_s, w_s = e_flat[order], t_flat[order], w_flat[order]
    counts = jnp.zeros((N_EXPERTS,), jnp.int32).at[e_flat].add(1)
    starts = jnp.cumsum(counts) - counts
    padded = (counts + MOE_BLOCK - 1) // MOE_BLOCK * MOE_BLOCK
    pad_ends = jnp.cumsum(padded)
    pad_starts = pad_ends - padded
    dest = pad_starts[e_s] + jnp.arange(e_s.shape[0], dtype=jnp.int32) - starts[e_s]
    n_pairs = n * TOP_K_IN_GROUP
    buf = (-(-n_pairs // MOE_BLOCK) + N_EXPERTS) * MOE_BLOCK
    tok_buf = jnp.full((buf,), n, jnp.int32).at[dest].set(t_s)
    w_buf = jnp.zeros((buf,), w_s.dtype).at[dest].set(w_s)
    n_blk = buf // MOE_BLOCK
    blk_exp = jnp.minimum(jnp.searchsorted(pad_ends, jnp.arange(n_blk, dtype=jnp.int32) * MOE_BLOCK, side='right'),
                          N_EXPERTS - 1)
    tok_pad = jnp.concatenate([tok, jnp.zeros((1, d), tok.dtype)], 0)

    def expert_block(args):
        idx, w, e = args
        xb = tok_pad[idx]
        hb = jax.nn.silu(xb @ ex_gate[e]) * (xb @ ex_up[e])
        return (hb @ ex_down[e]) * w[:, None]

    y = lax.map(expert_block, (tok_buf.reshape(n_blk, MOE_BLOCK), w_buf.reshape(n_blk, MOE_BLOCK), blk_exp))
    out = jnp.zeros((n + 1, d), y.dtype).at[tok_buf].add(y.reshape(buf, d))
    return out[:n].reshape(b, t, d).astype(h.dtype)


def kernel(x, c, ctx, c_ctx, w_ada, b_ada, w_in, conv_w, m_bias_i, m_bias_f, m_norm_w, r_w0, r_wB, r_a0, r_aB,
           r_gB, r_kk, r_ka, r_bonus, r_norm_w, r_norm_b, w_out, ln1_g, ln1_b, ln2_g, ln2_b, rt_g, rt_g_b, rt_e,
           rt_e_b, ex_gate, ex_up, ex_down):
    assert w_ada.shape[0] == DEPTH
    bsz, seq, d = x.shape
    ctx_len = ctx.shape[1]
    assert ctx_len == ROW_TILE and seq % ROW_TILE == 0
    assert seq % GRID_W == 0 and ROW_TILE % GRID_W == 0
    l = 0
    zrow = lambda n, w: jnp.zeros((n, w), jnp.float32)
    mrows = -(-(bsz + 1) // 8) * 8
    c_pad = jnp.concatenate([c, c_ctx[None, :], zrow(mrows - bsz - 1, d)], 0)
    mod_all = _ada(c_pad, w_ada[l], b_ada[l][None, :])[:bsz + 1]
    mod3 = mod_all.reshape(bsz + 1, 6, d)
    pc, pn = _inproj(ctx, x, mod3, _pad_w_in(w_in[l]))
    cw = jnp.concatenate([conv_w[l].reshape(CONV_K * CONV_K, CONV_CH), zrow(16 - CONV_K * CONV_K, CONV_CH)], 0)
    rpar = jnp.concatenate([r_w0[l], r_a0[l][None, :], r_kk[l][None, :], r_ka[l][None, :],
                            r_bonus[l].reshape(1, W_R), zrow(2, W_R)], 0)
    wbf = jnp.concatenate([r_wB[l][0], zrow(W_LORA, W_R)], 0)
    wbb = jnp.concatenate([zrow(W_LORA, W_R), r_wB[l][1]], 0)
    ab_mat = jnp.concatenate([r_aB[l], zrow(128 - A_LORA, W_R)], 0)
    q, k, rr, kmod, rv, kh, kha, logw_f, logw_b, gg, bonus = _prep(pc, pn, cw, rpar, wbf, wbb, ab_mat, r_gB[l])
    bias_row = jnp.concatenate([m_bias_i[l][0], m_bias_f[l][0], m_bias_i[l][1], m_bias_f[l][1],
                                jnp.zeros((128 - 4 * H_M,), jnp.float32)])[None, :]
    h_f, h_b = _mlstm_scan(q, k, pn, bias_row, ctx_len)
    y_f, y_b = _rwkv_scan(rr, kmod, rv, kh, kha, logw_f, logw_b, ctx_len)
    hpar = jnp.zeros((8, W_M), jnp.float32).at[0].set(m_norm_w[l]).at[1].set(r_norm_w[l]).at[2].set(r_norm_b[l])
    lnpar = jnp.zeros((8, d), jnp.float32).at[0].set(ln1_g[l]).at[1].set(ln1_b[l]).at[2].set(ln2_g[l]).at[3].set(
        ln2_b[l])
    rtw = jnp.concatenate([rt_g[l], rt_e[l], jnp.zeros((d, ROUTE_W - N_ROUTE), jnp.float32)], axis=1)
    rtb = jnp.concatenate([rt_g_b[l], rt_e_b[l], jnp.zeros((ROUTE_W - N_ROUTE,), jnp.float32)])[None, :]
    x1, tok, route, ridx, cnt = _outproj(h_f, h_b, pn, y_f, y_b, gg, bonus, x, mod3, w_out[l].astype(jnp.bfloat16),
                                         hpar, lnpar, rtw, rtb, ctx_len)
    n_tok = bsz * seq
    starts, blk_exp, nused, n_blk = _moe_plan(cnt, n_tok * TOP_K_IN_GROUP)
    xs = _moe_scatter(starts, ridx, tok.reshape(n_tok, d), jnp.zeros((n_blk * MOE_BLOCK, d), jnp.float32))
    ys = _moe_experts(blk_exp, nused, xs, ex_gate[l].astype(jnp.bfloat16), ex_up[l].astype(jnp.bfloat16),
                      ex_down[l].astype(jnp.bfloat16))
    return _moe_combine(starts, ridx, route, x1, mod3, lnpar, ys)
```

```python
import functools
import math

import jax
import jax.numpy as jnp
from jax import lax
import numpy as np
from jax.experimental import pallas as pl
from jax.experimental.pallas import tpu as pltpu

H_M = 8
D_HM = 64
W_M = H_M * D_HM
H_R = 8
D_HR = 64
W_R = H_R * D_HR
MIX_W = W_M + W_R
W_LORA = 64
A_LORA = 64
G_LORA = 128
CONV_K = 3
CONV_CH = 2 * W_M + 3 * W_R
SECTION_WIDTHS = (W_M, W_M, W_R, W_R, W_R, W_M, W_M, 4 * H_M, W_LORA, W_LORA, A_LORA, G_LORA)
IN_COLS = sum(SECTION_WIDTHS)
SPLIT_POINTS = tuple(int(v) for v in np.cumsum(SECTION_WIDTHS)[:-1])
CHUNK = 64
N_GROUPS = 4
E_PER_GROUP = 8
N_EXPERTS = N_GROUPS * E_PER_GROUP
TOP_K_IN_GROUP = 2
D_EXPERT = 512
MOE_BLOCK = 256
DEPTH = 1
ALPHA = (2.0 * DEPTH) ** 0.25
DECAY_SCALE = math.exp(-0.5)
LN_EPS = 1e-6
GN_EPS = 64e-5

ROW_TILE = 256
NONCONV_W = 1536
VMEM_LIMIT = 56 * 1024 * 1024


def _inproj_kernel(ctx_ref, x_ref, mod_ref, w_ref, oc_ref, on_ref):
    j = pl.program_id(1)

    def body(src_ref):
        z = src_ref[0]
        mu = jnp.mean(z, -1, keepdims=True)
        zc = z - mu
        var = jnp.mean(zc * zc, -1, keepdims=True)
        y = zc * lax.rsqrt(var + LN_EPS)
        h = (y * (1.0 + mod_ref[0, 1:2, :]) + mod_ref[0, 0:1, :]).astype(jnp.bfloat16)
        oc_ref[0] = jnp.dot(h, w_ref[:, :CONV_CH], preferred_element_type=jnp.float32)
        on_ref[0] = jnp.dot(h, w_ref[:, CONV_CH:], preferred_element_type=jnp.float32)

    @pl.when(j == 0)
    def _():
        body(ctx_ref)

    @pl.when(j > 0)
    def _():
        body(x_ref)


def _inproj(ctx, x, mod_all, w_pad):
    b, seq, d = x.shape
    nt = (ctx.shape[1] + seq) // ROW_TILE
    t_all = ctx.shape[1] + seq
    return pl.pallas_call(
        _inproj_kernel,
        grid=(b, nt),
        in_specs=[
            pl.BlockSpec((1, ROW_TILE, d), lambda i, j: (i, 0, 0)),
            pl.BlockSpec((1, ROW_TILE, d), lambda i, j: (i, jnp.maximum(j - 1, 0), 0)),
            pl.BlockSpec((1, 6, d), lambda i, j: (jnp.where(j == 0, b, i), 0, 0)),
            pl.BlockSpec((d, CONV_CH + NONCONV_W), lambda i, j: (0, 0)),
        ],
        out_specs=[
            pl.BlockSpec((1, ROW_TILE, CONV_CH), lambda i, j: (i, j, 0)),
            pl.BlockSpec((1, ROW_TILE, NONCONV_W), lambda i, j: (i, j, 0)),
        ],
        out_shape=[
            jax.ShapeDtypeStruct((b, t_all, CONV_CH), jnp.float32),
            jax.ShapeDtypeStruct((b, t_all, NONCONV_W), jnp.float32),
        ],
        compiler_params=pltpu.CompilerParams(
            dimension_semantics=("arbitrary", "arbitrary"), vmem_limit_bytes=VMEM_LIMIT),
        name="inproj",
    )(ctx, x, mod_all, w_pad)


def _pad_w_in(w_in):
    d = w_in.shape[0]
    sp = SPLIT_POINTS
    z = lambda n: jnp.zeros((d, n), w_in.dtype)
    gates = w_in[:, sp[6]:sp[7]]
    lw = w_in[:, sp[7]:sp[9]]
    la = w_in[:, sp[9]:sp[10]]
    lg = w_in[:, sp[10]:]
    return jnp.concatenate([w_in[:, :sp[6]], gates, z(96), lw, la, z(64), lg], axis=1).astype(jnp.bfloat16)


PAIR = 2 * D_HR
NPAIR = H_R // 2
_NN = (((1,), (0,)), ((), ()))
_NT = (((1,), (1,)), ((), ()))
_TN = (((0,), (0,)), ((), ()))


def _split2(a):
    hi = a.astype(jnp.bfloat16)
    lo = (a - hi.astype(jnp.float32)).astype(jnp.bfloat16)
    return hi, lo


def _split3(a):
    a1 = a.astype(jnp.bfloat16)
    r1 = a - a1.astype(jnp.float32)
    a2 = r1.astype(jnp.bfloat16)
    a3 = (r1 - a2.astype(jnp.float32)).astype(jnp.bfloat16)
    return a1, a2, a3


def _dg(a, b, dims):
    return lax.dot_general(a, b, dims, preferred_element_type=jnp.float32)


def _dot3(a, b, dims=_NN):
    ah, al = _split2(a)
    bh, bl = _split2(b)
    return _dg(ah, bh, dims) + _dg(ah, bl, dims) + _dg(al, bh, dims)


def _dot3c(a, b, dims=_NN):
    ah, al = _split2(a)
    bh, bl = _split2(b)
    ca, cb = dims[0][0][0], dims[0][1][0]
    return _dg(jnp.concatenate([ah, ah, al], axis=ca), jnp.concatenate([bh, bl, bh], axis=cb), dims)


def _dot_exact_lhs(a_bf16, b, dims=_NN):
    b1, b2, b3 = _split3(b)
    return _dg(a_bf16, b1, dims) + _dg(a_bf16, b2, dims) + _dg(a_bf16, b3, dims)


def _dot_exact_rhs(a, b_bf16):
    a1, a2, a3 = _split3(a)
    return _dg(a1, b_bf16, _NN) + _dg(a2, b_bf16, _NN) + _dg(a3, b_bf16, _NN)


def _head_block_diag(width, head, value):
    hrow = lax.broadcasted_iota(jnp.int32, (width, width), 0) // head
    hcol = lax.broadcasted_iota(jnp.int32, (width, width), 1) // head
    return jnp.where(hrow == hcol, value, 0.0).astype(jnp.bfloat16)


def _ada_kernel(c_ref, w_ref, b_ref, o_ref):
    o_ref[...] = _dot3(jax.nn.silu(c_ref[...]), w_ref[...]) + b_ref[...]


def _ada(c_pad, w, bias):
    m, d = c_pad.shape
    n = w.shape[1]
    return pl.pallas_call(
        _ada_kernel,
        grid=(n // d,),
        in_specs=[pl.BlockSpec((m, d), lambda j: (0, 0)),
                  pl.BlockSpec((d, d), lambda j: (0, j)),
                  pl.BlockSpec((1, d), lambda j: (0, j))],
        out_specs=pl.BlockSpec((m, d), lambda j: (0, j)),
        out_shape=jax.ShapeDtypeStruct((m, n), jnp.float32),
        compiler_params=pltpu.CompilerParams(dimension_semantics=("arbitrary",), vmem_limit_bytes=VMEM_LIMIT),
        name="adaln",
    )(c_pad, w, bias)


GRID_W = 64
HALO = GRID_W
XOFF = 8


def _prep_kernel(top_ref, main_ref, bot_ref, pn_ref, cw_ref, rpar_ref, wbf_ref, wbb_ref, ab_ref, gb_ref,
                 q_ref, k_ref, r_ref, km_ref, v_ref, kh_ref, abo_ref, lwf_ref, lwb_ref, gg_ref, bon_ref, xbuf):
    j = pl.program_id(1)
    nt = pl.num_programs(1)
    is_ctx = j == 0
    top_ok = j >= 2
    bot_ok = (j >= 1) & (j < nt - 1)
    l_idx = lax.broadcasted_iota(jnp.int32, (ROW_TILE, 1), 0)
    col = jnp.where(is_ctx, l_idx, l_idx % GRID_W)
    left_ok = col != 0
    right_ok = col != jnp.where(is_ctx, ROW_TILE - 1, GRID_W - 1)
    vert = jnp.where(is_ctx, 0.0, 1.0)
    xbuf[0:XOFF, :] = jnp.zeros((XOFF, W_M), jnp.float32)
    xbuf[XOFF + 2 * HALO + ROW_TILE:, :] = jnp.zeros((XOFF, W_M), jnp.float32)
    sec = {}
    for s in range(5):
        sl = slice(s * W_M, (s + 1) * W_M)
        xbuf[XOFF:XOFF + HALO, :] = jnp.where(top_ok, top_ref[0, :, sl], 0.0)
        xbuf[XOFF + HALO:XOFF + HALO + ROW_TILE, :] = main_ref[0, :, sl]
        xbuf[XOFF + HALO + ROW_TILE:XOFF + 2 * HALO + ROW_TILE, :] = jnp.where(bot_ok, bot_ref[0, :, sl], 0.0)
        acc = None
        for dr in range(3):
            base = XOFF + HALO + GRID_W * (dr - 1)
            scale = 1.0 if dr == 1 else vert
            left = jnp.where(left_ok, xbuf[base - 1:base - 1 + ROW_TILE, :], 0.0)
            mid = xbuf[base:base + ROW_TILE, :]
            right = jnp.where(right_ok, xbuf[base + 1:base + 1 + ROW_TILE, :], 0.0)
            term = (left * cw_ref[3 * dr:3 * dr + 1, sl] + mid * cw_ref[3 * dr + 1:3 * dr + 2, sl]
                    + right * cw_ref[3 * dr + 2:3 * dr + 3, sl]) * scale
            acc = term if acc is None else acc + term
        sec[s] = acc
    q_ref[0] = jax.nn.silu(sec[0])
    k_ref[0] = jax.nn.silu(sec[1]) * (D_HM ** -0.5)
    rr, rk, rv = sec[2], sec[3], sec[4]
    r_ref[0] = rr
    v_ref[0] = rv
    lw = jnp.tanh(pn_ref[0, :, 1152:1280])
    lwf_ref[0] = -DECAY_SCALE * jax.nn.sigmoid(rpar_ref[0:1, :] + _dot3(lw, wbf_ref[...]))
    lwb_ref[0] = -DECAY_SCALE * jax.nn.sigmoid(rpar_ref[1:2, :] + _dot3(lw, wbb_ref[...]))
    a = jax.nn.sigmoid(rpar_ref[2:3, :] + _dot3(pn_ref[0, :, 1280:1408], ab_ref[...]))
    gg_ref[0] = _dot3(jax.nn.sigmoid(pn_ref[0, :, 1408:1536]), gb_ref[...])
    ones_bd = _head_block_diag(W_R, D_HR, 1.0)
    kap = rk * rpar_ref[3:4, :]
    norm = jnp.sqrt(_dot_exact_rhs(kap * kap, ones_bd))
    kh = kap / jnp.maximum(norm, 1e-12)
    kmod = rk * (1.0 + (a - 1.0) * rpar_ref[4:5, :])
    kh_ref[0] = kh
    abo_ref[0] = kh * a
    km_ref[0] = kmod
    bon_ref[0] = _dot_exact_rhs(rr * kmod * rpar_ref[5:6, :], ones_bd) * rv


def _prep(pc, pn, cw, rpar, wbf, wbb, ab, gb):
    b, t, _ = pc.shape
    nt = t // ROW_TILE
    per = ROW_TILE // HALO
    nh = t // HALO
    w = W_M
    full = lambda shp: pl.BlockSpec(shp, lambda i, j: tuple(0 for _ in shp))
    in_specs = [
        pl.BlockSpec((1, HALO, CONV_CH), lambda i, j: (i, jnp.maximum(j * per - 1, 0), 0)),
        pl.BlockSpec((1, ROW_TILE, CONV_CH), lambda i, j: (i, j, 0)),
        pl.BlockSpec((1, HALO, CONV_CH), lambda i, j: (i, jnp.minimum(j * per + per, nh - 1), 0)),
        pl.BlockSpec((1, ROW_TILE, NONCONV_W), lambda i, j: (i, j, 0)),
        full((16, CONV_CH)), full((8, w)), full((128, w)), full((128, w)), full((128, w)), full((128, w)),
    ]
    out = pl.BlockSpec((1, ROW_TILE, w), lambda i, j: (i, j, 0))
    return pl.pallas_call(
        _prep_kernel,
        grid=(b, nt),
        in_specs=in_specs,
        out_specs=[out] * 11,
        out_shape=[jax.ShapeDtypeStruct((b, t, w), jnp.float32)] * 11,
        scratch_shapes=[pltpu.VMEM((2 * XOFF + 2 * HALO + ROW_TILE, w), jnp.float32)],
        compiler_params=pltpu.CompilerParams(
            dimension_semantics=("arbitrary", "arbitrary"), vmem_limit_bytes=VMEM_LIMIT),
        name="conv_prep",
    )(pc, pc, pc, pn, cw, rpar, wbf, wbb, ab, gb)


def _round_robin(gens):
    out = [None] * len(gens)
    live = list(range(len(gens)))
    while live:
        nxt = []
        for i in live:
            try:
                next(gens[i])
                nxt.append(i)
            except StopIteration as stop:
                out[i] = stop.value
        live = nxt
    return out


def _rwkv_pair_chunk(r, k, v, kh, ab, lw, s_mat, tri_incl, strict, incl, last_row, m0, m1):
    cw = _dot_exact_lhs(tri_incl, lw)
    yield
    e_pos = jnp.exp(cw)
    e_neg = jnp.exp(-cw)
    e_prev = jnp.exp(cw - lw)
    stack = lambda a: jnp.concatenate([a * m0, a * m1], axis=0)
    kt = stack(kh * e_prev)
    bt = stack(ab * e_neg)
    kk = stack(k * e_neg)
    rt = stack(r * e_pos)
    vs = stack(v)
    zero = jnp.zeros((), jnp.float32)
    kr = jnp.concatenate([kt, rt], axis=0)
    bk = jnp.concatenate([bt, kk], axis=0)
    amat = _dot3c(kr, bk, _NT)
    yield
    a_bk = jnp.where(strict, amat[:PAIR, :PAIR], zero)
    a_kk = jnp.where(strict, amat[:PAIR, PAIR:], zero)
    a_rb = jnp.where(incl, amat[PAIR:, :PAIR], zero)
    a_rk = jnp.where(incl, amat[PAIR:, PAIR:], zero)
    n_mat = -a_bk
    eye = (lax.broadcasted_iota(jnp.int32, (PAIR, PAIR), 0)
           == lax.broadcasted_iota(jnp.int32, (PAIR, PAIR), 1)).astype(jnp.float32)
    q = eye + n_mat
    p = _dot3c(n_mat, n_mat)
    ks = _dot3c(kr, s_mat, _NT)
    av = _dot3c(jnp.concatenate([a_kk, a_rk], axis=0), vs)
    yield
    for _ in range(4):
        qp = _dot3c(jnp.concatenate([q, p], axis=0), p)
        yield
        q = q + qp[:PAIR]
        p = qp[PAIR:]
    t_inv = q + _dot3c(q, p)
    yield
    u = -_dot3c(t_inv, ks[:PAIR] + av[:PAIR])
    yield
    y = ks[PAIR:] + av[PAIR:] + _dot3c(a_rb, u)
    yield
    w_last = jnp.sum(jnp.where(last_row, e_pos, zero), axis=0, keepdims=True)
    s_new = (s_mat + _dot3c(jnp.concatenate([u, vs], axis=0), bk, _TN)) * w_last
    return y[:CHUNK] + y[CHUNK:], s_new


def _rwkv_kernel(rf, kf, vf, khf, abf, lwf, rb, kb, vb, khb, abb, lwb, yf_ref, yb_ref, s_ref, *, nctx):
    j = pl.program_id(1)

    @pl.when(j == 0)
    def _():
        s_ref[...] = jnp.zeros_like(s_ref)

    row = lax.broadcasted_iota(jnp.int32, (PAIR, PAIR), 0)
    col = lax.broadcasted_iota(jnp.int32, (PAIR, PAIR), 1)
    same = (row // CHUNK) == (col // CHUNK)
    r64 = lax.broadcasted_iota(jnp.int32, (CHUNK, CHUNK), 0)
    c64 = lax.broadcasted_iota(jnp.int32, (CHUNK, CHUNK), 1)
    rowl = lax.broadcasted_iota(jnp.int32, (CHUNK, PAIR), 0)
    lane = lax.broadcasted_iota(jnp.int32, (1, PAIR), 1)
    m0 = (lane < D_HR).astype(jnp.float32)
    m1 = 1.0 - m0
    dirs = (
        (rf, kf, vf, khf, abf, lwf, yf_ref, (c64 <= r64), same & (col < row), same & (col <= row), rowl == CHUNK - 1),
        (rb, kb, vb, khb, abb, lwb, yb_ref, (c64 >= r64), same & (col > row), same & (col >= row), rowl == 0),
    )
    states = [[s_ref[d, p] for p in range(NPAIR)] for d in range(2)]
    gens, meta = [], []
    for d, (r_ref, k_ref, v_ref, kh_ref, ab_ref, lw_ref, y_ref, tri, strict, incl, last_row) in enumerate(dirs):
        tri = tri.astype(jnp.float32).astype(jnp.bfloat16)
        for p in range(NPAIR):
            sl = slice(p * PAIR, (p + 1) * PAIR)
            gens.append(_rwkv_pair_chunk(r_ref[0, :, sl], k_ref[0, :, sl], v_ref[0, :, sl], kh_ref[0, :, sl],
                                         ab_ref[0, :, sl], lw_ref[0, :, sl], states[d][p], tri, strict, incl,
                                         last_row, m0, m1))
            meta.append((d, p, sl, y_ref))
    results = [m + r for m, r in zip(meta, _round_robin(gens))]
    for d, p, sl, y_ref, y, s_new in results:
        s_ref[d, p] = s_new
        y_ref[0, :, sl] = y


def _rwkv_scan(r, k, v, kh, ab, lw_f, lw_b, ctx_len):
    b, t, w = r.shape
    nc = t // CHUNK
    nctx = ctx_len // CHUNK
    nlat = nc - nctx

    def fwd_map(i, j):
        return (i, j, 0)

    def bwd_map(i, j):
        return (i, jnp.where(j < nctx, nctx - 1 - j, nc - 1 - (j - nctx)), 0)

    blk = (1, CHUNK, w)
    in_specs = [pl.BlockSpec(blk, fwd_map)] * 6 + [pl.BlockSpec(blk, bwd_map)] * 6
    out_specs = [
        pl.BlockSpec(blk, lambda i, j: (i, jnp.maximum(j - nctx, 0), 0)),
        pl.BlockSpec(blk, lambda i, j: (i, nlat - 1 - jnp.maximum(j - nctx, 0), 0)),
    ]
    return pl.pallas_call(
        functools.partial(_rwkv_kernel, nctx=nctx),
        grid=(b, nc),
        in_specs=in_specs,
        out_specs=out_specs,
        out_shape=[jax.ShapeDtypeStruct((b, t - ctx_len, w), jnp.float32)] * 2,
        scratch_shapes=[pltpu.VMEM((2, NPAIR, PAIR, PAIR), jnp.float32)],
        compiler_params=pltpu.CompilerParams(
            dimension_semantics=("arbitrary", "arbitrary"), vmem_limit_bytes=VMEM_LIMIT),
        name="rwkv_scan",
    )(r, k, v, kh, ab, lw_f, r, k, v, kh, ab, lw_b)


def _col_stack(a, c0, c1):
    return jnp.concatenate([a[:, c0:c0 + 1], a[:, c1:c1 + 1]], axis=0)


def _half_bcast(v0, v1):
    return jnp.concatenate([jnp.broadcast_to(v0, (CHUNK, 1)), jnp.broadcast_to(v1, (CHUNK, 1))], axis=0)


def _mlstm_pair_chunk(q, k, v, b_col, li_col, c_row, c_mat, n_row, m_row, incl, last_sel, m0, m1):
    stack = lambda a: jnp.concatenate([a * m0, a * m1], axis=0)
    q_st, k_st, v_st = stack(q), stack(k), stack(v)
    lane = lax.broadcasted_iota(jnp.int32, (1, PAIR), 1)
    first = lane < D_HM
    neg_inf = jnp.full((), -jnp.inf, jnp.float32)
    m_h0, m_h1 = m_row[:, 0:1], m_row[:, D_HM:D_HM + 1]
    m_prev = _half_bcast(m_h0, m_h1)
    dmat = jnp.where(incl, b_col + c_row, neg_inf)
    m_inter = b_col + m_prev
    m_t = jnp.maximum(m_inter, jnp.max(dmat, axis=1, keepdims=True))
    bl = jnp.where(last_sel, b_col, neg_inf)
    bl0 = jnp.max(bl[:CHUNK], axis=0, keepdims=True)
    bl1 = jnp.max(bl[CHUNK:], axis=0, keepdims=True)
    g = _half_bcast(bl0, bl1) - b_col + li_col
    mn0 = jnp.maximum(bl0 + m_h0, jnp.max(g[:CHUNK], axis=0, keepdims=True))
    mn1 = jnp.maximum(bl1 + m_h1, jnp.max(g[CHUNK:], axis=0, keepdims=True))
    wts = jnp.exp(g - _half_bcast(mn0, mn1))
    decay_row = jnp.where(first, jnp.exp(bl0 + m_h0 - mn0), jnp.exp(bl1 + m_h1 - mn1))
    wk = wts * k_st
    qk = _dot3c(q_st, k_st, _NT)
    qc = _dot3c(q_st, c_mat)
    kv = _dot3c(wk, v_st, _TN)
    yield
    s = qk * jnp.exp(dmat - m_t)
    inter = jnp.exp(m_inter - m_t)
    sv = _dot3c(s, v_st)
    yield
    num = inter * qc + sv
    den = inter * jnp.sum(q_st * n_row, axis=1, keepdims=True) + jnp.sum(s, axis=1, keepdims=True)
    h_st = num / jnp.maximum(jnp.abs(den), jnp.exp(-m_t))
    h = h_st[:CHUNK] + h_st[CHUNK:]
    c_new = decay_row * c_mat + kv
    n_new = decay_row * n_row + jnp.sum(wk, axis=0, keepdims=True)
    m_new = jnp.where(first, mn0, mn1)
    return h, c_new, n_new, m_new


def _mlstm_kernel(qf, kf, vf, gf, qb, kb, vb, gb, bias_ref, hf_ref, hb_ref, c_ref, n_ref, m_ref, *, nctx):
    j = pl.program_id(1)

    @pl.when(j == 0)
    def _():
        c_ref[...] = jnp.zeros_like(c_ref)
        n_ref[...] = jnp.zeros_like(n_ref)
        m_ref[...] = jnp.zeros_like(m_ref)

    row = lax.broadcasted_iota(jnp.int32, (PAIR, PAIR), 0)
    col = lax.broadcasted_iota(jnp.int32, (PAIR, PAIR), 1)
    same = (row // CHUNK) == (col // CHUNK)
    r64 = lax.broadcasted_iota(jnp.int32, (CHUNK, CHUNK), 0)
    c64 = lax.broadcasted_iota(jnp.int32, (CHUNK, CHUNK), 1)
    rcol = lax.broadcasted_iota(jnp.int32, (PAIR, 1), 0) % CHUNK
    lane = lax.broadcasted_iota(jnp.int32, (1, PAIR), 1)
    m0 = (lane < D_HM).astype(jnp.float32)
    m1 = 1.0 - m0
    is_f = (lane % (2 * H_M)) >= H_M
    dirs = (
        (qf, kf, vf, gf, hf_ref, (c64 <= r64), same & (col <= row), rcol == CHUNK - 1),
        (qb, kb, vb, gb, hb_ref, (c64 >= r64), same & (col >= row), rcol == 0),
    )
    states = [[(c_ref[d, p], n_ref[d, p], m_ref[d, p]) for p in range(NPAIR)] for d in range(2)]
    gens, meta = [], []
    for d, (q_ref, k_ref, v_ref, g_ref, h_ref, tri, incl, last_sel) in enumerate(dirs):
        tri = tri.astype(jnp.float32).astype(jnp.bfloat16)
        gl = g_ref[0] + bias_ref[...]
        act = jnp.where(is_f, jax.nn.log_sigmoid(gl), gl)
        b_all = _dot_exact_lhs(tri, act)
        act_t = act.T
        b_t = b_all.T
        for p in range(NPAIR):
            sl = slice(p * PAIR, (p + 1) * PAIR)
            ci, cf = d * 2 * H_M + 2 * p, d * 2 * H_M + H_M + 2 * p
            b_col = _col_stack(b_all, cf, cf + 1)
            li_col = _col_stack(act, ci, ci + 1)
            c_row = jnp.concatenate([act_t[ci:ci + 1, :] - b_t[cf:cf + 1, :],
                                     act_t[ci + 1:ci + 2, :] - b_t[cf + 1:cf + 2, :]], axis=1)
            gens.append(_mlstm_pair_chunk(
                q_ref[0, :, sl], k_ref[0, :, sl], v_ref[0, :, sl], b_col, li_col, c_row,
                *states[d][p], incl, last_sel, m0, m1))
            meta.append((d, p, sl, h_ref))
    for (d, p, sl, h_ref), (h, c_new, n_new, m_new) in zip(meta, _round_robin(gens)):
        c_ref[d, p] = c_new
        n_ref[d, p] = n_new
        m_ref[d, p] = m_new
        h_ref[0, :, sl] = h


def _mlstm_scan(q, k, pn, bias_row, ctx_len):
    v = gates = pn
    b, t, w = q.shape
    nc = t // CHUNK
    nctx = ctx_len // CHUNK
    nlat = nc - nctx

    def fwd_map(i, j):
        return (i, j, 0)

    def bwd_map(i, j):
        return (i, jnp.where(j < nctx, nctx - 1 - j, nc - 1 - (j - nctx)), 0)

    blk = (1, CHUNK, w)
    gblk = (1, CHUNK, 128)
    gate_blk = (2 * W_M) // 128
    gate_of = lambda m: (lambda i, j: m(i, j)[:2] + (gate_blk,))
    in_specs = ([pl.BlockSpec(blk, fwd_map)] * 3 + [pl.BlockSpec(gblk, gate_of(fwd_map))]
                + [pl.BlockSpec(blk, bwd_map)] * 3 + [pl.BlockSpec(gblk, gate_of(bwd_map))]
                + [pl.BlockSpec((1, 128), lambda i, j: (0, 0))])
    out_specs = [
        pl.BlockSpec(blk, lambda i, j: (i, jnp.maximum(j - nctx, 0), 0)),
        pl.BlockSpec(blk, lambda i, j: (i, nlat - 1 - jnp.maximum(j - nctx, 0), 0)),
    ]
    return pl.pallas_call(
        functools.partial(_mlstm_kernel, nctx=nctx),
        grid=(b, nc),
        in_specs=in_specs,
        out_specs=out_specs,
        out_shape=[jax.ShapeDtypeStruct((b, t - ctx_len, w), jnp.float32)] * 2,
        scratch_shapes=[pltpu.VMEM((2, NPAIR, PAIR, PAIR), jnp.float32),
                        pltpu.VMEM((2, NPAIR, 1, PAIR), jnp.float32),
                        pltpu.VMEM((2, NPAIR, 1, PAIR), jnp.float32)],
        compiler_params=pltpu.CompilerParams(
            dimension_semantics=("arbitrary", "arbitrary"), vmem_limit_bytes=VMEM_LIMIT),
        name="mlstm_scan",
    )(q, k, v, gates, q, k, v, gates, bias_row)


ROUTE_W = 128
N_ROUTE = N_GROUPS + N_EXPERTS


def _ln_rows(z, eps):
    mu = jnp.mean(z, -1, keepdims=True)
    zc = z - mu
    var = jnp.mean(zc * zc, -1, keepdims=True)
    return zc * lax.rsqrt(var + eps)


def _headnorm_mxu(z, avg_bf16, eps):
    mu = _dot_exact_rhs(z, avg_bf16)
    zc = z - mu
    var = _dot_exact_rhs(zc * zc, avg_bf16)
    return zc * lax.rsqrt(var + eps)


def _outproj_kernel(hf, hb, mo, yf, yb, gg, bonus, x_ref, mod_ref, wout_ref, hpar_ref, ln_ref, rtw_ref, rtb_ref,
                    x1_ref, tok_ref, route_ref, ridx_ref, cnt_ref, carry_ref):
    i, j = pl.program_id(0), pl.program_id(1)

    @pl.when((i == 0) & (j == 0))
    def _():
        carry_ref[...] = jnp.zeros_like(carry_ref)

    avg = _head_block_diag(W_M, D_HM, 1.0 / D_HM)
    out_m = jax.nn.sigmoid(mo[0]) * (_headnorm_mxu(hf[0] + hb[0], avg, LN_EPS) * hpar_ref[0:1, :])
    y = _headnorm_mxu(yf[0] + yb[0], avg, GN_EPS) * hpar_ref[1:2, :] + hpar_ref[2:3, :]
    out_r = (y + bonus[0]) * gg[0]
    proj = (jnp.dot(out_m.astype(jnp.bfloat16), wout_ref[:W_M, :], preferred_element_type=jnp.float32)
            + jnp.dot(out_r.astype(jnp.bfloat16), wout_ref[W_M:, :], preferred_element_type=jnp.float32))
    g1, sh2, sc2 = mod_ref[0, 2:3, :], mod_ref[0, 3:4, :], mod_ref[0, 4:5, :]
    x1 = _ln_rows(ALPHA * x_ref[0] + g1 * proj, LN_EPS) * ln_ref[0:1, :] + ln_ref[1:2, :]
    x1_ref[0] = x1
    tok = _ln_rows(x1, LN_EPS) * (1.0 + sc2) + sh2
    tok_ref[0] = tok

    logits = _dot3(tok, rtw_ref[...]) + rtb_ref[...]
    lane = lax.broadcasted_iota(jnp.int32, (ROW_TILE, ROUTE_W), 1)
    neg_inf = jnp.full((), -jnp.inf, jnp.float32)
    big = jnp.int32(ROUTE_W)
    is_g = lane < N_GROUPS
    lg = jnp.where(is_g, logits, neg_inf)
    gmax = jnp.max(lg, axis=1, keepdims=True)
    grp = jnp.min(jnp.where(lg == gmax, lane, big), axis=1, keepdims=True)
    p_top = 1.0 / jnp.sum(jnp.where(is_g, jnp.exp(lg - gmax), 0.0), axis=1, keepdims=True)
    in_grp = (lane >= N_GROUPS) & (lane < N_ROUTE) & ((lane - N_GROUPS) // E_PER_GROUP == grp)
    le = jnp.where(in_grp, logits, neg_inf)
    v1 = jnp.max(le, axis=1, keepdims=True)
    i1 = jnp.min(jnp.where(le == v1, lane, big), axis=1, keepdims=True)
    le2 = jnp.where(lane == i1, neg_inf, le)
    v2 = jnp.max(le2, axis=1, keepdims=True)
    i2 = jnp.min(jnp.where(le2 == v2, lane, big), axis=1, keepdims=True)
    e21 = jnp.exp(v2 - v1)
    w1 = (1.0 / (1.0 + e21)) * p_top
    w2 = (e21 / (1.0 + e21)) * p_top

    sel1, sel2 = lane == i1, lane == i2
    onehot = jnp.where(sel1 | sel2, 1.0, 0.0)
    tr = lax.broadcasted_iota(jnp.int32, (ROW_TILE, ROW_TILE), 0)
    tc = lax.broadcasted_iota(jnp.int32, (ROW_TILE, ROW_TILE), 1)
    strict = jnp.where(tc < tr, 1.0, 0.0).astype(jnp.bfloat16)
    before = _dg(strict, onehot.astype(jnp.bfloat16), _NN) + carry_ref[...]
    r1 = jnp.sum(jnp.where(sel1, before, 0.0), axis=1, keepdims=True)
    r2 = jnp.sum(jnp.where(sel2, before, 0.0), axis=1, keepdims=True)
    carry_ref[...] = carry_ref[...] + jnp.sum(onehot, axis=0, keepdims=True)
    e1 = (i1 - N_GROUPS).astype(jnp.float32)
    e2 = (i2 - N_GROUPS).astype(jnp.float32)
    route = jnp.where(lane == 0, e1, jnp.where(lane == 1, e2, jnp.where(lane == 2, r1, jnp.where(
        lane == 3, r2, jnp.where(lane == 4, w1, jnp.where(lane == 5, w2, 0.0))))))
    route_ref[0] = route
    ridx_ref[0] = route.T[0:8, :].astype(jnp.int32)
    cnt_ref[...] = jnp.broadcast_to(carry_ref[...], cnt_ref.shape)


def _outproj(h_f, h_b, pn, y_f, y_b, gg, bonus, x, mod3, wout_bf16, hpar, lnpar, rtw, rtb, ctx_len):
    b, seq, d = x.shape
    nt = seq // ROW_TILE
    off = ctx_len // ROW_TILE
    lat = lambda w: pl.BlockSpec((1, ROW_TILE, w), lambda i, j: (i, j, 0))
    full = lambda shp: pl.BlockSpec(shp, lambda i, j: tuple(0 for _ in shp))
    in_specs = [
        lat(W_M), lat(W_M),
        pl.BlockSpec((1, ROW_TILE, W_M), lambda i, j: (i, j + off, 1)),
        lat(W_R), lat(W_R),
        pl.BlockSpec((1, ROW_TILE, W_R), lambda i, j: (i, j + off, 0)),
        pl.BlockSpec((1, ROW_TILE, W_R), lambda i, j: (i, j + off, 0)),
        lat(d),
        pl.BlockSpec((1, 6, d), lambda i, j: (i, 0, 0)),
        full((MIX_W, d)), full((8, W_M)), full((8, d)), full((d, ROUTE_W)), full((1, ROUTE_W)),
    ]
    out_specs = [
        lat(d), lat(d), lat(ROUTE_W),
        pl.BlockSpec((1, 8, ROW_TILE), lambda i, j: (i * nt + j, 0, 0)),
        full((8, ROUTE_W)),
    ]
    out_shape = [
        jax.ShapeDtypeStruct((b, seq, d), jnp.float32),
        jax.ShapeDtypeStruct((b, seq, d), jnp.float32),
        jax.ShapeDtypeStruct((b, seq, ROUTE_W), jnp.float32),
        jax.ShapeDtypeStruct((b * nt, 8, ROW_TILE), jnp.int32),
        jax.ShapeDtypeStruct((8, ROUTE_W), jnp.float32),
    ]
    return pl.pallas_call(
        _outproj_kernel,
        grid=(b, nt),
        in_specs=in_specs,
        out_specs=out_specs,
        out_shape=out_shape,
        scratch_shapes=[pltpu.VMEM((1, ROUTE_W), jnp.float32)],
        compiler_params=pltpu.CompilerParams(
            dimension_semantics=("arbitrary", "arbitrary"), vmem_limit_bytes=VMEM_LIMIT),
        name="outproj_router",
    )(h_f, h_b, pn, y_f, y_b, gg, bonus, x, mod3, wout_bf16, hpar, lnpar, rtw, rtb)


def _pair_dest(ridx_ref, starts_ref, r, slot):
    return starts_ref[ridx_ref[0, slot, r]] + ridx_ref[0, 2 + slot, r]


def _scatter_kernel(starts_ref, ridx_ref, tok_ref, xs_in_ref, xs_ref, sem):
    del xs_in_ref

    def row_copy(r, slot):
        dst = _pair_dest(ridx_ref, starts_ref, r, slot)
        return pltpu.make_async_copy(tok_ref.at[pl.ds(r, 1), :], xs_ref.at[pl.ds(dst, 1), :], sem)

    def issue(r, c):
        row_copy(r, 0).start()
        row_copy(r, 1).start()
        return c

    def drain(r, c):
        pltpu.make_async_copy(tok_ref.at[pl.ds(0, 1), :], xs_ref.at[pl.ds(0, 1), :], sem).wait()
        return c

    lax.fori_loop(0, ROW_TILE, issue, 0)
    lax.fori_loop(0, 2 * ROW_TILE, drain, 0)


def _moe_scatter(starts, ridx, tok2d, xs_zero):
    n, d = tok2d.shape
    nt = n // ROW_TILE
    return pl.pallas_call(
        _scatter_kernel,
        grid_spec=pltpu.PrefetchScalarGridSpec(
            num_scalar_prefetch=1,
            grid=(nt,),
            in_specs=[
                pl.BlockSpec((1, 8, ROW_TILE), lambda i, s: (i, 0, 0), memory_space=pltpu.SMEM),
                pl.BlockSpec((ROW_TILE, d), lambda i, s: (i, 0)),
                pl.BlockSpec(memory_space=pl.ANY),
            ],
            out_specs=pl.BlockSpec(memory_space=pl.ANY),
            scratch_shapes=[pltpu.SemaphoreType.DMA(())],
        ),
        out_shape=jax.ShapeDtypeStruct(xs_zero.shape, xs_zero.dtype),
        input_output_aliases={3: 0},
        compiler_params=pltpu.CompilerParams(dimension_semantics=("arbitrary",), vmem_limit_bytes=VMEM_LIMIT),
        name="moe_scatter",
    )(starts, ridx, tok2d, xs_zero)


def _expert_kernel(blk_exp_ref, nused_ref, xs_ref, wg_ref, wu_ref, wd_ref, ys_ref):
    i = pl.program_id(0)

    @pl.when(i < nused_ref[0])
    def _():
        xb = xs_ref[...].astype(jnp.bfloat16)
        hg = jnp.dot(xb, wg_ref[0], preferred_element_type=jnp.float32)
        hu = jnp.dot(xb, wu_ref[0], preferred_element_type=jnp.float32)
        hb = (jax.nn.silu(hg) * hu).astype(jnp.bfloat16)
        ys_ref[...] = jnp.dot(hb, wd_ref[0], preferred_element_type=jnp.float32)

    @pl.when(i >= nused_ref[0])
    def _():
        ys_ref[...] = jnp.zeros_like(ys_ref)


def _moe_experts(blk_exp, nused, xs, wg, wu, wd):
    nrow, d = xs.shape
    de = wg.shape[2]
    return pl.pallas_call(
        _expert_kernel,
        grid_spec=pltpu.PrefetchScalarGridSpec(
            num_scalar_prefetch=2,
            grid=(nrow // MOE_BLOCK,),
            in_specs=[
                pl.BlockSpec((MOE_BLOCK, d), lambda i, be, nu: (i, 0)),
                pl.BlockSpec((1, d, de), lambda i, be, nu: (be[i], 0, 0)),
                pl.BlockSpec((1, d, de), lambda i, be, nu: (be[i], 0, 0)),
                pl.BlockSpec((1, de, d), lambda i, be, nu: (be[i], 0, 0)),
            ],
            out_specs=pl.BlockSpec((MOE_BLOCK, d), lambda i, be, nu: (i, 0)),
        ),
        out_shape=jax.ShapeDtypeStruct((nrow, d), jnp.float32),
        compiler_params=pltpu.CompilerParams(dimension_semantics=("arbitrary",), vmem_limit_bytes=VMEM_LIMIT),
        name="moe_experts",
    )(blk_exp, nused, xs, wg, wu, wd)


def _combine_kernel(starts_ref, ridx_ref, route_ref, x1_ref, mod_ref, ln_ref, ys_ref, out_ref, ybuf, sem):
    def row_copy(r, slot):
        src = _pair_dest(ridx_ref, starts_ref, r, slot)
        return pltpu.make_async_copy(ys_ref.at[pl.ds(src, 1), :], ybuf.at[slot, pl.ds(r, 1), :], sem)

    def issue(r, c):
        row_copy(r, 0).start()
        row_copy(r, 1).start()
        return c

    def drain(r, c):
        pltpu.make_async_copy(ys_ref.at[pl.ds(0, 1), :], ybuf.at[0, pl.ds(0, 1), :], sem).wait()
        return c

    lax.fori_loop(0, ROW_TILE, issue, 0)
    lax.fori_loop(0, 2 * ROW_TILE, drain, 0)
    route = route_ref[0]
    ffn = ybuf[0] * route[:, 4:5] + ybuf[1] * route[:, 5:6]
    g2 = mod_ref[0, 5:6, :]
    out_ref[0] = _ln_rows(ALPHA * x1_ref[0] + g2 * ffn, LN_EPS) * ln_ref[2:3, :] + ln_ref[3:4, :]


def _moe_combine(starts, ridx, route, x1, mod3, lnpar, ys):
    b, seq, d = x1.shape
    nt = seq // ROW_TILE
    return pl.pallas_call(
        _combine_kernel,
        grid_spec=pltpu.PrefetchScalarGridSpec(
            num_scalar_prefetch=1,
            grid=(b, nt),
            in_specs=[
                pl.BlockSpec((1, 8, ROW_TILE), lambda i, j, s: (i * nt + j, 0, 0), memory_space=pltpu.SMEM),
                pl.BlockSpec((1, ROW_TILE, ROUTE_W), lambda i, j, s: (i, j, 0)),
                pl.BlockSpec((1, ROW_TILE, d), lambda i, j, s: (i, j, 0)),
                pl.BlockSpec((1, 6, d), lambda i, j, s: (i, 0, 0)),
                pl.BlockSpec((8, d), lambda i, j, s: (0, 0)),
                pl.BlockSpec(memory_space=pl.ANY),
            ],
            out_specs=pl.BlockSpec((1, ROW_TILE, d), lambda i, j, s: (i, j, 0)),
            scratch_shapes=[pltpu.VMEM((2, ROW_TILE, d), jnp.float32), pltpu.SemaphoreType.DMA(())],
        ),
        out_shape=jax.ShapeDtypeStruct((b, seq, d), jnp.float32),
        compiler_params=pltpu.CompilerParams(dimension_semantics=("arbitrary", "arbitrary"),
                                             vmem_limit_bytes=VMEM_LIMIT),
        name="moe_combine",
    )(starts, ridx, route, x1, mod3, lnpar, ys)


def _moe_plan(cnt, n_pairs):
    counts = cnt[0, N_GROUPS:N_ROUTE].astype(jnp.int32)
    padded = (counts + MOE_BLOCK - 1) // MOE_BLOCK * MOE_BLOCK
    pad_ends = jnp.cumsum(padded)
    pad_starts = pad_ends - padded
    n_blk = -(-n_pairs // MOE_BLOCK) + N_EXPERTS
    blk_row0 = jnp.arange(n_blk, dtype=jnp.int32) * MOE_BLOCK
    blk_exp = jnp.minimum(jnp.sum((pad_ends[None, :] <= blk_row0[:, None]).astype(jnp.int32), axis=1), N_EXPERTS - 1)
    nused = (pad_ends[-1:] // MOE_BLOCK).astype(jnp.int32)
    return pad_starts.astype(jnp.int32), blk_exp, nused, n_blk


def _layernorm(z, gain=None, bias=None):
    zf = z.astype(jnp.float32)
    mu = jnp.mean(zf, -1, keepdims=True)
    var = jnp.mean(jnp.square(zf - mu), -1, keepdims=True)
    y = (zf - mu) * lax.rsqrt(var + LN_EPS)
    if gain is not None:
        y = y * gain + bias
    return y.astype(z.dtype)


def _headnorm(z, eps):
    zf = z.astype(jnp.float32)
    mu = jnp.mean(zf, -1, keepdims=True)
    var = jnp.mean(jnp.square(zf - mu), -1, keepdims=True)
    return (zf - mu) * lax.rsqrt(var + eps)


def _modulate(z, shift, scale):
    return _layernorm(z) * (1 + scale) + shift


def _rev_segments(u, ctx_len):
    return jnp.concatenate([jnp.flip(u[:, :ctx_len], 1), jnp.flip(u[:, ctx_len:], 1)], axis=1)


def _dir_stack(u_fwd, u_bwd, ctx_len):
    return jnp.concatenate([u_fwd, _rev_segments(u_bwd, ctx_len)], axis=0)


def _dir_merge(y, ctx_len):
    b = y.shape[0] // 2
    return y[:b] + _rev_segments(y[b:], ctx_len)


def _conv_grid(u, w, rows):
    b, l, ch = u.shape
    img = u.reshape(b, rows, l // rows, ch)
    out = lax.conv_general_dilated(img, w[:, :, None, :].astype(u.dtype), (1, 1), 'SAME',
                                   dimension_numbers=('NHWC', 'HWIO', 'NHWC'), feature_group_count=ch)
    return out.reshape(b, l, ch)


def _conv_seq(u, w_row):
    ch = u.shape[-1]
    return lax.conv_general_dilated(u, w_row[:, None, :].astype(u.dtype), (1,), 'SAME',
                                    dimension_numbers=('NWC', 'WIO', 'NWC'), feature_group_count=ch)


def _mlstm_chunkwise(q, k, v, log_i, log_f):
    z, t, h, dh = q.shape
    nc = t // CHUNK

    def chunks(a):
        a = a.astype(jnp.float32).reshape((z, nc, CHUNK, h) + a.shape[3:])
        return jnp.swapaxes(jnp.swapaxes(a, 0, 1), 2, 3)
    qc, kc, vc, lic, lfc = chunks(q), chunks(k), chunks(v), chunks(log_i), chunks(log_f)
    tri = jnp.tril(jnp.ones((CHUNK, CHUNK), bool))

    def step(carry, inp):
        c_mat, n_vec, m = carry
        qb, kb, vb, li, lf = inp
        b = jnp.cumsum(lf, -1)
        dmat = jnp.where(tri, b[..., :, None] - b[..., None, :] + li[..., None, :], -jnp.inf)
        m_inter = b + m[..., None]
        m_t = jnp.maximum(m_inter, jnp.max(dmat, -1))
        s = jnp.einsum('zhtd,zhsd->zhts', qb, kb) * jnp.exp(dmat - m_t[..., None])
        inter = jnp.exp(m_inter - m_t)
        num = inter[..., None] * jnp.einsum('zhtd,zhde->zhte', qb, c_mat) + jnp.einsum('zhts,zhse->zhte', s, vb)
        den = inter * jnp.einsum('zhtd,zhd->zht', qb, n_vec) + jnp.sum(s, -1)
        h_out = num / jnp.maximum(jnp.abs(den), jnp.exp(-m_t))[..., None]
        b_last = b[..., -1]
        g = b_last[..., None] - b + li
        m_new = jnp.maximum(b_last + m, jnp.max(g, -1))
        wts = jnp.exp(g - m_new[..., None])
        decay = jnp.exp(b_last + m - m_new)
        c_mat = decay[..., None, None] * c_mat + jnp.einsum('zhs,zhsd,zhse->zhde', wts, kb, vb)
        n_vec = decay[..., None] * n_vec + jnp.einsum('zhs,zhsd->zhd', wts, kb)
        return (c_mat, n_vec, m_new), h_out

    init = (jnp.zeros((z, h, dh, dh), jnp.float32), jnp.zeros((z, h, dh), jnp.float32), jnp.zeros((z, h), jnp.float32))
    _, hs = lax.scan(step, init, (qc, kc, vc, lic, lfc))
    return jnp.swapaxes(jnp.swapaxes(hs, 2, 3), 0, 1).reshape(z, t, h, dh)


def _mlstm_group(mq, mk, mv, mo, gates, b_i, b_f, norm_w, ctx_len):
    b, t, _ = mq.shape
    q = jax.nn.silu(mq)
    k = jax.nn.silu(mk) * (D_HM ** -0.5)
    bias_row = jnp.concatenate([b_i[0], b_f[0], b_i[1], b_f[1], jnp.zeros((128 - 4 * H_M,), jnp.float32)])[None, :]
    h_f, h_b = _mlstm_scan(q, k, mv, gates, bias_row, ctx_len)
    h = (h_f + h_b).reshape(b, t - ctx_len, H_M, D_HM)
    h = _headnorm(h, LN_EPS) * norm_w.reshape(H_M, D_HM)
    return jax.nn.sigmoid(mo[:, ctx_len:]) * h.reshape(b, t - ctx_len, W_M)


def _rwkv7_scan(r, w, k, v, kh, a):
    z, t, h, n = r.shape
    seq = tuple(jnp.moveaxis(s.astype(jnp.float32), 1, 0) for s in (r, w, k, v, kh, a))

    def step(state, inp):
        r_t, w_t, k_t, v_t, kh_t, a_t = inp
        removed = jnp.einsum('zhvk,zhk->zhv', state, kh_t)
        state = (state * w_t[:, :, None, :] - removed[..., None] * (kh_t * a_t)[:, :, None, :]
                 + v_t[..., None] * k_t[:, :, None, :])
        return state, jnp.einsum('zhvk,zhk->zhv', state, r_t)

    _, y = lax.scan(step, jnp.zeros((z, h, n, n), jnp.float32), seq)
    return jnp.moveaxis(y, 0, 1)


def _rwkv7_group(rr, rk, rv, lw_f, lw_b, la, lg, w0, w_b_mat, a0, a_b_mat, g_b_mat, kk, ka, bonus_w, gn_w, gn_b,
                 ctx_len):
    b, t, _ = rr.shape
    shp = (b, t, H_R, D_HR)
    logw_f = -DECAY_SCALE * jax.nn.sigmoid((w0[0] + jnp.tanh(lw_f) @ w_b_mat[0]).astype(jnp.float32))
    logw_b = -DECAY_SCALE * jax.nn.sigmoid((w0[1] + jnp.tanh(lw_b) @ w_b_mat[1]).astype(jnp.float32))
    a = jax.nn.sigmoid(a0 + la @ a_b_mat)
    g = jax.nn.sigmoid(lg) @ g_b_mat
    kap = (rk * kk).reshape(shp).astype(jnp.float32)
    kh = kap / jnp.maximum(jnp.sqrt(jnp.sum(jnp.square(kap), -1, keepdims=True)), 1e-12)
    kmod = (rk * (1 + (a - 1) * ka)).reshape(shp)
    r = rr.reshape(shp)
    v = rv.reshape(shp)
    a_h = a.reshape(shp)
    flat = lambda u: u.reshape(b, t, W_R)
    y_f, y_b = _rwkv_scan(rr, flat(kmod), rv, flat(kh), flat(kh * a_h), logw_f, logw_b, ctx_len)
    y = (y_f + y_b).reshape(b, t - ctx_len, H_R, D_HR)
    y = _headnorm(y, GN_EPS) * gn_w.reshape(H_R, D_HR) + gn_b.reshape(H_R, D_HR)
    bonus = (jnp.sum(r * kmod * bonus_w, -1, keepdims=True) * v)[:, ctx_len:]
    return (y + bonus).reshape(b, t - ctx_len, W_R) * g[:, ctx_len:]


def _hier_moe(h, rt_g, rt_g_b, rt_e, rt_e_b, ex_gate, ex_up, ex_down):
    b, t, d = h.shape
    tok = h.reshape(b * t, d)
    n = tok.shape[0]
    lg = (tok @ rt_g + rt_g_b).astype(jnp.float32)
    p_grp = jax.nn.softmax(lg, axis=-1)
    grp = jnp.argmax(lg, axis=-1).astype(jnp.int32)
    p_top = jnp.take_along_axis(p_grp, grp[:, None], axis=-1)
    le = (tok @ rt_e + rt_e_b).astype(jnp.float32).reshape(n, N_GROUPS, E_PER_GROUP)
    le_grp = jnp.take_along_axis(le, jnp.broadcast_to(grp[:, None, None], (n, 1, E_PER_GROUP)), axis=1)[:, 0]
    top_val, top_idx = lax.top_k(le_grp, TOP_K_IN_GROUP)
    wts = jax.nn.softmax(top_val, axis=-1) * p_top
    e_flat = (grp[:, None] * E_PER_GROUP + top_idx.astype(jnp.int32)).reshape(-1)
    w_flat = wts.reshape(-1)
    t_flat = jnp.repeat(jnp.arange(n, dtype=jnp.int32), TOP_K_IN_GROUP)
    order = jnp.argsort(e_flat)
    e_s, t_s, w_s = e_flat[order], t_flat[order], w_flat[order]
    counts = jnp.zeros((N_EXPERTS,), jnp.int32).at[e_flat].add(1)
    starts = jnp.cumsum(counts) - counts
    padded = (counts + MOE_BLOCK - 1) // MOE_BLOCK * MOE_BLOCK
    pad_ends = jnp.cumsum(padded)
    pad_starts = pad_ends - padded
    dest = pad_starts[e_s] + jnp.arange(e_s.shape[0], dtype=jnp.int32) - starts[e_s]
    n_pairs = n * TOP_K_IN_GROUP
    buf = (-(-n_pairs // MOE_BLOCK) + N_EXPERTS) * MOE_BLOCK
    tok_buf = jnp.full((buf,), n, jnp.int32).at[dest].set(t_s)
    w_buf = jnp.zeros((buf,), w_s.dtype).at[dest].set(w_s)
    n_blk = buf // MOE_BLOCK
    blk_exp = jnp.minimum(jnp.searchsorted(pad_ends, jnp.arange(n_blk, dtype=jnp.int32) * MOE_BLOCK, side='right'),
                          N_EXPERTS - 1)
    tok_pad = jnp.concatenate([tok, jnp.zeros((1, d), tok.dtype)], 0)

    def expert_block(args):
        idx, w, e = args
        xb = tok_pad[idx]
        hb = jax.nn.silu(xb @ ex_gate[e]) * (xb @ ex_up[e])
        return (hb @ ex_down[e]) * w[:, None]

    y = lax.map(expert_block, (tok_buf.reshape(n_blk, MOE_BLOCK), w_buf.reshape(n_blk, MOE_BLOCK), blk_exp))
    out = jnp.zeros((n + 1, d), y.dtype).at[tok_buf].add(y.reshape(buf, d))
    return out[:n].reshape(b, t, d).astype(h.dtype)


def kernel(x, c, ctx, c_ctx, w_ada, b_ada, w_in, conv_w, m_bias_i, m_bias_f, m_norm_w, r_w0, r_wB, r_a0, r_aB,
           r_gB, r_kk, r_ka, r_bonus, r_norm_w, r_norm_b, w_out, ln1_g, ln1_b, ln2_g, ln2_b, rt_g, rt_g_b, rt_e,
           rt_e_b, ex_gate, ex_up, ex_down):
    assert w_ada.shape[0] == DEPTH
    bsz, seq, d = x.shape
    ctx_len = ctx.shape[1]
    assert ctx_len == ROW_TILE and seq % ROW_TILE == 0
    assert seq % GRID_W == 0 and ROW_TILE % GRID_W == 0
    l = 0
    zrow = lambda n, w: jnp.zeros((n, w), jnp.float32)
    mrows = -(-(bsz + 1) // 8) * 8
    c_pad = jnp.concatenate([c, c_ctx[None, :], zrow(mrows - bsz - 1, d)], 0)
    mod_all = _ada(c_pad, w_ada[l], b_ada[l][None, :])[:bsz + 1]
    mod3 = mod_all.reshape(bsz + 1, 6, d)
    pc, pn = _inproj(ctx, x, mod3, _pad_w_in(w_in[l]))
    cw = jnp.concatenate([conv_w[l].reshape(CONV_K * CONV_K, CONV_CH), zrow(16 - CONV_K * CONV_K, CONV_CH)], 0)
    rpar = jnp.concatenate([r_w0[l], r_a0[l][None, :], r_kk[l][None, :], r_ka[l][None, :],
                            r_bonus[l].reshape(1, W_R), zrow(2, W_R)], 0)
    wbf = jnp.concatenate([r_wB[l][0], zrow(W_LORA, W_R)], 0)
    wbb = jnp.concatenate([zrow(W_LORA, W_R), r_wB[l][1]], 0)
    ab_mat = jnp.concatenate([r_aB[l], zrow(128 - A_LORA, W_R)], 0)
    q, k, rr, kmod, rv, kh, kha, logw_f, logw_b, gg, bonus = _prep(pc, pn, cw, rpar, wbf, wbb, ab_mat, r_gB[l])
    bias_row = jnp.concatenate([m_bias_i[l][0], m_bias_f[l][0], m_bias_i[l][1], m_bias_f[l][1],
                                jnp.zeros((128 - 4 * H_M,), jnp.float32)])[None, :]
    h_f, h_b = _mlstm_scan(q, k, pn, bias_row, ctx_len)
    y_f, y_b = _rwkv_scan(rr, kmod, rv, kh, kha, logw_f, logw_b, ctx_len)
    hpar = jnp.zeros((8, W_M), jnp.float32).at[0].set(m_norm_w[l]).at[1].set(r_norm_w[l]).at[2].set(r_norm_b[l])
    lnpar = jnp.zeros((8, d), jnp.float32).at[0].set(ln1_g[l]).at[1].set(ln1_b[l]).at[2].set(ln2_g[l]).at[3].set(
        ln2_b[l])
    rtw = jnp.concatenate([rt_g[l], rt_e[l], jnp.zeros((d, ROUTE_W - N_ROUTE), jnp.float32)], axis=1)
    rtb = jnp.concatenate([rt_g_b[l], rt_e_b[l], jnp.zeros((ROUTE_W - N_ROUTE,), jnp.float32)])[None, :]
    x1, tok, route, ridx, cnt = _outproj(h_f, h_b, pn, y_f, y_b, gg, bonus, x, mod3, w_out[l].astype(jnp.bfloat16),
                                         hpar, lnpar, rtw, rtb, ctx_len)
    n_tok = bsz * seq
    starts, blk_exp, nused, n_blk = _moe_plan(cnt, n_tok * TOP_K_IN_GROUP)
    xs = _moe_scatter(starts, ridx, tok.reshape(n_tok, d), jnp.zeros((n_blk * MOE_BLOCK, d), jnp.float32))
    ys = _moe_experts(blk_exp, nused, xs, ex_gate[l].astype(jnp.bfloat16), ex_up[l].astype(jnp.bfloat16),
                      ex_down[l].astype(jnp.bfloat16))
    return _moe_combine(starts, ridx, route, x1, mod3, lnpar, ys)
```

```python
import functools
import math

import jax
import jax.numpy as jnp
from jax import lax
import numpy as np
from jax.experimental import pallas as pl
from jax.experimental.pallas import tpu as pltpu

H_M = 8
D_HM = 64
W_M = H_M * D_HM
H_R = 8
D_HR = 64
W_R = H_R * D_HR
MIX_W = W_M + W_R
W_LORA = 64
A_LORA = 64
G_LORA = 128
CONV_K = 3
CONV_CH = 2 * W_M + 3 * W_R
SECTION_WIDTHS = (W_M, W_M, W_R, W_R, W_R, W_M, W_M, 4 * H_M, W_LORA, W_LORA, A_LORA, G_LORA)
IN_COLS = sum(SECTION_WIDTHS)
SPLIT_POINTS = tuple(int(v) for v in np.cumsum(SECTION_WIDTHS)[:-1])
CHUNK = 64
N_GROUPS = 4
E_PER_GROUP = 8
N_EXPERTS = N_GROUPS * E_PER_GROUP
TOP_K_IN_GROUP = 2
D_EXPERT = 512
MOE_BLOCK = 256
DEPTH = 1
ALPHA = (2.0 * DEPTH) ** 0.25
DECAY_SCALE = math.exp(-0.5)
LN_EPS = 1e-6
GN_EPS = 64e-5

ROW_TILE = 256
NONCONV_W = 1536
VMEM_LIMIT = 56 * 1024 * 1024


def _inproj_kernel(ctx_ref, x_ref, mod_ref, w_ref, oc_ref, on_ref):
    j = pl.program_id(1)

    def body(src_ref):
        z = src_ref[0]
        mu = jnp.mean(z, -1, keepdims=True)
        zc = z - mu
        var = jnp.mean(zc * zc, -1, keepdims=True)
        y = zc * lax.rsqrt(var + LN_EPS)
        h = (y * (1.0 + mod_ref[0, 1:2, :]) + mod_ref[0, 0:1, :]).astype(jnp.bfloat16)
        oc_ref[0] = jnp.dot(h, w_ref[:, :CONV_CH], preferred_element_type=jnp.float32)
        on_ref[0] = jnp.dot(h, w_ref[:, CONV_CH:], preferred_element_type=jnp.float32)

    @pl.when(j == 0)
    def _():
        body(ctx_ref)

    @pl.when(j > 0)
    def _():
        body(x_ref)


def _inproj(ctx, x, mod_all, w_pad):
    b, seq, d = x.shape
    nt = (ctx.shape[1] + seq) // ROW_TILE
    t_all = ctx.shape[1] + seq
    return pl.pallas_call(
        _inproj_kernel,
        grid=(b, nt),
        in_specs=[
            pl.BlockSpec((1, ROW_TILE, d), lambda i, j: (i, 0, 0)),
            pl.BlockSpec((1, ROW_TILE, d), lambda i, j: (i, jnp.maximum(j - 1, 0), 0)),
            pl.BlockSpec((1, 6, d), lambda i, j: (jnp.where(j == 0, b, i), 0, 0)),
            pl.BlockSpec((d, CONV_CH + NONCONV_W), lambda i, j: (0, 0)),
        ],
        out_specs=[
            pl.BlockSpec((1, ROW_TILE, CONV_CH), lambda i, j: (i, j, 0)),
            pl.BlockSpec((1, ROW_TILE, NONCONV_W), lambda i, j: (i, j, 0)),
        ],
        out_shape=[
            jax.ShapeDtypeStruct((b, t_all, CONV_CH), jnp.float32),
            jax.ShapeDtypeStruct((b, t_all, NONCONV_W), jnp.float32),
        ],
        compiler_params=pltpu.CompilerParams(
            dimension_semantics=("arbitrary", "arbitrary"), vmem_limit_bytes=VMEM_LIMIT),
        name="inproj",
    )(ctx, x, mod_all, w_pad)


def _pad_w_in(w_in):
    d = w_in.shape[0]
    sp = SPLIT_POINTS
    z = lambda n: jnp.zeros((d, n), w_in.dtype)
    gates = w_in[:, sp[6]:sp[7]]
    lw = w_in[:, sp[7]:sp[9]]
    la = w_in[:, sp[9]:sp[10]]
    lg = w_in[:, sp[10]:]
    return jnp.concatenate([w_in[:, :sp[6]], gates, z(96), lw, la, z(64), lg], axis=1).astype(jnp.bfloat16)


PAIR = 2 * D_HR
NPAIR = H_R // 2
_NN = (((1,), (0,)), ((), ()))
_NT = (((1,), (1,)), ((), ()))
_TN = (((0,), (0,)), ((), ()))


def _split2(a):
    hi = a.astype(jnp.bfloat16)
    lo = (a - hi.astype(jnp.float32)).astype(jnp.bfloat16)
    return hi, lo


def _split3(a):
    a1 = a.astype(jnp.bfloat16)
    r1 = a - a1.astype(jnp.float32)
    a2 = r1.astype(jnp.bfloat16)
    a3 = (r1 - a2.astype(jnp.float32)).astype(jnp.bfloat16)
    return a1, a2, a3


def _dg(a, b, dims):
    return lax.dot_general(a, b, dims, preferred_element_type=jnp.float32)


def _dot3(a, b, dims=_NN):
    ah, al = _split2(a)
    bh, bl = _split2(b)
    return _dg(ah, bh, dims) + _dg(ah, bl, dims) + _dg(al, bh, dims)


def _dot1(a, b, dims=_NN):
    return _dg(a.astype(jnp.bfloat16), b.astype(jnp.bfloat16), dims)


def _dot_exact_lhs(a_bf16, b, dims=_NN):
    b1, b2, b3 = _split3(b)
    return _dg(a_bf16, b1, dims) + _dg(a_bf16, b2, dims) + _dg(a_bf16, b3, dims)


def _dot_exact_rhs(a, b_bf16):
    a1, a2, a3 = _split3(a)
    return _dg(a1, b_bf16, _NN) + _dg(a2, b_bf16, _NN) + _dg(a3, b_bf16, _NN)


def _head_block_diag(width, head, value):
    hrow = lax.broadcasted_iota(jnp.int32, (width, width), 0) // head
    hcol = lax.broadcasted_iota(jnp.int32, (width, width), 1) // head
    return jnp.where(hrow == hcol, value, 0.0).astype(jnp.bfloat16)


def _ada_kernel(c_ref, w_ref, b_ref, o_ref):
    o_ref[...] = _dot3(jax.nn.silu(c_ref[...]), w_ref[...]) + b_ref[...]


def _ada(c_pad, w, bias):
    m, d = c_pad.shape
    n = w.shape[1]
    return pl.pallas_call(
        _ada_kernel,
        grid=(n // d,),
        in_specs=[pl.BlockSpec((m, d), lambda j: (0, 0)),
                  pl.BlockSpec((d, d), lambda j: (0, j)),
                  pl.BlockSpec((1, d), lambda j: (0, j))],
        out_specs=pl.BlockSpec((m, d), lambda j: (0, j)),
        out_shape=jax.ShapeDtypeStruct((m, n), jnp.float32),
        compiler_params=pltpu.CompilerParams(dimension_semantics=("arbitrary",), vmem_limit_bytes=VMEM_LIMIT),
        name="adaln",
    )(c_pad, w, bias)


GRID_W = 64
HALO = GRID_W
XOFF = 8


def _prep_kernel(top_ref, main_ref, bot_ref, pn_ref, cw_ref, rpar_ref, wbf_ref, wbb_ref, ab_ref, gb_ref,
                 q_ref, k_ref, r_ref, km_ref, v_ref, kh_ref, abo_ref, lwf_ref, lwb_ref, gg_ref, bon_ref, xbuf):
    j = pl.program_id(1)
    nt = pl.num_programs(1)
    is_ctx = j == 0
    top_ok = j >= 2
    bot_ok = (j >= 1) & (j < nt - 1)
    l_idx = lax.broadcasted_iota(jnp.int32, (ROW_TILE, 1), 0)
    col = jnp.where(is_ctx, l_idx, l_idx % GRID_W)
    left_ok = col != 0
    right_ok = col != jnp.where(is_ctx, ROW_TILE - 1, GRID_W - 1)
    vert = jnp.where(is_ctx, 0.0, 1.0)
    xbuf[0:XOFF, :] = jnp.zeros((XOFF, W_M), jnp.float32)
    xbuf[XOFF + 2 * HALO + ROW_TILE:, :] = jnp.zeros((XOFF, W_M), jnp.float32)
    sec = {}
    for s in range(5):
        sl = slice(s * W_M, (s + 1) * W_M)
        xbuf[XOFF:XOFF + HALO, :] = jnp.where(top_ok, top_ref[0, :, sl], 0.0)
        xbuf[XOFF + HALO:XOFF + HALO + ROW_TILE, :] = main_ref[0, :, sl]
        xbuf[XOFF + HALO + ROW_TILE:XOFF + 2 * HALO + ROW_TILE, :] = jnp.where(bot_ok, bot_ref[0, :, sl], 0.0)
        acc = None
        for dr in range(3):
            base = XOFF + HALO + GRID_W * (dr - 1)
            scale = 1.0 if dr == 1 else vert
            left = jnp.where(left_ok, xbuf[base - 1:base - 1 + ROW_TILE, :], 0.0)
            mid = xbuf[base:base + ROW_TILE, :]
            right = jnp.where(right_ok, xbuf[base + 1:base + 1 + ROW_TILE, :], 0.0)
            term = (left * cw_ref[3 * dr:3 * dr + 1, sl] + mid * cw_ref[3 * dr + 1:3 * dr + 2, sl]
                    + right * cw_ref[3 * dr + 2:3 * dr + 3, sl]) * scale
            acc = term if acc is None else acc + term
        sec[s] = acc
    q_ref[0] = jax.nn.silu(sec[0])
    k_ref[0] = jax.nn.silu(sec[1]) * (D_HM ** -0.5)
    rr, rk, rv = sec[2], sec[3], sec[4]
    r_ref[0] = rr
    v_ref[0] = rv
    lw = jnp.tanh(pn_ref[0, :, 1152:1280])
    lwf_ref[0] = -DECAY_SCALE * jax.nn.sigmoid(rpar_ref[0:1, :] + _dot3(lw, wbf_ref[...]))
    lwb_ref[0] = -DECAY_SCALE * jax.nn.sigmoid(rpar_ref[1:2, :] + _dot3(lw, wbb_ref[...]))
    a = jax.nn.sigmoid(rpar_ref[2:3, :] + _dot3(pn_ref[0, :, 1280:1408], ab_ref[...]))
    gg_ref[0] = _dot3(jax.nn.sigmoid(pn_ref[0, :, 1408:1536]), gb_ref[...])
    ones_bd = _head_block_diag(W_R, D_HR, 1.0)
    kap = rk * rpar_ref[3:4, :]
    norm = jnp.sqrt(_dot_exact_rhs(kap * kap, ones_bd))
    kh = kap / jnp.maximum(norm, 1e-12)
    kmod = rk * (1.0 + (a - 1.0) * rpar_ref[4:5, :])
    kh_ref[0] = kh
    abo_ref[0] = kh * a
    km_ref[0] = kmod
    bon_ref[0] = _dot_exact_rhs(rr * kmod * rpar_ref[5:6, :], ones_bd) * rv


def _prep(pc, pn, cw, rpar, wbf, wbb, ab, gb):
    b, t, _ = pc.shape
    nt = t // ROW_TILE
    per = ROW_TILE // HALO
    nh = t // HALO
    w = W_M
    full = lambda shp: pl.BlockSpec(shp, lambda i, j: tuple(0 for _ in shp))
    in_specs = [
        pl.BlockSpec((1, HALO, CONV_CH), lambda i, j: (i, jnp.maximum(j * per - 1, 0), 0)),
        pl.BlockSpec((1, ROW_TILE, CONV_CH), lambda i, j: (i, j, 0)),
        pl.BlockSpec((1, HALO, CONV_CH), lambda i, j: (i, jnp.minimum(j * per + per, nh - 1), 0)),
        pl.BlockSpec((1, ROW_TILE, NONCONV_W), lambda i, j: (i, j, 0)),
        full((16, CONV_CH)), full((8, w)), full((128, w)), full((128, w)), full((128, w)), full((128, w)),
    ]
    out = pl.BlockSpec((1, ROW_TILE, w), lambda i, j: (i, j, 0))
    return pl.pallas_call(
        _prep_kernel,
        grid=(b, nt),
        in_specs=in_specs,
        out_specs=[out] * 11,
        out_shape=[jax.ShapeDtypeStruct((b, t, w), jnp.float32)] * 11,
        scratch_shapes=[pltpu.VMEM((2 * XOFF + 2 * HALO + ROW_TILE, w), jnp.float32)],
        compiler_params=pltpu.CompilerParams(
            dimension_semantics=("arbitrary", "arbitrary"), vmem_limit_bytes=VMEM_LIMIT),
        name="conv_prep",
    )(pc, pc, pc, pn, cw, rpar, wbf, wbb, ab, gb)


def _round_robin(gens):
    out = [None] * len(gens)
    live = list(range(len(gens)))
    while live:
        nxt = []
        for i in live:
            try:
                next(gens[i])
                nxt.append(i)
            except StopIteration as stop:
                out[i] = stop.value
        live = nxt
    return out


def _rwkv_pair_chunk(r, k, v, kh, ab, lw, s_mat, tri_incl, strict, incl, last_row, m0, m1):
    cw = _dot_exact_lhs(tri_incl, lw)
    yield
    e_pos = jnp.exp(cw)
    e_neg = jnp.exp(-cw)
    e_prev = jnp.exp(cw - lw)
    stack = lambda a: jnp.concatenate([a * m0, a * m1], axis=0)
    kt = stack(kh * e_prev)
    bt = stack(ab * e_neg)
    kk = stack(k * e_neg)
    rt = stack(r * e_pos)
    vs = stack(v)
    zero = jnp.zeros((), jnp.float32)
    kr = jnp.concatenate([kt, rt], axis=0)
    bk = jnp.concatenate([bt, kk], axis=0)
    amat = _dot1(kr, bk, _NT)
    yield
    a_bk = jnp.where(strict, amat[:PAIR, :PAIR], zero)
    a_kk = jnp.where(strict, amat[:PAIR, PAIR:], zero)
    a_rb = jnp.where(incl, amat[PAIR:, :PAIR], zero)
    a_rk = jnp.where(incl, amat[PAIR:, PAIR:], zero)
    n_mat = -a_bk
    eye = (lax.broadcasted_iota(jnp.int32, (PAIR, PAIR), 0)
           == lax.broadcasted_iota(jnp.int32, (PAIR, PAIR), 1)).astype(jnp.float32)
    q = eye + n_mat
    p = _dot1(n_mat, n_mat)
    ks = _dot1(kr, s_mat, _NT)
    av = _dot1(jnp.concatenate([a_kk, a_rk], axis=0), vs)
    yield
    for _ in range(4):
        qp = _dot1(jnp.concatenate([q, p], axis=0), p)
        yield
        q = q + qp[:PAIR]
        p = qp[PAIR:]
    t_inv = q + _dot1(q, p)
    yield
    u = -_dot1(t_inv, ks[:PAIR] + av[:PAIR])
    yield
    y = ks[PAIR:] + av[PAIR:] + _dot1(a_rb, u)
    yield
    w_last = jnp.sum(jnp.where(last_row, e_pos, zero), axis=0, keepdims=True)
    s_new = (s_mat + _dot1(jnp.concatenate([u, vs], axis=0), bk, _TN)) * w_last
    return y[:CHUNK] + y[CHUNK:], s_new


def _rwkv_kernel(rf, kf, vf, khf, abf, lwf, rb, kb, vb, khb, abb, lwb, yf_ref, yb_ref, s_ref, *, nctx):
    j = pl.program_id(1)

    @pl.when(j == 0)
    def _():
        s_ref[...] = jnp.zeros_like(s_ref)

    row = lax.broadcasted_iota(jnp.int32, (PAIR, PAIR), 0)
    col = lax.broadcasted_iota(jnp.int32, (PAIR, PAIR), 1)
    same = (row // CHUNK) == (col // CHUNK)
    r64 = lax.broadcasted_iota(jnp.int32, (CHUNK, CHUNK), 0)
    c64 = lax.broadcasted_iota(jnp.int32, (CHUNK, CHUNK), 1)
    rowl = lax.broadcasted_iota(jnp.int32, (CHUNK, PAIR), 0)
    lane = lax.broadcasted_iota(jnp.int32, (1, PAIR), 1)
    m0 = (lane < D_HR).astype(jnp.float32)
    m1 = 1.0 - m0
    dirs = (
        (rf, kf, vf, khf, abf, lwf, yf_ref, (c64 <= r64), same & (col < row), same & (col <= row), rowl == CHUNK - 1),
        (rb, kb, vb, khb, abb, lwb, yb_ref, (c64 >= r64), same & (col > row), same & (col >= row), rowl == 0),
    )
    states = [[s_ref[d, p] for p in range(NPAIR)] for d in range(2)]
    gens, meta = [], []
    for d, (r_ref, k_ref, v_ref, kh_ref, ab_ref, lw_ref, y_ref, tri, strict, incl, last_row) in enumerate(dirs):
        tri = tri.astype(jnp.float32).astype(jnp.bfloat16)
        for p in range(NPAIR):
            sl = slice(p * PAIR, (p + 1) * PAIR)
            gens.append(_rwkv_pair_chunk(r_ref[0, :, sl], k_ref[0, :, sl], v_ref[0, :, sl], kh_ref[0, :, sl],
                                         ab_ref[0, :, sl], lw_ref[0, :, sl], states[d][p], tri, strict, incl,
                                         last_row, m0, m1))
            meta.append((d, p, sl, y_ref))
    results = [m + r for m, r in zip(meta, _round_robin(gens))]
    for d, p, sl, y_ref, y, s_new in results:
        s_ref[d, p] = s_new
        y_ref[0, :, sl] = y


def _rwkv_scan(r, k, v, kh, ab, lw_f, lw_b, ctx_len):
    b, t, w = r.shape
    nc = t // CHUNK
    nctx = ctx_len // CHUNK
    nlat = nc - nctx

    def fwd_map(i, j):
        return (i, j, 0)

    def bwd_map(i, j):
        return (i, jnp.where(j < nctx, nctx - 1 - j, nc - 1 - (j - nctx)), 0)

    blk = (1, CHUNK, w)
    in_specs = [pl.BlockSpec(blk, fwd_map)] * 6 + [pl.BlockSpec(blk, bwd_map)] * 6
    out_specs = [
        pl.BlockSpec(blk, lambda i, j: (i, jnp.maximum(j - nctx, 0), 0)),
        pl.BlockSpec(blk, lambda i, j: (i, nlat - 1 - jnp.maximum(j - nctx, 0), 0)),
    ]
    return pl.pallas_call(
        functools.partial(_rwkv_kernel, nctx=nctx),
        grid=(b, nc),
        in_specs=in_specs,
        out_specs=out_specs,
        out_shape=[jax.ShapeDtypeStruct((b, t - ctx_len, w), jnp.float32)] * 2,
        scratch_shapes=[pltpu.VMEM((2, NPAIR, PAIR, PAIR), jnp.float32)],
        compiler_params=pltpu.CompilerParams(
            dimension_semantics=("arbitrary", "arbitrary"), vmem_limit_bytes=VMEM_LIMIT),
        name="rwkv_scan",
    )(r, k, v, kh, ab, lw_f, r, k, v, kh, ab, lw_b)


def _col_stack(a, c0, c1):
    return jnp.concatenate([a[:, c0:c0 + 1], a[:, c1:c1 + 1]], axis=0)


def _half_bcast(v0, v1):
    return jnp.concatenate([jnp.broadcast_to(v0, (CHUNK, 1)), jnp.broadcast_to(v1, (CHUNK, 1))], axis=0)


def _mlstm_pair_chunk(q, k, v, b_col, li_col, c_row, c_mat, n_row, m_row, incl, last_sel, m0, m1):
    stack = lambda a: jnp.concatenate([a * m0, a * m1], axis=0)
    q_st, k_st, v_st = stack(q), stack(k), stack(v)
    lane = lax.broadcasted_iota(jnp.int32, (1, PAIR), 1)
    first = lane < D_HM
    neg_inf = jnp.full((), -jnp.inf, jnp.float32)
    m_h0, m_h1 = m_row[:, 0:1], m_row[:, D_HM:D_HM + 1]
    m_prev = _half_bcast(m_h0, m_h1)
    dmat = jnp.where(incl, b_col + c_row, neg_inf)
    m_inter = b_col + m_prev
    m_t = jnp.maximum(m_inter, jnp.max(dmat, axis=1, keepdims=True))
    bl = jnp.where(last_sel, b_col, neg_inf)
    bl0 = jnp.max(bl[:CHUNK], axis=0, keepdims=True)
    bl1 = jnp.max(bl[CHUNK:], axis=0, keepdims=True)
    g = _half_bcast(bl0, bl1) - b_col + li_col
    mn0 = jnp.maximum(bl0 + m_h0, jnp.max(g[:CHUNK], axis=0, keepdims=True))
    mn1 = jnp.maximum(bl1 + m_h1, jnp.max(g[CHUNK:], axis=0, keepdims=True))
    wts = jnp.exp(g - _half_bcast(mn0, mn1))
    decay_row = jnp.where(first, jnp.exp(bl0 + m_h0 - mn0), jnp.exp(bl1 + m_h1 - mn1))
    wk = wts * k_st
    qk = _dot1(q_st, k_st, _NT)
    qc = _dot1(q_st, c_mat)
    kv = _dot1(wk, v_st, _TN)
    yield
    s = qk * jnp.exp(dmat - m_t)
    inter = jnp.exp(m_inter - m_t)
    sv = _dot1(s, v_st)
    yield
    num = inter * qc + sv
    den = inter * jnp.sum(q_st * n_row, axis=1, keepdims=True) + jnp.sum(s, axis=1, keepdims=True)
    h_st = num / jnp.maximum(jnp.abs(den), jnp.exp(-m_t))
    h = h_st[:CHUNK] + h_st[CHUNK:]
    c_new = decay_row * c_mat + kv
    n_new = decay_row * n_row + jnp.sum(wk, axis=0, keepdims=True)
    m_new = jnp.where(first, mn0, mn1)
    return h, c_new, n_new, m_new


def _mlstm_kernel(qf, kf, vf, gf, qb, kb, vb, gb, bias_ref, hf_ref, hb_ref, c_ref, n_ref, m_ref, *, nctx):
    j = pl.program_id(1)

    @pl.when(j == 0)
    def _():
        c_ref[...] = jnp.zeros_like(c_ref)
        n_ref[...] = jnp.zeros_like(n_ref)
        m_ref[...] = jnp.zeros_like(m_ref)

    row = lax.broadcasted_iota(jnp.int32, (PAIR, PAIR), 0)
    col = lax.broadcasted_iota(jnp.int32, (PAIR, PAIR), 1)
    same = (row // CHUNK) == (col // CHUNK)
    r64 = lax.broadcasted_iota(jnp.int32, (CHUNK, CHUNK), 0)
    c64 = lax.broadcasted_iota(jnp.int32, (CHUNK, CHUNK), 1)
    rcol = lax.broadcasted_iota(jnp.int32, (PAIR, 1), 0) % CHUNK
    lane = lax.broadcasted_iota(jnp.int32, (1, PAIR), 1)
    m0 = (lane < D_HM).astype(jnp.float32)
    m1 = 1.0 - m0
    is_f = (lane % (2 * H_M)) >= H_M
    dirs = (
        (qf, kf, vf, gf, hf_ref, (c64 <= r64), same & (col <= row), rcol == CHUNK - 1),
        (qb, kb, vb, gb, hb_ref, (c64 >= r64), same & (col >= row), rcol == 0),
    )
    states = [[(c_ref[d, p], n_ref[d, p], m_ref[d, p]) for p in range(NPAIR)] for d in range(2)]
    gens, meta = [], []
    for d, (q_ref, k_ref, v_ref, g_ref, h_ref, tri, incl, last_sel) in enumerate(dirs):
        tri = tri.astype(jnp.float32).astype(jnp.bfloat16)
        gl = g_ref[0] + bias_ref[...]
        act = jnp.where(is_f, jax.nn.log_sigmoid(gl), gl)
        b_all = _dot_exact_lhs(tri, act)
        act_t = act.T
        b_t = b_all.T
        for p in range(NPAIR):
            sl = slice(p * PAIR, (p + 1) * PAIR)
            ci, cf = d * 2 * H_M + 2 * p, d * 2 * H_M + H_M + 2 * p
            b_col = _col_stack(b_all, cf, cf + 1)
            li_col = _col_stack(act, ci, ci + 1)
            c_row = jnp.concatenate([act_t[ci:ci + 1, :] - b_t[cf:cf + 1, :],
                                     act_t[ci + 1:ci + 2, :] - b_t[cf + 1:cf + 2, :]], axis=1)
            gens.append(_mlstm_pair_chunk(
                q_ref[0, :, sl], k_ref[0, :, sl], v_ref[0, :, sl], b_col, li_col, c_row,
                *states[d][p], incl, last_sel, m0, m1))
            meta.append((d, p, sl, h_ref))
    for (d, p, sl, h_ref), (h, c_new, n_new, m_new) in zip(meta, _round_robin(gens)):
        c_ref[d, p] = c_new
        n_ref[d, p] = n_new
        m_ref[d, p] = m_new
        h_ref[0, :, sl] = h


def _mlstm_scan(q, k, pn, bias_row, ctx_len):
    v = gates = pn
    b, t, w = q.shape
    nc = t // CHUNK
    nctx = ctx_len // CHUNK
    nlat = nc - nctx

    def fwd_map(i, j):
        return (i, j, 0)

    def bwd_map(i, j):
        return (i, jnp.where(j < nctx, nctx - 1 - j, nc - 1 - (j - nctx)), 0)

    blk = (1, CHUNK, w)
    gblk = (1, CHUNK, 128)
    gate_blk = (2 * W_M) // 128
    gate_of = lambda m: (lambda i, j: m(i, j)[:2] + (gate_blk,))
    in_specs = ([pl.BlockSpec(blk, fwd_map)] * 3 + [pl.BlockSpec(gblk, gate_of(fwd_map))]
                + [pl.BlockSpec(blk, bwd_map)] * 3 + [pl.BlockSpec(gblk, gate_of(bwd_map))]
                + [pl.BlockSpec((1, 128), lambda i, j: (0, 0))])
    out_specs = [
        pl.BlockSpec(blk, lambda i, j: (i, jnp.maximum(j - nctx, 0), 0)),
        pl.BlockSpec(blk, lambda i, j: (i, nlat - 1 - jnp.maximum(j - nctx, 0), 0)),
    ]
    return pl.pallas_call(
        functools.partial(_mlstm_kernel, nctx=nctx),
        grid=(b, nc),
        in_specs=in_specs,
        out_specs=out_specs,
        out_shape=[jax.ShapeDtypeStruct((b, t - ctx_len, w), jnp.float32)] * 2,
        scratch_shapes=[pltpu.VMEM((2, NPAIR, PAIR, PAIR), jnp.float32),
                        pltpu.VMEM((2, NPAIR, 1, PAIR), jnp.float32),
                        pltpu.VMEM((2, NPAIR, 1, PAIR), jnp.float32)],
        compiler_params=pltpu.CompilerParams(
            dimension_semantics=("arbitrary", "arbitrary"), vmem_limit_bytes=VMEM_LIMIT),
        name="mlstm_scan",
    )(q, k, v, gates, q, k, v, gates, bias_row)


ROUTE_W = 128
N_ROUTE = N_GROUPS + N_EXPERTS


def _ln_rows(z, eps):
    mu = jnp.mean(z, -1, keepdims=True)
    zc = z - mu
    var = jnp.mean(zc * zc, -1, keepdims=True)
    return zc * lax.rsqrt(var + eps)


def _headnorm_mxu(z, avg_bf16, eps):
    mu = _dot_exact_rhs(z, avg_bf16)
    zc = z - mu
    var = _dot_exact_rhs(zc * zc, avg_bf16)
    return zc * lax.rsqrt(var + eps)


def _outproj_kernel(hf, hb, mo, yf, yb, gg, bonus, x_ref, mod_ref, wout_ref, hpar_ref, ln_ref, rtw_ref, rtb_ref,
                    x1_ref, tok_ref, route_ref, ridx_ref, cnt_ref, carry_ref):
    i, j = pl.program_id(0), pl.program_id(1)

    @pl.when((i == 0) & (j == 0))
    def _():
        carry_ref[...] = jnp.zeros_like(carry_ref)

    avg = _head_block_diag(W_M, D_HM, 1.0 / D_HM)
    out_m = jax.nn.sigmoid(mo[0]) * (_headnorm_mxu(hf[0] + hb[0], avg, LN_EPS) * hpar_ref[0:1, :])
    y = _headnorm_mxu(yf[0] + yb[0], avg, GN_EPS) * hpar_ref[1:2, :] + hpar_ref[2:3, :]
    out_r = (y + bonus[0]) * gg[0]
    proj = (jnp.dot(out_m.astype(jnp.bfloat16), wout_ref[:W_M, :], preferred_element_type=jnp.float32)
            + jnp.dot(out_r.astype(jnp.bfloat16), wout_ref[W_M:, :], preferred_element_type=jnp.float32))
    g1, sh2, sc2 = mod_ref[0, 2:3, :], mod_ref[0, 3:4, :], mod_ref[0, 4:5, :]
    x1 = _ln_rows(ALPHA * x_ref[0] + g1 * proj, LN_EPS) * ln_ref[0:1, :] + ln_ref[1:2, :]
    x1_ref[0] = x1
    tok = _ln_rows(x1, LN_EPS) * (1.0 + sc2) + sh2
    tok_ref[0] = tok

    logits = _dot3(tok, rtw_ref[...]) + rtb_ref[...]
    lane = lax.broadcasted_iota(jnp.int32, (ROW_TILE, ROUTE_W), 1)
    neg_inf = jnp.full((), -jnp.inf, jnp.float32)
    big = jnp.int32(ROUTE_W)
    is_g = lane < N_GROUPS
    lg = jnp.where(is_g, logits, neg_inf)
    gmax = jnp.max(lg, axis=1, keepdims=True)
    grp = jnp.min(jnp.where(lg == gmax, lane, big), axis=1, keepdims=True)
    p_top = 1.0 / jnp.sum(jnp.where(is_g, jnp.exp(lg - gmax), 0.0), axis=1, keepdims=True)
    in_grp = (lane >= N_GROUPS) & (lane < N_ROUTE) & ((lane - N_GROUPS) // E_PER_GROUP == grp)
    le = jnp.where(in_grp, logits, neg_inf)
    v1 = jnp.max(le, axis=1, keepdims=True)
    i1 = jnp.min(jnp.where(le == v1, lane, big), axis=1, keepdims=True)
    le2 = jnp.where(lane == i1, neg_inf, le)
    v2 = jnp.max(le2, axis=1, keepdims=True)
    i2 = jnp.min(jnp.where(le2 == v2, lane, big), axis=1, keepdims=True)
    e21 = jnp.exp(v2 - v1)
    w1 = (1.0 / (1.0 + e21)) * p_top
    w2 = (e21 / (1.0 + e21)) * p_top

    sel1, sel2 = lane == i1, lane == i2
    onehot = jnp.where(sel1 | sel2, 1.0, 0.0)
    tr = lax.broadcasted_iota(jnp.int32, (ROW_TILE, ROW_TILE), 0)
    tc = lax.broadcasted_iota(jnp.int32, (ROW_TILE, ROW_TILE), 1)
    strict = jnp.where(tc < tr, 1.0, 0.0).astype(jnp.bfloat16)
    before = _dg(strict, onehot.astype(jnp.bfloat16), _NN) + carry_ref[...]
    r1 = jnp.sum(jnp.where(sel1, before, 0.0), axis=1, keepdims=True)
    r2 = jnp.sum(jnp.where(sel2, before, 0.0), axis=1, keepdims=True)
    carry_ref[...] = carry_ref[...] + jnp.sum(onehot, axis=0, keepdims=True)
    e1 = (i1 - N_GROUPS).astype(jnp.float32)
    e2 = (i2 - N_GROUPS).astype(jnp.float32)
    route = jnp.where(lane == 0, e1, jnp.where(lane == 1, e2, jnp.where(lane == 2, r1, jnp.where(
        lane == 3, r2, jnp.where(lane == 4, w1, jnp.where(lane == 5, w2, 0.0))))))
    route_ref[0] = route
    ridx_ref[0] = route.T[0:8, :].astype(jnp.int32)
    cnt_ref[...] = jnp.broadcast_to(carry_ref[...], cnt_ref.shape)


def _outproj(h_f, h_b, pn, y_f, y_b, gg, bonus, x, mod3, wout_bf16, hpar, lnpar, rtw, rtb, ctx_len):
    b, seq, d = x.shape
    nt = seq // ROW_TILE
    off = ctx_len // ROW_TILE
    lat = lambda w: pl.BlockSpec((1, ROW_TILE, w), lambda i, j: (i, j, 0))
    full = lambda shp: pl.BlockSpec(shp, lambda i, j: tuple(0 for _ in shp))
    in_specs = [
        lat(W_M), lat(W_M),
        pl.BlockSpec((1, ROW_TILE, W_M), lambda i, j: (i, j + off, 1)),
        lat(W_R), lat(W_R),
        pl.BlockSpec((1, ROW_TILE, W_R), lambda i, j: (i, j + off, 0)),
        pl.BlockSpec((1, ROW_TILE, W_R), lambda i, j: (i, j + off, 0)),
        lat(d),
        pl.BlockSpec((1, 6, d), lambda i, j: (i, 0, 0)),
        full((MIX_W, d)), full((8, W_M)), full((8, d)), full((d, ROUTE_W)), full((1, ROUTE_W)),
    ]
    out_specs = [
        lat(d), lat(d), lat(ROUTE_W),
        pl.BlockSpec((1, 8, ROW_TILE), lambda i, j: (i * nt + j, 0, 0)),
        full((8, ROUTE_W)),
    ]
    out_shape = [
        jax.ShapeDtypeStruct((b, seq, d), jnp.float32),
        jax.ShapeDtypeStruct((b, seq, d), jnp.float32),
        jax.ShapeDtypeStruct((b, seq, ROUTE_W), jnp.float32),
        jax.ShapeDtypeStruct((b * nt, 8, ROW_TILE), jnp.int32),
        jax.ShapeDtypeStruct((8, ROUTE_W), jnp.float32),
    ]
    return pl.pallas_call(
        _outproj_kernel,
        grid=(b, nt),
        in_specs=in_specs,
        out_specs=out_specs,
        out_shape=out_shape,
        scratch_shapes=[pltpu.VMEM((1, ROUTE_W), jnp.float32)],
        compiler_params=pltpu.CompilerParams(
            dimension_semantics=("arbitrary", "arbitrary"), vmem_limit_bytes=VMEM_LIMIT),
        name="outproj_router",
    )(h_f, h_b, pn, y_f, y_b, gg, bonus, x, mod3, wout_bf16, hpar, lnpar, rtw, rtb)


def _pair_dest(ridx_ref, starts_ref, r, slot):
    return starts_ref[ridx_ref[0, slot, r]] + ridx_ref[0, 2 + slot, r]


def _scatter_kernel(starts_ref, ridx_ref, tok_ref, xs_in_ref, xs_ref, sem):
    del xs_in_ref

    def row_copy(r, slot):
        dst = _pair_dest(ridx_ref, starts_ref, r, slot)
        return pltpu.make_async_copy(tok_ref.at[pl.ds(r, 1), :], xs_ref.at[pl.ds(dst, 1), :], sem)

    def issue(r, c):
        row_copy(r, 0).start()
        row_copy(r, 1).start()
        return c

    def drain(r, c):
        pltpu.make_async_copy(tok_ref.at[pl.ds(0, 1), :], xs_ref.at[pl.ds(0, 1), :], sem).wait()
        return c

    lax.fori_loop(0, ROW_TILE, issue, 0)
    lax.fori_loop(0, 2 * ROW_TILE, drain, 0)


def _moe_scatter(starts, ridx, tok2d, xs_zero):
    n, d = tok2d.shape
    nt = n // ROW_TILE
    return pl.pallas_call(
        _scatter_kernel,
        grid_spec=pltpu.PrefetchScalarGridSpec(
            num_scalar_prefetch=1,
            grid=(nt,),
            in_specs=[
                pl.BlockSpec((1, 8, ROW_TILE), lambda i, s: (i, 0, 0), memory_space=pltpu.SMEM),
                pl.BlockSpec((ROW_TILE, d), lambda i, s: (i, 0)),
                pl.BlockSpec(memory_space=pl.ANY),
            ],
            out_specs=pl.BlockSpec(memory_space=pl.ANY),
            scratch_shapes=[pltpu.SemaphoreType.DMA(())],
        ),
        out_shape=jax.ShapeDtypeStruct(xs_zero.shape, xs_zero.dtype),
        input_output_aliases={3: 0},
        compiler_params=pltpu.CompilerParams(dimension_semantics=("arbitrary",), vmem_limit_bytes=VMEM_LIMIT),
        name="moe_scatter",
    )(starts, ridx, tok2d, xs_zero)


def _expert_kernel(blk_exp_ref, nused_ref, xs_ref, wg_ref, wu_ref, wd_ref, ys_ref):
    i = pl.program_id(0)

    @pl.when(i < nused_ref[0])
    def _():
        xb = xs_ref[...].astype(jnp.bfloat16)
        hg = jnp.dot(xb, wg_ref[0], preferred_element_type=jnp.float32)
        hu = jnp.dot(xb, wu_ref[0], preferred_element_type=jnp.float32)
        hb = (jax.nn.silu(hg) * hu).astype(jnp.bfloat16)
        ys_ref[...] = jnp.dot(hb, wd_ref[0], preferred_element_type=jnp.float32)

    @pl.when(i >= nused_ref[0])
    def _():
        ys_ref[...] = jnp.zeros_like(ys_ref)


def _moe_experts(blk_exp, nused, xs, wg, wu, wd):
    nrow, d = xs.shape
    de = wg.shape[2]
    return pl.pallas_call(
        _expert_kernel,
        grid_spec=pltpu.PrefetchScalarGridSpec(
            num_scalar_prefetch=2,
            grid=(nrow // MOE_BLOCK,),
            in_specs=[
                pl.BlockSpec((MOE_BLOCK, d), lambda i, be, nu: (i, 0)),
                pl.BlockSpec((1, d, de), lambda i, be, nu: (be[i], 0, 0)),
                pl.BlockSpec((1, d, de), lambda i, be, nu: (be[i], 0, 0)),
                pl.BlockSpec((1, de, d), lambda i, be, nu: (be[i], 0, 0)),
            ],
            out_specs=pl.BlockSpec((MOE_BLOCK, d), lambda i, be, nu: (i, 0)),
        ),
        out_shape=jax.ShapeDtypeStruct((nrow, d), jnp.float32),
        compiler_params=pltpu.CompilerParams(dimension_semantics=("arbitrary",), vmem_limit_bytes=VMEM_LIMIT),
        name="moe_experts",
    )(blk_exp, nused, xs, wg, wu, wd)


def _combine_kernel(starts_ref, ridx_ref, route_ref, x1_ref, mod_ref, ln_ref, ys_ref, out_ref, ybuf, sem):
    def row_copy(r, slot):
        src = _pair_dest(ridx_ref, starts_ref, r, slot)
        return pltpu.make_async_copy(ys_ref.at[pl.ds(src, 1), :], ybuf.at[slot, pl.ds(r, 1), :], sem)

    def issue(r, c):
        row_copy(r, 0).start()
        row_copy(r, 1).start()
        return c

    def drain(r, c):
        pltpu.make_async_copy(ys_ref.at[pl.ds(0, 1), :], ybuf.at[0, pl.ds(0, 1), :], sem).wait()
        return c

    lax.fori_loop(0, ROW_TILE, issue, 0)
    lax.fori_loop(0, 2 * ROW_TILE, drain, 0)
    route = route_ref[0]
    ffn = ybuf[0] * route[:, 4:5] + ybuf[1] * route[:, 5:6]
    g2 = mod_ref[0, 5:6, :]
    out_ref[0] = _ln_rows(ALPHA * x1_ref[0] + g2 * ffn, LN_EPS) * ln_ref[2:3, :] + ln_ref[3:4, :]


def _moe_combine(starts, ridx, route, x1, mod3, lnpar, ys):
    b, seq, d = x1.shape
    nt = seq // ROW_TILE
    return pl.pallas_call(
        _combine_kernel,
        grid_spec=pltpu.PrefetchScalarGridSpec(
            num_scalar_prefetch=1,
            grid=(b, nt),
            in_specs=[
                pl.BlockSpec((1, 8, ROW_TILE), lambda i, j, s: (i * nt + j, 0, 0), memory_space=pltpu.SMEM),
                pl.BlockSpec((1, ROW_TILE, ROUTE_W), lambda i, j, s: (i, j, 0)),
                pl.BlockSpec((1, ROW_TILE, d), lambda i, j, s: (i, j, 0)),
                pl.BlockSpec((1, 6, d), lambda i, j, s: (i, 0, 0)),
                pl.BlockSpec((8, d), lambda i, j, s: (0, 0)),
                pl.BlockSpec(memory_space=pl.ANY),
            ],
            out_specs=pl.BlockSpec((1, ROW_TILE, d), lambda i, j, s: (i, j, 0)),
            scratch_shapes=[pltpu.VMEM((2, ROW_TILE, d), jnp.float32), pltpu.SemaphoreType.DMA(())],
        ),
        out_shape=jax.ShapeDtypeStruct((b, seq, d), jnp.float32),
        compiler_params=pltpu.CompilerParams(dimension_semantics=("arbitrary", "arbitrary"),
                                             vmem_limit_bytes=VMEM_LIMIT),
        name="moe_combine",
    )(starts, ridx, route, x1, mod3, lnpar, ys)


def _moe_plan(cnt, n_pairs):
    counts = cnt[0, N_GROUPS:N_ROUTE].astype(jnp.int32)
    padded = (counts + MOE_BLOCK - 1) // MOE_BLOCK * MOE_BLOCK
    pad_ends = jnp.cumsum(padded)
    pad_starts = pad_ends - padded
    n_blk = -(-n_pairs // MOE_BLOCK) + N_EXPERTS
    blk_row0 = jnp.arange(n_blk, dtype=jnp.int32) * MOE_BLOCK
    blk_exp = jnp.minimum(jnp.sum((pad_ends[None, :] <= blk_row0[:, None]).astype(jnp.int32), axis=1), N_EXPERTS - 1)
    nused = (pad_ends[-1:] // MOE_BLOCK).astype(jnp.int32)
    return pad_starts.astype(jnp.int32), blk_exp, nused, n_blk


def _layernorm(z, gain=None, bias=None):
    zf = z.astype(jnp.float32)
    mu = jnp.mean(zf, -1, keepdims=True)
    var = jnp.mean(jnp.square(zf - mu), -1, keepdims=True)
    y = (zf - mu) * lax.rsqrt(var + LN_EPS)
    if gain is not None:
        y = y * gain + bias
    return y.astype(z.dtype)


def _headnorm(z, eps):
    zf = z.astype(jnp.float32)
    mu = jnp.mean(zf, -1, keepdims=True)
    var = jnp.mean(jnp.square(zf - mu), -1, keepdims=True)
    return (zf - mu) * lax.rsqrt(var + eps)


def _modulate(z, shift, scale):
    return _layernorm(z) * (1 + scale) + shift


def _rev_segments(u, ctx_len):
    return jnp.concatenate([jnp.flip(u[:, :ctx_len], 1), jnp.flip(u[:, ctx_len:], 1)], axis=1)


def _dir_stack(u_fwd, u_bwd, ctx_len):
    return jnp.concatenate([u_fwd, _rev_segments(u_bwd, ctx_len)], axis=0)


def _dir_merge(y, ctx_len):
    b = y.shape[0] // 2
    return y[:b] + _rev_segments(y[b:], ctx_len)


def _conv_grid(u, w, rows):
    b, l, ch = u.shape
    img = u.reshape(b, rows, l // rows, ch)
    out = lax.conv_general_dilated(img, w[:, :, None, :].astype(u.dtype), (1, 1), 'SAME',
                                   dimension_numbers=('NHWC', 'HWIO', 'NHWC'), feature_group_count=ch)
    return out.reshape(b, l, ch)


def _conv_seq(u, w_row):
    ch = u.shape[-1]
    return lax.conv_general_dilated(u, w_row[:, None, :].astype(u.dtype), (1,), 'SAME',
                                    dimension_numbers=('NWC', 'WIO', 'NWC'), feature_group_count=ch)


def _mlstm_chunkwise(q, k, v, log_i, log_f):
    z, t, h, dh = q.shape
    nc = t // CHUNK

    def chunks(a):
        a = a.astype(jnp.float32).reshape((z, nc, CHUNK, h) + a.shape[3:])
        return jnp.swapaxes(jnp.swapaxes(a, 0, 1), 2, 3)
    qc, kc, vc, lic, lfc = chunks(q), chunks(k), chunks(v), chunks(log_i), chunks(log_f)
    tri = jnp.tril(jnp.ones((CHUNK, CHUNK), bool))

    def step(carry, inp):
        c_mat, n_vec, m = carry
        qb, kb, vb, li, lf = inp
        b = jnp.cumsum(lf, -1)
        dmat = jnp.where(tri, b[..., :, None] - b[..., None, :] + li[..., None, :], -jnp.inf)
        m_inter = b + m[..., None]
        m_t = jnp.maximum(m_inter, jnp.max(dmat, -1))
        s = jnp.einsum('zhtd,zhsd->zhts', qb, kb) * jnp.exp(dmat - m_t[..., None])
        inter = jnp.exp(m_inter - m_t)
        num = inter[..., None] * jnp.einsum('zhtd,zhde->zhte', qb, c_mat) + jnp.einsum('zhts,zhse->zhte', s, vb)
        den = inter * jnp.einsum('zhtd,zhd->zht', qb, n_vec) + jnp.sum(s, -1)
        h_out = num / jnp.maximum(jnp.abs(den), jnp.exp(-m_t))[..., None]
        b_last = b[..., -1]
        g = b_last[..., None] - b + li
        m_new = jnp.maximum(b_last + m, jnp.max(g, -1))
        wts = jnp.exp(g - m_new[..., None])
        decay = jnp.exp(b_last + m - m_new)
        c_mat = decay[..., None, None] * c_mat + jnp.einsum('zhs,zhsd,zhse->zhde', wts, kb, vb)
        n_vec = decay[..., None] * n_vec + jnp.einsum('zhs,zhsd->zhd', wts, kb)
        return (c_mat, n_vec, m_new), h_out

    init = (jnp.zeros((z, h, dh, dh), jnp.float32), jnp.zeros((z, h, dh), jnp.float32), jnp.zeros((z, h), jnp.float32))
    _, hs = lax.scan(step, init, (qc, kc, vc, lic, lfc))
    return jnp.swapaxes(jnp.swapaxes(hs, 2, 3), 0, 1).reshape(z, t, h, dh)


def _mlstm_group(mq, mk, mv, mo, gates, b_i, b_f, norm_w, ctx_len):
    b, t, _ = mq.shape
    q = jax.nn.silu(mq)
    k = jax.nn.silu(mk) * (D_HM ** -0.5)
    bias_row = jnp.concatenate([b_i[0], b_f[0], b_i[1], b_f[1], jnp.zeros((128 - 4 * H_M,), jnp.float32)])[None, :]
    h_f, h_b = _mlstm_scan(q, k, mv, gates, bias_row, ctx_len)
    h = (h_f + h_b).reshape(b, t - ctx_len, H_M, D_HM)
    h = _headnorm(h, LN_EPS) * norm_w.reshape(H_M, D_HM)
    return jax.nn.sigmoid(mo[:, ctx_len:]) * h.reshape(b, t - ctx_len, W_M)


def _rwkv7_scan(r, w, k, v, kh, a):
    z, t, h, n = r.shape
    seq = tuple(jnp.moveaxis(s.astype(jnp.float32), 1, 0) for s in (r, w, k, v, kh, a))

    def step(state, inp):
        r_t, w_t, k_t, v_t, kh_t, a_t = inp
        removed = jnp.einsum('zhvk,zhk->zhv', state, kh_t)
        state = (state * w_t[:, :, None, :] - removed[..., None] * (kh_t * a_t)[:, :, None, :]
                 + v_t[..., None] * k_t[:, :, None, :])
        return state, jnp.einsum('zhvk,zhk->zhv', state, r_t)

    _, y = lax.scan(step, jnp.zeros((z, h, n, n), jnp.float32), seq)
    return jnp.moveaxis(y, 0, 1)


def _rwkv7_group(rr, rk, rv, lw_f, lw_b, la, lg, w0, w_b_mat, a0, a_b_mat, g_b_mat, kk, ka, bonus_w, gn_w, gn_b,
                 ctx_len):
    b, t, _ = rr.shape
    shp = (b, t, H_R, D_HR)
    logw_f = -DECAY_SCALE * jax.nn.sigmoid((w0[0] + jnp.tanh(lw_f) @ w_b_mat[0]).astype(jnp.float32))
    logw_b = -DECAY_SCALE * jax.nn.sigmoid((w0[1] + jnp.tanh(lw_b) @ w_b_mat[1]).astype(jnp.float32))
    a = jax.nn.sigmoid(a0 + la @ a_b_mat)
    g = jax.nn.sigmoid(lg) @ g_b_mat
    kap = (rk * kk).reshape(shp).astype(jnp.float32)
    kh = kap / jnp.maximum(jnp.sqrt(jnp.sum(jnp.square(kap), -1, keepdims=True)), 1e-12)
    kmod = (rk * (1 + (a - 1) * ka)).reshape(shp)
    r = rr.reshape(shp)
    v = rv.reshape(shp)
    a_h = a.reshape(shp)
    flat = lambda u: u.reshape(b, t, W_R)
    y_f, y_b = _rwkv_scan(rr, flat(kmod), rv, flat(kh), flat(kh * a_h), logw_f, logw_b, ctx_len)
    y = (y_f + y_b).reshape(b, t - ctx_len, H_R, D_HR)
    y = _headnorm(y, GN_EPS) * gn_w.reshape(H_R, D_HR) + gn_b.reshape(H_R, D_HR)
    bonus = (jnp.sum(r * kmod * bonus_w, -1, keepdims=True) * v)[:, ctx_len:]
    return (y + bonus).reshape(b, t - ctx_len, W_R) * g[:, ctx_len:]


def _hier_moe(h, rt_g, rt_g_b, rt_e, rt_e_b, ex_gate, ex_up, ex_down):
    b, t, d = h.shape
    tok = h.reshape(b * t, d)
    n = tok.shape[0]
    lg = (tok @ rt_g + rt_g_b).astype(jnp.float32)
    p_grp = jax.nn.softmax(lg, axis=-1)
    grp = jnp.argmax(lg, axis=-1).astype(jnp.int32)
    p_top = jnp.take_along_axis(p_grp, grp[:, None], axis=-1)
    le = (tok @ rt_e + rt_e_b).astype(jnp.float32).reshape(n, N_GROUPS, E_PER_GROUP)
    le_grp = jnp.take_along_axis(le, jnp.broadcast_to(grp[:, None, None], (n, 1, E_PER_GROUP)), axis=1)[:, 0]
    top_val, top_idx = lax.top_k(le_grp, TOP_K_IN_GROUP)
    wts = jax.nn.softmax(top_val, axis=-1) * p_top
    e_flat = (grp[:, None] * E_PER_GROUP + top_idx.astype(jnp.int32)).reshape(-1)
    w_flat = wts.reshape(-1)
    t_flat = jnp.repeat(jnp.arange(n, dtype=jnp.int32), TOP_K_IN_GROUP)
    order = jnp.argsort(e_flat)
    e_s, t_s, w_s = e_flat[order], t_flat[order], w_flat[order]
    counts = jnp.zeros((N_EXPERTS,), jnp.int32).at[e_flat].add(1)
    starts = jnp.cumsum(counts) - counts
    padded = (counts + MOE_BLOCK - 1) // MOE_BLOCK * MOE_BLOCK
    pad_ends = jnp.cumsum(padded)
    pad_starts = pad_ends - padded
    dest = pad_starts[e_s] + jnp.arange(e_s.shape[0], dtype=jnp.int32) - starts[e_s]
    n_pairs = n * TOP_K_IN_GROUP
    buf = (-(-n_pairs // MOE_BLOCK) + N_EXPERTS) * MOE_BLOCK
    tok_buf = jnp.full((buf,), n, jnp.int32).at[dest].set(t_s)
    w_buf = jnp.zeros((buf,), w_s.dtype).at[dest].set(w_s)
    n_blk = buf // MOE_BLOCK
    blk_exp = jnp.minimum(jnp.searchsorted(pad_ends, jnp.arange(n_blk, dtype=jnp.int32) * MOE_BLOCK, side='right'),
                          N_EXPERTS - 1)
    tok_pad = jnp.concatenate([tok, jnp.zeros((1, d), tok.dtype)], 0)

    def expert_block(args):
        idx, w, e = args
        xb = tok_pad[idx]
        hb = jax.nn.silu(xb @ ex_gate[e]) * (xb @ ex_up[e])
        return (hb @ ex_down[e]) * w[:, None]

    y = lax.map(expert_block, (tok_buf.reshape(n_blk, MOE_BLOCK), w_buf.reshape(n_blk, MOE_BLOCK), blk_exp))
    out = jnp.zeros((n + 1, d), y.dtype).at[tok_buf].add(y.reshape(buf, d))
    return out[:n].reshape(b, t, d).astype(h.dtype)


def kernel(x, c, ctx, c_ctx, w_ada, b_ada, w_in, conv_w, m_bias_i, m_bias_f, m_norm_w, r_w0, r_wB, r_a0, r_aB,
           r_gB, r_kk, r_ka, r_bonus, r_norm_w, r_norm_b, w_out, ln1_g, ln1_b, ln2_g, ln2_b, rt_g, rt_g_b, rt_e,
           rt_e_b, ex_gate, ex_up, ex_down):
    assert w_ada.shape[0] == DEPTH
    bsz, seq, d = x.shape
    ctx_len = ctx.shape[1]
    assert ctx_len == ROW_TILE and seq % ROW_TILE == 0
    assert seq % GRID_W == 0 and ROW_TILE % GRID_W == 0
    l = 0
    zrow = lambda n, w: jnp.zeros((n, w), jnp.float32)
    mrows = -(-(bsz + 1) // 8) * 8
    c_pad = jnp.concatenate([c, c_ctx[None, :], zrow(mrows - bsz - 1, d)], 0)
    mod_all = _ada(c_pad, w_ada[l], b_ada[l][None, :])[:bsz + 1]
    mod3 = mod_all.reshape(bsz + 1, 6, d)
    pc, pn = _inproj(ctx, x, mod3, _pad_w_in(w_in[l]))
    cw = jnp.concatenate([conv_w[l].reshape(CONV_K * CONV_K, CONV_CH), zrow(16 - CONV_K * CONV_K, CONV_CH)], 0)
    rpar = jnp.concatenate([r_w0[l], r_a0[l][None, :], r_kk[l][None, :], r_ka[l][None, :],
                            r_bonus[l].reshape(1, W_R), zrow(2, W_R)], 0)
    wbf = jnp.concatenate([r_wB[l][0], zrow(W_LORA, W_R)], 0)
    wbb = jnp.concatenate([zrow(W_LORA, W_R), r_wB[l][1]], 0)
    ab_mat = jnp.concatenate([r_aB[l], zrow(128 - A_LORA, W_R)], 0)
    q, k, rr, kmod, rv, kh, kha, logw_f, logw_b, gg, bonus = _prep(pc, pn, cw, rpar, wbf, wbb, ab_mat, r_gB[l])
    bias_row = jnp.concatenate([m_bias_i[l][0], m_bias_f[l][0], m_bias_i[l][1], m_bias_f[l][1],
                                jnp.zeros((128 - 4 * H_M,), jnp.float32)])[None, :]
    h_f, h_b = _mlstm_scan(q, k, pn, bias_row, ctx_len)
    y_f, y_b = _rwkv_scan(rr, kmod, rv, kh, kha, logw_f, logw_b, ctx_len)
    hpar = jnp.zeros((8, W_M), jnp.float32).at[0].set(m_norm_w[l]).at[1].set(r_norm_w[l]).at[2].set(r_norm_b[l])
    lnpar = jnp.zeros((8, d), jnp.float32).at[0].set(ln1_g[l]).at[1].set(ln1_b[l]).at[2].set(ln2_g[l]).at[3].set(
        ln2_b[l])
    rtw = jnp.concatenate([rt_g[l], rt_e[l], jnp.zeros((d, ROUTE_W - N_ROUTE), jnp.float32)], axis=1)
    rtb = jnp.concatenate([rt_g_b[l], rt_e_b[l], jnp.zeros((ROUTE_W - N_ROUTE,), jnp.float32)])[None, :]
    x1, tok, route, ridx, cnt = _outproj(h_f, h_b, pn, y_f, y_b, gg, bonus, x, mod3, w_out[l].astype(jnp.bfloat16),
                                         hpar, lnpar, rtw, rtb, ctx_len)
    n_tok = bsz * seq
    starts, blk_exp, nused, n_blk = _moe_plan(cnt, n_tok * TOP_K_IN_GROUP)
    xs = _moe_scatter(starts, ridx, tok.reshape(n_tok, d), jnp.zeros((n_blk * MOE_BLOCK, d), jnp.float32))
    ys = _moe_experts(blk_exp, nused, xs, ex_gate[l].astype(jnp.bfloat16), ex_up[l].astype(jnp.bfloat16),
                      ex_down[l].astype(jnp.bfloat16))
    return _moe_combine(starts, ridx, route, x1, mod3, lnpar, ys)
```

```python
import functools
import math

import jax
import jax.numpy as jnp
from jax import lax
import numpy as np
from jax.experimental import pallas as pl
from jax.experimental.pallas import tpu as pltpu

H_M = 8
D_HM = 64
W_M = H_M * D_HM
H_R = 8
D_HR = 64
W_R = H_R * D_HR
MIX_W = W_M + W_R
W_LORA = 64
A_LORA = 64
G_LORA = 128
CONV_K = 3
CONV_CH = 2 * W_M + 3 * W_R
SECTION_WIDTHS = (W_M, W_M, W_R, W_R, W_R, W_M, W_M, 4 * H_M, W_LORA, W_LORA, A_LORA, G_LORA)
IN_COLS = sum(SECTION_WIDTHS)
SPLIT_POINTS = tuple(int(v) for v in np.cumsum(SECTION_WIDTHS)[:-1])
CHUNK = 64
N_GROUPS = 4
E_PER_GROUP = 8
N_EXPERTS = N_GROUPS * E_PER_GROUP
TOP_K_IN_GROUP = 2
D_EXPERT = 512
MOE_BLOCK = 256
DEPTH = 1
ALPHA = (2.0 * DEPTH) ** 0.25
DECAY_SCALE = math.exp(-0.5)
LN_EPS = 1e-6
GN_EPS = 64e-5

ROW_TILE = 256
NONCONV_W = 1536
VMEM_LIMIT = 56 * 1024 * 1024
DMA_UNROLL = 8


def _inproj_kernel(ctx_ref, x_ref, mod_ref, w_ref, oc_ref, on_ref):
    j = pl.program_id(1)

    def body(src_ref):
        z = src_ref[0]
        mu = jnp.mean(z, -1, keepdims=True)
        zc = z - mu
        var = jnp.mean(zc * zc, -1, keepdims=True)
        y = zc * lax.rsqrt(var + LN_EPS)
        h = (y * (1.0 + mod_ref[0, 1:2, :]) + mod_ref[0, 0:1, :]).astype(jnp.bfloat16)
        oc_ref[0] = jnp.dot(h, w_ref[:, :CONV_CH], preferred_element_type=jnp.float32)
        on_ref[0] = jnp.dot(h, w_ref[:, CONV_CH:], preferred_element_type=jnp.float32)

    @pl.when(j == 0)
    def _():
        body(ctx_ref)

    @pl.when(j > 0)
    def _():
        body(x_ref)


def _inproj(ctx, x, mod_all, w_pad):
    b, seq, d = x.shape
    nt = (ctx.shape[1] + seq) // ROW_TILE
    t_all = ctx.shape[1] + seq
    return pl.pallas_call(
        _inproj_kernel,
        grid=(b, nt),
        in_specs=[
            pl.BlockSpec((1, ROW_TILE, d), lambda i, j: (i, 0, 0)),
            pl.BlockSpec((1, ROW_TILE, d), lambda i, j: (i, jnp.maximum(j - 1, 0), 0)),
            pl.BlockSpec((1, 6, d), lambda i, j: (jnp.where(j == 0, b, i), 0, 0)),
            pl.BlockSpec((d, CONV_CH + NONCONV_W), lambda i, j: (0, 0)),
        ],
        out_specs=[
            pl.BlockSpec((1, ROW_TILE, CONV_CH), lambda i, j: (i, j, 0)),
            pl.BlockSpec((1, ROW_TILE, NONCONV_W), lambda i, j: (i, j, 0)),
        ],
        out_shape=[
            jax.ShapeDtypeStruct((b, t_all, CONV_CH), jnp.float32),
            jax.ShapeDtypeStruct((b, t_all, NONCONV_W), jnp.float32),
        ],
        compiler_params=pltpu.CompilerParams(
            dimension_semantics=("arbitrary", "arbitrary"), vmem_limit_bytes=VMEM_LIMIT),
        name="inproj",
    )(ctx, x, mod_all, w_pad)


def _pad_w_in(w_in):
    d = w_in.shape[0]
    sp = SPLIT_POINTS
    z = lambda n: jnp.zeros((d, n), w_in.dtype)
    gates = w_in[:, sp[6]:sp[7]]
    lw = w_in[:, sp[7]:sp[9]]
    la = w_in[:, sp[9]:sp[10]]
    lg = w_in[:, sp[10]:]
    return jnp.concatenate([w_in[:, :sp[6]], gates, z(96), lw, la, z(64), lg], axis=1).astype(jnp.bfloat16)


PAIR = 2 * D_HR
NPAIR = H_R // 2
_NN = (((1,), (0,)), ((), ()))
_NT = (((1,), (1,)), ((), ()))
_TN = (((0,), (0,)), ((), ()))


def _split2(a):
    hi = a.astype(jnp.bfloat16)
    lo = (a - hi.astype(jnp.float32)).astype(jnp.bfloat16)
    return hi, lo


def _split3(a):
    a1 = a.astype(jnp.bfloat16)
    r1 = a - a1.astype(jnp.float32)
    a2 = r1.astype(jnp.bfloat16)
    a3 = (r1 - a2.astype(jnp.float32)).astype(jnp.bfloat16)
    return a1, a2, a3


def _dg(a, b, dims):
    return lax.dot_general(a, b, dims, preferred_element_type=jnp.float32)


def _dot3(a, b, dims=_NN):
    ah, al = _split2(a)
    bh, bl = _split2(b)
    return _dg(ah, bh, dims) + _dg(ah, bl, dims) + _dg(al, bh, dims)


def _dot1(a, b, dims=_NN):
    return _dg(a.astype(jnp.bfloat16), b.astype(jnp.bfloat16), dims)


def _dot_exact_lhs(a_bf16, b, dims=_NN):
    b1, b2, b3 = _split3(b)
    return _dg(a_bf16, b1, dims) + _dg(a_bf16, b2, dims) + _dg(a_bf16, b3, dims)


def _dot_exact_rhs(a, b_bf16):
    a1, a2, a3 = _split3(a)
    return _dg(a1, b_bf16, _NN) + _dg(a2, b_bf16, _NN) + _dg(a3, b_bf16, _NN)


def _head_block_diag(width, head, value):
    hrow = lax.broadcasted_iota(jnp.int32, (width, width), 0) // head
    hcol = lax.broadcasted_iota(jnp.int32, (width, width), 1) // head
    return jnp.where(hrow == hcol, value, 0.0).astype(jnp.bfloat16)


def _ada_kernel(c_ref, w_ref, b_ref, o_ref):
    o_ref[...] = _dot3(jax.nn.silu(c_ref[...]), w_ref[...]) + b_ref[...]


def _ada(c_pad, w, bias):
    m, d = c_pad.shape
    n = w.shape[1]
    return pl.pallas_call(
        _ada_kernel,
        grid=(n // d,),
        in_specs=[pl.BlockSpec((m, d), lambda j: (0, 0)),
                  pl.BlockSpec((d, d), lambda j: (0, j)),
                  pl.BlockSpec((1, d), lambda j: (0, j))],
        out_specs=pl.BlockSpec((m, d), lambda j: (0, j)),
        out_shape=jax.ShapeDtypeStruct((m, n), jnp.float32),
        compiler_params=pltpu.CompilerParams(dimension_semantics=("arbitrary",), vmem_limit_bytes=VMEM_LIMIT),
        name="adaln",
    )(c_pad, w, bias)


GRID_W = 64
HALO = GRID_W
XOFF = 8


def _prep_kernel(top_ref, main_ref, bot_ref, pn_ref, cw_ref, rpar_ref, wbf_ref, wbb_ref, ab_ref, gb_ref,
                 q_ref, k_ref, r_ref, km_ref, v_ref, kh_ref, abo_ref, lwf_ref, lwb_ref, gg_ref, bon_ref, xbuf):
    j = pl.program_id(1)
    nt = pl.num_programs(1)
    is_ctx = j == 0
    top_ok = j >= 2
    bot_ok = (j >= 1) & (j < nt - 1)
    l_idx = lax.broadcasted_iota(jnp.int32, (ROW_TILE, 1), 0)
    col = jnp.where(is_ctx, l_idx, l_idx % GRID_W)
    left_ok = col != 0
    right_ok = col != jnp.where(is_ctx, ROW_TILE - 1, GRID_W - 1)
    vert = jnp.where(is_ctx, 0.0, 1.0)
    xbuf[0:XOFF, :] = jnp.zeros((XOFF, W_M), jnp.float32)
    xbuf[XOFF + 2 * HALO + ROW_TILE:, :] = jnp.zeros((XOFF, W_M), jnp.float32)
    sec = {}
    for s in range(5):
        sl = slice(s * W_M, (s + 1) * W_M)
        xbuf[XOFF:XOFF + HALO, :] = jnp.where(top_ok, top_ref[0, :, sl], 0.0)
        xbuf[XOFF + HALO:XOFF + HALO + ROW_TILE, :] = main_ref[0, :, sl]
        xbuf[XOFF + HALO + ROW_TILE:XOFF + 2 * HALO + ROW_TILE, :] = jnp.where(bot_ok, bot_ref[0, :, sl], 0.0)
        acc = None
        for dr in range(3):
            base = XOFF + HALO + GRID_W * (dr - 1)
            scale = 1.0 if dr == 1 else vert
            left = jnp.where(left_ok, xbuf[base - 1:base - 1 + ROW_TILE, :], 0.0)
            mid = xbuf[base:base + ROW_TILE, :]
            right = jnp.where(right_ok, xbuf[base + 1:base + 1 + ROW_TILE, :], 0.0)
            term = (left * cw_ref[3 * dr:3 * dr + 1, sl] + mid * cw_ref[3 * dr + 1:3 * dr + 2, sl]
                    + right * cw_ref[3 * dr + 2:3 * dr + 3, sl]) * scale
            acc = term if acc is None else acc + term
        sec[s] = acc
    q_ref[0] = jax.nn.silu(sec[0])
    k_ref[0] = jax.nn.silu(sec[1]) * (D_HM ** -0.5)
    rr, rk, rv = sec[2], sec[3], sec[4]
    r_ref[0] = rr
    v_ref[0] = rv
    lw = jnp.tanh(pn_ref[0, :, 1152:1280])
    lwf_ref[0] = -DECAY_SCALE * jax.nn.sigmoid(rpar_ref[0:1, :] + _dot3(lw, wbf_ref[...]))
    lwb_ref[0] = -DECAY_SCALE * jax.nn.sigmoid(rpar_ref[1:2, :] + _dot3(lw, wbb_ref[...]))
    a = jax.nn.sigmoid(rpar_ref[2:3, :] + _dot3(pn_ref[0, :, 1280:1408], ab_ref[...]))
    gg_ref[0] = _dot3(jax.nn.sigmoid(pn_ref[0, :, 1408:1536]), gb_ref[...])
    ones_bd = _head_block_diag(W_R, D_HR, 1.0)
    kap = rk * rpar_ref[3:4, :]
    norm = jnp.sqrt(_dot_exact_rhs(kap * kap, ones_bd))
    kh = kap / jnp.maximum(norm, 1e-12)
    kmod = rk * (1.0 + (a - 1.0) * rpar_ref[4:5, :])
    kh_ref[0] = kh
    abo_ref[0] = kh * a
    km_ref[0] = kmod
    bon_ref[0] = _dot_exact_rhs(rr * kmod * rpar_ref[5:6, :], ones_bd) * rv


def _prep(pc, pn, cw, rpar, wbf, wbb, ab, gb):
    b, t, _ = pc.shape
    nt = t // ROW_TILE
    per = ROW_TILE // HALO
    nh = t // HALO
    w = W_M
    full = lambda shp: pl.BlockSpec(shp, lambda i, j: tuple(0 for _ in shp))
    in_specs = [
        pl.BlockSpec((1, HALO, CONV_CH), lambda i, j: (i, jnp.maximum(j * per - 1, 0), 0)),
        pl.BlockSpec((1, ROW_TILE, CONV_CH), lambda i, j: (i, j, 0)),
        pl.BlockSpec((1, HALO, CONV_CH), lambda i, j: (i, jnp.minimum(j * per + per, nh - 1), 0)),
        pl.BlockSpec((1, ROW_TILE, NONCONV_W), lambda i, j: (i, j, 0)),
        full((16, CONV_CH)), full((8, w)), full((128, w)), full((128, w)), full((128, w)), full((128, w)),
    ]
    out = pl.BlockSpec((1, ROW_TILE, w), lambda i, j: (i, j, 0))
    return pl.pallas_call(
        _prep_kernel,
        grid=(b, nt),
        in_specs=in_specs,
        out_specs=[out] * 11,
        out_shape=[jax.ShapeDtypeStruct((b, t, w), jnp.float32)] * 11,
        scratch_shapes=[pltpu.VMEM((2 * XOFF + 2 * HALO + ROW_TILE, w), jnp.float32)],
        compiler_params=pltpu.CompilerParams(
            dimension_semantics=("arbitrary", "arbitrary"), vmem_limit_bytes=VMEM_LIMIT),
        name="conv_prep",
    )(pc, pc, pc, pn, cw, rpar, wbf, wbb, ab, gb)


def _round_robin(gens):
    out = [None] * len(gens)
    live = list(range(len(gens)))
    while live:
        nxt = []
        for i in live:
            try:
                next(gens[i])
                nxt.append(i)
            except StopIteration as stop:
                out[i] = stop.value
        live = nxt
    return out


def _rwkv_pair_chunk(r, k, v, kh, ab, lw, s_mat, tri_incl, strict, incl, last_row, m0, m1):
    cw = _dot_exact_lhs(tri_incl, lw)
    yield
    e_pos = jnp.exp(cw)
    e_neg = jnp.exp(-cw)
    e_prev = jnp.exp(cw - lw)
    stack = lambda a: jnp.concatenate([a * m0, a * m1], axis=0)
    kt = stack(kh * e_prev)
    bt = stack(ab * e_neg)
    kk = stack(k * e_neg)
    rt = stack(r * e_pos)
    vs = stack(v)
    zero = jnp.zeros((), jnp.float32)
    kr = jnp.concatenate([kt, rt], axis=0)
    bk = jnp.concatenate([bt, kk], axis=0)
    amat = _dot1(kr, bk, _NT)
    yield
    a_bk = jnp.where(strict, amat[:PAIR, :PAIR], zero)
    a_kk = jnp.where(strict, amat[:PAIR, PAIR:], zero)
    a_rb = jnp.where(incl, amat[PAIR:, :PAIR], zero)
    a_rk = jnp.where(incl, amat[PAIR:, PAIR:], zero)
    n_mat = -a_bk
    eye = (lax.broadcasted_iota(jnp.int32, (PAIR, PAIR), 0)
           == lax.broadcasted_iota(jnp.int32, (PAIR, PAIR), 1)).astype(jnp.float32)
    q = eye + n_mat
    p = _dot1(n_mat, n_mat)
    ks = _dot1(kr, s_mat, _NT)
    av = _dot1(jnp.concatenate([a_kk, a_rk], axis=0), vs)
    yield
    for _ in range(4):
        qp = _dot1(jnp.concatenate([q, p], axis=0), p)
        yield
        q = q + qp[:PAIR]
        p = qp[PAIR:]
    t_inv = q + _dot1(q, p)
    yield
    u = -_dot1(t_inv, ks[:PAIR] + av[:PAIR])
    yield
    y = ks[PAIR:] + av[PAIR:] + _dot1(a_rb, u)
    yield
    w_last = jnp.sum(jnp.where(last_row, e_pos, zero), axis=0, keepdims=True)
    s_new = (s_mat + _dot1(jnp.concatenate([u, vs], axis=0), bk, _TN)) * w_last
    return y[:CHUNK] + y[CHUNK:], s_new


def _rwkv_kernel(rf, kf, vf, khf, abf, lwf, rb, kb, vb, khb, abb, lwb, yf_ref, yb_ref, s_ref, *, nctx):
    j = pl.program_id(1)

    @pl.when(j == 0)
    def _():
        s_ref[...] = jnp.zeros_like(s_ref)

    row = lax.broadcasted_iota(jnp.int32, (PAIR, PAIR), 0)
    col = lax.broadcasted_iota(jnp.int32, (PAIR, PAIR), 1)
    same = (row // CHUNK) == (col // CHUNK)
    r64 = lax.broadcasted_iota(jnp.int32, (CHUNK, CHUNK), 0)
    c64 = lax.broadcasted_iota(jnp.int32, (CHUNK, CHUNK), 1)
    rowl = lax.broadcasted_iota(jnp.int32, (CHUNK, PAIR), 0)
    lane = lax.broadcasted_iota(jnp.int32, (1, PAIR), 1)
    m0 = (lane < D_HR).astype(jnp.float32)
    m1 = 1.0 - m0
    dirs = (
        (rf, kf, vf, khf, abf, lwf, yf_ref, (c64 <= r64), same & (col < row), same & (col <= row), rowl == CHUNK - 1),
        (rb, kb, vb, khb, abb, lwb, yb_ref, (c64 >= r64), same & (col > row), same & (col >= row), rowl == 0),
    )
    states = [[s_ref[d, p] for p in range(NPAIR)] for d in range(2)]
    gens, meta = [], []
    for d, (r_ref, k_ref, v_ref, kh_ref, ab_ref, lw_ref, y_ref, tri, strict, incl, last_row) in enumerate(dirs):
        tri = tri.astype(jnp.float32).astype(jnp.bfloat16)
        for p in range(NPAIR):
            sl = slice(p * PAIR, (p + 1) * PAIR)
            gens.append(_rwkv_pair_chunk(r_ref[0, :, sl], k_ref[0, :, sl], v_ref[0, :, sl], kh_ref[0, :, sl],
                                         ab_ref[0, :, sl], lw_ref[0, :, sl], states[d][p], tri, strict, incl,
                                         last_row, m0, m1))
            meta.append((d, p, sl, y_ref))
    results = [m + r for m, r in zip(meta, _round_robin(gens))]
    for d, p, sl, y_ref, y, s_new in results:
        s_ref[d, p] = s_new
        y_ref[0, :, sl] = y


def _rwkv_scan(r, k, v, kh, ab, lw_f, lw_b, ctx_len):
    b, t, w = r.shape
    nc = t // CHUNK
    nctx = ctx_len // CHUNK
    nlat = nc - nctx

    def fwd_map(i, j):
        return (i, j, 0)

    def bwd_map(i, j):
        return (i, jnp.where(j < nctx, nctx - 1 - j, nc - 1 - (j - nctx)), 0)

    blk = (1, CHUNK, w)
    in_specs = [pl.BlockSpec(blk, fwd_map)] * 6 + [pl.BlockSpec(blk, bwd_map)] * 6
    out_specs = [
        pl.BlockSpec(blk, lambda i, j: (i, jnp.maximum(j - nctx, 0), 0)),
        pl.BlockSpec(blk, lambda i, j: (i, nlat - 1 - jnp.maximum(j - nctx, 0), 0)),
    ]
    return pl.pallas_call(
        functools.partial(_rwkv_kernel, nctx=nctx),
        grid=(b, nc),
        in_specs=in_specs,
        out_specs=out_specs,
        out_shape=[jax.ShapeDtypeStruct((b, t - ctx_len, w), jnp.float32)] * 2,
        scratch_shapes=[pltpu.VMEM((2, NPAIR, PAIR, PAIR), jnp.float32)],
        compiler_params=pltpu.CompilerParams(
            dimension_semantics=("arbitrary", "arbitrary"), vmem_limit_bytes=VMEM_LIMIT),
        name="rwkv_scan",
    )(r, k, v, kh, ab, lw_f, r, k, v, kh, ab, lw_b)


def _col_stack(a, c0, c1):
    return jnp.concatenate([a[:, c0:c0 + 1], a[:, c1:c1 + 1]], axis=0)


def _half_bcast(v0, v1):
    return jnp.concatenate([jnp.broadcast_to(v0, (CHUNK, 1)), jnp.broadcast_to(v1, (CHUNK, 1))], axis=0)


def _mlstm_pair_chunk(q, k, v, b_col, li_col, c_row, c_mat, n_row, m_row, incl, last_sel, m0, m1):
    stack = lambda a: jnp.concatenate([a * m0, a * m1], axis=0)
    q_st, k_st, v_st = stack(q), stack(k), stack(v)
    lane = lax.broadcasted_iota(jnp.int32, (1, PAIR), 1)
    first = lane < D_HM
    neg_inf = jnp.full((), -jnp.inf, jnp.float32)
    m_h0, m_h1 = m_row[:, 0:1], m_row[:, D_HM:D_HM + 1]
    m_prev = _half_bcast(m_h0, m_h1)
    dmat = jnp.where(incl, b_col + c_row, neg_inf)
    m_inter = b_col + m_prev
    m_t = jnp.maximum(m_inter, jnp.max(dmat, axis=1, keepdims=True))
    bl = jnp.where(last_sel, b_col, neg_inf)
    bl0 = jnp.max(bl[:CHUNK], axis=0, keepdims=True)
    bl1 = jnp.max(bl[CHUNK:], axis=0, keepdims=True)
    g = _half_bcast(bl0, bl1) - b_col + li_col
    mn0 = jnp.maximum(bl0 + m_h0, jnp.max(g[:CHUNK], axis=0, keepdims=True))
    mn1 = jnp.maximum(bl1 + m_h1, jnp.max(g[CHUNK:], axis=0, keepdims=True))
    wts = jnp.exp(g - _half_bcast(mn0, mn1))
    decay_row = jnp.where(first, jnp.exp(bl0 + m_h0 - mn0), jnp.exp(bl1 + m_h1 - mn1))
    wk = wts * k_st
    qk = _dot1(q_st, k_st, _NT)
    qc = _dot1(q_st, c_mat)
    kv = _dot1(wk, v_st, _TN)
    yield
    s = qk * jnp.exp(dmat - m_t)
    inter = jnp.exp(m_inter - m_t)
    sv = _dot1(s, v_st)
    yield
    num = inter * qc + sv
    den = inter * jnp.sum(q_st * n_row, axis=1, keepdims=True) + jnp.sum(s, axis=1, keepdims=True)
    h_st = num / jnp.maximum(jnp.abs(den), jnp.exp(-m_t))
    h = h_st[:CHUNK] + h_st[CHUNK:]
    c_new = decay_row * c_mat + kv
    n_new = decay_row * n_row + jnp.sum(wk, axis=0, keepdims=True)
    m_new = jnp.where(first, mn0, mn1)
    return h, c_new, n_new, m_new


def _mlstm_kernel(qf, kf, vf, gf, qb, kb, vb, gb, bias_ref, hf_ref, hb_ref, c_ref, n_ref, m_ref, *, nctx):
    j = pl.program_id(1)

    @pl.when(j == 0)
    def _():
        c_ref[...] = jnp.zeros_like(c_ref)
        n_ref[...] = jnp.zeros_like(n_ref)
        m_ref[...] = jnp.zeros_like(m_ref)

    row = lax.broadcasted_iota(jnp.int32, (PAIR, PAIR), 0)
    col = lax.broadcasted_iota(jnp.int32, (PAIR, PAIR), 1)
    same = (row // CHUNK) == (col // CHUNK)
    r64 = lax.broadcasted_iota(jnp.int32, (CHUNK, CHUNK), 0)
    c64 = lax.broadcasted_iota(jnp.int32, (CHUNK, CHUNK), 1)
    rcol = lax.broadcasted_iota(jnp.int32, (PAIR, 1), 0) % CHUNK
    lane = lax.broadcasted_iota(jnp.int32, (1, PAIR), 1)
    m0 = (lane < D_HM).astype(jnp.float32)
    m1 = 1.0 - m0
    is_f = (lane % (2 * H_M)) >= H_M
    dirs = (
        (qf, kf, vf, gf, hf_ref, (c64 <= r64), same & (col <= row), rcol == CHUNK - 1),
        (qb, kb, vb, gb, hb_ref, (c64 >= r64), same & (col >= row), rcol == 0),
    )
    states = [[(c_ref[d, p], n_ref[d, p], m_ref[d, p]) for p in range(NPAIR)] for d in range(2)]
    gens, meta = [], []
    for d, (q_ref, k_ref, v_ref, g_ref, h_ref, tri, incl, last_sel) in enumerate(dirs):
        tri = tri.astype(jnp.float32).astype(jnp.bfloat16)
        gl = g_ref[0] + bias_ref[...]
        act = jnp.where(is_f, jax.nn.log_sigmoid(gl), gl)
        b_all = _dot_exact_lhs(tri, act)
        act_t = act.T
        b_t = b_all.T
        for p in range(NPAIR):
            sl = slice(p * PAIR, (p + 1) * PAIR)
            ci, cf = d * 2 * H_M + 2 * p, d * 2 * H_M + H_M + 2 * p
            b_col = _col_stack(b_all, cf, cf + 1)
            li_col = _col_stack(act, ci, ci + 1)
            c_row = jnp.concatenate([act_t[ci:ci + 1, :] - b_t[cf:cf + 1, :],
                                     act_t[ci + 1:ci + 2, :] - b_t[cf + 1:cf + 2, :]], axis=1)
            gens.append(_mlstm_pair_chunk(
                q_ref[0, :, sl], k_ref[0, :, sl], v_ref[0, :, sl], b_col, li_col, c_row,
                *states[d][p], incl, last_sel, m0, m1))
            meta.append((d, p, sl, h_ref))
    for (d, p, sl, h_ref), (h, c_new, n_new, m_new) in zip(meta, _round_robin(gens)):
        c_ref[d, p] = c_new
        n_ref[d, p] = n_new
        m_ref[d, p] = m_new
        h_ref[0, :, sl] = h


def _scan_max(x, reverse):
    rows = lax.broadcasted_iota(jnp.int32, x.shape, 0)
    neg_inf = jnp.full((), -jnp.inf, jnp.float32)
    step = 1
    while step < CHUNK:
        if reverse:
            shifted = jnp.where(rows < CHUNK - step, pltpu.roll(x, CHUNK - step, 0), neg_inf)
        else:
            shifted = jnp.where(rows >= step, pltpu.roll(x, step, 0), neg_inf)
        x = jnp.maximum(x, shifted)
        step *= 2
    return x


def _mlstm_gates(gates, bias, tri, m_prev, is_f, reverse):
    gl = gates + bias
    act = jnp.where(is_f, jax.nn.log_sigmoid(gl), gl)
    b_all = _dot_exact_lhs(tri, act)
    li = pltpu.roll(act, H_M, 1)
    x = li - b_all
    m_inter = b_all + m_prev
    m_t = jnp.maximum(m_inter, b_all + _scan_max(x, reverse))
    b_last = b_all[0:1, :] if reverse else b_all[CHUNK - 1:CHUNK, :]
    g = b_last + x
    m_new = jnp.maximum(b_last + m_prev, jnp.max(g, axis=0, keepdims=True))
    wts = jnp.exp(g - m_new)
    decay = jnp.exp(b_last + m_prev - m_new)
    return b_all - m_t, jnp.exp(m_inter - m_t), jnp.exp(-m_t), x, wts, decay, m_new


def _mlstm_pair_chunk2(q, k, v, pieces, cf, decay_g, c_mat, n_row, incl, m0, m1):
    stack = lambda a: jnp.concatenate([a * m0, a * m1], axis=0)
    q_st, k_st, v_st = stack(q), stack(k), stack(v)
    lane = lax.broadcasted_iota(jnp.int32, (1, PAIR), 1)
    zero_bf = jnp.zeros((), jnp.bfloat16)
    sel = lambda a: jnp.concatenate([jnp.where(lane == cf, a, zero_bf), jnp.where(lane == cf + 1, a, zero_bf)], 0)
    wide = lambda parts: jnp.concatenate([sel(a) for a in parts], axis=1)
    ones = jnp.ones((3 * PAIR, PAIR), jnp.bfloat16)
    bc = _dg(jnp.concatenate([wide(p) for p in pieces[:4]], axis=0), ones, _NN)
    r_mat = _dg(jnp.ones((PAIR, 3 * PAIR), jnp.bfloat16), wide(pieces[4]), _NT)
    bm_bc, inter_bc, enm_bc, wts_bc = (bc[i * PAIR:(i + 1) * PAIR] for i in range(4))
    qk = _dot1(q_st, k_st, _NT)
    qc = _dot1(q_st, c_mat)
    yield
    neg_inf = jnp.full((), -jnp.inf, jnp.float32)
    s = qk * jnp.exp(jnp.where(incl, bm_bc + r_mat, neg_inf))
    wk = wts_bc * k_st
    sv = _dot1(s, v_st)
    kv = _dot1(wk, v_st, _TN)
    sums = _dot1(jnp.concatenate([q_st * n_row, s], axis=0), jnp.ones((PAIR, PAIR), jnp.float32))
    yield
    den = inter_bc * sums[:PAIR] + sums[PAIR:]
    h_st = (inter_bc * qc + sv) / jnp.maximum(jnp.abs(den), enm_bc)
    h = h_st[:CHUNK] + h_st[CHUNK:]
    decay_row = jnp.where(lane < D_HM, decay_g[:, cf:cf + 1], decay_g[:, cf + 1:cf + 2])
    c_new = decay_row * c_mat + kv
    n_new = decay_row * n_row + jnp.sum(wk, axis=0, keepdims=True)
    return h, c_new, n_new


def _mlstm_kernel2(qf, kf, vf, gf, qb, kb, vb, gb, bias_ref, hf_ref, hb_ref, c_ref, n_ref, m_ref, *, nctx):
    del nctx
    j = pl.program_id(1)

    @pl.when(j == 0)
    def _():
        c_ref[...] = jnp.zeros_like(c_ref)
        n_ref[...] = jnp.zeros_like(n_ref)
        m_ref[...] = jnp.zeros_like(m_ref)

    row = lax.broadcasted_iota(jnp.int32, (PAIR, PAIR), 0)
    col = lax.broadcasted_iota(jnp.int32, (PAIR, PAIR), 1)
    same = (row // CHUNK) == (col // CHUNK)
    r64 = lax.broadcasted_iota(jnp.int32, (CHUNK, CHUNK), 0)
    c64 = lax.broadcasted_iota(jnp.int32, (CHUNK, CHUNK), 1)
    lane = lax.broadcasted_iota(jnp.int32, (1, PAIR), 1)
    m0 = (lane < D_HM).astype(jnp.float32)
    m1 = 1.0 - m0
    is_f = (lane % (2 * H_M)) >= H_M
    dirs = (
        (qf, kf, vf, gf, hf_ref, (c64 <= r64), same & (col <= row)),
        (qb, kb, vb, gb, hb_ref, (c64 >= r64), same & (col >= row)),
    )
    states = [[(c_ref[d, p], n_ref[d, p]) for p in range(NPAIR)] for d in range(2)]
    gens, meta, m_news = [], [], []
    for d, (q_ref, k_ref, v_ref, g_ref, h_ref, tri, incl) in enumerate(dirs):
        tri = tri.astype(jnp.float32).astype(jnp.bfloat16)
        bm, inter, enm, x, wts, decay_g, m_new = _mlstm_gates(g_ref[0], bias_ref[...], tri, m_ref[d], is_f, d == 1)
        m_news.append(m_new)
        pieces = [_split3(a) for a in (bm, inter, enm, wts, x)]
        for p in range(NPAIR):
            sl = slice(p * PAIR, (p + 1) * PAIR)
            cf = d * 2 * H_M + H_M + 2 * p
            gens.append(_mlstm_pair_chunk2(q_ref[0, :, sl], k_ref[0, :, sl], v_ref[0, :, sl], pieces, cf, decay_g,
                                           *states[d][p], incl, m0, m1))
            meta.append((d, p, sl, h_ref))
    for (d, p, sl, h_ref), (h, c_new, n_new) in zip(meta, _round_robin(gens)):
        c_ref[d, p] = c_new
        n_ref[d, p] = n_new
        h_ref[0, :, sl] = h
    for d in range(2):
        m_ref[d] = m_news[d]


def _mlstm_scan(q, k, pn, bias_row, ctx_len):
    v = gates = pn
    b, t, w = q.shape
    nc = t // CHUNK
    nctx = ctx_len // CHUNK
    nlat = nc - nctx

    def fwd_map(i, j):
        return (i, j, 0)

    def bwd_map(i, j):
        return (i, jnp.where(j < nctx, nctx - 1 - j, nc - 1 - (j - nctx)), 0)

    blk = (1, CHUNK, w)
    gblk = (1, CHUNK, 128)
    gate_blk = (2 * W_M) // 128
    gate_of = lambda m: (lambda i, j: m(i, j)[:2] + (gate_blk,))
    in_specs = ([pl.BlockSpec(blk, fwd_map)] * 3 + [pl.BlockSpec(gblk, gate_of(fwd_map))]
                + [pl.BlockSpec(blk, bwd_map)] * 3 + [pl.BlockSpec(gblk, gate_of(bwd_map))]
                + [pl.BlockSpec((1, 128), lambda i, j: (0, 0))])
    out_specs = [
        pl.BlockSpec(blk, lambda i, j: (i, jnp.maximum(j - nctx, 0), 0)),
        pl.BlockSpec(blk, lambda i, j: (i, nlat - 1 - jnp.maximum(j - nctx, 0), 0)),
    ]
    return pl.pallas_call(
        functools.partial(_mlstm_kernel2, nctx=nctx),
        grid=(b, nc),
        in_specs=in_specs,
        out_specs=out_specs,
        out_shape=[jax.ShapeDtypeStruct((b, t - ctx_len, w), jnp.float32)] * 2,
        scratch_shapes=[pltpu.VMEM((2, NPAIR, PAIR, PAIR), jnp.float32),
                        pltpu.VMEM((2, NPAIR, 1, PAIR), jnp.float32),
                        pltpu.VMEM((2, 1, 128), jnp.float32)],
        compiler_params=pltpu.CompilerParams(
            dimension_semantics=("arbitrary", "arbitrary"), vmem_limit_bytes=VMEM_LIMIT),
        name="mlstm_scan",
    )(q, k, v, gates, q, k, v, gates, bias_row)


ROUTE_W = 128
N_ROUTE = N_GROUPS + N_EXPERTS


def _ln_rows(z, eps):
    mu = jnp.mean(z, -1, keepdims=True)
    zc = z - mu
    var = jnp.mean(zc * zc, -1, keepdims=True)
    return zc * lax.rsqrt(var + eps)


def _headnorm_mxu(z, avg_bf16, eps):
    mu = _dot_exact_rhs(z, avg_bf16)
    zc = z - mu
    var = _dot_exact_rhs(zc * zc, avg_bf16)
    return zc * lax.rsqrt(var + eps)


def _outproj_kernel(hf, hb, mo, yf, yb, gg, bonus, x_ref, mod_ref, wout_ref, hpar_ref, ln_ref, rtw_ref, rtb_ref,
                    x1_ref, tok_ref, route_ref, ridx_ref, cnt_ref, carry_ref):
    i, j = pl.program_id(0), pl.program_id(1)

    @pl.when((i == 0) & (j == 0))
    def _():
        carry_ref[...] = jnp.zeros_like(carry_ref)

    avg = _head_block_diag(W_M, D_HM, 1.0 / D_HM)
    out_m = jax.nn.sigmoid(mo[0]) * (_headnorm_mxu(hf[0] + hb[0], avg, LN_EPS) * hpar_ref[0:1, :])
    y = _headnorm_mxu(yf[0] + yb[0], avg, GN_EPS) * hpar_ref[1:2, :] + hpar_ref[2:3, :]
    out_r = (y + bonus[0]) * gg[0]
    proj = (jnp.dot(out_m.astype(jnp.bfloat16), wout_ref[:W_M, :], preferred_element_type=jnp.float32)
            + jnp.dot(out_r.astype(jnp.bfloat16), wout_ref[W_M:, :], preferred_element_type=jnp.float32))
    g1, sh2, sc2 = mod_ref[0, 2:3, :], mod_ref[0, 3:4, :], mod_ref[0, 4:5, :]
    x1 = _ln_rows(ALPHA * x_ref[0] + g1 * proj, LN_EPS) * ln_ref[0:1, :] + ln_ref[1:2, :]
    x1_ref[0] = x1
    tok = _ln_rows(x1, LN_EPS) * (1.0 + sc2) + sh2
    tok_ref[0] = tok

    logits = _dot3(tok, rtw_ref[...]) + rtb_ref[...]
    lane = lax.broadcasted_iota(jnp.int32, (ROW_TILE, ROUTE_W), 1)
    neg_inf = jnp.full((), -jnp.inf, jnp.float32)
    big = jnp.int32(ROUTE_W)
    is_g = lane < N_GROUPS
    lg = jnp.where(is_g, logits, neg_inf)
    gmax = jnp.max(lg, axis=1, keepdims=True)
    grp = jnp.min(jnp.where(lg == gmax, lane, big), axis=1, keepdims=True)
    p_top = 1.0 / jnp.sum(jnp.where(is_g, jnp.exp(lg - gmax), 0.0), axis=1, keepdims=True)
    in_grp = (lane >= N_GROUPS) & (lane < N_ROUTE) & ((lane - N_GROUPS) // E_PER_GROUP == grp)
    le = jnp.where(in_grp, logits, neg_inf)
    v1 = jnp.max(le, axis=1, keepdims=True)
    i1 = jnp.min(jnp.where(le == v1, lane, big), axis=1, keepdims=True)
    le2 = jnp.where(lane == i1, neg_inf, le)
    v2 = jnp.max(le2, axis=1, keepdims=True)
    i2 = jnp.min(jnp.where(le2 == v2, lane, big), axis=1, keepdims=True)
    e21 = jnp.exp(v2 - v1)
    w1 = (1.0 / (1.0 + e21)) * p_top
    w2 = (e21 / (1.0 + e21)) * p_top

    sel1, sel2 = lane == i1, lane == i2
    onehot = jnp.where(sel1 | sel2, 1.0, 0.0)
    tr = lax.broadcasted_iota(jnp.int32, (ROW_TILE, ROW_TILE), 0)
    tc = lax.broadcasted_iota(jnp.int32, (ROW_TILE, ROW_TILE), 1)
    strict = jnp.where(tc < tr, 1.0, 0.0).astype(jnp.bfloat16)
    before = _dg(strict, onehot.astype(jnp.bfloat16), _NN) + carry_ref[...]
    r1 = jnp.sum(jnp.where(sel1, before, 0.0), axis=1, keepdims=True)
    r2 = jnp.sum(jnp.where(sel2, before, 0.0), axis=1, keepdims=True)
    carry_ref[...] = carry_ref[...] + jnp.sum(onehot, axis=0, keepdims=True)
    e1 = (i1 - N_GROUPS).astype(jnp.float32)
    e2 = (i2 - N_GROUPS).astype(jnp.float32)
    route = jnp.where(lane == 0, e1, jnp.where(lane == 1, e2, jnp.where(lane == 2, r1, jnp.where(
        lane == 3, r2, jnp.where(lane == 4, w1, jnp.where(lane == 5, w2, 0.0))))))
    route_ref[0] = route
    ridx_ref[0] = route.T[0:8, :].astype(jnp.int32)
    cnt_ref[...] = jnp.broadcast_to(carry_ref[...], cnt_ref.shape)


def _outproj(h_f, h_b, pn, y_f, y_b, gg, bonus, x, mod3, wout_bf16, hpar, lnpar, rtw, rtb, ctx_len):
    b, seq, d = x.shape
    nt = seq // ROW_TILE
    off = ctx_len // ROW_TILE
    lat = lambda w: pl.BlockSpec((1, ROW_TILE, w), lambda i, j: (i, j, 0))
    full = lambda shp: pl.BlockSpec(shp, lambda i, j: tuple(0 for _ in shp))
    in_specs = [
        lat(W_M), lat(W_M),
        pl.BlockSpec((1, ROW_TILE, W_M), lambda i, j: (i, j + off, 1)),
        lat(W_R), lat(W_R),
        pl.BlockSpec((1, ROW_TILE, W_R), lambda i, j: (i, j + off, 0)),
        pl.BlockSpec((1, ROW_TILE, W_R), lambda i, j: (i, j + off, 0)),
        lat(d),
        pl.BlockSpec((1, 6, d), lambda i, j: (i, 0, 0)),
        full((MIX_W, d)), full((8, W_M)), full((8, d)), full((d, ROUTE_W)), full((1, ROUTE_W)),
    ]
    out_specs = [
        lat(d), lat(d), lat(ROUTE_W),
        pl.BlockSpec((1, 8, ROW_TILE), lambda i, j: (i * nt + j, 0, 0)),
        full((8, ROUTE_W)),
    ]
    out_shape = [
        jax.ShapeDtypeStruct((b, seq, d), jnp.float32),
        jax.ShapeDtypeStruct((b, seq, d), jnp.float32),
        jax.ShapeDtypeStruct((b, seq, ROUTE_W), jnp.float32),
        jax.ShapeDtypeStruct((b * nt, 8, ROW_TILE), jnp.int32),
        jax.ShapeDtypeStruct((8, ROUTE_W), jnp.float32),
    ]
    return pl.pallas_call(
        _outproj_kernel,
        grid=(b, nt),
        in_specs=in_specs,
        out_specs=out_specs,
        out_shape=out_shape,
        scratch_shapes=[pltpu.VMEM((1, ROUTE_W), jnp.float32)],
        compiler_params=pltpu.CompilerParams(
            dimension_semantics=("arbitrary", "arbitrary"), vmem_limit_bytes=VMEM_LIMIT),
        name="outproj_router",
    )(h_f, h_b, pn, y_f, y_b, gg, bonus, x, mod3, wout_bf16, hpar, lnpar, rtw, rtb)


def _pair_dest(ridx_ref, starts_ref, r, slot):
    return starts_ref[ridx_ref[0, slot, r]] + ridx_ref[0, 2 + slot, r]


def _scatter_kernel(starts_ref, ridx_ref, tok_ref, xs_in_ref, xs_ref, sem):
    del xs_in_ref

    def row_copy(r, slot):
        dst = _pair_dest(ridx_ref, starts_ref, r, slot)
        return pltpu.make_async_copy(tok_ref.at[pl.ds(r, 1), :], xs_ref.at[pl.ds(dst, 1), :], sem)

    def issue(r, c):
        row_copy(r, 0).start()
        row_copy(r, 1).start()
        return c

    lax.fori_loop(0, ROW_TILE, issue, 0, unroll=DMA_UNROLL)
    for half in range(TOP_K_IN_GROUP):
        pltpu.make_async_copy(tok_ref, xs_ref.at[pl.ds(half * ROW_TILE, ROW_TILE), :], sem).wait()


def _moe_scatter(starts, ridx, tok2d, xs_zero):
    n, d = tok2d.shape
    nt = n // ROW_TILE
    return pl.pallas_call(
        _scatter_kernel,
        grid_spec=pltpu.PrefetchScalarGridSpec(
            num_scalar_prefetch=1,
            grid=(nt,),
            in_specs=[
                pl.BlockSpec((1, 8, ROW_TILE), lambda i, s: (i, 0, 0), memory_space=pltpu.SMEM),
                pl.BlockSpec((ROW_TILE, d), lambda i, s: (i, 0)),
                pl.BlockSpec(memory_space=pl.ANY),
            ],
            out_specs=pl.BlockSpec(memory_space=pl.ANY),
            scratch_shapes=[pltpu.SemaphoreType.DMA(())],
        ),
        out_shape=jax.ShapeDtypeStruct(xs_zero.shape, xs_zero.dtype),
        input_output_aliases={3: 0},
        compiler_params=pltpu.CompilerParams(dimension_semantics=("arbitrary",), vmem_limit_bytes=VMEM_LIMIT),
        name="moe_scatter",
    )(starts, ridx, tok2d, xs_zero)


def _expert_kernel(blk_exp_ref, nused_ref, xs_ref, wg_ref, wu_ref, wd_ref, ys_ref):
    i = pl.program_id(0)

    @pl.when(i < nused_ref[0])
    def _():
        xb = xs_ref[...].astype(jnp.bfloat16)
        hg = jnp.dot(xb, wg_ref[0], preferred_element_type=jnp.float32)
        hu = jnp.dot(xb, wu_ref[0], preferred_element_type=jnp.float32)
        hb = (jax.nn.silu(hg) * hu).astype(jnp.bfloat16)
        ys_ref[...] = jnp.dot(hb, wd_ref[0], preferred_element_type=jnp.float32)

    @pl.when(i >= nused_ref[0])
    def _():
        ys_ref[...] = jnp.zeros_like(ys_ref)


def _moe_experts(blk_exp, nused, xs, wg, wu, wd):
    nrow, d = xs.shape
    de = wg.shape[2]
    return pl.pallas_call(
        _expert_kernel,
        grid_spec=pltpu.PrefetchScalarGridSpec(
            num_scalar_prefetch=2,
            grid=(nrow // MOE_BLOCK,),
            in_specs=[
                pl.BlockSpec((MOE_BLOCK, d), lambda i, be, nu: (i, 0)),
                pl.BlockSpec((1, d, de), lambda i, be, nu: (be[i], 0, 0)),
                pl.BlockSpec((1, d, de), lambda i, be, nu: (be[i], 0, 0)),
                pl.BlockSpec((1, de, d), lambda i, be, nu: (be[i], 0, 0)),
            ],
            out_specs=pl.BlockSpec((MOE_BLOCK, d), lambda i, be, nu: (i, 0)),
        ),
        out_shape=jax.ShapeDtypeStruct((nrow, d), jnp.float32),
        compiler_params=pltpu.CompilerParams(dimension_semantics=("arbitrary",), vmem_limit_bytes=VMEM_LIMIT),
        name="moe_experts",
    )(blk_exp, nused, xs, wg, wu, wd)


def _combine_kernel(starts_ref, ridx_ref, ridx_next_ref, route_ref, x1_ref, mod_ref, ln_ref, ys_ref, out_ref,
                    ybuf, sems):
    i = pl.program_id(0)
    n = pl.num_programs(0)
    cur = i % 2

    def gather(idx_ref, buf):
        def issue(r, c):
            for slot in range(TOP_K_IN_GROUP):
                src = _pair_dest(idx_ref, starts_ref, r, slot)
                pltpu.make_async_copy(ys_ref.at[pl.ds(src, 1), :],
                                      ybuf.at[buf, pl.ds(slot * ROW_TILE + r, 1), :], sems.at[buf]).start()
            return c

        lax.fori_loop(0, ROW_TILE, issue, 0, unroll=DMA_UNROLL)

    @pl.when(i == 0)
    def _():
        gather(ridx_ref, 0)

    @pl.when(i + 1 < n)
    def _():
        gather(ridx_next_ref, 1 - cur)

    pltpu.make_async_copy(ys_ref.at[pl.ds(0, TOP_K_IN_GROUP * ROW_TILE), :], ybuf.at[cur], sems.at[cur]).wait()
    route = route_ref[0]
    ffn = ybuf[cur, 0:ROW_TILE, :] * route[:, 4:5] + ybuf[cur, ROW_TILE:2 * ROW_TILE, :] * route[:, 5:6]
    g2 = mod_ref[0, 5:6, :]
    out_ref[0] = _ln_rows(ALPHA * x1_ref[0] + g2 * ffn, LN_EPS) * ln_ref[2:3, :] + ln_ref[3:4, :]


def _moe_combine(starts, ridx, route, x1, mod3, lnpar, ys):
    b, seq, d = x1.shape
    nt = seq // ROW_TILE
    n = b * nt
    tile = lambda w: pl.BlockSpec((1, ROW_TILE, w), lambda i, s: (i // nt, i % nt, 0))
    return pl.pallas_call(
        _combine_kernel,
        grid_spec=pltpu.PrefetchScalarGridSpec(
            num_scalar_prefetch=1,
            grid=(n,),
            in_specs=[
                pl.BlockSpec((1, 8, ROW_TILE), lambda i, s: (i, 0, 0), memory_space=pltpu.SMEM),
                pl.BlockSpec((1, 8, ROW_TILE), lambda i, s: (jnp.minimum(i + 1, n - 1), 0, 0),
                             memory_space=pltpu.SMEM),
                tile(ROUTE_W),
                tile(d),
                pl.BlockSpec((1, 6, d), lambda i, s: (i // nt, 0, 0)),
                pl.BlockSpec((8, d), lambda i, s: (0, 0)),
                pl.BlockSpec(memory_space=pl.ANY),
            ],
            out_specs=tile(d),
            scratch_shapes=[pltpu.VMEM((2, TOP_K_IN_GROUP * ROW_TILE, d), jnp.float32),
                            pltpu.SemaphoreType.DMA((2,))],
        ),
        out_shape=jax.ShapeDtypeStruct((b, seq, d), jnp.float32),
        compiler_params=pltpu.CompilerParams(dimension_semantics=("arbitrary",), vmem_limit_bytes=VMEM_LIMIT),
        name="moe_combine",
    )(starts, ridx, ridx, route, x1, mod3, lnpar, ys)


def _moe_plan(cnt, n_pairs):
    counts = cnt[0, N_GROUPS:N_ROUTE].astype(jnp.int32)
    padded = (counts + MOE_BLOCK - 1) // MOE_BLOCK * MOE_BLOCK
    pad_ends = jnp.cumsum(padded)
    pad_starts = pad_ends - padded
    n_blk = -(-n_pairs // MOE_BLOCK) + N_EXPERTS
    blk_row0 = jnp.arange(n_blk, dtype=jnp.int32) * MOE_BLOCK
    blk_exp = jnp.minimum(jnp.sum((pad_ends[None, :] <= blk_row0[:, None]).astype(jnp.int32), axis=1), N_EXPERTS - 1)
    nused = (pad_ends[-1:] // MOE_BLOCK).astype(jnp.int32)
    return pad_starts.astype(jnp.int32), blk_exp, nused, n_blk


def _layernorm(z, gain=None, bias=None):
    zf = z.astype(jnp.float32)
    mu = jnp.mean(zf, -1, keepdims=True)
    var = jnp.mean(jnp.square(zf - mu), -1, keepdims=True)
    y = (zf - mu) * lax.rsqrt(var + LN_EPS)
    if gain is not None:
        y = y * gain + bias
    return y.astype(z.dtype)


def _headnorm(z, eps):
    zf = z.astype(jnp.float32)
    mu = jnp.mean(zf, -1, keepdims=True)
    var = jnp.mean(jnp.square(zf - mu), -1, keepdims=True)
    return (zf - mu) * lax.rsqrt(var + eps)


def _modulate(z, shift, scale):
    return _layernorm(z) * (1 + scale) + shift


def _rev_segments(u, ctx_len):
    return jnp.concatenate([jnp.flip(u[:, :ctx_len], 1), jnp.flip(u[:, ctx_len:], 1)], axis=1)


def _dir_stack(u_fwd, u_bwd, ctx_len):
    return jnp.concatenate([u_fwd, _rev_segments(u_bwd, ctx_len)], axis=0)


def _dir_merge(y, ctx_len):
    b = y.shape[0] // 2
    return y[:b] + _rev_segments(y[b:], ctx_len)


def _conv_grid(u, w, rows):
    b, l, ch = u.shape
    img = u.reshape(b, rows, l // rows, ch)
    out = lax.conv_general_dilated(img, w[:, :, None, :].astype(u.dtype), (1, 1), 'SAME',
                                   dimension_numbers=('NHWC', 'HWIO', 'NHWC'), feature_group_count=ch)
    return out.reshape(b, l, ch)


def _conv_seq(u, w_row):
    ch = u.shape[-1]
    return lax.conv_general_dilated(u, w_row[:, None, :].astype(u.dtype), (1,), 'SAME',
                                    dimension_numbers=('NWC', 'WIO', 'NWC'), feature_group_count=ch)


def _mlstm_chunkwise(q, k, v, log_i, log_f):
    z, t, h, dh = q.shape
    nc = t // CHUNK

    def chunks(a):
        a = a.astype(jnp.float32).reshape((z, nc, CHUNK, h) + a.shape[3:])
        return jnp.swapaxes(jnp.swapaxes(a, 0, 1), 2, 3)
    qc, kc, vc, lic, lfc = chunks(q), chunks(k), chunks(v), chunks(log_i), chunks(log_f)
    tri = jnp.tril(jnp.ones((CHUNK, CHUNK), bool))

    def step(carry, inp):
        c_mat, n_vec, m = carry
        qb, kb, vb, li, lf = inp
        b = jnp.cumsum(lf, -1)
        dmat = jnp.where(tri, b[..., :, None] - b[..., None, :] + li[..., None, :], -jnp.inf)
        m_inter = b + m[..., None]
        m_t = jnp.maximum(m_inter, jnp.max(dmat, -1))
        s = jnp.einsum('zhtd,zhsd->zhts', qb, kb) * jnp.exp(dmat - m_t[..., None])
        inter = jnp.exp(m_inter - m_t)
        num = inter[..., None] * jnp.einsum('zhtd,zhde->zhte', qb, c_mat) + jnp.einsum('zhts,zhse->zhte', s, vb)
        den = inter * jnp.einsum('zhtd,zhd->zht', qb, n_vec) + jnp.sum(s, -1)
        h_out = num / jnp.maximum(jnp.abs(den), jnp.exp(-m_t))[..., None]
        b_last = b[..., -1]
        g = b_last[..., None] - b + li
        m_new = jnp.maximum(b_last + m, jnp.max(g, -1))
        wts = jnp.exp(g - m_new[..., None])
        decay = jnp.exp(b_last + m - m_new)
        c_mat = decay[..., None, None] * c_mat + jnp.einsum('zhs,zhsd,zhse->zhde', wts, kb, vb)
        n_vec = decay[..., None] * n_vec + jnp.einsum('zhs,zhsd->zhd', wts, kb)
        return (c_mat, n_vec, m_new), h_out

    init = (jnp.zeros((z, h, dh, dh), jnp.float32), jnp.zeros((z, h, dh), jnp.float32), jnp.zeros((z, h), jnp.float32))
    _, hs = lax.scan(step, init, (qc, kc, vc, lic, lfc))
    return jnp.swapaxes(jnp.swapaxes(hs, 2, 3), 0, 1).reshape(z, t, h, dh)


def _mlstm_group(mq, mk, mv, mo, gates, b_i, b_f, norm_w, ctx_len):
    b, t, _ = mq.shape
    q = jax.nn.silu(mq)
    k = jax.nn.silu(mk) * (D_HM ** -0.5)
    bias_row = jnp.concatenate([b_i[0], b_f[0], b_i[1], b_f[1], jnp.zeros((128 - 4 * H_M,), jnp.float32)])[None, :]
    h_f, h_b = _mlstm_scan(q, k, mv, gates, bias_row, ctx_len)
    h = (h_f + h_b).reshape(b, t - ctx_len, H_M, D_HM)
    h = _headnorm(h, LN_EPS) * norm_w.reshape(H_M, D_HM)
    return jax.nn.sigmoid(mo[:, ctx_len:]) * h.reshape(b, t - ctx_len, W_M)


def _rwkv7_scan(r, w, k, v, kh, a):
    z, t, h, n = r.shape
    seq = tuple(jnp.moveaxis(s.astype(jnp.float32), 1, 0) for s in (r, w, k, v, kh, a))

    def step(state, inp):
        r_t, w_t, k_t, v_t, kh_t, a_t = inp
        removed = jnp.einsum('zhvk,zhk->zhv', state, kh_t)
        state = (state * w_t[:, :, None, :] - removed[..., None] * (kh_t * a_t)[:, :, None, :]
                 + v_t[..., None] * k_t[:, :, None, :])
        return state, jnp.einsum('zhvk,zhk->zhv', state, r_t)

    _, y = lax.scan(step, jnp.zeros((z, h, n, n), jnp.float32), seq)
    return jnp.moveaxis(y, 0, 1)


def _rwkv7_group(rr, rk, rv, lw_f, lw_b, la, lg, w0, w_b_mat, a0, a_b_mat, g_b_mat, kk, ka, bonus_w, gn_w, gn_b,
                 ctx_len):
    b, t, _ = rr.shape
    shp = (b, t, H_R, D_HR)
    logw_f = -DECAY_SCALE * jax.nn.sigmoid((w0[0] + jnp.tanh(lw_f) @ w_b_mat[0]).astype(jnp.float32))
    logw_b = -DECAY_SCALE * jax.nn.sigmoid((w0[1] + jnp.tanh(lw_b) @ w_b_mat[1]).astype(jnp.float32))
    a = jax.nn.sigmoid(a0 + la @ a_b_mat)
    g = jax.nn.sigmoid(lg) @ g_b_mat
    kap = (rk * kk).reshape(shp).astype(jnp.float32)
    kh = kap / jnp.maximum(jnp.sqrt(jnp.sum(jnp.square(kap), -1, keepdims=True)), 1e-12)
    kmod = (rk * (1 + (a - 1) * ka)).reshape(shp)
    r = rr.reshape(shp)
    v = rv.reshape(shp)
    a_h = a.reshape(shp)
    flat = lambda u: u.reshape(b, t, W_R)
    y_f, y_b = _rwkv_scan(rr, flat(kmod), rv, flat(kh), flat(kh * a_h), logw_f, logw_b, ctx_len)
    y = (y_f + y_b).reshape(b, t - ctx_len, H_R, D_HR)
    y = _headnorm(y, GN_EPS) * gn_w.reshape(H_R, D_HR) + gn_b.reshape(H_R, D_HR)
    bonus = (jnp.sum(r * kmod * bonus_w, -1, keepdims=True) * v)[:, ctx_len:]
    return (y + bonus).reshape(b, t - ctx_len, W_R) * g[:, ctx_len:]


def _hier_moe(h, rt_g, rt_g_b, rt_e, rt_e_b, ex_gate, ex_up, ex_down):
    b, t, d = h.shape
    tok = h.reshape(b * t, d)
    n = tok.shape[0]
    lg = (tok @ rt_g + rt_g_b).astype(jnp.float32)
    p_grp = jax.nn.softmax(lg, axis=-1)
    grp = jnp.argmax(lg, axis=-1).astype(jnp.int32)
    p_top = jnp.take_along_axis(p_grp, grp[:, None], axis=-1)
    le = (tok @ rt_e + rt_e_b).astype(jnp.float32).reshape(n, N_GROUPS, E_PER_GROUP)
    le_grp = jnp.take_along_axis(le, jnp.broadcast_to(grp[:, None, None], (n, 1, E_PER_GROUP)), axis=1)[:, 0]
    top_val, top_idx = lax.top_k(le_grp, TOP_K_IN_GROUP)
    wts = jax.nn.softmax(top_val, axis=-1) * p_top
    e_flat = (grp[:, None] * E_PER_GROUP + top_idx.astype(jnp.int32)).reshape(-1)
    w_flat = wts.reshape(-1)
    t_flat = jnp.repeat(jnp.arange(n, dtype=jnp.int32), TOP_K_IN_GROUP)
    order = jnp.argsort(e_flat)
    e_s, t_s, w_s = e_flat[order], t_flat[order], w_flat[order]
    counts = jnp.zeros((N_EXPERTS,), jnp.int32).at[e_flat].add(1)
    starts = jnp.cumsum(counts) - counts
    padded = (counts + MOE_BLOCK - 1) // MOE_BLOCK * MOE_BLOCK
    pad_ends = jnp.cumsum(padded)
    pad_starts = pad_ends - padded
    dest = pad_starts[e_s] + jnp.arange(e_s.shape[0], dtype=jnp.int32) - starts[e_s]
    n_pairs = n * TOP_K_IN_GROUP
    buf = (-(-n_pairs // MOE_BLOCK) + N_EXPERTS) * MOE_BLOCK
    tok_buf = jnp.full((buf,), n, jnp.int32).at[dest].set(t_s)
    w_buf = jnp.zeros((buf,), w_s.dtype).at[dest].set(w_s)
    n_blk = buf // MOE_BLOCK
    blk_exp = jnp.minimum(jnp.searchsorted(pad_ends, jnp.arange(n_blk, dtype=jnp.int32) * MOE_BLOCK, side='right'),
                          N_EXPERTS - 1)
    tok_pad = jnp.concatenate([tok, jnp.zeros((1, d), tok.dtype)], 0)

    def expert_block(args):
        idx, w, e = args
        xb = tok_pad[idx]
        hb = jax.nn.silu(xb @ ex_gate[e]) * (xb @ ex_up[e])
        return (hb @ ex_down[e]) * w[:, None]

    y = lax.map(expert_block, (tok_buf.reshape(n_blk, MOE_BLOCK), w_buf.reshape(n_blk, MOE_BLOCK), blk_exp))
    out = jnp.zeros((n + 1, d), y.dtype).at[tok_buf].add(y.reshape(buf, d))
    return out[:n].reshape(b, t, d).astype(h.dtype)


def kernel(x, c, ctx, c_ctx, w_ada, b_ada, w_in, conv_w, m_bias_i, m_bias_f, m_norm_w, r_w0, r_wB, r_a0, r_aB,
           r_gB, r_kk, r_ka, r_bonus, r_norm_w, r_norm_b, w_out, ln1_g, ln1_b, ln2_g, ln2_b, rt_g, rt_g_b, rt_e,
           rt_e_b, ex_gate, ex_up, ex_down):
    assert w_ada.shape[0] == DEPTH
    bsz, seq, d = x.shape
    ctx_len = ctx.shape[1]
    assert ctx_len == ROW_TILE and seq % ROW_TILE == 0
    assert seq % GRID_W == 0 and ROW_TILE % GRID_W == 0
    l = 0
    zrow = lambda n, w: jnp.zeros((n, w), jnp.float32)
    mrows = -(-(bsz + 1) // 8) * 8
    c_pad = jnp.concatenate([c, c_ctx[None, :], zrow(mrows - bsz - 1, d)], 0)
    mod_all = _ada(c_pad, w_ada[l], b_ada[l][None, :])[:bsz + 1]
    mod3 = mod_all.reshape(bsz + 1, 6, d)
    pc, pn = _inproj(ctx, x, mod3, _pad_w_in(w_in[l]))
    cw = jnp.concatenate([conv_w[l].reshape(CONV_K * CONV_K, CONV_CH), zrow(16 - CONV_K * CONV_K, CONV_CH)], 0)
    rpar = jnp.concatenate([r_w0[l], r_a0[l][None, :], r_kk[l][None, :], r_ka[l][None, :],
                            r_bonus[l].reshape(1, W_R), zrow(2, W_R)], 0)
    wbf = jnp.concatenate([r_wB[l][0], zrow(W_LORA, W_R)], 0)
    wbb = jnp.concatenate([zrow(W_LORA, W_R), r_wB[l][1]], 0)
    ab_mat = jnp.concatenate([r_aB[l], zrow(128 - A_LORA, W_R)], 0)
    q, k, rr, kmod, rv, kh, kha, logw_f, logw_b, gg, bonus = _prep(pc, pn, cw, rpar, wbf, wbb, ab_mat, r_gB[l])
    bias_row = jnp.concatenate([m_bias_i[l][0], m_bias_f[l][0], m_bias_i[l][1], m_bias_f[l][1],
                                jnp.zeros((128 - 4 * H_M,), jnp.float32)])[None, :]
    h_f, h_b = _mlstm_scan(q, k, pn, bias_row, ctx_len)
    y_f, y_b = _rwkv_scan(rr, kmod, rv, kh, kha, logw_f, logw_b, ctx_len)
    hpar = jnp.zeros((8, W_M), jnp.float32).at[0].set(m_norm_w[l]).at[1].set(r_norm_w[l]).at[2].set(r_norm_b[l])
    lnpar = jnp.zeros((8, d), jnp.float32).at[0].set(ln1_g[l]).at[1].set(ln1_b[l]).at[2].set(ln2_g[l]).at[3].set(
        ln2_b[l])
    rtw = jnp.concatenate([rt_g[l], rt_e[l], jnp.zeros((d, ROUTE_W - N_ROUTE), jnp.float32)], axis=1)
    rtb = jnp.concatenate([rt_g_b[l], rt_e_b[l], jnp.zeros((ROUTE_W - N_ROUTE,), jnp.float32)])[None, :]
    x1, tok, route, ridx, cnt = _outproj(h_f, h_b, pn, y_f, y_b, gg, bonus, x, mod3, w_out[l].astype(jnp.bfloat16),
                                         hpar, lnpar, rtw, rtb, ctx_len)
    n_tok = bsz * seq
    starts, blk_exp, nused, n_blk = _moe_plan(cnt, n_tok * TOP_K_IN_GROUP)
    xs = _moe_scatter(starts, ridx, tok.reshape(n_tok, d), jnp.zeros((n_blk * MOE_BLOCK, d), jnp.float32))
    ys = _moe_experts(blk_exp, nused, xs, ex_gate[l].astype(jnp.bfloat16), ex_up[l].astype(jnp.bfloat16),
                      ex_down[l].astype(jnp.bfloat16))
    return _moe_combine(starts, ridx, route, x1, mod3, lnpar, ys)
```

```python
import functools
import math

import jax
import jax.numpy as jnp
from jax import lax
import numpy as np
from jax.experimental import pallas as pl
from jax.experimental.pallas import tpu as pltpu

H_M = 8
D_HM = 64
W_M = H_M * D_HM
H_R = 8
D_HR = 64
W_R = H_R * D_HR
MIX_W = W_M + W_R
W_LORA = 64
A_LORA = 64
G_LORA = 128
CONV_K = 3
CONV_CH = 2 * W_M + 3 * W_R
SECTION_WIDTHS = (W_M, W_M, W_R, W_R, W_R, W_M, W_M, 4 * H_M, W_LORA, W_LORA, A_LORA, G_LORA)
IN_COLS = sum(SECTION_WIDTHS)
SPLIT_POINTS = tuple(int(v) for v in np.cumsum(SECTION_WIDTHS)[:-1])
CHUNK = 64
N_GROUPS = 4
E_PER_GROUP = 8
N_EXPERTS = N_GROUPS * E_PER_GROUP
TOP_K_IN_GROUP = 2
D_EXPERT = 512
MOE_BLOCK = 256
DEPTH = 1
ALPHA = (2.0 * DEPTH) ** 0.25
DECAY_SCALE = math.exp(-0.5)
LN_EPS = 1e-6
GN_EPS = 64e-5

ROW_TILE = 256
NONCONV_W = 1536
VMEM_LIMIT = 56 * 1024 * 1024
DMA_UNROLL = 8


def _inproj_kernel(ctx_ref, x_ref, mod_ref, w_ref, oc_ref, on_ref):
    j = pl.program_id(1)

    def body(src_ref):
        z = src_ref[0]
        mu = jnp.mean(z, -1, keepdims=True)
        zc = z - mu
        var = jnp.mean(zc * zc, -1, keepdims=True)
        y = zc * lax.rsqrt(var + LN_EPS)
        h = (y * (1.0 + mod_ref[0, 1:2, :]) + mod_ref[0, 0:1, :]).astype(jnp.bfloat16)
        oc_ref[0] = jnp.dot(h, w_ref[:, :CONV_CH], preferred_element_type=jnp.float32)
        on_ref[0] = jnp.dot(h, w_ref[:, CONV_CH:], preferred_element_type=jnp.float32)

    @pl.when(j == 0)
    def _():
        body(ctx_ref)

    @pl.when(j > 0)
    def _():
        body(x_ref)


def _inproj(ctx, x, mod_all, w_pad):
    b, seq, d = x.shape
    nt = (ctx.shape[1] + seq) // ROW_TILE
    t_all = ctx.shape[1] + seq
    return pl.pallas_call(
        _inproj_kernel,
        grid=(b, nt),
        in_specs=[
            pl.BlockSpec((1, ROW_TILE, d), lambda i, j: (i, 0, 0)),
            pl.BlockSpec((1, ROW_TILE, d), lambda i, j: (i, jnp.maximum(j - 1, 0), 0)),
            pl.BlockSpec((1, 6, d), lambda i, j: (jnp.where(j == 0, b, i), 0, 0)),
            pl.BlockSpec((d, CONV_CH + NONCONV_W), lambda i, j: (0, 0)),
        ],
        out_specs=[
            pl.BlockSpec((1, ROW_TILE, CONV_CH), lambda i, j: (i, j, 0)),
            pl.BlockSpec((1, ROW_TILE, NONCONV_W), lambda i, j: (i, j, 0)),
        ],
        out_shape=[
            jax.ShapeDtypeStruct((b, t_all, CONV_CH), jnp.float32),
            jax.ShapeDtypeStruct((b, t_all, NONCONV_W), jnp.float32),
        ],
        compiler_params=pltpu.CompilerParams(
            dimension_semantics=("arbitrary", "arbitrary"), vmem_limit_bytes=VMEM_LIMIT),
        name="inproj",
    )(ctx, x, mod_all, w_pad)


def _pad_w_in(w_in):
    d = w_in.shape[0]
    sp = SPLIT_POINTS
    z = lambda n: jnp.zeros((d, n), w_in.dtype)
    gates = w_in[:, sp[6]:sp[7]]
    lw = w_in[:, sp[7]:sp[9]]
    la = w_in[:, sp[9]:sp[10]]
    lg = w_in[:, sp[10]:]
    return jnp.concatenate([w_in[:, :sp[6]], gates, z(96), lw, la, z(64), lg], axis=1).astype(jnp.bfloat16)


PAIR = 2 * D_HR
NPAIR = H_R // 2
_NN = (((1,), (0,)), ((), ()))
_NT = (((1,), (1,)), ((), ()))
_TN = (((0,), (0,)), ((), ()))


def _split2(a):
    hi = a.astype(jnp.bfloat16)
    lo = (a - hi.astype(jnp.float32)).astype(jnp.bfloat16)
    return hi, lo


def _split3(a):
    a1 = a.astype(jnp.bfloat16)
    r1 = a - a1.astype(jnp.float32)
    a2 = r1.astype(jnp.bfloat16)
    a3 = (r1 - a2.astype(jnp.float32)).astype(jnp.bfloat16)
    return a1, a2, a3


def _dg(a, b, dims):
    return lax.dot_general(a, b, dims, preferred_element_type=jnp.float32)


def _dot3(a, b, dims=_NN):
    ah, al = _split2(a)
    bh, bl = _split2(b)
    return _dg(ah, bh, dims) + _dg(ah, bl, dims) + _dg(al, bh, dims)


def _dot1(a, b, dims=_NN):
    return _dg(a.astype(jnp.bfloat16), b.astype(jnp.bfloat16), dims)


def _stack3(w):
    hi = w.astype(jnp.bfloat16)
    lo = (w - hi.astype(jnp.float32)).astype(jnp.bfloat16)
    return jnp.concatenate([hi, lo, hi], axis=0)


def _dot3_pre(a, w3):
    ah, al = _split2(a)
    return _dg(jnp.concatenate([ah, ah, al], axis=1), w3, _NN)


HEAD_BLK = 256


def _head_sum_operator(head):
    r = np.arange(HEAD_BLK) // head
    bd = (r[:, None] == r[None, :]).astype(np.float32)
    return jnp.asarray(np.concatenate([bd, bd], axis=0), jnp.bfloat16)


def _head_sums(z, bd2):
    hi, lo = _split2(z)
    out = []
    for c in range(0, z.shape[1], HEAD_BLK):
        lhs = jnp.concatenate([hi[:, c:c + HEAD_BLK], lo[:, c:c + HEAD_BLK]], axis=1)
        out.append(_dg(lhs, bd2, _NN))
    return jnp.concatenate(out, axis=1)


def _dot_exact_lhs(a_bf16, b, dims=_NN):
    b1, b2, b3 = _split3(b)
    return _dg(a_bf16, b1, dims) + _dg(a_bf16, b2, dims) + _dg(a_bf16, b3, dims)


def _dot_exact_rhs(a, b_bf16):
    a1, a2, a3 = _split3(a)
    return _dg(a1, b_bf16, _NN) + _dg(a2, b_bf16, _NN) + _dg(a3, b_bf16, _NN)


def _head_block_diag(width, head, value):
    hrow = lax.broadcasted_iota(jnp.int32, (width, width), 0) // head
    hcol = lax.broadcasted_iota(jnp.int32, (width, width), 1) // head
    return jnp.where(hrow == hcol, value, 0.0).astype(jnp.bfloat16)


def _ada_kernel(c_ref, w_ref, b_ref, o_ref):
    o_ref[...] = _dot3(jax.nn.silu(c_ref[...]), w_ref[...]) + b_ref[...]


def _ada(c_pad, w, bias):
    m, d = c_pad.shape
    n = w.shape[1]
    return pl.pallas_call(
        _ada_kernel,
        grid=(n // d,),
        in_specs=[pl.BlockSpec((m, d), lambda j: (0, 0)),
                  pl.BlockSpec((d, d), lambda j: (0, j)),
                  pl.BlockSpec((1, d), lambda j: (0, j))],
        out_specs=pl.BlockSpec((m, d), lambda j: (0, j)),
        out_shape=jax.ShapeDtypeStruct((m, n), jnp.float32),
        compiler_params=pltpu.CompilerParams(dimension_semantics=("arbitrary",), vmem_limit_bytes=VMEM_LIMIT),
        name="adaln",
    )(c_pad, w, bias)


GRID_W = 64
HALO = GRID_W


def _prep_kernel(top_ref, main_ref, bot_ref, pn_ref, cw_ref, rpar_ref, wbf_ref, wbb_ref, ab_ref, gb_ref, bd_ref,
                 q_ref, k_ref, r_ref, km_ref, v_ref, kh_ref, abo_ref, lwf_ref, lwb_ref, gg_ref, bon_ref, xbuf):
    j = pl.program_id(1)
    nt = pl.num_programs(1)
    is_ctx = j == 0
    top_ok = j >= 2
    bot_ok = (j >= 1) & (j < nt - 1)
    l_idx = lax.broadcasted_iota(jnp.int32, (ROW_TILE, 1), 0)
    col = jnp.where(is_ctx, l_idx, l_idx % GRID_W)
    left_ok = col != 0
    right_ok = col != jnp.where(is_ctx, ROW_TILE - 1, GRID_W - 1)
    vert = jnp.where(is_ctx, 0.0, 1.0)
    sec = {}
    for s in range(5):
        sl = slice(s * W_M, (s + 1) * W_M)
        xbuf[0:HALO, :] = jnp.where(top_ok, top_ref[0, :, sl], 0.0)
        xbuf[HALO:HALO + ROW_TILE, :] = main_ref[0, :, sl]
        xbuf[HALO + ROW_TILE:2 * HALO + ROW_TILE, :] = jnp.where(bot_ok, bot_ref[0, :, sl], 0.0)
        cols = [None, None, None]
        for dr in range(3):
            base = HALO + GRID_W * (dr - 1)
            w3 = cw_ref[3 * dr:3 * dr + 3, sl] if dr == 1 else cw_ref[3 * dr:3 * dr + 3, sl] * vert
            xrow = xbuf[base:base + ROW_TILE, :]
            for dc in range(3):
                term = xrow * w3[dc:dc + 1]
                cols[dc] = term if cols[dc] is None else cols[dc] + term
        left = jnp.where(left_ok, pltpu.roll(cols[0], 1, 0), 0.0)
        right = jnp.where(right_ok, pltpu.roll(cols[2], ROW_TILE - 1, 0), 0.0)
        sec[s] = left + cols[1] + right
    q_ref[0] = jax.nn.silu(sec[0])
    k_ref[0] = jax.nn.silu(sec[1]) * (D_HM ** -0.5)
    rr, rk, rv = sec[2], sec[3], sec[4]
    r_ref[0] = rr
    v_ref[0] = rv
    lw = jnp.tanh(pn_ref[0, :, 1152:1280])
    lwf_ref[0] = -DECAY_SCALE * jax.nn.sigmoid(rpar_ref[0:1, :] + _dot3_pre(lw, wbf_ref[...]))
    lwb_ref[0] = -DECAY_SCALE * jax.nn.sigmoid(rpar_ref[1:2, :] + _dot3_pre(lw, wbb_ref[...]))
    a = jax.nn.sigmoid(rpar_ref[2:3, :] + _dot3_pre(pn_ref[0, :, 1280:1408], ab_ref[...]))
    gg_ref[0] = _dot3_pre(jax.nn.sigmoid(pn_ref[0, :, 1408:1536]), gb_ref[...])
    kap = rk * rpar_ref[3:4, :]
    norm = jnp.sqrt(_head_sums(kap * kap, bd_ref[...]))
    kh = kap / jnp.maximum(norm, 1e-12)
    kmod = rk * (1.0 + (a - 1.0) * rpar_ref[4:5, :])
    kh_ref[0] = kh
    abo_ref[0] = kh * a
    km_ref[0] = kmod
    bon_ref[0] = _head_sums(rr * kmod * rpar_ref[5:6, :], bd_ref[...]) * rv


def _prep(pc, pn, cw, rpar, wbf, wbb, ab, gb, bd):
    b, t, _ = pc.shape
    nt = t // ROW_TILE
    per = ROW_TILE // HALO
    nh = t // HALO
    w = W_M
    full = lambda shp: pl.BlockSpec(shp, lambda i, j: tuple(0 for _ in shp))
    in_specs = [
        pl.BlockSpec((1, HALO, CONV_CH), lambda i, j: (i, jnp.maximum(j * per - 1, 0), 0)),
        pl.BlockSpec((1, ROW_TILE, CONV_CH), lambda i, j: (i, j, 0)),
        pl.BlockSpec((1, HALO, CONV_CH), lambda i, j: (i, jnp.minimum(j * per + per, nh - 1), 0)),
        pl.BlockSpec((1, ROW_TILE, NONCONV_W), lambda i, j: (i, j, 0)),
        full((16, CONV_CH)), full((8, w)), full((384, w)), full((384, w)), full((384, w)), full((384, w)),
        full((2 * HEAD_BLK, HEAD_BLK)),
    ]
    out = pl.BlockSpec((1, ROW_TILE, w), lambda i, j: (i, j, 0))
    return pl.pallas_call(
        _prep_kernel,
        grid=(b, nt),
        in_specs=in_specs,
        out_specs=[out] * 11,
        out_shape=[jax.ShapeDtypeStruct((b, t, w), jnp.float32)] * 11,
        scratch_shapes=[pltpu.VMEM((2 * HALO + ROW_TILE, w), jnp.float32)],
        compiler_params=pltpu.CompilerParams(
            dimension_semantics=("arbitrary", "arbitrary"), vmem_limit_bytes=VMEM_LIMIT),
        name="conv_prep",
    )(pc, pc, pc, pn, cw, rpar, wbf, wbb, ab, gb, bd)


def _round_robin(gens):
    out = [None] * len(gens)
    live = list(range(len(gens)))
    while live:
        nxt = []
        for i in live:
            try:
                next(gens[i])
                nxt.append(i)
            except StopIteration as stop:
                out[i] = stop.value
        live = nxt
    return out


def _rwkv_pair_chunk(r, k, v, kh, ab, lw, s_mat, tri_incl, strict, incl, last_row, m0, m1):
    cw = _dot_exact_lhs(tri_incl, lw)
    yield
    e_pos = jnp.exp(cw)
    e_neg = jnp.exp(-cw)
    e_prev = jnp.exp(cw - lw)
    stack = lambda a: jnp.concatenate([a * m0, a * m1], axis=0)
    kt = stack(kh * e_prev)
    bt = stack(ab * e_neg)
    kk = stack(k * e_neg)
    rt = stack(r * e_pos)
    vs = stack(v)
    zero = jnp.zeros((), jnp.float32)
    kr = jnp.concatenate([kt, rt], axis=0)
    bk = jnp.concatenate([bt, kk], axis=0)
    amat = _dot1(kr, bk, _NT)
    yield
    a_bk = jnp.where(strict, amat[:PAIR, :PAIR], zero)
    a_kk = jnp.where(strict, amat[:PAIR, PAIR:], zero)
    a_rb = jnp.where(incl, amat[PAIR:, :PAIR], zero)
    a_rk = jnp.where(incl, amat[PAIR:, PAIR:], zero)
    n_mat = -a_bk
    eye = (lax.broadcasted_iota(jnp.int32, (PAIR, PAIR), 0)
           == lax.broadcasted_iota(jnp.int32, (PAIR, PAIR), 1)).astype(jnp.float32)
    q = eye + n_mat
    p = _dot1(n_mat, n_mat)
    ks = _dot1(kr, s_mat, _NT)
    av = _dot1(jnp.concatenate([a_kk, a_rk], axis=0), vs)
    yield
    for _ in range(4):
        qp = _dot1(jnp.concatenate([q, p], axis=0), p)
        yield
        q = q + qp[:PAIR]
        p = qp[PAIR:]
    t_inv = q + _dot1(q, p)
    yield
    u = -_dot1(t_inv, ks[:PAIR] + av[:PAIR])
    yield
    y = ks[PAIR:] + av[PAIR:] + _dot1(a_rb, u)
    yield
    w_last = jnp.sum(jnp.where(last_row, e_pos, zero), axis=0, keepdims=True)
    s_new = (s_mat + _dot1(jnp.concatenate([u, vs], axis=0), bk, _TN)) * w_last
    return y[:CHUNK] + y[CHUNK:], s_new


def _rwkv_kernel(rf, kf, vf, khf, abf, lwf, rb, kb, vb, khb, abb, lwb, yf_ref, yb_ref, s_ref, *, nctx):
    j = pl.program_id(1)

    @pl.when(j == 0)
    def _():
        s_ref[...] = jnp.zeros_like(s_ref)

    row = lax.broadcasted_iota(jnp.int32, (PAIR, PAIR), 0)
    col = lax.broadcasted_iota(jnp.int32, (PAIR, PAIR), 1)
    same = (row // CHUNK) == (col // CHUNK)
    r64 = lax.broadcasted_iota(jnp.int32, (CHUNK, CHUNK), 0)
    c64 = lax.broadcasted_iota(jnp.int32, (CHUNK, CHUNK), 1)
    rowl = lax.broadcasted_iota(jnp.int32, (CHUNK, PAIR), 0)
    lane = lax.broadcasted_iota(jnp.int32, (1, PAIR), 1)
    m0 = (lane < D_HR).astype(jnp.float32)
    m1 = 1.0 - m0
    dirs = (
        (rf, kf, vf, khf, abf, lwf, yf_ref, (c64 <= r64), same & (col < row), same & (col <= row), rowl == CHUNK - 1),
        (rb, kb, vb, khb, abb, lwb, yb_ref, (c64 >= r64), same & (col > row), same & (col >= row), rowl == 0),
    )
    states = [[s_ref[d, p] for p in range(NPAIR)] for d in range(2)]
    gens, meta = [], []
    for d, (r_ref, k_ref, v_ref, kh_ref, ab_ref, lw_ref, y_ref, tri, strict, incl, last_row) in enumerate(dirs):
        tri = tri.astype(jnp.float32).astype(jnp.bfloat16)
        for p in range(NPAIR):
            sl = slice(p * PAIR, (p + 1) * PAIR)
            gens.append(_rwkv_pair_chunk(r_ref[0, :, sl], k_ref[0, :, sl], v_ref[0, :, sl], kh_ref[0, :, sl],
                                         ab_ref[0, :, sl], lw_ref[0, :, sl], states[d][p], tri, strict, incl,
                                         last_row, m0, m1))
            meta.append((d, p, sl, y_ref))
    results = [m + r for m, r in zip(meta, _round_robin(gens))]
    for d, p, sl, y_ref, y, s_new in results:
        s_ref[d, p] = s_new
        y_ref[0, :, sl] = y


def _rwkv_scan(r, k, v, kh, ab, lw_f, lw_b, ctx_len):
    b, t, w = r.shape
    nc = t // CHUNK
    nctx = ctx_len // CHUNK
    nlat = nc - nctx

    def fwd_map(i, j):
        return (i, j, 0)

    def bwd_map(i, j):
        return (i, jnp.where(j < nctx, nctx - 1 - j, nc - 1 - (j - nctx)), 0)

    blk = (1, CHUNK, w)
    in_specs = [pl.BlockSpec(blk, fwd_map)] * 6 + [pl.BlockSpec(blk, bwd_map)] * 6
    out_specs = [
        pl.BlockSpec(blk, lambda i, j: (i, jnp.maximum(j - nctx, 0), 0)),
        pl.BlockSpec(blk, lambda i, j: (i, nlat - 1 - jnp.maximum(j - nctx, 0), 0)),
    ]
    return pl.pallas_call(
        functools.partial(_rwkv_kernel, nctx=nctx),
        grid=(b, nc),
        in_specs=in_specs,
        out_specs=out_specs,
        out_shape=[jax.ShapeDtypeStruct((b, t - ctx_len, w), jnp.float32)] * 2,
        scratch_shapes=[pltpu.VMEM((2, NPAIR, PAIR, PAIR), jnp.float32)],
        compiler_params=pltpu.CompilerParams(
            dimension_semantics=("arbitrary", "arbitrary"), vmem_limit_bytes=VMEM_LIMIT),
        name="rwkv_scan",
    )(r, k, v, kh, ab, lw_f, r, k, v, kh, ab, lw_b)


def _col_stack(a, c0, c1):
    return jnp.concatenate([a[:, c0:c0 + 1], a[:, c1:c1 + 1]], axis=0)


def _half_bcast(v0, v1):
    return jnp.concatenate([jnp.broadcast_to(v0, (CHUNK, 1)), jnp.broadcast_to(v1, (CHUNK, 1))], axis=0)


def _mlstm_pair_chunk(q, k, v, b_col, li_col, c_row, c_mat, n_row, m_row, incl, last_sel, m0, m1):
    stack = lambda a: jnp.concatenate([a * m0, a * m1], axis=0)
    q_st, k_st, v_st = stack(q), stack(k), stack(v)
    lane = lax.broadcasted_iota(jnp.int32, (1, PAIR), 1)
    first = lane < D_HM
    neg_inf = jnp.full((), -jnp.inf, jnp.float32)
    m_h0, m_h1 = m_row[:, 0:1], m_row[:, D_HM:D_HM + 1]
    m_prev = _half_bcast(m_h0, m_h1)
    dmat = jnp.where(incl, b_col + c_row, neg_inf)
    m_inter = b_col + m_prev
    m_t = jnp.maximum(m_inter, jnp.max(dmat, axis=1, keepdims=True))
    bl = jnp.where(last_sel, b_col, neg_inf)
    bl0 = jnp.max(bl[:CHUNK], axis=0, keepdims=True)
    bl1 = jnp.max(bl[CHUNK:], axis=0, keepdims=True)
    g = _half_bcast(bl0, bl1) - b_col + li_col
    mn0 = jnp.maximum(bl0 + m_h0, jnp.max(g[:CHUNK], axis=0, keepdims=True))
    mn1 = jnp.maximum(bl1 + m_h1, jnp.max(g[CHUNK:], axis=0, keepdims=True))
    wts = jnp.exp(g - _half_bcast(mn0, mn1))
    decay_row = jnp.where(first, jnp.exp(bl0 + m_h0 - mn0), jnp.exp(bl1 + m_h1 - mn1))
    wk = wts * k_st
    qk = _dot1(q_st, k_st, _NT)
    qc = _dot1(q_st, c_mat)
    kv = _dot1(wk, v_st, _TN)
    yield
    s = qk * jnp.exp(dmat - m_t)
    inter = jnp.exp(m_inter - m_t)
    sv = _dot1(s, v_st)
    yield
    num = inter * qc + sv
    den = inter * jnp.sum(q_st * n_row, axis=1, keepdims=True) + jnp.sum(s, axis=1, keepdims=True)
    h_st = num / jnp.maximum(jnp.abs(den), jnp.exp(-m_t))
    h = h_st[:CHUNK] + h_st[CHUNK:]
    c_new = decay_row * c_mat + kv
    n_new = decay_row * n_row + jnp.sum(wk, axis=0, keepdims=True)
    m_new = jnp.where(first, mn0, mn1)
    return h, c_new, n_new, m_new


def _mlstm_kernel(qf, kf, vf, gf, qb, kb, vb, gb, bias_ref, hf_ref, hb_ref, c_ref, n_ref, m_ref, *, nctx):
    j = pl.program_id(1)

    @pl.when(j == 0)
    def _():
        c_ref[...] = jnp.zeros_like(c_ref)
        n_ref[...] = jnp.zeros_like(n_ref)
        m_ref[...] = jnp.zeros_like(m_ref)

    row = lax.broadcasted_iota(jnp.int32, (PAIR, PAIR), 0)
    col = lax.broadcasted_iota(jnp.int32, (PAIR, PAIR), 1)
    same = (row // CHUNK) == (col // CHUNK)
    r64 = lax.broadcasted_iota(jnp.int32, (CHUNK, CHUNK), 0)
    c64 = lax.broadcasted_iota(jnp.int32, (CHUNK, CHUNK), 1)
    rcol = lax.broadcasted_iota(jnp.int32, (PAIR, 1), 0) % CHUNK
    lane = lax.broadcasted_iota(jnp.int32, (1, PAIR), 1)
    m0 = (lane < D_HM).astype(jnp.float32)
    m1 = 1.0 - m0
    is_f = (lane % (2 * H_M)) >= H_M
    dirs = (
        (qf, kf, vf, gf, hf_ref, (c64 <= r64), same & (col <= row), rcol == CHUNK - 1),
        (qb, kb, vb, gb, hb_ref, (c64 >= r64), same & (col >= row), rcol == 0),
    )
    states = [[(c_ref[d, p], n_ref[d, p], m_ref[d, p]) for p in range(NPAIR)] for d in range(2)]
    gens, meta = [], []
    for d, (q_ref, k_ref, v_ref, g_ref, h_ref, tri, incl, last_sel) in enumerate(dirs):
        tri = tri.astype(jnp.float32).astype(jnp.bfloat16)
        gl = g_ref[0] + bias_ref[...]
        act = jnp.where(is_f, jax.nn.log_sigmoid(gl), gl)
        b_all = _dot_exact_lhs(tri, act)
        act_t = act.T
        b_t = b_all.T
        for p in range(NPAIR):
            sl = slice(p * PAIR, (p + 1) * PAIR)
            ci, cf = d * 2 * H_M + 2 * p, d * 2 * H_M + H_M + 2 * p
            b_col = _col_stack(b_all, cf, cf + 1)
            li_col = _col_stack(act, ci, ci + 1)
            c_row = jnp.concatenate([act_t[ci:ci + 1, :] - b_t[cf:cf + 1, :],
                                     act_t[ci + 1:ci + 2, :] - b_t[cf + 1:cf + 2, :]], axis=1)
            gens.append(_mlstm_pair_chunk(
                q_ref[0, :, sl], k_ref[0, :, sl], v_ref[0, :, sl], b_col, li_col, c_row,
                *states[d][p], incl, last_sel, m0, m1))
            meta.append((d, p, sl, h_ref))
    for (d, p, sl, h_ref), (h, c_new, n_new, m_new) in zip(meta, _round_robin(gens)):
        c_ref[d, p] = c_new
        n_ref[d, p] = n_new
        m_ref[d, p] = m_new
        h_ref[0, :, sl] = h


def _scan_max(x, reverse):
    rows = lax.broadcasted_iota(jnp.int32, x.shape, 0)
    neg_inf = jnp.full((), -jnp.inf, jnp.float32)
    step = 1
    while step < CHUNK:
        if reverse:
            shifted = jnp.where(rows < CHUNK - step, pltpu.roll(x, CHUNK - step, 0), neg_inf)
        else:
            shifted = jnp.where(rows >= step, pltpu.roll(x, step, 0), neg_inf)
        x = jnp.maximum(x, shifted)
        step *= 2
    return x


def _mlstm_gates(gates, bias, tri, m_prev, is_f, reverse):
    gl = gates + bias
    act = jnp.where(is_f, jax.nn.log_sigmoid(gl), gl)
    b_all = _dot_exact_lhs(tri, act)
    li = pltpu.roll(act, H_M, 1)
    x = li - b_all
    m_inter = b_all + m_prev
    m_t = jnp.maximum(m_inter, b_all + _scan_max(x, reverse))
    b_last = b_all[0:1, :] if reverse else b_all[CHUNK - 1:CHUNK, :]
    g = b_last + x
    m_new = jnp.maximum(b_last + m_prev, jnp.max(g, axis=0, keepdims=True))
    wts = jnp.exp(g - m_new)
    decay = jnp.exp(b_last + m_prev - m_new)
    return b_all - m_t, jnp.exp(m_inter - m_t), jnp.exp(-m_t), x, wts, decay, m_new


def _mlstm_pair_chunk2(q, k, v, pieces, cf, decay_g, c_mat, n_row, incl, m0, m1):
    stack = lambda a: jnp.concatenate([a * m0, a * m1], axis=0)
    q_st, k_st, v_st = stack(q), stack(k), stack(v)
    lane = lax.broadcasted_iota(jnp.int32, (1, PAIR), 1)
    zero_bf = jnp.zeros((), jnp.bfloat16)
    sel = lambda a: jnp.concatenate([jnp.where(lane == cf, a, zero_bf), jnp.where(lane == cf + 1, a, zero_bf)], 0)
    wide = lambda parts: jnp.concatenate([sel(a) for a in parts], axis=1)
    ones = lambda rows: jnp.ones((rows, PAIR), jnp.bfloat16)
    bc = _dg(jnp.concatenate([wide(p[:2]) for p in pieces[:4]], axis=0), ones(2 * PAIR), _NN)
    bm_bc = bc[:PAIR] + _dg(sel(pieces[0][2]), ones(PAIR), _NN)
    inter_bc, enm_bc, wts_bc = (bc[i * PAIR:(i + 1) * PAIR] for i in range(1, 4))
    r_mat = _dg(jnp.ones((PAIR, 3 * PAIR), jnp.bfloat16), wide(pieces[4]), _NT)
    qk = _dot1(q_st, k_st, _NT)
    qc = _dot1(q_st, c_mat)
    yield
    neg_inf = jnp.full((), -jnp.inf, jnp.float32)
    s = qk * jnp.exp(jnp.where(incl, bm_bc + r_mat, neg_inf))
    wk = wts_bc * k_st
    sv = _dot1(s, v_st)
    kv = _dot1(wk, v_st, _TN)
    sums = _dot1(jnp.concatenate([q_st * n_row, s], axis=0), jnp.ones((PAIR, PAIR), jnp.float32))
    yield
    den = inter_bc * sums[:PAIR] + sums[PAIR:]
    h_st = (inter_bc * qc + sv) / jnp.maximum(jnp.abs(den), enm_bc)
    h = h_st[:CHUNK] + h_st[CHUNK:]
    decay_row = jnp.where(lane < D_HM, decay_g[:, cf:cf + 1], decay_g[:, cf + 1:cf + 2])
    c_new = decay_row * c_mat + kv
    n_new = decay_row * n_row + jnp.sum(wk, axis=0, keepdims=True)
    return h, c_new, n_new


def _mlstm_kernel2(qf, kf, vf, gf, qb, kb, vb, gb, bias_ref, hf_ref, hb_ref, c_ref, n_ref, m_ref, *, nctx):
    del nctx
    j = pl.program_id(1)

    @pl.when(j == 0)
    def _():
        c_ref[...] = jnp.zeros_like(c_ref)
        n_ref[...] = jnp.zeros_like(n_ref)
        m_ref[...] = jnp.zeros_like(m_ref)

    row = lax.broadcasted_iota(jnp.int32, (PAIR, PAIR), 0)
    col = lax.broadcasted_iota(jnp.int32, (PAIR, PAIR), 1)
    same = (row // CHUNK) == (col // CHUNK)
    r64 = lax.broadcasted_iota(jnp.int32, (CHUNK, CHUNK), 0)
    c64 = lax.broadcasted_iota(jnp.int32, (CHUNK, CHUNK), 1)
    lane = lax.broadcasted_iota(jnp.int32, (1, PAIR), 1)
    m0 = (lane < D_HM).astype(jnp.float32)
    m1 = 1.0 - m0
    is_f = (lane % (2 * H_M)) >= H_M
    dirs = (
        (qf, kf, vf, gf, hf_ref, (c64 <= r64), same & (col <= row)),
        (qb, kb, vb, gb, hb_ref, (c64 >= r64), same & (col >= row)),
    )
    states = [[(c_ref[d, p], n_ref[d, p]) for p in range(NPAIR)] for d in range(2)]
    gens, meta, m_news = [], [], []
    for d, (q_ref, k_ref, v_ref, g_ref, h_ref, tri, incl) in enumerate(dirs):
        tri = tri.astype(jnp.float32).astype(jnp.bfloat16)
        bm, inter, enm, x, wts, decay_g, m_new = _mlstm_gates(g_ref[0], bias_ref[...], tri, m_ref[d], is_f, d == 1)
        m_news.append(m_new)
        pieces = [_split3(a) for a in (bm, inter, enm, wts, x)]
        for p in range(NPAIR):
            sl = slice(p * PAIR, (p + 1) * PAIR)
            cf = d * 2 * H_M + H_M + 2 * p
            gens.append(_mlstm_pair_chunk2(q_ref[0, :, sl], k_ref[0, :, sl], v_ref[0, :, sl], pieces, cf, decay_g,
                                           *states[d][p], incl, m0, m1))
            meta.append((d, p, sl, h_ref))
    for (d, p, sl, h_ref), (h, c_new, n_new) in zip(meta, _round_robin(gens)):
        c_ref[d, p] = c_new
        n_ref[d, p] = n_new
        h_ref[0, :, sl] = h
    for d in range(2):
        m_ref[d] = m_news[d]


def _mlstm_scan(q, k, pn, bias_row, ctx_len):
    v = gates = pn
    b, t, w = q.shape
    nc = t // CHUNK
    nctx = ctx_len // CHUNK
    nlat = nc - nctx

    def fwd_map(i, j):
        return (i, j, 0)

    def bwd_map(i, j):
        return (i, jnp.where(j < nctx, nctx - 1 - j, nc - 1 - (j - nctx)), 0)

    blk = (1, CHUNK, w)
    gblk = (1, CHUNK, 128)
    gate_blk = (2 * W_M) // 128
    gate_of = lambda m: (lambda i, j: m(i, j)[:2] + (gate_blk,))
    in_specs = ([pl.BlockSpec(blk, fwd_map)] * 3 + [pl.BlockSpec(gblk, gate_of(fwd_map))]
                + [pl.BlockSpec(blk, bwd_map)] * 3 + [pl.BlockSpec(gblk, gate_of(bwd_map))]
                + [pl.BlockSpec((1, 128), lambda i, j: (0, 0))])
    out_specs = [
        pl.BlockSpec(blk, lambda i, j: (i, jnp.maximum(j - nctx, 0), 0)),
        pl.BlockSpec(blk, lambda i, j: (i, nlat - 1 - jnp.maximum(j - nctx, 0), 0)),
    ]
    return pl.pallas_call(
        functools.partial(_mlstm_kernel2, nctx=nctx),
        grid=(b, nc),
        in_specs=in_specs,
        out_specs=out_specs,
        out_shape=[jax.ShapeDtypeStruct((b, t - ctx_len, w), jnp.float32)] * 2,
        scratch_shapes=[pltpu.VMEM((2, NPAIR, PAIR, PAIR), jnp.float32),
                        pltpu.VMEM((2, NPAIR, 1, PAIR), jnp.float32),
                        pltpu.VMEM((2, 1, 128), jnp.float32)],
        compiler_params=pltpu.CompilerParams(
            dimension_semantics=("arbitrary", "arbitrary"), vmem_limit_bytes=VMEM_LIMIT),
        name="mlstm_scan",
    )(q, k, v, gates, q, k, v, gates, bias_row)


ROUTE_W = 128
N_ROUTE = N_GROUPS + N_EXPERTS


def _ln_rows(z, eps):
    mu = jnp.mean(z, -1, keepdims=True)
    zc = z - mu
    var = jnp.mean(zc * zc, -1, keepdims=True)
    return zc * lax.rsqrt(var + eps)


def _headnorm_mxu(z, bd2, eps):
    mu = _head_sums(z, bd2) * (1.0 / D_HM)
    zc = z - mu
    var = _head_sums(zc * zc, bd2) * (1.0 / D_HM)
    return zc * lax.rsqrt(var + eps)


def _outproj_kernel(hf, hb, mo, yf, yb, gg, bonus, x_ref, mod_ref, wout_ref, hpar_ref, ln_ref, rtw_ref, rtb_ref,
                    bd_ref, x1_ref, tok_ref, route_ref, ridx_ref, cnt_ref, carry_ref):
    i, j = pl.program_id(0), pl.program_id(1)

    @pl.when((i == 0) & (j == 0))
    def _():
        carry_ref[...] = jnp.zeros_like(carry_ref)

    avg = bd_ref[...]
    out_m = jax.nn.sigmoid(mo[0]) * (_headnorm_mxu(hf[0] + hb[0], avg, LN_EPS) * hpar_ref[0:1, :])
    y = _headnorm_mxu(yf[0] + yb[0], avg, GN_EPS) * hpar_ref[1:2, :] + hpar_ref[2:3, :]
    out_r = (y + bonus[0]) * gg[0]
    proj = (jnp.dot(out_m.astype(jnp.bfloat16), wout_ref[:W_M, :], preferred_element_type=jnp.float32)
            + jnp.dot(out_r.astype(jnp.bfloat16), wout_ref[W_M:, :], preferred_element_type=jnp.float32))
    g1, sh2, sc2 = mod_ref[0, 2:3, :], mod_ref[0, 3:4, :], mod_ref[0, 4:5, :]
    x1 = _ln_rows(ALPHA * x_ref[0] + g1 * proj, LN_EPS) * ln_ref[0:1, :] + ln_ref[1:2, :]
    x1_ref[0] = x1
    tok = _ln_rows(x1, LN_EPS) * (1.0 + sc2) + sh2
    tok_ref[0] = tok

    logits = _dot3_pre(tok, rtw_ref[...]) + rtb_ref[...]
    lane = lax.broadcasted_iota(jnp.int32, (ROW_TILE, ROUTE_W), 1)
    neg_inf = jnp.full((), -jnp.inf, jnp.float32)
    big = jnp.int32(ROUTE_W)
    is_g = lane < N_GROUPS
    lg = jnp.where(is_g, logits, neg_inf)
    gmax = jnp.max(lg, axis=1, keepdims=True)
    grp = jnp.min(jnp.where(lg == gmax, lane, big), axis=1, keepdims=True)
    p_top = 1.0 / jnp.sum(jnp.where(is_g, jnp.exp(lg - gmax), 0.0), axis=1, keepdims=True)
    in_grp = (lane >= N_GROUPS) & (lane < N_ROUTE) & ((lane - N_GROUPS) // E_PER_GROUP == grp)
    le = jnp.where(in_grp, logits, neg_inf)
    v1 = jnp.max(le, axis=1, keepdims=True)
    i1 = jnp.min(jnp.where(le == v1, lane, big), axis=1, keepdims=True)
    le2 = jnp.where(lane == i1, neg_inf, le)
    v2 = jnp.max(le2, axis=1, keepdims=True)
    i2 = jnp.min(jnp.where(le2 == v2, lane, big), axis=1, keepdims=True)
    e21 = jnp.exp(v2 - v1)
    w1 = (1.0 / (1.0 + e21)) * p_top
    w2 = (e21 / (1.0 + e21)) * p_top

    sel1, sel2 = lane == i1, lane == i2
    onehot = jnp.where(sel1 | sel2, 1.0, 0.0)
    tr = lax.broadcasted_iota(jnp.int32, (ROW_TILE, ROW_TILE), 0)
    tc = lax.broadcasted_iota(jnp.int32, (ROW_TILE, ROW_TILE), 1)
    strict = jnp.where(tc < tr, 1.0, 0.0).astype(jnp.bfloat16)
    before = _dg(strict, onehot.astype(jnp.bfloat16), _NN) + carry_ref[...]
    r1 = jnp.sum(jnp.where(sel1, before, 0.0), axis=1, keepdims=True)
    r2 = jnp.sum(jnp.where(sel2, before, 0.0), axis=1, keepdims=True)
    carry_ref[...] = carry_ref[...] + jnp.sum(onehot, axis=0, keepdims=True)
    e1 = (i1 - N_GROUPS).astype(jnp.float32)
    e2 = (i2 - N_GROUPS).astype(jnp.float32)
    route = jnp.where(lane == 0, e1, jnp.where(lane == 1, e2, jnp.where(lane == 2, r1, jnp.where(
        lane == 3, r2, jnp.where(lane == 4, w1, jnp.where(lane == 5, w2, 0.0))))))
    route_ref[0] = route
    ridx_ref[0] = route.T[0:8, :].astype(jnp.int32)
    cnt_ref[...] = jnp.broadcast_to(carry_ref[...], cnt_ref.shape)


def _outproj(h_f, h_b, pn, y_f, y_b, gg, bonus, x, mod3, wout_bf16, hpar, lnpar, rtw, rtb, bd, ctx_len):
    b, seq, d = x.shape
    nt = seq // ROW_TILE
    off = ctx_len // ROW_TILE
    lat = lambda w: pl.BlockSpec((1, ROW_TILE, w), lambda i, j: (i, j, 0))
    full = lambda shp: pl.BlockSpec(shp, lambda i, j: tuple(0 for _ in shp))
    in_specs = [
        lat(W_M), lat(W_M),
        pl.BlockSpec((1, ROW_TILE, W_M), lambda i, j: (i, j + off, 1)),
        lat(W_R), lat(W_R),
        pl.BlockSpec((1, ROW_TILE, W_R), lambda i, j: (i, j + off, 0)),
        pl.BlockSpec((1, ROW_TILE, W_R), lambda i, j: (i, j + off, 0)),
        lat(d),
        pl.BlockSpec((1, 6, d), lambda i, j: (i, 0, 0)),
        full((MIX_W, d)), full((8, W_M)), full((8, d)), full((3 * d, ROUTE_W)), full((1, ROUTE_W)),
        full((2 * HEAD_BLK, HEAD_BLK)),
    ]
    out_specs = [
        lat(d), lat(d), lat(ROUTE_W),
        pl.BlockSpec((1, 8, ROW_TILE), lambda i, j: (i * nt + j, 0, 0)),
        full((8, ROUTE_W)),
    ]
    out_shape = [
        jax.ShapeDtypeStruct((b, seq, d), jnp.float32),
        jax.ShapeDtypeStruct((b, seq, d), jnp.float32),
        jax.ShapeDtypeStruct((b, seq, ROUTE_W), jnp.float32),
        jax.ShapeDtypeStruct((b * nt, 8, ROW_TILE), jnp.int32),
        jax.ShapeDtypeStruct((8, ROUTE_W), jnp.float32),
    ]
    return pl.pallas_call(
        _outproj_kernel,
        grid=(b, nt),
        in_specs=in_specs,
        out_specs=out_specs,
        out_shape=out_shape,
        scratch_shapes=[pltpu.VMEM((1, ROUTE_W), jnp.float32)],
        compiler_params=pltpu.CompilerParams(
            dimension_semantics=("arbitrary", "arbitrary"), vmem_limit_bytes=VMEM_LIMIT),
        name="outproj_router",
    )(h_f, h_b, pn, y_f, y_b, gg, bonus, x, mod3, wout_bf16, hpar, lnpar, rtw, rtb, bd)


def _pair_dest(ridx_ref, starts_ref, r, slot):
    return starts_ref[ridx_ref[0, slot, r]] + ridx_ref[0, 2 + slot, r]


def _scatter_kernel(starts_ref, ridx_ref, tok_ref, xs_in_ref, xs_ref, sem):
    del xs_in_ref

    def row_copy(r, slot):
        dst = _pair_dest(ridx_ref, starts_ref, r, slot)
        return pltpu.make_async_copy(tok_ref.at[pl.ds(r, 1), :], xs_ref.at[pl.ds(dst, 1), :], sem)

    def issue(r, c):
        row_copy(r, 0).start()
        row_copy(r, 1).start()
        return c

    lax.fori_loop(0, ROW_TILE, issue, 0, unroll=DMA_UNROLL)
    for half in range(TOP_K_IN_GROUP):
        pltpu.make_async_copy(tok_ref, xs_ref.at[pl.ds(half * ROW_TILE, ROW_TILE), :], sem).wait()


def _moe_scatter(starts, ridx, tok2d, n_rows):
    n, d = tok2d.shape
    xs_zero = jnp.zeros((n_rows, d), tok2d.dtype)
    nt = n // ROW_TILE
    return pl.pallas_call(
        _scatter_kernel,
        grid_spec=pltpu.PrefetchScalarGridSpec(
            num_scalar_prefetch=1,
            grid=(nt,),
            in_specs=[
                pl.BlockSpec((1, 8, ROW_TILE), lambda i, s: (i, 0, 0), memory_space=pltpu.SMEM),
                pl.BlockSpec((ROW_TILE, d), lambda i, s: (i, 0)),
                pl.BlockSpec(memory_space=pl.ANY),
            ],
            out_specs=pl.BlockSpec(memory_space=pl.ANY),
            scratch_shapes=[pltpu.SemaphoreType.DMA(())],
        ),
        out_shape=jax.ShapeDtypeStruct((n_rows, d), tok2d.dtype),
        input_output_aliases={3: 0},
        compiler_params=pltpu.CompilerParams(dimension_semantics=("arbitrary",), vmem_limit_bytes=VMEM_LIMIT),
        name="moe_scatter",
    )(starts, ridx, tok2d, xs_zero)


def _expert_kernel(blk_exp_ref, valid_ref, xs_ref, wg_ref, wu_ref, wd_ref, ys_ref):
    i = pl.program_id(0)
    valid = valid_ref[i]

    @pl.when(valid > 0)
    def _():
        xb = xs_ref[...].astype(jnp.bfloat16)
        hg = jnp.dot(xb, wg_ref[0], preferred_element_type=jnp.float32)
        hu = jnp.dot(xb, wu_ref[0], preferred_element_type=jnp.float32)
        hb = (jax.nn.silu(hg) * hu).astype(jnp.bfloat16)
        ys_ref[...] = jnp.dot(hb, wd_ref[0], preferred_element_type=jnp.float32)

    @pl.when(valid == 0)
    def _():
        ys_ref[...] = jnp.zeros_like(ys_ref)


def _moe_experts(blk_exp, blk_valid, xs, wg, wu, wd):
    nrow, d = xs.shape
    de = wg.shape[2]
    return pl.pallas_call(
        _expert_kernel,
        grid_spec=pltpu.PrefetchScalarGridSpec(
            num_scalar_prefetch=2,
            grid=(nrow // MOE_BLOCK,),
            in_specs=[
                pl.BlockSpec((MOE_BLOCK, d), lambda i, be, nu: (i, 0)),
                pl.BlockSpec((1, d, de), lambda i, be, nu: (be[i], 0, 0)),
                pl.BlockSpec((1, d, de), lambda i, be, nu: (be[i], 0, 0)),
                pl.BlockSpec((1, de, d), lambda i, be, nu: (be[i], 0, 0)),
            ],
            out_specs=pl.BlockSpec((MOE_BLOCK, d), lambda i, be, nu: (i, 0)),
        ),
        out_shape=jax.ShapeDtypeStruct((nrow, d), jnp.float32),
        compiler_params=pltpu.CompilerParams(dimension_semantics=("arbitrary",), vmem_limit_bytes=VMEM_LIMIT),
        name="moe_experts",
    )(blk_exp, blk_valid, xs, wg, wu, wd)


def _combine_kernel(starts_ref, ridx_ref, ridx_next_ref, route_ref, x1_ref, mod_ref, ln_ref, ys_ref, out_ref,
                    ybuf, sems):
    i = pl.program_id(0)
    n = pl.num_programs(0)
    cur = i % 2

    def gather(idx_ref, buf):
        def issue(r, c):
            for slot in range(TOP_K_IN_GROUP):
                src = _pair_dest(idx_ref, starts_ref, r, slot)
                pltpu.make_async_copy(ys_ref.at[pl.ds(src, 1), :],
                                      ybuf.at[buf, pl.ds(slot * ROW_TILE + r, 1), :], sems.at[buf]).start()
            return c

        lax.fori_loop(0, ROW_TILE, issue, 0, unroll=DMA_UNROLL)

    @pl.when(i == 0)
    def _():
        gather(ridx_ref, 0)

    @pl.when(i + 1 < n)
    def _():
        gather(ridx_next_ref, 1 - cur)

    pltpu.make_async_copy(ys_ref.at[pl.ds(0, TOP_K_IN_GROUP * ROW_TILE), :], ybuf.at[cur], sems.at[cur]).wait()
    route = route_ref[0]
    ffn = ybuf[cur, 0:ROW_TILE, :] * route[:, 4:5] + ybuf[cur, ROW_TILE:2 * ROW_TILE, :] * route[:, 5:6]
    g2 = mod_ref[0, 5:6, :]
    out_ref[0] = _ln_rows(ALPHA * x1_ref[0] + g2 * ffn, LN_EPS) * ln_ref[2:3, :] + ln_ref[3:4, :]


def _moe_combine(starts, ridx, route, x1, mod3, lnpar, ys):
    b, seq, d = x1.shape
    nt = seq // ROW_TILE
    n = b * nt
    tile = lambda w: pl.BlockSpec((1, ROW_TILE, w), lambda i, s: (i // nt, i % nt, 0))
    return pl.pallas_call(
        _combine_kernel,
        grid_spec=pltpu.PrefetchScalarGridSpec(
            num_scalar_prefetch=1,
            grid=(n,),
            in_specs=[
                pl.BlockSpec((1, 8, ROW_TILE), lambda i, s: (i, 0, 0), memory_space=pltpu.SMEM),
                pl.BlockSpec((1, 8, ROW_TILE), lambda i, s: (jnp.minimum(i + 1, n - 1), 0, 0),
                             memory_space=pltpu.SMEM),
                tile(ROUTE_W),
                tile(d),
                pl.BlockSpec((1, 6, d), lambda i, s: (i // nt, 0, 0)),
                pl.BlockSpec((8, d), lambda i, s: (0, 0)),
                pl.BlockSpec(memory_space=pl.ANY),
            ],
            out_specs=tile(d),
            scratch_shapes=[pltpu.VMEM((2, TOP_K_IN_GROUP * ROW_TILE, d), jnp.float32),
                            pltpu.SemaphoreType.DMA((2,))],
        ),
        out_shape=jax.ShapeDtypeStruct((b, seq, d), jnp.float32),
        compiler_params=pltpu.CompilerParams(dimension_semantics=("arbitrary",), vmem_limit_bytes=VMEM_LIMIT),
        name="moe_combine",
    )(starts, ridx, ridx, route, x1, mod3, lnpar, ys)


def _moe_plan(cnt, n_pairs):
    counts = cnt[0, N_GROUPS:N_ROUTE].astype(jnp.int32)
    padded = (counts + MOE_BLOCK - 1) // MOE_BLOCK * MOE_BLOCK
    pad_ends = jnp.cumsum(padded)
    pad_starts = pad_ends - padded
    n_blk = -(-n_pairs // MOE_BLOCK) + N_EXPERTS
    blk_row0 = jnp.arange(n_blk, dtype=jnp.int32) * MOE_BLOCK
    blk_exp = jnp.minimum(jnp.sum((pad_ends[None, :] <= blk_row0[:, None]).astype(jnp.int32), axis=1), N_EXPERTS - 1)
    blk_valid = jnp.clip(counts[blk_exp] - (blk_row0 - pad_starts[blk_exp]), 0, MOE_BLOCK)
    blk_valid = jnp.where(blk_row0 < pad_ends[-1], blk_valid, 0).astype(jnp.int32)
    return pad_starts.astype(jnp.int32), blk_exp, blk_valid, n_blk


def _layernorm(z, gain=None, bias=None):
    zf = z.astype(jnp.float32)
    mu = jnp.mean(zf, -1, keepdims=True)
    var = jnp.mean(jnp.square(zf - mu), -1, keepdims=True)
    y = (zf - mu) * lax.rsqrt(var + LN_EPS)
    if gain is not None:
        y = y * gain + bias
    return y.astype(z.dtype)


def _headnorm(z, eps):
    zf = z.astype(jnp.float32)
    mu = jnp.mean(zf, -1, keepdims=True)
    var = jnp.mean(jnp.square(zf - mu), -1, keepdims=True)
    return (zf - mu) * lax.rsqrt(var + eps)


def _modulate(z, shift, scale):
    return _layernorm(z) * (1 + scale) + shift


def _rev_segments(u, ctx_len):
    return jnp.concatenate([jnp.flip(u[:, :ctx_len], 1), jnp.flip(u[:, ctx_len:], 1)], axis=1)


def _dir_stack(u_fwd, u_bwd, ctx_len):
    return jnp.concatenate([u_fwd, _rev_segments(u_bwd, ctx_len)], axis=0)


def _dir_merge(y, ctx_len):
    b = y.shape[0] // 2
    return y[:b] + _rev_segments(y[b:], ctx_len)


def _conv_grid(u, w, rows):
    b, l, ch = u.shape
    img = u.reshape(b, rows, l // rows, ch)
    out = lax.conv_general_dilated(img, w[:, :, None, :].astype(u.dtype), (1, 1), 'SAME',
                                   dimension_numbers=('NHWC', 'HWIO', 'NHWC'), feature_group_count=ch)
    return out.reshape(b, l, ch)


def _conv_seq(u, w_row):
    ch = u.shape[-1]
    return lax.conv_general_dilated(u, w_row[:, None, :].astype(u.dtype), (1,), 'SAME',
                                    dimension_numbers=('NWC', 'WIO', 'NWC'), feature_group_count=ch)


def _mlstm_chunkwise(q, k, v, log_i, log_f):
    z, t, h, dh = q.shape
    nc = t // CHUNK

    def chunks(a):
        a = a.astype(jnp.float32).reshape((z, nc, CHUNK, h) + a.shape[3:])
        return jnp.swapaxes(jnp.swapaxes(a, 0, 1), 2, 3)
    qc, kc, vc, lic, lfc = chunks(q), chunks(k), chunks(v), chunks(log_i), chunks(log_f)
    tri = jnp.tril(jnp.ones((CHUNK, CHUNK), bool))

    def step(carry, inp):
        c_mat, n_vec, m = carry
        qb, kb, vb, li, lf = inp
        b = jnp.cumsum(lf, -1)
        dmat = jnp.where(tri, b[..., :, None] - b[..., None, :] + li[..., None, :], -jnp.inf)
        m_inter = b + m[..., None]
        m_t = jnp.maximum(m_inter, jnp.max(dmat, -1))
        s = jnp.einsum('zhtd,zhsd->zhts', qb, kb) * jnp.exp(dmat - m_t[..., None])
        inter = jnp.exp(m_inter - m_t)
        num = inter[..., None] * jnp.einsum('zhtd,zhde->zhte', qb, c_mat) + jnp.einsum('zhts,zhse->zhte', s, vb)
        den = inter * jnp.einsum('zhtd,zhd->zht', qb, n_vec) + jnp.sum(s, -1)
        h_out = num / jnp.maximum(jnp.abs(den), jnp.exp(-m_t))[..., None]
        b_last = b[..., -1]
        g = b_last[..., None] - b + li
        m_new = jnp.maximum(b_last + m, jnp.max(g, -1))
        wts = jnp.exp(g - m_new[..., None])
        decay = jnp.exp(b_last + m - m_new)
        c_mat = decay[..., None, None] * c_mat + jnp.einsum('zhs,zhsd,zhse->zhde', wts, kb, vb)
        n_vec = decay[..., None] * n_vec + jnp.einsum('zhs,zhsd->zhd', wts, kb)
        return (c_mat, n_vec, m_new), h_out

    init = (jnp.zeros((z, h, dh, dh), jnp.float32), jnp.zeros((z, h, dh), jnp.float32), jnp.zeros((z, h), jnp.float32))
    _, hs = lax.scan(step, init, (qc, kc, vc, lic, lfc))
    return jnp.swapaxes(jnp.swapaxes(hs, 2, 3), 0, 1).reshape(z, t, h, dh)


def _mlstm_group(mq, mk, mv, mo, gates, b_i, b_f, norm_w, ctx_len):
    b, t, _ = mq.shape
    q = jax.nn.silu(mq)
    k = jax.nn.silu(mk) * (D_HM ** -0.5)
    bias_row = jnp.concatenate([b_i[0], b_f[0], b_i[1], b_f[1], jnp.zeros((128 - 4 * H_M,), jnp.float32)])[None, :]
    h_f, h_b = _mlstm_scan(q, k, mv, gates, bias_row, ctx_len)
    h = (h_f + h_b).reshape(b, t - ctx_len, H_M, D_HM)
    h = _headnorm(h, LN_EPS) * norm_w.reshape(H_M, D_HM)
    return jax.nn.sigmoid(mo[:, ctx_len:]) * h.reshape(b, t - ctx_len, W_M)


def _rwkv7_scan(r, w, k, v, kh, a):
    z, t, h, n = r.shape
    seq = tuple(jnp.moveaxis(s.astype(jnp.float32), 1, 0) for s in (r, w, k, v, kh, a))

    def step(state, inp):
        r_t, w_t, k_t, v_t, kh_t, a_t = inp
        removed = jnp.einsum('zhvk,zhk->zhv', state, kh_t)
        state = (state * w_t[:, :, None, :] - removed[..., None] * (kh_t * a_t)[:, :, None, :]
                 + v_t[..., None] * k_t[:, :, None, :])
        return state, jnp.einsum('zhvk,zhk->zhv', state, r_t)

    _, y = lax.scan(step, jnp.zeros((z, h, n, n), jnp.float32), seq)
    return jnp.moveaxis(y, 0, 1)


def _rwkv7_group(rr, rk, rv, lw_f, lw_b, la, lg, w0, w_b_mat, a0, a_b_mat, g_b_mat, kk, ka, bonus_w, gn_w, gn_b,
                 ctx_len):
    b, t, _ = rr.shape
    shp = (b, t, H_R, D_HR)
    logw_f = -DECAY_SCALE * jax.nn.sigmoid((w0[0] + jnp.tanh(lw_f) @ w_b_mat[0]).astype(jnp.float32))
    logw_b = -DECAY_SCALE * jax.nn.sigmoid((w0[1] + jnp.tanh(lw_b) @ w_b_mat[1]).astype(jnp.float32))
    a = jax.nn.sigmoid(a0 + la @ a_b_mat)
    g = jax.nn.sigmoid(lg) @ g_b_mat
    kap = (rk * kk).reshape(shp).astype(jnp.float32)
    kh = kap / jnp.maximum(jnp.sqrt(jnp.sum(jnp.square(kap), -1, keepdims=True)), 1e-12)
    kmod = (rk * (1 + (a - 1) * ka)).reshape(shp)
    r = rr.reshape(shp)
    v = rv.reshape(shp)
    a_h = a.reshape(shp)
    flat = lambda u: u.reshape(b, t, W_R)
    y_f, y_b = _rwkv_scan(rr, flat(kmod), rv, flat(kh), flat(kh * a_h), logw_f, logw_b, ctx_len)
    y = (y_f + y_b).reshape(b, t - ctx_len, H_R, D_HR)
    y = _headnorm(y, GN_EPS) * gn_w.reshape(H_R, D_HR) + gn_b.reshape(H_R, D_HR)
    bonus = (jnp.sum(r * kmod * bonus_w, -1, keepdims=True) * v)[:, ctx_len:]
    return (y + bonus).reshape(b, t - ctx_len, W_R) * g[:, ctx_len:]


def _hier_moe(h, rt_g, rt_g_b, rt_e, rt_e_b, ex_gate, ex_up, ex_down):
    b, t, d = h.shape
    tok = h.reshape(b * t, d)
    n = tok.shape[0]
    lg = (tok @ rt_g + rt_g_b).astype(jnp.float32)
    p_grp = jax.nn.softmax(lg, axis=-1)
    grp = jnp.argmax(lg, axis=-1).astype(jnp.int32)
    p_top = jnp.take_along_axis(p_grp, grp[:, None], axis=-1)
    le = (tok @ rt_e + rt_e_b).astype(jnp.float32).reshape(n, N_GROUPS, E_PER_GROUP)
    le_grp = jnp.take_along_axis(le, jnp.broadcast_to(grp[:, None, None], (n, 1, E_PER_GROUP)), axis=1)[:, 0]
    top_val, top_idx = lax.top_k(le_grp, TOP_K_IN_GROUP)
    wts = jax.nn.softmax(top_val, axis=-1) * p_top
    e_flat = (grp[:, None] * E_PER_GROUP + top_idx.astype(jnp.int32)).reshape(-1)
    w_flat = wts.reshape(-1)
    t_flat = jnp.repeat(jnp.arange(n, dtype=jnp.int32), TOP_K_IN_GROUP)
    order = jnp.argsort(e_flat)
    e_s, t_s, w_s = e_flat[order], t_flat[order], w_flat[order]
    counts = jnp.zeros((N_EXPERTS,), jnp.int32).at[e_flat].add(1)
    starts = jnp.cumsum(counts) - counts
    padded = (counts + MOE_BLOCK - 1) // MOE_BLOCK * MOE_BLOCK
    pad_ends = jnp.cumsum(padded)
    pad_starts = pad_ends - padded
    dest = pad_starts[e_s] + jnp.arange(e_s.shape[0], dtype=jnp.int32) - starts[e_s]
    n_pairs = n * TOP_K_IN_GROUP
    buf = (-(-n_pairs // MOE_BLOCK) + N_EXPERTS) * MOE_BLOCK
    tok_buf = jnp.full((buf,), n, jnp.int32).at[dest].set(t_s)
    w_buf = jnp.zeros((buf,), w_s.dtype).at[dest].set(w_s)
    n_blk = buf // MOE_BLOCK
    blk_exp = jnp.minimum(jnp.searchsorted(pad_ends, jnp.arange(n_blk, dtype=jnp.int32) * MOE_BLOCK, side='right'),
                          N_EXPERTS - 1)
    tok_pad = jnp.concatenate([tok, jnp.zeros((1, d), tok.dtype)], 0)

    def expert_block(args):
        idx, w, e = args
        xb = tok_pad[idx]
        hb = jax.nn.silu(xb @ ex_gate[e]) * (xb @ ex_up[e])
        return (hb @ ex_down[e]) * w[:, None]

    y = lax.map(expert_block, (tok_buf.reshape(n_blk, MOE_BLOCK), w_buf.reshape(n_blk, MOE_BLOCK), blk_exp))
    out = jnp.zeros((n + 1, d), y.dtype).at[tok_buf].add(y.reshape(buf, d))
    return out[:n].reshape(b, t, d).astype(h.dtype)


def kernel(x, c, ctx, c_ctx, w_ada, b_ada, w_in, conv_w, m_bias_i, m_bias_f, m_norm_w, r_w0, r_wB, r_a0, r_aB,
           r_gB, r_kk, r_ka, r_bonus, r_norm_w, r_norm_b, w_out, ln1_g, ln1_b, ln2_g, ln2_b, rt_g, rt_g_b, rt_e,
           rt_e_b, ex_gate, ex_up, ex_down):
    assert w_ada.shape[0] == DEPTH
    bsz, seq, d = x.shape
    ctx_len = ctx.shape[1]
    assert ctx_len == ROW_TILE and seq % ROW_TILE == 0
    assert seq % GRID_W == 0 and ROW_TILE % GRID_W == 0
    l = 0
    zrow = lambda n, w: jnp.zeros((n, w), jnp.float32)
    mrows = -(-(bsz + 1) // 8) * 8
    c_pad = jnp.concatenate([c, c_ctx[None, :], zrow(mrows - bsz - 1, d)], 0)
    mod_all = _ada(c_pad, w_ada[l], b_ada[l][None, :])[:bsz + 1]
    mod3 = mod_all.reshape(bsz + 1, 6, d)
    pc, pn = _inproj(ctx, x, mod3, _pad_w_in(w_in[l]))
    cw = jnp.concatenate([conv_w[l].reshape(CONV_K * CONV_K, CONV_CH), zrow(16 - CONV_K * CONV_K, CONV_CH)], 0)
    rpar = jnp.concatenate([r_w0[l], r_a0[l][None, :], r_kk[l][None, :], r_ka[l][None, :],
                            r_bonus[l].reshape(1, W_R), zrow(2, W_R)], 0)
    wbf = jnp.concatenate([r_wB[l][0], zrow(W_LORA, W_R)], 0)
    wbb = jnp.concatenate([zrow(W_LORA, W_R), r_wB[l][1]], 0)
    ab_mat = jnp.concatenate([r_aB[l], zrow(128 - A_LORA, W_R)], 0)
    bd = _head_sum_operator(D_HR)
    q, k, rr, kmod, rv, kh, kha, logw_f, logw_b, gg, bonus = _prep(
        pc, pn, cw, rpar, _stack3(wbf), _stack3(wbb), _stack3(ab_mat), _stack3(r_gB[l]), bd)
    bias_row = jnp.concatenate([m_bias_i[l][0], m_bias_f[l][0], m_bias_i[l][1], m_bias_f[l][1],
                                jnp.zeros((128 - 4 * H_M,), jnp.float32)])[None, :]
    h_f, h_b = _mlstm_scan(q, k, pn, bias_row, ctx_len)
    y_f, y_b = _rwkv_scan(rr, kmod, rv, kh, kha, logw_f, logw_b, ctx_len)
    hpar = jnp.zeros((8, W_M), jnp.float32).at[0].set(m_norm_w[l]).at[1].set(r_norm_w[l]).at[2].set(r_norm_b[l])
    lnpar = jnp.zeros((8, d), jnp.float32).at[0].set(ln1_g[l]).at[1].set(ln1_b[l]).at[2].set(ln2_g[l]).at[3].set(
        ln2_b[l])
    rtw = jnp.concatenate([rt_g[l], rt_e[l], jnp.zeros((d, ROUTE_W - N_ROUTE), jnp.float32)], axis=1)
    rtb = jnp.concatenate([rt_g_b[l], rt_e_b[l], jnp.zeros((ROUTE_W - N_ROUTE,), jnp.float32)])[None, :]
    x1, tok, route, ridx, cnt = _outproj(h_f, h_b, pn, y_f, y_b, gg, bonus, x, mod3, w_out[l].astype(jnp.bfloat16),
                                         hpar, lnpar, _stack3(rtw), rtb, bd, ctx_len)
    n_tok = bsz * seq
    starts, blk_exp, blk_valid, n_blk = _moe_plan(cnt, n_tok * TOP_K_IN_GROUP)
    xs = _moe_scatter(starts, ridx, tok.reshape(n_tok, d), n_blk * MOE_BLOCK)
    ys = _moe_experts(blk_exp, blk_valid, xs, ex_gate[l].astype(jnp.bfloat16), ex_up[l].astype(jnp.bfloat16),
                      ex_down[l].astype(jnp.bfloat16))
    return _moe_combine(starts, ridx, route, x1, mod3, lnpar, ys)
```

```python
import functools
import math

import jax
import jax.numpy as jnp
from jax import lax
import numpy as np
from jax.experimental import pallas as pl
from jax.experimental.pallas import tpu as pltpu

H_M = 8
D_HM = 64
W_M = H_M * D_HM
H_R = 8
D_HR = 64
W_R = H_R * D_HR
MIX_W = W_M + W_R
W_LORA = 64
A_LORA = 64
G_LORA = 128
CONV_K = 3
CONV_CH = 2 * W_M + 3 * W_R
SECTION_WIDTHS = (W_M, W_M, W_R, W_R, W_R, W_M, W_M, 4 * H_M, W_LORA, W_LORA, A_LORA, G_LORA)
IN_COLS = sum(SECTION_WIDTHS)
SPLIT_POINTS = tuple(int(v) for v in np.cumsum(SECTION_WIDTHS)[:-1])
CHUNK = 64
N_GROUPS = 4
E_PER_GROUP = 8
N_EXPERTS = N_GROUPS * E_PER_GROUP
TOP_K_IN_GROUP = 2
D_EXPERT = 512
MOE_BLOCK = 256
DEPTH = 1
ALPHA = (2.0 * DEPTH) ** 0.25
DECAY_SCALE = math.exp(-0.5)
LN_EPS = 1e-6
GN_EPS = 64e-5

ROW_TILE = 256
NONCONV_W = 1536
VMEM_LIMIT = 56 * 1024 * 1024
DMA_UNROLL = 8


def _inproj_kernel(ctx_ref, x_ref, mod_ref, w_ref, oc_ref, on_ref):
    j = pl.program_id(1)

    def body(src_ref):
        z = src_ref[0]
        mu = jnp.mean(z, -1, keepdims=True)
        zc = z - mu
        var = jnp.mean(zc * zc, -1, keepdims=True)
        y = zc * lax.rsqrt(var + LN_EPS)
        h = (y * (1.0 + mod_ref[0, 1:2, :]) + mod_ref[0, 0:1, :]).astype(jnp.bfloat16)
        oc_ref[0] = jnp.dot(h, w_ref[:, :CONV_CH], preferred_element_type=jnp.float32)
        on_ref[0] = jnp.dot(h, w_ref[:, CONV_CH:], preferred_element_type=jnp.float32)

    @pl.when(j == 0)
    def _():
        body(ctx_ref)

    @pl.when(j > 0)
    def _():
        body(x_ref)


def _inproj(ctx, x, mod_all, w_pad):
    b, seq, d = x.shape
    nt = (ctx.shape[1] + seq) // ROW_TILE
    t_all = ctx.shape[1] + seq
    return pl.pallas_call(
        _inproj_kernel,
        grid=(b, nt),
        in_specs=[
            pl.BlockSpec((1, ROW_TILE, d), lambda i, j: (i, 0, 0)),
            pl.BlockSpec((1, ROW_TILE, d), lambda i, j: (i, jnp.maximum(j - 1, 0), 0)),
            pl.BlockSpec((1, 6, d), lambda i, j: (jnp.where(j == 0, b, i), 0, 0)),
            pl.BlockSpec((d, CONV_CH + NONCONV_W), lambda i, j: (0, 0)),
        ],
        out_specs=[
            pl.BlockSpec((1, ROW_TILE, CONV_CH), lambda i, j: (i, j, 0)),
            pl.BlockSpec((1, ROW_TILE, NONCONV_W), lambda i, j: (i, j, 0)),
        ],
        out_shape=[
            jax.ShapeDtypeStruct((b, t_all, CONV_CH), jnp.float32),
            jax.ShapeDtypeStruct((b, t_all, NONCONV_W), jnp.float32),
        ],
        compiler_params=pltpu.CompilerParams(
            dimension_semantics=("arbitrary", "arbitrary"), vmem_limit_bytes=VMEM_LIMIT),
        name="inproj",
    )(ctx, x, mod_all, w_pad)


def _pad_w_in(w_in):
    d = w_in.shape[0]
    sp = SPLIT_POINTS
    z = lambda n: jnp.zeros((d, n), w_in.dtype)
    gates = w_in[:, sp[6]:sp[7]]
    lw = w_in[:, sp[7]:sp[9]]
    la = w_in[:, sp[9]:sp[10]]
    lg = w_in[:, sp[10]:]
    return jnp.concatenate([w_in[:, :sp[6]], gates, z(96), lw, la, z(64), lg], axis=1).astype(jnp.bfloat16)


PAIR = 2 * D_HR
NPAIR = H_R // 2
_NN = (((1,), (0,)), ((), ()))
_NT = (((1,), (1,)), ((), ()))
_TN = (((0,), (0,)), ((), ()))


def _split2(a):
    hi = a.astype(jnp.bfloat16)
    lo = (a - hi.astype(jnp.float32)).astype(jnp.bfloat16)
    return hi, lo


def _split3(a):
    a1 = a.astype(jnp.bfloat16)
    r1 = a - a1.astype(jnp.float32)
    a2 = r1.astype(jnp.bfloat16)
    a3 = (r1 - a2.astype(jnp.float32)).astype(jnp.bfloat16)
    return a1, a2, a3


def _dg(a, b, dims):
    return lax.dot_general(a, b, dims, preferred_element_type=jnp.float32)


def _dot3(a, b, dims=_NN):
    ah, al = _split2(a)
    bh, bl = _split2(b)
    return _dg(ah, bh, dims) + _dg(ah, bl, dims) + _dg(al, bh, dims)


def _dot1(a, b, dims=_NN):
    return _dg(a.astype(jnp.bfloat16), b.astype(jnp.bfloat16), dims)


def _stack3(w):
    hi = w.astype(jnp.bfloat16)
    lo = (w - hi.astype(jnp.float32)).astype(jnp.bfloat16)
    return jnp.concatenate([hi, lo, hi], axis=0)


def _dot3_pre(a, w3):
    ah, al = _split2(a)
    return _dg(jnp.concatenate([ah, ah, al], axis=1), w3, _NN)


HEAD_BLK = 256


def _head_sum_operator(head):
    r = np.arange(HEAD_BLK) // head
    bd = (r[:, None] == r[None, :]).astype(np.float32)
    return jnp.asarray(np.concatenate([bd, bd], axis=0), jnp.bfloat16)


def _head_sums(z, bd2):
    hi, lo = _split2(z)
    out = []
    for c in range(0, z.shape[1], HEAD_BLK):
        lhs = jnp.concatenate([hi[:, c:c + HEAD_BLK], lo[:, c:c + HEAD_BLK]], axis=1)
        out.append(_dg(lhs, bd2, _NN))
    return jnp.concatenate(out, axis=1)


def _dot_exact_lhs(a_bf16, b, dims=_NN):
    b1, b2, b3 = _split3(b)
    return _dg(a_bf16, b1, dims) + _dg(a_bf16, b2, dims) + _dg(a_bf16, b3, dims)


def _ada_kernel(c_ref, w_ref, b_ref, o_ref):
    o_ref[...] = _dot3(jax.nn.silu(c_ref[...]), w_ref[...]) + b_ref[...]


def _ada(c_pad, w, bias):
    m, d = c_pad.shape
    n = w.shape[1]
    return pl.pallas_call(
        _ada_kernel,
        grid=(n // d,),
        in_specs=[pl.BlockSpec((m, d), lambda j: (0, 0)),
                  pl.BlockSpec((d, d), lambda j: (0, j)),
                  pl.BlockSpec((1, d), lambda j: (0, j))],
        out_specs=pl.BlockSpec((m, d), lambda j: (0, j)),
        out_shape=jax.ShapeDtypeStruct((m, n), jnp.float32),
        compiler_params=pltpu.CompilerParams(dimension_semantics=("arbitrary",), vmem_limit_bytes=VMEM_LIMIT),
        name="adaln",
    )(c_pad, w, bias)


GRID_W = 64
HALO = GRID_W


def _prep_kernel(top_ref, main_ref, bot_ref, pn_ref, cw_ref, rpar_ref, wbf_ref, wbb_ref, ab_ref, gb_ref, bd_ref,
                 q_ref, k_ref, r_ref, km_ref, v_ref, kh_ref, abo_ref, lwf_ref, lwb_ref, gg_ref, bon_ref, xbuf):
    j = pl.program_id(1)
    nt = pl.num_programs(1)
    is_ctx = j == 0
    top_ok = j >= 2
    bot_ok = (j >= 1) & (j < nt - 1)
    l_idx = lax.broadcasted_iota(jnp.int32, (ROW_TILE, 1), 0)
    col = jnp.where(is_ctx, l_idx, l_idx % GRID_W)
    left_ok = col != 0
    right_ok = col != jnp.where(is_ctx, ROW_TILE - 1, GRID_W - 1)
    vert = jnp.where(is_ctx, 0.0, 1.0)
    sec = {}
    for s in range(5):
        sl = slice(s * W_M, (s + 1) * W_M)
        xbuf[0:HALO, :] = jnp.where(top_ok, top_ref[0, :, sl], 0.0)
        xbuf[HALO:HALO + ROW_TILE, :] = main_ref[0, :, sl]
        xbuf[HALO + ROW_TILE:2 * HALO + ROW_TILE, :] = jnp.where(bot_ok, bot_ref[0, :, sl], 0.0)
        cols = [None, None, None]
        for dr in range(3):
            base = HALO + GRID_W * (dr - 1)
            w3 = cw_ref[3 * dr:3 * dr + 3, sl] if dr == 1 else cw_ref[3 * dr:3 * dr + 3, sl] * vert
            xrow = xbuf[base:base + ROW_TILE, :]
            for dc in range(3):
                term = xrow * w3[dc:dc + 1]
                cols[dc] = term if cols[dc] is None else cols[dc] + term
        left = jnp.where(left_ok, pltpu.roll(cols[0], 1, 0), 0.0)
        right = jnp.where(right_ok, pltpu.roll(cols[2], ROW_TILE - 1, 0), 0.0)
        sec[s] = left + cols[1] + right
    q_ref[0] = jax.nn.silu(sec[0])
    k_ref[0] = jax.nn.silu(sec[1]) * (D_HM ** -0.5)
    rr, rk, rv = sec[2], sec[3], sec[4]
    r_ref[0] = rr
    v_ref[0] = rv
    lw = jnp.tanh(pn_ref[0, :, 1152:1280])
    lwf_ref[0] = -DECAY_SCALE * jax.nn.sigmoid(rpar_ref[0:1, :] + _dot3_pre(lw, wbf_ref[...]))
    lwb_ref[0] = -DECAY_SCALE * jax.nn.sigmoid(rpar_ref[1:2, :] + _dot3_pre(lw, wbb_ref[...]))
    a = jax.nn.sigmoid(rpar_ref[2:3, :] + _dot3_pre(pn_ref[0, :, 1280:1408], ab_ref[...]))
    gg_ref[0] = _dot3_pre(jax.nn.sigmoid(pn_ref[0, :, 1408:1536]), gb_ref[...])
    kap = rk * rpar_ref[3:4, :]
    norm = jnp.sqrt(_head_sums(kap * kap, bd_ref[...]))
    kh = kap / jnp.maximum(norm, 1e-12)
    kmod = rk * (1.0 + (a - 1.0) * rpar_ref[4:5, :])
    kh_ref[0] = kh
    abo_ref[0] = kh * a
    km_ref[0] = kmod
    bon_ref[0] = _head_sums(rr * kmod * rpar_ref[5:6, :], bd_ref[...]) * rv


def _prep(pc, pn, cw, rpar, wbf, wbb, ab, gb, bd):
    b, t, _ = pc.shape
    nt = t // ROW_TILE
    per = ROW_TILE // HALO
    nh = t // HALO
    w = W_M
    full = lambda shp: pl.BlockSpec(shp, lambda i, j: tuple(0 for _ in shp))
    in_specs = [
        pl.BlockSpec((1, HALO, CONV_CH), lambda i, j: (i, jnp.maximum(j * per - 1, 0), 0)),
        pl.BlockSpec((1, ROW_TILE, CONV_CH), lambda i, j: (i, j, 0)),
        pl.BlockSpec((1, HALO, CONV_CH), lambda i, j: (i, jnp.minimum(j * per + per, nh - 1), 0)),
        pl.BlockSpec((1, ROW_TILE, NONCONV_W), lambda i, j: (i, j, 0)),
        full((16, CONV_CH)), full((8, w)), full((384, w)), full((384, w)), full((384, w)), full((384, w)),
        full((2 * HEAD_BLK, HEAD_BLK)),
    ]
    out = pl.BlockSpec((1, ROW_TILE, w), lambda i, j: (i, j, 0))
    return pl.pallas_call(
        _prep_kernel,
        grid=(b, nt),
        in_specs=in_specs,
        out_specs=[out] * 11,
        out_shape=[jax.ShapeDtypeStruct((b, t, w), jnp.float32)] * 11,
        scratch_shapes=[pltpu.VMEM((2 * HALO + ROW_TILE, w), jnp.float32)],
        compiler_params=pltpu.CompilerParams(
            dimension_semantics=("arbitrary", "arbitrary"), vmem_limit_bytes=VMEM_LIMIT),
        name="conv_prep",
    )(pc, pc, pc, pn, cw, rpar, wbf, wbb, ab, gb, bd)


def _round_robin(gens):
    out = [None] * len(gens)
    live = list(range(len(gens)))
    while live:
        nxt = []
        for i in live:
            try:
                next(gens[i])
                nxt.append(i)
            except StopIteration as stop:
                out[i] = stop.value
        live = nxt
    return out


def _rwkv_pair_chunk(r, k, v, kh, ab, lw, s_mat, tri_incl, strict, incl, last_row, m0, m1):
    cw = _dot_exact_lhs(tri_incl, lw)
    yield
    e_pos = jnp.exp(cw)
    e_neg = jnp.exp(-cw)
    e_prev = jnp.exp(cw - lw)
    stack = lambda a: jnp.concatenate([a * m0, a * m1], axis=0)
    kt = stack(kh * e_prev)
    bt = stack(ab * e_neg)
    kk = stack(k * e_neg)
    rt = stack(r * e_pos)
    vs = stack(v)
    zero = jnp.zeros((), jnp.float32)
    kr = jnp.concatenate([kt, rt], axis=0)
    bk = jnp.concatenate([bt, kk], axis=0)
    amat = _dot1(kr, bk, _NT)
    yield
    a_bk = jnp.where(strict, amat[:PAIR, :PAIR], zero)
    a_kk = jnp.where(strict, amat[:PAIR, PAIR:], zero)
    a_rb = jnp.where(incl, amat[PAIR:, :PAIR], zero)
    a_rk = jnp.where(incl, amat[PAIR:, PAIR:], zero)
    n_mat = -a_bk
    eye = (lax.broadcasted_iota(jnp.int32, (PAIR, PAIR), 0)
           == lax.broadcasted_iota(jnp.int32, (PAIR, PAIR), 1)).astype(jnp.float32)
    q = eye + n_mat
    p = _dot1(n_mat, n_mat)
    ks = _dot1(kr, s_mat, _NT)
    av = _dot1(jnp.concatenate([a_kk, a_rk], axis=0), vs)
    yield
    for _ in range(4):
        qp = _dot1(jnp.concatenate([q, p], axis=0), p)
        yield
        q = q + qp[:PAIR]
        p = qp[PAIR:]
    t_inv = q + _dot1(q, p)
    yield
    u = -_dot1(t_inv, ks[:PAIR] + av[:PAIR])
    yield
    y = ks[PAIR:] + av[PAIR:] + _dot1(a_rb, u)
    yield
    w_last = jnp.sum(jnp.where(last_row, e_pos, zero), axis=0, keepdims=True)
    s_new = (s_mat + _dot1(jnp.concatenate([u, vs], axis=0), bk, _TN)) * w_last
    return y[:CHUNK] + y[CHUNK:], s_new


def _rwkv_kernel(rf, kf, vf, khf, abf, lwf, rb, kb, vb, khb, abb, lwb, yf_ref, yb_ref, s_ref, *, nctx):
    j = pl.program_id(1)

    @pl.when(j == 0)
    def _():
        s_ref[...] = jnp.zeros_like(s_ref)

    row = lax.broadcasted_iota(jnp.int32, (PAIR, PAIR), 0)
    col = lax.broadcasted_iota(jnp.int32, (PAIR, PAIR), 1)
    same = (row // CHUNK) == (col // CHUNK)
    r64 = lax.broadcasted_iota(jnp.int32, (CHUNK, CHUNK), 0)
    c64 = lax.broadcasted_iota(jnp.int32, (CHUNK, CHUNK), 1)
    rowl = lax.broadcasted_iota(jnp.int32, (CHUNK, PAIR), 0)
    lane = lax.broadcasted_iota(jnp.int32, (1, PAIR), 1)
    m0 = (lane < D_HR).astype(jnp.float32)
    m1 = 1.0 - m0
    dirs = (
        (rf, kf, vf, khf, abf, lwf, yf_ref, (c64 <= r64), same & (col < row), same & (col <= row), rowl == CHUNK - 1),
        (rb, kb, vb, khb, abb, lwb, yb_ref, (c64 >= r64), same & (col > row), same & (col >= row), rowl == 0),
    )
    states = [[s_ref[d, p] for p in range(NPAIR)] for d in range(2)]
    gens, meta = [], []
    for d, (r_ref, k_ref, v_ref, kh_ref, ab_ref, lw_ref, y_ref, tri, strict, incl, last_row) in enumerate(dirs):
        tri = tri.astype(jnp.float32).astype(jnp.bfloat16)
        for p in range(NPAIR):
            sl = slice(p * PAIR, (p + 1) * PAIR)
            gens.append(_rwkv_pair_chunk(r_ref[0, :, sl], k_ref[0, :, sl], v_ref[0, :, sl], kh_ref[0, :, sl],
                                         ab_ref[0, :, sl], lw_ref[0, :, sl], states[d][p], tri, strict, incl,
                                         last_row, m0, m1))
            meta.append((d, p, sl, y_ref))
    results = [m + r for m, r in zip(meta, _round_robin(gens))]
    for d, p, sl, y_ref, y, s_new in results:
        s_ref[d, p] = s_new
        y_ref[0, :, sl] = y


def _rwkv_scan(r, k, v, kh, ab, lw_f, lw_b, ctx_len):
    b, t, w = r.shape
    nc = t // CHUNK
    nctx = ctx_len // CHUNK
    nlat = nc - nctx

    def fwd_map(i, j):
        return (i, j, 0)

    def bwd_map(i, j):
        return (i, jnp.where(j < nctx, nctx - 1 - j, nc - 1 - (j - nctx)), 0)

    blk = (1, CHUNK, w)
    in_specs = [pl.BlockSpec(blk, fwd_map)] * 6 + [pl.BlockSpec(blk, bwd_map)] * 6
    out_specs = [
        pl.BlockSpec(blk, lambda i, j: (i, jnp.maximum(j - nctx, 0), 0)),
        pl.BlockSpec(blk, lambda i, j: (i, nlat - 1 - jnp.maximum(j - nctx, 0), 0)),
    ]
    return pl.pallas_call(
        functools.partial(_rwkv_kernel, nctx=nctx),
        grid=(b, nc),
        in_specs=in_specs,
        out_specs=out_specs,
        out_shape=[jax.ShapeDtypeStruct((b, t - ctx_len, w), jnp.float32)] * 2,
        scratch_shapes=[pltpu.VMEM((2, NPAIR, PAIR, PAIR), jnp.float32)],
        compiler_params=pltpu.CompilerParams(
            dimension_semantics=("arbitrary", "arbitrary"), vmem_limit_bytes=VMEM_LIMIT),
        name="rwkv_scan",
    )(r, k, v, kh, ab, lw_f, r, k, v, kh, ab, lw_b)


def _scan_max(x, reverse):
    rows = lax.broadcasted_iota(jnp.int32, x.shape, 0)
    neg_inf = jnp.full((), -jnp.inf, jnp.float32)
    step = 1
    while step < CHUNK:
        if reverse:
            shifted = jnp.where(rows < CHUNK - step, pltpu.roll(x, CHUNK - step, 0), neg_inf)
        else:
            shifted = jnp.where(rows >= step, pltpu.roll(x, step, 0), neg_inf)
        x = jnp.maximum(x, shifted)
        step *= 2
    return x


def _mlstm_gates(gates, bias, tri, m_prev, is_f, reverse):
    gl = gates + bias
    act = jnp.where(is_f, jax.nn.log_sigmoid(gl), gl)
    b_all = _dot_exact_lhs(tri, act)
    li = pltpu.roll(act, H_M, 1)
    x = li - b_all
    m_inter = b_all + m_prev
    m_t = jnp.maximum(m_inter, b_all + _scan_max(x, reverse))
    b_last = b_all[0:1, :] if reverse else b_all[CHUNK - 1:CHUNK, :]
    g = b_last + x
    m_new = jnp.maximum(b_last + m_prev, jnp.max(g, axis=0, keepdims=True))
    wts = jnp.exp(g - m_new)
    decay = jnp.exp(b_last + m_prev - m_new)
    return b_all - m_t, jnp.exp(m_inter - m_t), jnp.exp(-m_t), x, wts, decay, m_new


def _mlstm_pair_chunk(q, k, v, pieces, cf, decay_g, c_mat, n_row, incl, m0, m1):
    stack = lambda a: jnp.concatenate([a * m0, a * m1], axis=0)
    q_st, k_st, v_st = stack(q), stack(k), stack(v)
    lane = lax.broadcasted_iota(jnp.int32, (1, PAIR), 1)
    zero_bf = jnp.zeros((), jnp.bfloat16)
    sel = lambda a: jnp.concatenate([jnp.where(lane == cf, a, zero_bf), jnp.where(lane == cf + 1, a, zero_bf)], 0)
    wide = lambda parts: jnp.concatenate([sel(a) for a in parts], axis=1)
    ones = lambda rows: jnp.ones((rows, PAIR), jnp.bfloat16)
    bc = _dg(jnp.concatenate([wide(p[:2]) for p in pieces[:4]], axis=0), ones(2 * PAIR), _NN)
    bm_bc = bc[:PAIR] + _dg(sel(pieces[0][2]), ones(PAIR), _NN)
    inter_bc, enm_bc, wts_bc = (bc[i * PAIR:(i + 1) * PAIR] for i in range(1, 4))
    r_mat = _dg(jnp.ones((PAIR, 3 * PAIR), jnp.bfloat16), wide(pieces[4]), _NT)
    qk = _dot1(q_st, k_st, _NT)
    qc = _dot1(q_st, c_mat)
    yield
    neg_inf = jnp.full((), -jnp.inf, jnp.float32)
    s = qk * jnp.exp(jnp.where(incl, bm_bc + r_mat, neg_inf))
    wk = wts_bc * k_st
    sv = _dot1(s, v_st)
    kv = _dot1(wk, v_st, _TN)
    sums = _dot1(jnp.concatenate([q_st * n_row, s], axis=0), jnp.ones((PAIR, PAIR), jnp.float32))
    yield
    den = inter_bc * sums[:PAIR] + sums[PAIR:]
    h_st = (inter_bc * qc + sv) / jnp.maximum(jnp.abs(den), enm_bc)
    h = h_st[:CHUNK] + h_st[CHUNK:]
    decay_row = jnp.where(lane < D_HM, decay_g[:, cf:cf + 1], decay_g[:, cf + 1:cf + 2])
    c_new = decay_row * c_mat + kv
    n_new = decay_row * n_row + jnp.sum(wk, axis=0, keepdims=True)
    return h, c_new, n_new


def _mlstm_kernel(qf, kf, vf, gf, qb, kb, vb, gb, bias_ref, hf_ref, hb_ref, c_ref, n_ref, m_ref, *, nctx):
    del nctx
    j = pl.program_id(1)

    @pl.when(j == 0)
    def _():
        c_ref[...] = jnp.zeros_like(c_ref)
        n_ref[...] = jnp.zeros_like(n_ref)
        m_ref[...] = jnp.zeros_like(m_ref)

    row = lax.broadcasted_iota(jnp.int32, (PAIR, PAIR), 0)
    col = lax.broadcasted_iota(jnp.int32, (PAIR, PAIR), 1)
    same = (row // CHUNK) == (col // CHUNK)
    r64 = lax.broadcasted_iota(jnp.int32, (CHUNK, CHUNK), 0)
    c64 = lax.broadcasted_iota(jnp.int32, (CHUNK, CHUNK), 1)
    lane = lax.broadcasted_iota(jnp.int32, (1, PAIR), 1)
    m0 = (lane < D_HM).astype(jnp.float32)
    m1 = 1.0 - m0
    is_f = (lane % (2 * H_M)) >= H_M
    dirs = (
        (qf, kf, vf, gf, hf_ref, (c64 <= r64), same & (col <= row)),
        (qb, kb, vb, gb, hb_ref, (c64 >= r64), same & (col >= row)),
    )
    states = [[(c_ref[d, p], n_ref[d, p]) for p in range(NPAIR)] for d in range(2)]
    gens, meta, m_news = [], [], []
    for d, (q_ref, k_ref, v_ref, g_ref, h_ref, tri, incl) in enumerate(dirs):
        tri = tri.astype(jnp.float32).astype(jnp.bfloat16)
        bm, inter, enm, x, wts, decay_g, m_new = _mlstm_gates(g_ref[0], bias_ref[...], tri, m_ref[d], is_f, d == 1)
        m_news.append(m_new)
        pieces = [_split3(a) for a in (bm, inter, enm, wts, x)]
        for p in range(NPAIR):
            sl = slice(p * PAIR, (p + 1) * PAIR)
            cf = d * 2 * H_M + H_M + 2 * p
            gens.append(_mlstm_pair_chunk(q_ref[0, :, sl], k_ref[0, :, sl], v_ref[0, :, sl], pieces, cf, decay_g,
                                           *states[d][p], incl, m0, m1))
            meta.append((d, p, sl, h_ref))
    for (d, p, sl, h_ref), (h, c_new, n_new) in zip(meta, _round_robin(gens)):
        c_ref[d, p] = c_new
        n_ref[d, p] = n_new
        h_ref[0, :, sl] = h
    for d in range(2):
        m_ref[d] = m_news[d]


def _mlstm_scan(q, k, pn, bias_row, ctx_len):
    v = gates = pn
    b, t, w = q.shape
    nc = t // CHUNK
    nctx = ctx_len // CHUNK
    nlat = nc - nctx

    def fwd_map(i, j):
        return (i, j, 0)

    def bwd_map(i, j):
        return (i, jnp.where(j < nctx, nctx - 1 - j, nc - 1 - (j - nctx)), 0)

    blk = (1, CHUNK, w)
    gblk = (1, CHUNK, 128)
    gate_blk = (2 * W_M) // 128
    gate_of = lambda m: (lambda i, j: m(i, j)[:2] + (gate_blk,))
    in_specs = ([pl.BlockSpec(blk, fwd_map)] * 3 + [pl.BlockSpec(gblk, gate_of(fwd_map))]
                + [pl.BlockSpec(blk, bwd_map)] * 3 + [pl.BlockSpec(gblk, gate_of(bwd_map))]
                + [pl.BlockSpec((1, 128), lambda i, j: (0, 0))])
    out_specs = [
        pl.BlockSpec(blk, lambda i, j: (i, jnp.maximum(j - nctx, 0), 0)),
        pl.BlockSpec(blk, lambda i, j: (i, nlat - 1 - jnp.maximum(j - nctx, 0), 0)),
    ]
    return pl.pallas_call(
        functools.partial(_mlstm_kernel, nctx=nctx),
        grid=(b, nc),
        in_specs=in_specs,
        out_specs=out_specs,
        out_shape=[jax.ShapeDtypeStruct((b, t - ctx_len, w), jnp.float32)] * 2,
        scratch_shapes=[pltpu.VMEM((2, NPAIR, PAIR, PAIR), jnp.float32),
                        pltpu.VMEM((2, NPAIR, 1, PAIR), jnp.float32),
                        pltpu.VMEM((2, 1, 128), jnp.float32)],
        compiler_params=pltpu.CompilerParams(
            dimension_semantics=("arbitrary", "arbitrary"), vmem_limit_bytes=VMEM_LIMIT),
        name="mlstm_scan",
    )(q, k, v, gates, q, k, v, gates, bias_row)


ROUTE_W = 128
N_ROUTE = N_GROUPS + N_EXPERTS


def _ln_rows(z, eps):
    mu = jnp.mean(z, -1, keepdims=True)
    zc = z - mu
    var = jnp.mean(zc * zc, -1, keepdims=True)
    return zc * lax.rsqrt(var + eps)


def _headnorm_mxu(z, bd2, eps):
    mu = _head_sums(z, bd2) * (1.0 / D_HM)
    zc = z - mu
    var = _head_sums(zc * zc, bd2) * (1.0 / D_HM)
    return zc * lax.rsqrt(var + eps)


def _outproj_kernel(hf, hb, mo, yf, yb, gg, bonus, x_ref, mod_ref, wout_ref, hpar_ref, ln_ref, rtw_ref, rtb_ref,
                    bd_ref, x1_ref, tok_ref, route_ref, ridx_ref, cnt_ref, carry_ref):
    i, j = pl.program_id(0), pl.program_id(1)

    @pl.when((i == 0) & (j == 0))
    def _():
        carry_ref[...] = jnp.zeros_like(carry_ref)

    avg = bd_ref[...]
    out_m = jax.nn.sigmoid(mo[0]) * (_headnorm_mxu(hf[0] + hb[0], avg, LN_EPS) * hpar_ref[0:1, :])
    y = _headnorm_mxu(yf[0] + yb[0], avg, GN_EPS) * hpar_ref[1:2, :] + hpar_ref[2:3, :]
    out_r = (y + bonus[0]) * gg[0]
    proj = (jnp.dot(out_m.astype(jnp.bfloat16), wout_ref[:W_M, :], preferred_element_type=jnp.float32)
            + jnp.dot(out_r.astype(jnp.bfloat16), wout_ref[W_M:, :], preferred_element_type=jnp.float32))
    g1, sh2, sc2 = mod_ref[0, 2:3, :], mod_ref[0, 3:4, :], mod_ref[0, 4:5, :]
    x1 = _ln_rows(ALPHA * x_ref[0] + g1 * proj, LN_EPS) * ln_ref[0:1, :] + ln_ref[1:2, :]
    x1_ref[0] = x1
    tok = _ln_rows(x1, LN_EPS) * (1.0 + sc2) + sh2
    tok_ref[0] = tok

    logits = _dot3_pre(tok, rtw_ref[...]) + rtb_ref[...]
    lane = lax.broadcasted_iota(jnp.int32, (ROW_TILE, ROUTE_W), 1)
    neg_inf = jnp.full((), -jnp.inf, jnp.float32)
    big = jnp.int32(ROUTE_W)
    is_g = lane < N_GROUPS
    lg = jnp.where(is_g, logits, neg_inf)
    gmax = jnp.max(lg, axis=1, keepdims=True)
    grp = jnp.min(jnp.where(lg == gmax, lane, big), axis=1, keepdims=True)
    p_top = 1.0 / jnp.sum(jnp.where(is_g, jnp.exp(lg - gmax), 0.0), axis=1, keepdims=True)
    in_grp = (lane >= N_GROUPS) & (lane < N_ROUTE) & ((lane - N_GROUPS) // E_PER_GROUP == grp)
    le = jnp.where(in_grp, logits, neg_inf)
    v1 = jnp.max(le, axis=1, keepdims=True)
    i1 = jnp.min(jnp.where(le == v1, lane, big), axis=1, keepdims=True)
    le2 = jnp.where(lane == i1, neg_inf, le)
    v2 = jnp.max(le2, axis=1, keepdims=True)
    i2 = jnp.min(jnp.where(le2 == v2, lane, big), axis=1, keepdims=True)
    e21 = jnp.exp(v2 - v1)
    w1 = (1.0 / (1.0 + e21)) * p_top
    w2 = (e21 / (1.0 + e21)) * p_top

    sel1, sel2 = lane == i1, lane == i2
    onehot = jnp.where(sel1 | sel2, 1.0, 0.0)
    tr = lax.broadcasted_iota(jnp.int32, (ROW_TILE, ROW_TILE), 0)
    tc = lax.broadcasted_iota(jnp.int32, (ROW_TILE, ROW_TILE), 1)
    strict = jnp.where(tc < tr, 1.0, 0.0).astype(jnp.bfloat16)
    before = _dg(strict, onehot.astype(jnp.bfloat16), _NN) + carry_ref[...]
    r1 = jnp.sum(jnp.where(sel1, before, 0.0), axis=1, keepdims=True)
    r2 = jnp.sum(jnp.where(sel2, before, 0.0), axis=1, keepdims=True)
    carry_ref[...] = carry_ref[...] + jnp.sum(onehot, axis=0, keepdims=True)
    e1 = (i1 - N_GROUPS).astype(jnp.float32)
    e2 = (i2 - N_GROUPS).astype(jnp.float32)
    route = jnp.where(lane == 0, e1, jnp.where(lane == 1, e2, jnp.where(lane == 2, r1, jnp.where(
        lane == 3, r2, jnp.where(lane == 4, w1, jnp.where(lane == 5, w2, 0.0))))))
    route_ref[0] = route
    ridx_ref[0] = route.T[0:8, :].astype(jnp.int32)
    cnt_ref[...] = jnp.broadcast_to(carry_ref[...], cnt_ref.shape)


def _outproj(h_f, h_b, pn, y_f, y_b, gg, bonus, x, mod3, wout_bf16, hpar, lnpar, rtw, rtb, bd, ctx_len):
    b, seq, d = x.shape
    nt = seq // ROW_TILE
    off = ctx_len // ROW_TILE
    lat = lambda w: pl.BlockSpec((1, ROW_TILE, w), lambda i, j: (i, j, 0))
    full = lambda shp: pl.BlockSpec(shp, lambda i, j: tuple(0 for _ in shp))
    in_specs = [
        lat(W_M), lat(W_M),
        pl.BlockSpec((1, ROW_TILE, W_M), lambda i, j: (i, j + off, 1)),
        lat(W_R), lat(W_R),
        pl.BlockSpec((1, ROW_TILE, W_R), lambda i, j: (i, j + off, 0)),
        pl.BlockSpec((1, ROW_TILE, W_R), lambda i, j: (i, j + off, 0)),
        lat(d),
        pl.BlockSpec((1, 6, d), lambda i, j: (i, 0, 0)),
        full((MIX_W, d)), full((8, W_M)), full((8, d)), full((3 * d, ROUTE_W)), full((1, ROUTE_W)),
        full((2 * HEAD_BLK, HEAD_BLK)),
    ]
    out_specs = [
        lat(d), lat(d), lat(ROUTE_W),
        pl.BlockSpec((1, 8, ROW_TILE), lambda i, j: (i * nt + j, 0, 0)),
        full((8, ROUTE_W)),
    ]
    out_shape = [
        jax.ShapeDtypeStruct((b, seq, d), jnp.float32),
        jax.ShapeDtypeStruct((b, seq, d), jnp.float32),
        jax.ShapeDtypeStruct((b, seq, ROUTE_W), jnp.float32),
        jax.ShapeDtypeStruct((b * nt, 8, ROW_TILE), jnp.int32),
        jax.ShapeDtypeStruct((8, ROUTE_W), jnp.float32),
    ]
    return pl.pallas_call(
        _outproj_kernel,
        grid=(b, nt),
        in_specs=in_specs,
        out_specs=out_specs,
        out_shape=out_shape,
        scratch_shapes=[pltpu.VMEM((1, ROUTE_W), jnp.float32)],
        compiler_params=pltpu.CompilerParams(
            dimension_semantics=("arbitrary", "arbitrary"), vmem_limit_bytes=VMEM_LIMIT),
        name="outproj_router",
    )(h_f, h_b, pn, y_f, y_b, gg, bonus, x, mod3, wout_bf16, hpar, lnpar, rtw, rtb, bd)


DEST_TILES = 8


def _dest_kernel(starts_ref, ridx_ref, dest_ref):
    blk = ridx_ref[...]
    dest = pltpu.roll(blk, 8 - TOP_K_IN_GROUP, 1)
    for e in range(N_EXPERTS):
        dest = dest + jnp.where(blk == e, starts_ref[e], 0)
    dest_ref[...] = dest


def _moe_dest(starts, ridx):
    nt = ridx.shape[0]
    per_step = math.gcd(nt, DEST_TILES)
    blk = pl.BlockSpec((per_step, 8, ROW_TILE), lambda i, s: (i, 0, 0))
    return pl.pallas_call(
        _dest_kernel,
        grid_spec=pltpu.PrefetchScalarGridSpec(num_scalar_prefetch=1, grid=(nt // per_step,),
                                               in_specs=[blk], out_specs=blk),
        out_shape=jax.ShapeDtypeStruct(ridx.shape, jnp.int32),
        compiler_params=pltpu.CompilerParams(dimension_semantics=("arbitrary",), vmem_limit_bytes=VMEM_LIMIT),
        name="moe_dest",
    )(starts, ridx)


def _scatter_kernel(dest_ref, tok_ref, xs_in_ref, xs_ref, sem):
    del xs_in_ref

    def row_copy(r, slot):
        dst = dest_ref[0, slot, r]
        return pltpu.make_async_copy(tok_ref.at[pl.ds(r, 1), :], xs_ref.at[pl.ds(dst, 1), :], sem)

    def issue(r, c):
        row_copy(r, 0).start()
        row_copy(r, 1).start()
        return c

    lax.fori_loop(0, ROW_TILE, issue, 0, unroll=DMA_UNROLL)
    for half in range(TOP_K_IN_GROUP):
        pltpu.make_async_copy(tok_ref, xs_ref.at[pl.ds(half * ROW_TILE, ROW_TILE), :], sem).wait()


def _moe_scatter(dest, tok2d, n_rows):
    n, d = tok2d.shape
    xs_zero = jnp.zeros((n_rows, d), tok2d.dtype)
    nt = n // ROW_TILE
    return pl.pallas_call(
        _scatter_kernel,
        grid=(nt,),
        in_specs=[
            pl.BlockSpec((1, 8, ROW_TILE), lambda i: (i, 0, 0), memory_space=pltpu.SMEM),
            pl.BlockSpec((ROW_TILE, d), lambda i: (i, 0)),
            pl.BlockSpec(memory_space=pl.ANY),
        ],
        out_specs=pl.BlockSpec(memory_space=pl.ANY),
        scratch_shapes=[pltpu.SemaphoreType.DMA(())],
        out_shape=jax.ShapeDtypeStruct((n_rows, d), tok2d.dtype),
        input_output_aliases={2: 0},
        compiler_params=pltpu.CompilerParams(dimension_semantics=("arbitrary",), vmem_limit_bytes=VMEM_LIMIT),
        name="moe_scatter",
    )(dest, tok2d, xs_zero)


def _expert_kernel(blk_exp_ref, valid_ref, xs_ref, wg_ref, wu_ref, wd_ref, ys_ref):
    i = pl.program_id(0)
    valid = valid_ref[i]

    @pl.when(valid > 0)
    def _():
        xb = xs_ref[...].astype(jnp.bfloat16)
        hg = jnp.dot(xb, wg_ref[0], preferred_element_type=jnp.float32)
        hu = jnp.dot(xb, wu_ref[0], preferred_element_type=jnp.float32)
        hb = (jax.nn.silu(hg) * hu).astype(jnp.bfloat16)
        ys_ref[...] = jnp.dot(hb, wd_ref[0], preferred_element_type=jnp.float32)

    @pl.when(valid == 0)
    def _():
        ys_ref[...] = jnp.zeros_like(ys_ref)


def _moe_experts(blk_exp, blk_valid, xs, wg, wu, wd):
    nrow, d = xs.shape
    de = wg.shape[2]
    return pl.pallas_call(
        _expert_kernel,
        grid_spec=pltpu.PrefetchScalarGridSpec(
            num_scalar_prefetch=2,
            grid=(nrow // MOE_BLOCK,),
            in_specs=[
                pl.BlockSpec((MOE_BLOCK, d), lambda i, be, nu: (i, 0)),
                pl.BlockSpec((1, d, de), lambda i, be, nu: (be[i], 0, 0)),
                pl.BlockSpec((1, d, de), lambda i, be, nu: (be[i], 0, 0)),
                pl.BlockSpec((1, de, d), lambda i, be, nu: (be[i], 0, 0)),
            ],
            out_specs=pl.BlockSpec((MOE_BLOCK, d), lambda i, be, nu: (i, 0)),
        ),
        out_shape=jax.ShapeDtypeStruct((nrow, d), jnp.float32),
        compiler_params=pltpu.CompilerParams(dimension_semantics=("arbitrary",), vmem_limit_bytes=VMEM_LIMIT),
        name="moe_experts",
    )(blk_exp, blk_valid, xs, wg, wu, wd)


def _combine_kernel(dest_ref, dest_next_ref, route_ref, x1_ref, mod_ref, ln_ref, ys_ref, out_ref,
                    ybuf, sems):
    i = pl.program_id(0)
    n = pl.num_programs(0)
    cur = i % 2

    def gather(idx_ref, buf):
        def issue(r, c):
            for slot in range(TOP_K_IN_GROUP):
                src = idx_ref[0, slot, r]
                pltpu.make_async_copy(ys_ref.at[pl.ds(src, 1), :],
                                      ybuf.at[buf, pl.ds(slot * ROW_TILE + r, 1), :], sems.at[buf]).start()
            return c

        lax.fori_loop(0, ROW_TILE, issue, 0, unroll=DMA_UNROLL)

    @pl.when(i == 0)
    def _():
        gather(dest_ref, 0)

    @pl.when(i + 1 < n)
    def _():
        gather(dest_next_ref, 1 - cur)

    pltpu.make_async_copy(ys_ref.at[pl.ds(0, TOP_K_IN_GROUP * ROW_TILE), :], ybuf.at[cur], sems.at[cur]).wait()
    route = route_ref[0]
    ffn = ybuf[cur, 0:ROW_TILE, :] * route[:, 4:5] + ybuf[cur, ROW_TILE:2 * ROW_TILE, :] * route[:, 5:6]
    g2 = mod_ref[0, 5:6, :]
    out_ref[0] = _ln_rows(ALPHA * x1_ref[0] + g2 * ffn, LN_EPS) * ln_ref[2:3, :] + ln_ref[3:4, :]


def _moe_combine(dest, route, x1, mod3, lnpar, ys):
    b, seq, d = x1.shape
    nt = seq // ROW_TILE
    n = b * nt
    tile = lambda w: pl.BlockSpec((1, ROW_TILE, w), lambda i: (i // nt, i % nt, 0))
    return pl.pallas_call(
        _combine_kernel,
        grid=(n,),
        in_specs=[
            pl.BlockSpec((1, 8, ROW_TILE), lambda i: (i, 0, 0), memory_space=pltpu.SMEM),
            pl.BlockSpec((1, 8, ROW_TILE), lambda i: (jnp.minimum(i + 1, n - 1), 0, 0), memory_space=pltpu.SMEM),
            tile(ROUTE_W),
            tile(d),
            pl.BlockSpec((1, 6, d), lambda i: (i // nt, 0, 0)),
            pl.BlockSpec((8, d), lambda i: (0, 0)),
            pl.BlockSpec(memory_space=pl.ANY),
        ],
        out_specs=tile(d),
        scratch_shapes=[pltpu.VMEM((2, TOP_K_IN_GROUP * ROW_TILE, d), jnp.float32), pltpu.SemaphoreType.DMA((2,))],
        out_shape=jax.ShapeDtypeStruct((b, seq, d), jnp.float32),
        compiler_params=pltpu.CompilerParams(dimension_semantics=("arbitrary",), vmem_limit_bytes=VMEM_LIMIT),
        name="moe_combine",
    )(dest, dest, route, x1, mod3, lnpar, ys)


def _moe_plan(cnt, n_pairs):
    counts = cnt[0, N_GROUPS:N_ROUTE].astype(jnp.int32)
    padded = (counts + MOE_BLOCK - 1) // MOE_BLOCK * MOE_BLOCK
    pad_ends = jnp.cumsum(padded)
    pad_starts = pad_ends - padded
    n_blk = -(-n_pairs // MOE_BLOCK) + N_EXPERTS
    blk_row0 = jnp.arange(n_blk, dtype=jnp.int32) * MOE_BLOCK
    blk_exp = jnp.minimum(jnp.sum((pad_ends[None, :] <= blk_row0[:, None]).astype(jnp.int32), axis=1), N_EXPERTS - 1)
    blk_valid = jnp.clip(counts[blk_exp] - (blk_row0 - pad_starts[blk_exp]), 0, MOE_BLOCK)
    blk_valid = jnp.where(blk_row0 < pad_ends[-1], blk_valid, 0).astype(jnp.int32)
    return pad_starts.astype(jnp.int32), blk_exp, blk_valid, n_blk


def kernel(x, c, ctx, c_ctx, w_ada, b_ada, w_in, conv_w, m_bias_i, m_bias_f, m_norm_w, r_w0, r_wB, r_a0, r_aB,
           r_gB, r_kk, r_ka, r_bonus, r_norm_w, r_norm_b, w_out, ln1_g, ln1_b, ln2_g, ln2_b, rt_g, rt_g_b, rt_e,
           rt_e_b, ex_gate, ex_up, ex_down):
    assert w_ada.shape[0] == DEPTH
    bsz, seq, d = x.shape
    ctx_len = ctx.shape[1]
    assert ctx_len == ROW_TILE and seq % ROW_TILE == 0
    assert seq % GRID_W == 0 and ROW_TILE % GRID_W == 0
    l = 0
    zrow = lambda n, w: jnp.zeros((n, w), jnp.float32)
    mrows = -(-(bsz + 1) // 8) * 8
    c_pad = jnp.concatenate([c, c_ctx[None, :], zrow(mrows - bsz - 1, d)], 0)
    mod_all = _ada(c_pad, w_ada[l], b_ada[l][None, :])[:bsz + 1]
    mod3 = mod_all.reshape(bsz + 1, 6, d)
    pc, pn = _inproj(ctx, x, mod3, _pad_w_in(w_in[l]))
    cw = jnp.concatenate([conv_w[l].reshape(CONV_K * CONV_K, CONV_CH), zrow(16 - CONV_K * CONV_K, CONV_CH)], 0)
    rpar = jnp.concatenate([r_w0[l], r_a0[l][None, :], r_kk[l][None, :], r_ka[l][None, :],
                            r_bonus[l].reshape(1, W_R), zrow(2, W_R)], 0)
    wbf = jnp.concatenate([r_wB[l][0], zrow(W_LORA, W_R)], 0)
    wbb = jnp.concatenate([zrow(W_LORA, W_R), r_wB[l][1]], 0)
    ab_mat = jnp.concatenate([r_aB[l], zrow(128 - A_LORA, W_R)], 0)
    bd = _head_sum_operator(D_HR)
    q, k, rr, kmod, rv, kh, kha, logw_f, logw_b, gg, bonus = _prep(
        pc, pn, cw, rpar, _stack3(wbf), _stack3(wbb), _stack3(ab_mat), _stack3(r_gB[l]), bd)
    bias_row = jnp.concatenate([m_bias_i[l][0], m_bias_f[l][0], m_bias_i[l][1], m_bias_f[l][1],
                                jnp.zeros((128 - 4 * H_M,), jnp.float32)])[None, :]
    h_f, h_b = _mlstm_scan(q, k, pn, bias_row, ctx_len)
    y_f, y_b = _rwkv_scan(rr, kmod, rv, kh, kha, logw_f, logw_b, ctx_len)
    hpar = jnp.zeros((8, W_M), jnp.float32).at[0].set(m_norm_w[l]).at[1].set(r_norm_w[l]).at[2].set(r_norm_b[l])
    lnpar = jnp.zeros((8, d), jnp.float32).at[0].set(ln1_g[l]).at[1].set(ln1_b[l]).at[2].set(ln2_g[l]).at[3].set(
        ln2_b[l])
    rtw = jnp.concatenate([rt_g[l], rt_e[l], jnp.zeros((d, ROUTE_W - N_ROUTE), jnp.float32)], axis=1)
    rtb = jnp.concatenate([rt_g_b[l], rt_e_b[l], jnp.zeros((ROUTE_W - N_ROUTE,), jnp.float32)])[None, :]
    x1, tok, route, ridx, cnt = _outproj(h_f, h_b, pn, y_f, y_b, gg, bonus, x, mod3, w_out[l].astype(jnp.bfloat16),
                                         hpar, lnpar, _stack3(rtw), rtb, bd, ctx_len)
    n_tok = bsz * seq
    starts, blk_exp, blk_valid, n_blk = _moe_plan(cnt, n_tok * TOP_K_IN_GROUP)
    dest = _moe_dest(starts, ridx)
    xs = _moe_scatter(dest, tok.reshape(n_tok, d), n_blk * MOE_BLOCK)
    ys = _moe_experts(blk_exp, blk_valid, xs, ex_gate[l].astype(jnp.bfloat16), ex_up[l].astype(jnp.bfloat16),
                      ex_down[l].astype(jnp.bfloat16))
    return _moe_combine(dest, route, x1, mod3, lnpar, ys)
```

```python
import functools
import math

import jax
import jax.numpy as jnp
from jax import lax
import numpy as np
from jax.experimental import pallas as pl
from jax.experimental.pallas import tpu as pltpu

H_M = 8
D_HM = 64
W_M = H_M * D_HM
H_R = 8
D_HR = 64
W_R = H_R * D_HR
MIX_W = W_M + W_R
W_LORA = 64
A_LORA = 64
G_LORA = 128
CONV_K = 3
CONV_CH = 2 * W_M + 3 * W_R
SECTION_WIDTHS = (W_M, W_M, W_R, W_R, W_R, W_M, W_M, 4 * H_M, W_LORA, W_LORA, A_LORA, G_LORA)
IN_COLS = sum(SECTION_WIDTHS)
SPLIT_POINTS = tuple(int(v) for v in np.cumsum(SECTION_WIDTHS)[:-1])
CHUNK = 64
N_GROUPS = 4
E_PER_GROUP = 8
N_EXPERTS = N_GROUPS * E_PER_GROUP
TOP_K_IN_GROUP = 2
D_EXPERT = 512
MOE_BLOCK = 256
DEPTH = 1
ALPHA = (2.0 * DEPTH) ** 0.25
DECAY_SCALE = math.exp(-0.5)
LN_EPS = 1e-6
GN_EPS = 64e-5

ROW_TILE = 256
NONCONV_W = 1536
LORA_TAIL_W = 384
VMEM_LIMIT = 56 * 1024 * 1024
DMA_UNROLL = 8


def _inproj_kernel(ctx_ref, x_ref, mod_ref, w_ref, oc_ref, on_ref):
    j = pl.program_id(1)

    def body(src_ref):
        z = src_ref[0]
        mu = jnp.mean(z, -1, keepdims=True)
        zc = z - mu
        var = jnp.mean(zc * zc, -1, keepdims=True)
        y = zc * lax.rsqrt(var + LN_EPS)
        h = (y * (1.0 + mod_ref[0, 1:2, :]) + mod_ref[0, 0:1, :]).astype(jnp.bfloat16)
        oc_ref[0] = jnp.dot(h, w_ref[:, :CONV_CH], preferred_element_type=jnp.float32)
        on_ref[0] = jnp.dot(h, w_ref[:, CONV_CH:], preferred_element_type=jnp.float32)

    @pl.when(j == 0)
    def _():
        body(ctx_ref)

    @pl.when(j > 0)
    def _():
        body(x_ref)


def _inproj(ctx, x, mod_all, w_pad):
    b, seq, d = x.shape
    nt = (ctx.shape[1] + seq) // ROW_TILE
    t_all = ctx.shape[1] + seq
    return pl.pallas_call(
        _inproj_kernel,
        grid=(b, nt),
        in_specs=[
            pl.BlockSpec((1, ROW_TILE, d), lambda i, j: (i, 0, 0)),
            pl.BlockSpec((1, ROW_TILE, d), lambda i, j: (i, jnp.maximum(j - 1, 0), 0)),
            pl.BlockSpec((1, 6, d), lambda i, j: (jnp.where(j == 0, b, i), 0, 0)),
            pl.BlockSpec((d, CONV_CH + NONCONV_W), lambda i, j: (0, 0)),
        ],
        out_specs=[
            pl.BlockSpec((1, ROW_TILE, CONV_CH), lambda i, j: (i, j, 0)),
            pl.BlockSpec((1, ROW_TILE, NONCONV_W), lambda i, j: (i, j, 0)),
        ],
        out_shape=[
            jax.ShapeDtypeStruct((b, t_all, CONV_CH), jnp.float32),
            jax.ShapeDtypeStruct((b, t_all, NONCONV_W), jnp.float32),
        ],
        compiler_params=pltpu.CompilerParams(
            dimension_semantics=("arbitrary", "arbitrary"), vmem_limit_bytes=VMEM_LIMIT),
        name="inproj",
    )(ctx, x, mod_all, w_pad)


def _pad_w_in(w_in):
    d = w_in.shape[0]
    sp = SPLIT_POINTS
    z = lambda n: jnp.zeros((d, n), w_in.dtype)
    gates = w_in[:, sp[6]:sp[7]]
    lw = w_in[:, sp[7]:sp[9]]
    la = w_in[:, sp[9]:sp[10]]
    lg = w_in[:, sp[10]:]
    return jnp.concatenate([w_in[:, :sp[6]], gates, z(96), lw, la, z(64), lg], axis=1).astype(jnp.bfloat16)


PAIR = 2 * D_HR
NPAIR = H_R // 2
_NN = (((1,), (0,)), ((), ()))
_NT = (((1,), (1,)), ((), ()))
_TN = (((0,), (0,)), ((), ()))


def _split2(a):
    hi = a.astype(jnp.bfloat16)
    lo = (a - hi.astype(jnp.float32)).astype(jnp.bfloat16)
    return hi, lo


def _split3(a):
    a1 = a.astype(jnp.bfloat16)
    r1 = a - a1.astype(jnp.float32)
    a2 = r1.astype(jnp.bfloat16)
    a3 = (r1 - a2.astype(jnp.float32)).astype(jnp.bfloat16)
    return a1, a2, a3


def _dg(a, b, dims):
    return lax.dot_general(a, b, dims, preferred_element_type=jnp.float32)


def _dot3(a, b, dims=_NN):
    ah, al = _split2(a)
    bh, bl = _split2(b)
    return _dg(ah, bh, dims) + _dg(ah, bl, dims) + _dg(al, bh, dims)


def _dot1(a, b, dims=_NN):
    return _dg(a.astype(jnp.bfloat16), b.astype(jnp.bfloat16), dims)


def _stack3(w):
    hi = w.astype(jnp.bfloat16)
    lo = (w - hi.astype(jnp.float32)).astype(jnp.bfloat16)
    return jnp.concatenate([hi, lo, hi], axis=0)


def _dot3_pre(a, w3):
    ah, al = _split2(a)
    return _dg(jnp.concatenate([ah, ah, al], axis=1), w3, _NN)


HEAD_BLK = 256


def _head_sum_operator(head):
    r = np.arange(HEAD_BLK) // head
    bd = (r[:, None] == r[None, :]).astype(np.float32)
    return jnp.asarray(np.concatenate([bd, bd], axis=0), jnp.bfloat16)


def _head_sums(z, bd2):
    hi, lo = _split2(z)
    out = []
    for c in range(0, z.shape[1], HEAD_BLK):
        lhs = jnp.concatenate([hi[:, c:c + HEAD_BLK], lo[:, c:c + HEAD_BLK]], axis=1)
        out.append(_dg(lhs, bd2, _NN))
    return jnp.concatenate(out, axis=1)


def _dot_exact_lhs(a_bf16, b, dims=_NN):
    b1, b2, b3 = _split3(b)
    return _dg(a_bf16, b1, dims) + _dg(a_bf16, b2, dims) + _dg(a_bf16, b3, dims)


def _ada_kernel(c_ref, w_ref, b_ref, o_ref):
    o_ref[...] = _dot3(jax.nn.silu(c_ref[...]), w_ref[...]) + b_ref[...]


def _ada(c_pad, w, bias):
    m, d = c_pad.shape
    n = w.shape[1]
    return pl.pallas_call(
        _ada_kernel,
        grid=(n // d,),
        in_specs=[pl.BlockSpec((m, d), lambda j: (0, 0)),
                  pl.BlockSpec((d, d), lambda j: (0, j)),
                  pl.BlockSpec((1, d), lambda j: (0, j))],
        out_specs=pl.BlockSpec((m, d), lambda j: (0, j)),
        out_shape=jax.ShapeDtypeStruct((m, n), jnp.float32),
        compiler_params=pltpu.CompilerParams(dimension_semantics=("arbitrary",), vmem_limit_bytes=VMEM_LIMIT),
        name="adaln",
    )(c_pad, w, bias)


GRID_W = 64
HALO = GRID_W


def _prep_kernel(top_ref, main_ref, bot_ref, pn_ref, cw_ref, rpar_ref, wbf_ref, wbb_ref, ab_ref, gb_ref, bd_ref,
                 q_ref, k_ref, r_ref, km_ref, v_ref, kh_ref, abo_ref, lwf_ref, lwb_ref, gg_ref, bon_ref, xbuf):
    j = pl.program_id(1)
    nt = pl.num_programs(1)
    is_ctx = j == 0
    top_ok = j >= 2
    bot_ok = (j >= 1) & (j < nt - 1)
    l_idx = lax.broadcasted_iota(jnp.int32, (ROW_TILE, 1), 0)
    col = jnp.where(is_ctx, l_idx, l_idx % GRID_W)
    left_ok = col != 0
    right_ok = col != jnp.where(is_ctx, ROW_TILE - 1, GRID_W - 1)
    vert = jnp.where(is_ctx, 0.0, 1.0)
    sec = {}
    for s in range(5):
        sl = slice(s * W_M, (s + 1) * W_M)
        xbuf[0:HALO, :] = jnp.where(top_ok, top_ref[0, :, sl], 0.0)
        xbuf[HALO:HALO + ROW_TILE, :] = main_ref[0, :, sl]
        xbuf[HALO + ROW_TILE:2 * HALO + ROW_TILE, :] = jnp.where(bot_ok, bot_ref[0, :, sl], 0.0)
        cols = [None, None, None]
        for dr in range(3):
            base = HALO + GRID_W * (dr - 1)
            w3 = cw_ref[3 * dr:3 * dr + 3, sl] if dr == 1 else cw_ref[3 * dr:3 * dr + 3, sl] * vert
            xrow = xbuf[base:base + ROW_TILE, :]
            for dc in range(3):
                term = xrow * w3[dc:dc + 1]
                cols[dc] = term if cols[dc] is None else cols[dc] + term
        left = jnp.where(left_ok, pltpu.roll(cols[0], 1, 0), 0.0)
        right = jnp.where(right_ok, pltpu.roll(cols[2], ROW_TILE - 1, 0), 0.0)
        sec[s] = left + cols[1] + right
    q_ref[0] = jax.nn.silu(sec[0]).astype(q_ref.dtype)
    k_ref[0] = (jax.nn.silu(sec[1]) * (D_HM ** -0.5)).astype(k_ref.dtype)
    rr, rk, rv = sec[2], sec[3], sec[4]
    r_ref[0] = rr
    v_ref[0] = rv.astype(v_ref.dtype)
    lw = jnp.tanh(pn_ref[0, :, 0:128])
    lwf_ref[0] = -DECAY_SCALE * jax.nn.sigmoid(rpar_ref[0:1, :] + _dot3_pre(lw, wbf_ref[...]))
    lwb_ref[0] = -DECAY_SCALE * jax.nn.sigmoid(rpar_ref[1:2, :] + _dot3_pre(lw, wbb_ref[...]))
    a = jax.nn.sigmoid(rpar_ref[2:3, :] + _dot3_pre(pn_ref[0, :, 128:256], ab_ref[...]))
    gg_ref[0] = _dot3_pre(jax.nn.sigmoid(pn_ref[0, :, 256:384]), gb_ref[...])
    kap = rk * rpar_ref[3:4, :]
    norm = jnp.sqrt(_head_sums(kap * kap, bd_ref[...]))
    kh = kap / jnp.maximum(norm, 1e-12)
    kmod = rk * (1.0 + (a - 1.0) * rpar_ref[4:5, :])
    kh_ref[0] = kh
    abo_ref[0] = kh * a
    km_ref[0] = kmod
    bon_ref[0] = _head_sums(rr * kmod * rpar_ref[5:6, :], bd_ref[...]) * rv


def _prep(pc, pn, cw, rpar, wbf, wbb, ab, gb, bd):
    b, t, _ = pc.shape
    nt = t // ROW_TILE
    per = ROW_TILE // HALO
    nh = t // HALO
    w = W_M
    full = lambda shp: pl.BlockSpec(shp, lambda i, j: tuple(0 for _ in shp))
    in_specs = [
        pl.BlockSpec((1, HALO, CONV_CH), lambda i, j: (i, jnp.maximum(j * per - 1, 0), 0)),
        pl.BlockSpec((1, ROW_TILE, CONV_CH), lambda i, j: (i, j, 0)),
        pl.BlockSpec((1, HALO, CONV_CH), lambda i, j: (i, jnp.minimum(j * per + per, nh - 1), 0)),
        pl.BlockSpec((1, ROW_TILE, LORA_TAIL_W), lambda i, j: (i, j, NONCONV_W // LORA_TAIL_W - 1)),
        full((16, CONV_CH)), full((8, w)), full((384, w)), full((384, w)), full((384, w)), full((384, w)),
        full((2 * HEAD_BLK, HEAD_BLK)),
    ]
    out = pl.BlockSpec((1, ROW_TILE, w), lambda i, j: (i, j, 0))
    return pl.pallas_call(
        _prep_kernel,
        grid=(b, nt),
        in_specs=in_specs,
        out_specs=[out] * 11,
        out_shape=[jax.ShapeDtypeStruct((b, t, w), jnp.bfloat16 if name in ("q", "k", "v") else jnp.float32)
                   for name in ("q", "k", "r", "kmod", "v", "kh", "ab", "lwf", "lwb", "gg", "bonus")],
        scratch_shapes=[pltpu.VMEM((2 * HALO + ROW_TILE, w), jnp.float32)],
        compiler_params=pltpu.CompilerParams(
            dimension_semantics=("arbitrary", "arbitrary"), vmem_limit_bytes=VMEM_LIMIT),
        name="conv_prep",
    )(pc, pc, pc, pn, cw, rpar, wbf, wbb, ab, gb, bd)


def _round_robin(gens):
    out = [None] * len(gens)
    live = list(range(len(gens)))
    while live:
        nxt = []
        for i in live:
            try:
                next(gens[i])
                nxt.append(i)
            except StopIteration as stop:
                out[i] = stop.value
        live = nxt
    return out


def _rwkv_pair_chunk(r, k, v, kh, ab, lw, s_mat, tri_incl, strict, incl, last_row, m0, m1):
    cw = _dot_exact_lhs(tri_incl, lw)
    yield
    e_pos = jnp.exp(cw)
    e_neg = jnp.exp(-cw)
    e_prev = jnp.exp(cw - lw)
    stack = lambda a: jnp.concatenate([a * m0, a * m1], axis=0)
    kt = stack(kh * e_prev)
    bt = stack(ab * e_neg)
    kk = stack(k * e_neg)
    rt = stack(r * e_pos)
    vs = stack(v)
    zero = jnp.zeros((), jnp.float32)
    kr = jnp.concatenate([kt, rt], axis=0)
    bk = jnp.concatenate([bt, kk], axis=0)
    amat = _dot1(kr, bk, _NT)
    yield
    a_bk = jnp.where(strict, amat[:PAIR, :PAIR], zero)
    a_kk = jnp.where(strict, amat[:PAIR, PAIR:], zero)
    a_rb = jnp.where(incl, amat[PAIR:, :PAIR], zero)
    a_rk = jnp.where(incl, amat[PAIR:, PAIR:], zero)
    n_mat = -a_bk
    eye = (lax.broadcasted_iota(jnp.int32, (PAIR, PAIR), 0)
           == lax.broadcasted_iota(jnp.int32, (PAIR, PAIR), 1)).astype(jnp.float32)
    q = eye + n_mat
    p = _dot1(n_mat, n_mat)
    ks = _dot1(kr, s_mat, _NT)
    av = _dot1(jnp.concatenate([a_kk, a_rk], axis=0), vs)
    yield
    for _ in range(4):
        qp = _dot1(jnp.concatenate([q, p], axis=0), p)
        yield
        q = q + qp[:PAIR]
        p = qp[PAIR:]
    t_inv = q + _dot1(q, p)
    yield
    u = -_dot1(t_inv, ks[:PAIR] + av[:PAIR])
    yield
    y = ks[PAIR:] + av[PAIR:] + _dot1(a_rb, u)
    yield
    w_last = jnp.sum(jnp.where(last_row, e_pos, zero), axis=0, keepdims=True)
    s_new = (s_mat + _dot1(jnp.concatenate([u, vs], axis=0), bk, _TN)) * w_last
    return y[:CHUNK] + y[CHUNK:], s_new


def _rwkv_kernel(rf, kf, vf, khf, abf, lwf, rb, kb, vb, khb, abb, lwb, yf_ref, yb_ref, s_ref, *, nctx):
    j = pl.program_id(1)

    @pl.when(j == 0)
    def _():
        s_ref[...] = jnp.zeros_like(s_ref)

    row = lax.broadcasted_iota(jnp.int32, (PAIR, PAIR), 0)
    col = lax.broadcasted_iota(jnp.int32, (PAIR, PAIR), 1)
    same = (row // CHUNK) == (col // CHUNK)
    r64 = lax.broadcasted_iota(jnp.int32, (CHUNK, CHUNK), 0)
    c64 = lax.broadcasted_iota(jnp.int32, (CHUNK, CHUNK), 1)
    rowl = lax.broadcasted_iota(jnp.int32, (CHUNK, PAIR), 0)
    lane = lax.broadcasted_iota(jnp.int32, (1, PAIR), 1)
    m0 = (lane < D_HR).astype(jnp.float32)
    m1 = 1.0 - m0
    dirs = (
        (rf, kf, vf, khf, abf, lwf, yf_ref, (c64 <= r64), same & (col < row), same & (col <= row), rowl == CHUNK - 1),
        (rb, kb, vb, khb, abb, lwb, yb_ref, (c64 >= r64), same & (col > row), same & (col >= row), rowl == 0),
    )
    states = [[s_ref[d, p] for p in range(NPAIR)] for d in range(2)]
    gens, meta = [], []
    for d, (r_ref, k_ref, v_ref, kh_ref, ab_ref, lw_ref, y_ref, tri, strict, incl, last_row) in enumerate(dirs):
        tri = tri.astype(jnp.float32).astype(jnp.bfloat16)
        for p in range(NPAIR):
            sl = slice(p * PAIR, (p + 1) * PAIR)
            gens.append(_rwkv_pair_chunk(r_ref[0, :, sl], k_ref[0, :, sl], v_ref[0, :, sl], kh_ref[0, :, sl],
                                         ab_ref[0, :, sl], lw_ref[0, :, sl], states[d][p], tri, strict, incl,
                                         last_row, m0, m1))
            meta.append((d, p, sl, y_ref))
    results = [m + r for m, r in zip(meta, _round_robin(gens))]
    for d, p, sl, y_ref, y, s_new in results:
        s_ref[d, p] = s_new
        y_ref[0, :, sl] = y


def _rwkv_scan(r, k, v, kh, ab, lw_f, lw_b, ctx_len):
    b, t, w = r.shape
    nc = t // CHUNK
    nctx = ctx_len // CHUNK
    nlat = nc - nctx

    def fwd_map(i, j):
        return (i, j, 0)

    def bwd_map(i, j):
        return (i, jnp.where(j < nctx, nctx - 1 - j, nc - 1 - (j - nctx)), 0)

    blk = (1, CHUNK, w)
    in_specs = [pl.BlockSpec(blk, fwd_map)] * 6 + [pl.BlockSpec(blk, bwd_map)] * 6
    out_specs = [
        pl.BlockSpec(blk, lambda i, j: (i, jnp.maximum(j - nctx, 0), 0)),
        pl.BlockSpec(blk, lambda i, j: (i, nlat - 1 - jnp.maximum(j - nctx, 0), 0)),
    ]
    return pl.pallas_call(
        functools.partial(_rwkv_kernel, nctx=nctx),
        grid=(b, nc),
        in_specs=in_specs,
        out_specs=out_specs,
        out_shape=[jax.ShapeDtypeStruct((b, t - ctx_len, w), jnp.float32)] * 2,
        scratch_shapes=[pltpu.VMEM((2, NPAIR, PAIR, PAIR), jnp.float32)],
        compiler_params=pltpu.CompilerParams(
            dimension_semantics=("arbitrary", "arbitrary"), vmem_limit_bytes=VMEM_LIMIT),
        name="rwkv_scan",
    )(r, k, v, kh, ab, lw_f, r, k, v, kh, ab, lw_b)


def _scan_max(x, reverse):
    rows = lax.broadcasted_iota(jnp.int32, x.shape, 0)
    neg_inf = jnp.full((), -jnp.inf, jnp.float32)
    step = 1
    while step < CHUNK:
        if reverse:
            shifted = jnp.where(rows < CHUNK - step, pltpu.roll(x, CHUNK - step, 0), neg_inf)
        else:
            shifted = jnp.where(rows >= step, pltpu.roll(x, step, 0), neg_inf)
        x = jnp.maximum(x, shifted)
        step *= 2
    return x


def _mlstm_gates(gates, bias, tri, m_prev, is_f, reverse):
    gl = gates + bias
    act = jnp.where(is_f, jax.nn.log_sigmoid(gl), gl)
    b_all = _dot_exact_lhs(tri, act)
    li = pltpu.roll(act, H_M, 1)
    x = li - b_all
    m_inter = b_all + m_prev
    m_t = jnp.maximum(m_inter, b_all + _scan_max(x, reverse))
    b_last = b_all[0:1, :] if reverse else b_all[CHUNK - 1:CHUNK, :]
    g = b_last + x
    m_new = jnp.maximum(b_last + m_prev, jnp.max(g, axis=0, keepdims=True))
    wts = jnp.exp(g - m_new)
    decay = jnp.exp(b_last + m_prev - m_new)
    return b_all - m_t, jnp.exp(m_inter - m_t), jnp.exp(-m_t), x, wts, decay, m_new


def _mlstm_pair_chunk(q, k, v, pieces, cf, decay_g, c_mat, n_row, incl, m0, m1):
    stack = lambda a: jnp.concatenate([a * m0, a * m1], axis=0)
    q_st, k_st, v_st = stack(q), stack(k), stack(v)
    lane = lax.broadcasted_iota(jnp.int32, (1, PAIR), 1)
    zero_bf = jnp.zeros((), jnp.bfloat16)
    sel = lambda a: jnp.concatenate([jnp.where(lane == cf, a, zero_bf), jnp.where(lane == cf + 1, a, zero_bf)], 0)
    ones = jnp.ones((PAIR, PAIR), jnp.bfloat16)
    spread = [sel(a) for a in pieces[0]] + [sel(p[0]) for p in pieces[1:4]]
    bc = _dg(jnp.concatenate(spread, axis=0), ones, _NN)
    bm_bc = bc[:PAIR] + bc[PAIR:2 * PAIR] + bc[2 * PAIR:3 * PAIR]
    inter_bc, enm_bc, wts_bc = (bc[i * PAIR:(i + 1) * PAIR] for i in range(3, 6))
    r3 = _dg(ones, jnp.concatenate([sel(a) for a in pieces[4]], axis=0), _NT)
    r_mat = r3[:, :PAIR] + r3[:, PAIR:2 * PAIR] + r3[:, 2 * PAIR:]
    qk = _dot1(q_st, k_st, _NT)
    qc = _dot1(q_st, c_mat)
    yield
    neg_inf = jnp.full((), -jnp.inf, jnp.float32)
    s = qk * jnp.exp(jnp.where(incl, bm_bc + r_mat, neg_inf))
    wk = wts_bc * k_st
    sv = _dot1(s, v_st)
    kv = _dot1(wk, v_st, _TN)
    sums = _dot1(jnp.concatenate([q_st * n_row, s], axis=0), jnp.ones((PAIR, PAIR), jnp.float32))
    yield
    den = inter_bc * sums[:PAIR] + sums[PAIR:]
    h_st = (inter_bc * qc + sv) / jnp.maximum(jnp.abs(den), enm_bc)
    h = h_st[:CHUNK] + h_st[CHUNK:]
    decay_row = jnp.where(lane < D_HM, decay_g[:, cf:cf + 1], decay_g[:, cf + 1:cf + 2])
    c_new = decay_row * c_mat + kv
    n_new = decay_row * n_row + jnp.sum(wk, axis=0, keepdims=True)
    return h, c_new, n_new


def _mlstm_kernel(qf, kf, vf, gf, qb, kb, vb, gb, bias_ref, hf_ref, hb_ref, c_ref, n_ref, m_ref, *, nctx):
    del nctx
    j = pl.program_id(1)

    @pl.when(j == 0)
    def _():
        c_ref[...] = jnp.zeros_like(c_ref)
        n_ref[...] = jnp.zeros_like(n_ref)
        m_ref[...] = jnp.zeros_like(m_ref)

    row = lax.broadcasted_iota(jnp.int32, (PAIR, PAIR), 0)
    col = lax.broadcasted_iota(jnp.int32, (PAIR, PAIR), 1)
    same = (row // CHUNK) == (col // CHUNK)
    r64 = lax.broadcasted_iota(jnp.int32, (CHUNK, CHUNK), 0)
    c64 = lax.broadcasted_iota(jnp.int32, (CHUNK, CHUNK), 1)
    lane = lax.broadcasted_iota(jnp.int32, (1, PAIR), 1)
    m0 = (lane < D_HM).astype(jnp.float32)
    m1 = 1.0 - m0
    is_f = (lane % (2 * H_M)) >= H_M
    dirs = (
        (qf, kf, vf, gf, hf_ref, (c64 <= r64), same & (col <= row)),
        (qb, kb, vb, gb, hb_ref, (c64 >= r64), same & (col >= row)),
    )
    states = [[(c_ref[d, p], n_ref[d, p]) for p in range(NPAIR)] for d in range(2)]
    gens, meta, m_news = [], [], []
    for d, (q_ref, k_ref, v_ref, g_ref, h_ref, tri, incl) in enumerate(dirs):
        tri = tri.astype(jnp.float32).astype(jnp.bfloat16)
        bm, inter, enm, x, wts, decay_g, m_new = _mlstm_gates(g_ref[0], bias_ref[...], tri, m_ref[d], is_f, d == 1)
        m_news.append(m_new)
        lead = lambda a: (a.astype(jnp.bfloat16),)
        pieces = [_split3(bm), lead(inter), lead(enm), lead(wts), _split3(x)]
        for p in range(NPAIR):
            sl = slice(p * PAIR, (p + 1) * PAIR)
            cf = d * 2 * H_M + H_M + 2 * p
            gens.append(_mlstm_pair_chunk(q_ref[0, :, sl], k_ref[0, :, sl], v_ref[0, :, sl], pieces, cf, decay_g,
                                           *states[d][p], incl, m0, m1))
            meta.append((d, p, sl, h_ref))
    for (d, p, sl, h_ref), (h, c_new, n_new) in zip(meta, _round_robin(gens)):
        c_ref[d, p] = c_new
        n_ref[d, p] = n_new
        h_ref[0, :, sl] = h
    for d in range(2):
        m_ref[d] = m_news[d]


def _mlstm_scan(q, k, pn, bias_row, ctx_len):
    v = gates = pn
    b, t, w = q.shape
    nc = t // CHUNK
    nctx = ctx_len // CHUNK
    nlat = nc - nctx

    def fwd_map(i, j):
        return (i, j, 0)

    def bwd_map(i, j):
        return (i, jnp.where(j < nctx, nctx - 1 - j, nc - 1 - (j - nctx)), 0)

    blk = (1, CHUNK, w)
    gblk = (1, CHUNK, 128)
    gate_blk = (2 * W_M) // 128
    gate_of = lambda m: (lambda i, j: m(i, j)[:2] + (gate_blk,))
    in_specs = ([pl.BlockSpec(blk, fwd_map)] * 3 + [pl.BlockSpec(gblk, gate_of(fwd_map))]
                + [pl.BlockSpec(blk, bwd_map)] * 3 + [pl.BlockSpec(gblk, gate_of(bwd_map))]
                + [pl.BlockSpec((1, 128), lambda i, j: (0, 0))])
    out_specs = [
        pl.BlockSpec(blk, lambda i, j: (i, jnp.maximum(j - nctx, 0), 0)),
        pl.BlockSpec(blk, lambda i, j: (i, nlat - 1 - jnp.maximum(j - nctx, 0), 0)),
    ]
    return pl.pallas_call(
        functools.partial(_mlstm_kernel, nctx=nctx),
        grid=(b, nc),
        in_specs=in_specs,
        out_specs=out_specs,
        out_shape=[jax.ShapeDtypeStruct((b, t - ctx_len, w), jnp.float32)] * 2,
        scratch_shapes=[pltpu.VMEM((2, NPAIR, PAIR, PAIR), jnp.float32),
                        pltpu.VMEM((2, NPAIR, 1, PAIR), jnp.float32),
                        pltpu.VMEM((2, 1, 128), jnp.float32)],
        compiler_params=pltpu.CompilerParams(
            dimension_semantics=("arbitrary", "arbitrary"), vmem_limit_bytes=VMEM_LIMIT),
        name="mlstm_scan",
    )(q, k, v, gates, q, k, v, gates, bias_row)


ROUTE_W = 128
N_ROUTE = N_GROUPS + N_EXPERTS


def _ln_rows(z, eps):
    mu = jnp.mean(z, -1, keepdims=True)
    zc = z - mu
    var = jnp.mean(zc * zc, -1, keepdims=True)
    return zc * lax.rsqrt(var + eps)


def _headnorm_mxu(z, bd2, eps):
    mu = _head_sums(z, bd2) * (1.0 / D_HM)
    zc = z - mu
    var = _head_sums(zc * zc, bd2) * (1.0 / D_HM)
    return zc * lax.rsqrt(var + eps)


def _outproj_kernel(hf, hb, mo, yf, yb, gg, bonus, x_ref, mod_ref, wout_ref, hpar_ref, ln_ref, rtw_ref, rtb_ref,
                    bd_ref, x1_ref, tok_ref, route_ref, ridx_ref, cnt_ref, carry_ref):
    i, j = pl.program_id(0), pl.program_id(1)

    @pl.when((i == 0) & (j == 0))
    def _():
        carry_ref[...] = jnp.zeros_like(carry_ref)

    avg = bd_ref[...]
    out_m = jax.nn.sigmoid(mo[0]) * (_headnorm_mxu(hf[0] + hb[0], avg, LN_EPS) * hpar_ref[0:1, :])
    y = _headnorm_mxu(yf[0] + yb[0], avg, GN_EPS) * hpar_ref[1:2, :] + hpar_ref[2:3, :]
    out_r = (y + bonus[0]) * gg[0]
    proj = (jnp.dot(out_m.astype(jnp.bfloat16), wout_ref[:W_M, :], preferred_element_type=jnp.float32)
            + jnp.dot(out_r.astype(jnp.bfloat16), wout_ref[W_M:, :], preferred_element_type=jnp.float32))
    g1, sh2, sc2 = mod_ref[0, 2:3, :], mod_ref[0, 3:4, :], mod_ref[0, 4:5, :]
    x1 = _ln_rows(ALPHA * x_ref[0] + g1 * proj, LN_EPS) * ln_ref[0:1, :] + ln_ref[1:2, :]
    x1_ref[0] = x1
    tok = _ln_rows(x1, LN_EPS) * (1.0 + sc2) + sh2
    tok_ref[0] = tok

    logits = _dot3_pre(tok, rtw_ref[...]) + rtb_ref[...]
    lane = lax.broadcasted_iota(jnp.int32, (ROW_TILE, ROUTE_W), 1)
    neg_inf = jnp.full((), -jnp.inf, jnp.float32)
    big = jnp.int32(ROUTE_W)
    is_g = lane < N_GROUPS
    lg = jnp.where(is_g, logits, neg_inf)
    gmax = jnp.max(lg, axis=1, keepdims=True)
    grp = jnp.min(jnp.where(lg == gmax, lane, big), axis=1, keepdims=True)
    p_top = 1.0 / jnp.sum(jnp.where(is_g, jnp.exp(lg - gmax), 0.0), axis=1, keepdims=True)
    in_grp = (lane >= N_GROUPS) & (lane < N_ROUTE) & ((lane - N_GROUPS) // E_PER_GROUP == grp)
    le = jnp.where(in_grp, logits, neg_inf)
    v1 = jnp.max(le, axis=1, keepdims=True)
    i1 = jnp.min(jnp.where(le == v1, lane, big), axis=1, keepdims=True)
    le2 = jnp.where(lane == i1, neg_inf, le)
    v2 = jnp.max(le2, axis=1, keepdims=True)
    i2 = jnp.min(jnp.where(le2 == v2, lane, big), axis=1, keepdims=True)
    e21 = jnp.exp(v2 - v1)
    w1 = (1.0 / (1.0 + e21)) * p_top
    w2 = (e21 / (1.0 + e21)) * p_top

    sel1, sel2 = lane == i1, lane == i2
    onehot = jnp.where(sel1 | sel2, 1.0, 0.0)
    tr = lax.broadcasted_iota(jnp.int32, (ROW_TILE, ROW_TILE), 0)
    tc = lax.broadcasted_iota(jnp.int32, (ROW_TILE, ROW_TILE), 1)
    strict = jnp.where(tc < tr, 1.0, 0.0).astype(jnp.bfloat16)
    before = _dg(strict, onehot.astype(jnp.bfloat16), _NN) + carry_ref[...]
    r1 = jnp.sum(jnp.where(sel1, before, 0.0), axis=1, keepdims=True)
    r2 = jnp.sum(jnp.where(sel2, before, 0.0), axis=1, keepdims=True)
    carry_ref[...] = carry_ref[...] + jnp.sum(onehot, axis=0, keepdims=True)
    e1 = (i1 - N_GROUPS).astype(jnp.float32)
    e2 = (i2 - N_GROUPS).astype(jnp.float32)
    route = jnp.where(lane == 0, e1, jnp.where(lane == 1, e2, jnp.where(lane == 2, r1, jnp.where(
        lane == 3, r2, jnp.where(lane == 4, w1, jnp.where(lane == 5, w2, 0.0))))))
    route_ref[0] = route
    ridx_ref[0] = route.T[0:8, :].astype(jnp.int32)
    cnt_ref[...] = jnp.broadcast_to(carry_ref[...], cnt_ref.shape)


def _outproj(h_f, h_b, pn, y_f, y_b, gg, bonus, x, mod3, wout_bf16, hpar, lnpar, rtw, rtb, bd, ctx_len):
    b, seq, d = x.shape
    nt = seq // ROW_TILE
    off = ctx_len // ROW_TILE
    lat = lambda w: pl.BlockSpec((1, ROW_TILE, w), lambda i, j: (i, j, 0))
    full = lambda shp: pl.BlockSpec(shp, lambda i, j: tuple(0 for _ in shp))
    in_specs = [
        lat(W_M), lat(W_M),
        pl.BlockSpec((1, ROW_TILE, W_M), lambda i, j: (i, j + off, 1)),
        lat(W_R), lat(W_R),
        pl.BlockSpec((1, ROW_TILE, W_R), lambda i, j: (i, j + off, 0)),
        pl.BlockSpec((1, ROW_TILE, W_R), lambda i, j: (i, j + off, 0)),
        lat(d),
        pl.BlockSpec((1, 6, d), lambda i, j: (i, 0, 0)),
        full((MIX_W, d)), full((8, W_M)), full((8, d)), full((3 * d, ROUTE_W)), full((1, ROUTE_W)),
        full((2 * HEAD_BLK, HEAD_BLK)),
    ]
    out_specs = [
        lat(d), lat(d), lat(ROUTE_W),
        pl.BlockSpec((1, 8, ROW_TILE), lambda i, j: (i * nt + j, 0, 0)),
        full((8, ROUTE_W)),
    ]
    out_shape = [
        jax.ShapeDtypeStruct((b, seq, d), jnp.float32),
        jax.ShapeDtypeStruct((b, seq, d), jnp.float32),
        jax.ShapeDtypeStruct((b, seq, ROUTE_W), jnp.float32),
        jax.ShapeDtypeStruct((b * nt, 8, ROW_TILE), jnp.int32),
        jax.ShapeDtypeStruct((8, ROUTE_W), jnp.float32),
    ]
    return pl.pallas_call(
        _outproj_kernel,
        grid=(b, nt),
        in_specs=in_specs,
        out_specs=out_specs,
        out_shape=out_shape,
        scratch_shapes=[pltpu.VMEM((1, ROUTE_W), jnp.float32)],
        compiler_params=pltpu.CompilerParams(
            dimension_semantics=("arbitrary", "arbitrary"), vmem_limit_bytes=VMEM_LIMIT),
        name="outproj_router",
    )(h_f, h_b, pn, y_f, y_b, gg, bonus, x, mod3, wout_bf16, hpar, lnpar, rtw, rtb, bd)


DEST_TILES = 8


def _dest_kernel(starts_ref, ridx_ref, dest_ref):
    blk = ridx_ref[...]
    dest = pltpu.roll(blk, 8 - TOP_K_IN_GROUP, 1)
    for e in range(N_EXPERTS):
        dest = dest + jnp.where(blk == e, starts_ref[e], 0)
    dest_ref[...] = dest


def _moe_dest(starts, ridx):
    nt = ridx.shape[0]
    per_step = math.gcd(nt, DEST_TILES)
    blk = pl.BlockSpec((per_step, 8, ROW_TILE), lambda i, s: (i, 0, 0))
    return pl.pallas_call(
        _dest_kernel,
        grid_spec=pltpu.PrefetchScalarGridSpec(num_scalar_prefetch=1, grid=(nt // per_step,),
                                               in_specs=[blk], out_specs=blk),
        out_shape=jax.ShapeDtypeStruct(ridx.shape, jnp.int32),
        compiler_params=pltpu.CompilerParams(dimension_semantics=("arbitrary",), vmem_limit_bytes=VMEM_LIMIT),
        name="moe_dest",
    )(starts, ridx)


SUB = 8
SUB_SHIFT = 3
TILE_GROUPS = ROW_TILE // SUB


def _scatter_kernel(dest_ref, tok_ref, xs_in_ref, xs_ref, sem):
    del xs_in_ref

    def issue(g, c):
        for k in range(SUB):
            for slot in range(TOP_K_IN_GROUP):
                dst = dest_ref[slot * ROW_TILE + g * SUB + k]
                pltpu.make_async_copy(tok_ref.at[g, pl.ds(k, 1), :],
                                      xs_ref.at[dst >> SUB_SHIFT, pl.ds(dst & (SUB - 1), 1), :], sem).start()
        return c

    lax.fori_loop(0, TILE_GROUPS, issue, 0)
    for half in range(TOP_K_IN_GROUP):
        pltpu.make_async_copy(tok_ref, xs_ref.at[pl.ds(half * TILE_GROUPS, TILE_GROUPS)], sem).wait()


def _moe_scatter(dest_flat, tok2d, n_rows):
    n, d = tok2d.shape
    xs_zero = jnp.zeros((n_rows // SUB, SUB, d), tok2d.dtype)
    nt = n // ROW_TILE
    xs = pl.pallas_call(
        _scatter_kernel,
        grid=(nt,),
        in_specs=[
            pl.BlockSpec((SUB * ROW_TILE,), lambda i: (i,), memory_space=pltpu.SMEM),
            pl.BlockSpec((TILE_GROUPS, SUB, d), lambda i: (i, 0, 0)),
            pl.BlockSpec(memory_space=pl.ANY),
        ],
        out_specs=pl.BlockSpec(memory_space=pl.ANY),
        scratch_shapes=[pltpu.SemaphoreType.DMA(())],
        out_shape=jax.ShapeDtypeStruct(xs_zero.shape, tok2d.dtype),
        input_output_aliases={2: 0},
        compiler_params=pltpu.CompilerParams(dimension_semantics=("arbitrary",), vmem_limit_bytes=VMEM_LIMIT),
        name="moe_scatter",
    )(dest_flat, tok2d.reshape(n // SUB, SUB, d), xs_zero)
    return xs.reshape(n_rows, d)


def _expert_kernel(blk_exp_ref, valid_ref, xs_ref, wg_ref, wu_ref, wd_ref, ys_ref):
    i = pl.program_id(0)
    valid = valid_ref[i]

    @pl.when(valid > 0)
    def _():
        xb = xs_ref[...].astype(jnp.bfloat16)
        hg = jnp.dot(xb, wg_ref[0], preferred_element_type=jnp.float32)
        hu = jnp.dot(xb, wu_ref[0], preferred_element_type=jnp.float32)
        hb = (jax.nn.silu(hg) * hu).astype(jnp.bfloat16)
        ys_ref[...] = jnp.dot(hb, wd_ref[0], preferred_element_type=jnp.float32)

    @pl.when(valid == 0)
    def _():
        ys_ref[...] = jnp.zeros_like(ys_ref)


def _moe_experts(blk_exp, blk_valid, xs, wg, wu, wd):
    nrow, d = xs.shape
    de = wg.shape[2]
    return pl.pallas_call(
        _expert_kernel,
        grid_spec=pltpu.PrefetchScalarGridSpec(
            num_scalar_prefetch=2,
            grid=(nrow // MOE_BLOCK,),
            in_specs=[
                pl.BlockSpec((MOE_BLOCK, d), lambda i, be, nu: (i, 0)),
                pl.BlockSpec((1, d, de), lambda i, be, nu: (be[i], 0, 0)),
                pl.BlockSpec((1, d, de), lambda i, be, nu: (be[i], 0, 0)),
                pl.BlockSpec((1, de, d), lambda i, be, nu: (be[i], 0, 0)),
            ],
            out_specs=pl.BlockSpec((MOE_BLOCK, d), lambda i, be, nu: (i, 0)),
        ),
        out_shape=jax.ShapeDtypeStruct((nrow, d), jnp.float32),
        compiler_params=pltpu.CompilerParams(dimension_semantics=("arbitrary",), vmem_limit_bytes=VMEM_LIMIT),
        name="moe_experts",
    )(blk_exp, blk_valid, xs, wg, wu, wd)


def _combine_kernel(dest_ref, dest_next_ref, route_ref, x1_ref, mod_ref, ln_ref, ys_ref, out_ref,
                    ybuf, sems):
    i = pl.program_id(0)
    n = pl.num_programs(0)
    cur = i % 2

    def gather(idx_ref, buf):
        def issue(r, c):
            for slot in range(TOP_K_IN_GROUP):
                src = idx_ref[0, slot, r]
                pltpu.make_async_copy(ys_ref.at[pl.ds(src, 1), :],
                                      ybuf.at[buf, pl.ds(slot * ROW_TILE + r, 1), :], sems.at[buf]).start()
            return c

        lax.fori_loop(0, ROW_TILE, issue, 0, unroll=DMA_UNROLL)

    @pl.when(i == 0)
    def _():
        gather(dest_ref, 0)

    @pl.when(i + 1 < n)
    def _():
        gather(dest_next_ref, 1 - cur)

    pltpu.make_async_copy(ys_ref.at[pl.ds(0, TOP_K_IN_GROUP * ROW_TILE), :], ybuf.at[cur], sems.at[cur]).wait()
    route = route_ref[0]
    ffn = ybuf[cur, 0:ROW_TILE, :] * route[:, 4:5] + ybuf[cur, ROW_TILE:2 * ROW_TILE, :] * route[:, 5:6]
    g2 = mod_ref[0, 5:6, :]
    out_ref[0] = _ln_rows(ALPHA * x1_ref[0] + g2 * ffn, LN_EPS) * ln_ref[2:3, :] + ln_ref[3:4, :]


def _moe_combine(dest, route, x1, mod3, lnpar, ys):
    b, seq, d = x1.shape
    nt = seq // ROW_TILE
    n = b * nt
    tile = lambda w: pl.BlockSpec((1, ROW_TILE, w), lambda i: (i // nt, i % nt, 0))
    return pl.pallas_call(
        _combine_kernel,
        grid=(n,),
        in_specs=[
            pl.BlockSpec((1, 8, ROW_TILE), lambda i: (i, 0, 0), memory_space=pltpu.SMEM),
            pl.BlockSpec((1, 8, ROW_TILE), lambda i: (jnp.minimum(i + 1, n - 1), 0, 0), memory_space=pltpu.SMEM),
            tile(ROUTE_W),
            tile(d),
            pl.BlockSpec((1, 6, d), lambda i: (i // nt, 0, 0)),
            pl.BlockSpec((8, d), lambda i: (0, 0)),
            pl.BlockSpec(memory_space=pl.ANY),
        ],
        out_specs=tile(d),
        scratch_shapes=[pltpu.VMEM((2, TOP_K_IN_GROUP * ROW_TILE, d), jnp.float32), pltpu.SemaphoreType.DMA((2,))],
        out_shape=jax.ShapeDtypeStruct((b, seq, d), jnp.float32),
        compiler_params=pltpu.CompilerParams(dimension_semantics=("arbitrary",), vmem_limit_bytes=VMEM_LIMIT),
        name="moe_combine",
    )(dest, dest, route, x1, mod3, lnpar, ys)


def _moe_plan(cnt, n_pairs):
    counts = cnt[0, N_GROUPS:N_ROUTE].astype(jnp.int32)
    padded = (counts + MOE_BLOCK - 1) // MOE_BLOCK * MOE_BLOCK
    pad_ends = jnp.cumsum(padded)
    pad_starts = pad_ends - padded
    n_blk = -(-n_pairs // MOE_BLOCK) + N_EXPERTS
    blk_row0 = jnp.arange(n_blk, dtype=jnp.int32) * MOE_BLOCK
    blk_exp = jnp.minimum(jnp.sum((pad_ends[None, :] <= blk_row0[:, None]).astype(jnp.int32), axis=1), N_EXPERTS - 1)
    blk_valid = jnp.clip(counts[blk_exp] - (blk_row0 - pad_starts[blk_exp]), 0, MOE_BLOCK)
    blk_valid = jnp.where(blk_row0 < pad_ends[-1], blk_valid, 0).astype(jnp.int32)
    return pad_starts.astype(jnp.int32), blk_exp, blk_valid, n_blk


def kernel(x, c, ctx, c_ctx, w_ada, b_ada, w_in, conv_w, m_bias_i, m_bias_f, m_norm_w, r_w0, r_wB, r_a0, r_aB,
           r_gB, r_kk, r_ka, r_bonus, r_norm_w, r_norm_b, w_out, ln1_g, ln1_b, ln2_g, ln2_b, rt_g, rt_g_b, rt_e,
           rt_e_b, ex_gate, ex_up, ex_down):
    assert w_ada.shape[0] == DEPTH
    bsz, seq, d = x.shape
    ctx_len = ctx.shape[1]
    assert ctx_len == ROW_TILE and seq % ROW_TILE == 0
    assert seq % GRID_W == 0 and ROW_TILE % GRID_W == 0
    l = 0
    zrow = lambda n, w: jnp.zeros((n, w), jnp.float32)
    mrows = -(-(bsz + 1) // 8) * 8
    c_pad = jnp.concatenate([c, c_ctx[None, :], zrow(mrows - bsz - 1, d)], 0)
    mod_all = _ada(c_pad, w_ada[l], b_ada[l][None, :])[:bsz + 1]
    mod3 = mod_all.reshape(bsz + 1, 6, d)
    pc, pn = _inproj(ctx, x, mod3, _pad_w_in(w_in[l]))
    cw = jnp.concatenate([conv_w[l].reshape(CONV_K * CONV_K, CONV_CH), zrow(16 - CONV_K * CONV_K, CONV_CH)], 0)
    rpar = jnp.concatenate([r_w0[l], r_a0[l][None, :], r_kk[l][None, :], r_ka[l][None, :],
                            r_bonus[l].reshape(1, W_R), zrow(2, W_R)], 0)
    wbf = jnp.concatenate([r_wB[l][0], zrow(W_LORA, W_R)], 0)
    wbb = jnp.concatenate([zrow(W_LORA, W_R), r_wB[l][1]], 0)
    ab_mat = jnp.concatenate([r_aB[l], zrow(128 - A_LORA, W_R)], 0)
    bd = _head_sum_operator(D_HR)
    q, k, rr, kmod, rv, kh, kha, logw_f, logw_b, gg, bonus = _prep(
        pc, pn, cw, rpar, _stack3(wbf), _stack3(wbb), _stack3(ab_mat), _stack3(r_gB[l]), bd)
    bias_row = jnp.concatenate([m_bias_i[l][0], m_bias_f[l][0], m_bias_i[l][1], m_bias_f[l][1],
                                jnp.zeros((128 - 4 * H_M,), jnp.float32)])[None, :]
    h_f, h_b = _mlstm_scan(q, k, pn, bias_row, ctx_len)
    y_f, y_b = _rwkv_scan(rr, kmod, rv, kh, kha, logw_f, logw_b, ctx_len)
    hpar = jnp.zeros((8, W_M), jnp.float32).at[0].set(m_norm_w[l]).at[1].set(r_norm_w[l]).at[2].set(r_norm_b[l])
    lnpar = jnp.zeros((8, d), jnp.float32).at[0].set(ln1_g[l]).at[1].set(ln1_b[l]).at[2].set(ln2_g[l]).at[3].set(
        ln2_b[l])
    rtw = jnp.concatenate([rt_g[l], rt_e[l], jnp.zeros((d, ROUTE_W - N_ROUTE), jnp.float32)], axis=1)
    rtb = jnp.concatenate([rt_g_b[l], rt_e_b[l], jnp.zeros((ROUTE_W - N_ROUTE,), jnp.float32)])[None, :]
    x1, tok, route, ridx, cnt = _outproj(h_f, h_b, pn, y_f, y_b, gg, bonus, x, mod3, w_out[l].astype(jnp.bfloat16),
                                         hpar, lnpar, _stack3(rtw), rtb, bd, ctx_len)
    n_tok = bsz * seq
    starts, blk_exp, blk_valid, n_blk = _moe_plan(cnt, n_tok * TOP_K_IN_GROUP)
    dest = _moe_dest(starts, ridx)
    dest_flat = dest.reshape(-1)
    xs = _moe_scatter(dest_flat, tok.reshape(n_tok, d), n_blk * MOE_BLOCK)
    ys = _moe_experts(blk_exp, blk_valid, xs, ex_gate[l].astype(jnp.bfloat16), ex_up[l].astype(jnp.bfloat16),
                      ex_down[l].astype(jnp.bfloat16))
    return _moe_combine(dest, route, x1, mod3, lnpar, ys)
```

```python
import functools
import math

import jax
import jax.numpy as jnp
from jax import lax
import numpy as np
from jax.experimental import pallas as pl
from jax.experimental.pallas import tpu as pltpu

H_M = 8
D_HM = 64
W_M = H_M * D_HM
H_R = 8
D_HR = 64
W_R = H_R * D_HR
MIX_W = W_M + W_R
W_LORA = 64
A_LORA = 64
G_LORA = 128
CONV_K = 3
CONV_CH = 2 * W_M + 3 * W_R
SECTION_WIDTHS = (W_M, W_M, W_R, W_R, W_R, W_M, W_M, 4 * H_M, W_LORA, W_LORA, A_LORA, G_LORA)
IN_COLS = sum(SECTION_WIDTHS)
SPLIT_POINTS = tuple(int(v) for v in np.cumsum(SECTION_WIDTHS)[:-1])
CHUNK = 64
N_GROUPS = 4
E_PER_GROUP = 8
N_EXPERTS = N_GROUPS * E_PER_GROUP
TOP_K_IN_GROUP = 2
D_EXPERT = 512
MOE_BLOCK = 256
DEPTH = 1
ALPHA = (2.0 * DEPTH) ** 0.25
DECAY_SCALE = math.exp(-0.5)
LN_EPS = 1e-6
GN_EPS = 64e-5

ROW_TILE = 256
NONCONV_W = 1536
LORA_TAIL_W = 384
VMEM_LIMIT = 56 * 1024 * 1024
DMA_UNROLL = 8
SCAN_BATCH = 4


def _inproj_kernel(ctx_ref, x_ref, mod_ref, w_ref, oc_ref, on_ref):
    j = pl.program_id(1)

    def body(src_ref):
        z = src_ref[0]
        mu = jnp.mean(z, -1, keepdims=True)
        zc = z - mu
        var = jnp.mean(zc * zc, -1, keepdims=True)
        y = zc * lax.rsqrt(var + LN_EPS)
        h = (y * (1.0 + mod_ref[0, 1:2, :]) + mod_ref[0, 0:1, :]).astype(jnp.bfloat16)
        oc_ref[0] = jnp.dot(h, w_ref[:, :CONV_CH], preferred_element_type=jnp.float32)
        on_ref[0] = jnp.dot(h, w_ref[:, CONV_CH:], preferred_element_type=jnp.float32)

    @pl.when(j == 0)
    def _():
        body(ctx_ref)

    @pl.when(j > 0)
    def _():
        body(x_ref)


def _inproj(ctx, x, mod_all, w_pad):
    b, seq, d = x.shape
    nt = (ctx.shape[1] + seq) // ROW_TILE
    t_all = ctx.shape[1] + seq
    return pl.pallas_call(
        _inproj_kernel,
        grid=(b, nt),
        in_specs=[
            pl.BlockSpec((1, ROW_TILE, d), lambda i, j: (i, 0, 0)),
            pl.BlockSpec((1, ROW_TILE, d), lambda i, j: (i, jnp.maximum(j - 1, 0), 0)),
            pl.BlockSpec((1, 6, d), lambda i, j: (jnp.where(j == 0, b, i), 0, 0)),
            pl.BlockSpec((d, CONV_CH + NONCONV_W), lambda i, j: (0, 0)),
        ],
        out_specs=[
            pl.BlockSpec((1, ROW_TILE, CONV_CH), lambda i, j: (i, j, 0)),
            pl.BlockSpec((1, ROW_TILE, NONCONV_W), lambda i, j: (i, j, 0)),
        ],
        out_shape=[
            jax.ShapeDtypeStruct((b, t_all, CONV_CH), jnp.float32),
            jax.ShapeDtypeStruct((b, t_all, NONCONV_W), jnp.float32),
        ],
        compiler_params=pltpu.CompilerParams(
            dimension_semantics=("arbitrary", "arbitrary"), vmem_limit_bytes=VMEM_LIMIT),
        name="inproj",
    )(ctx, x, mod_all, w_pad)


def _pad_w_in(w_in):
    d = w_in.shape[0]
    sp = SPLIT_POINTS
    z = lambda n: jnp.zeros((d, n), w_in.dtype)
    gates = w_in[:, sp[6]:sp[7]]
    lw = w_in[:, sp[7]:sp[9]]
    la = w_in[:, sp[9]:sp[10]]
    lg = w_in[:, sp[10]:]
    return jnp.concatenate([w_in[:, :sp[6]], gates, z(96), lw, la, z(64), lg], axis=1).astype(jnp.bfloat16)


PAIR = 2 * D_HR
NPAIR = H_R // 2
_NN = (((1,), (0,)), ((), ()))
_NT = (((1,), (1,)), ((), ()))
_TN = (((0,), (0,)), ((), ()))


def _split2(a):
    hi = a.astype(jnp.bfloat16)
    lo = (a - hi.astype(jnp.float32)).astype(jnp.bfloat16)
    return hi, lo


def _split3(a):
    a1 = a.astype(jnp.bfloat16)
    r1 = a - a1.astype(jnp.float32)
    a2 = r1.astype(jnp.bfloat16)
    a3 = (r1 - a2.astype(jnp.float32)).astype(jnp.bfloat16)
    return a1, a2, a3


def _dg(a, b, dims):
    return lax.dot_general(a, b, dims, preferred_element_type=jnp.float32)


def _dot3(a, b, dims=_NN):
    ah, al = _split2(a)
    bh, bl = _split2(b)
    return _dg(ah, bh, dims) + _dg(ah, bl, dims) + _dg(al, bh, dims)


def _dot1(a, b, dims=_NN):
    return _dg(a.astype(jnp.bfloat16), b.astype(jnp.bfloat16), dims)


def _stack3(w):
    hi = w.astype(jnp.bfloat16)
    lo = (w - hi.astype(jnp.float32)).astype(jnp.bfloat16)
    return jnp.concatenate([hi, lo, hi], axis=0)


def _dot3_pre(a, w3):
    ah, al = _split2(a)
    return _dg(jnp.concatenate([ah, ah, al], axis=1), w3, _NN)


HEAD_BLK = 256


def _head_sum_operator(head):
    r = np.arange(HEAD_BLK) // head
    bd = (r[:, None] == r[None, :]).astype(np.float32)
    return jnp.asarray(np.concatenate([bd, bd], axis=0), jnp.bfloat16)


def _head_sums(z, bd2):
    hi, lo = _split2(z)
    out = []
    for c in range(0, z.shape[1], HEAD_BLK):
        lhs = jnp.concatenate([hi[:, c:c + HEAD_BLK], lo[:, c:c + HEAD_BLK]], axis=1)
        out.append(_dg(lhs, bd2, _NN))
    return jnp.concatenate(out, axis=1)


def _dot_exact_lhs(a_bf16, b, dims=_NN):
    b1, b2, b3 = _split3(b)
    return _dg(a_bf16, b1, dims) + _dg(a_bf16, b2, dims) + _dg(a_bf16, b3, dims)


def _ada_kernel(c_ref, w_ref, b_ref, o_ref):
    o_ref[...] = _dot3(jax.nn.silu(c_ref[...]), w_ref[...]) + b_ref[...]


def _ada(c_pad, w, bias):
    m, d = c_pad.shape
    n = w.shape[1]
    return pl.pallas_call(
        _ada_kernel,
        grid=(n // d,),
        in_specs=[pl.BlockSpec((m, d), lambda j: (0, 0)),
                  pl.BlockSpec((d, d), lambda j: (0, j)),
                  pl.BlockSpec((1, d), lambda j: (0, j))],
        out_specs=pl.BlockSpec((m, d), lambda j: (0, j)),
        out_shape=jax.ShapeDtypeStruct((m, n), jnp.float32),
        compiler_params=pltpu.CompilerParams(dimension_semantics=("arbitrary",), vmem_limit_bytes=VMEM_LIMIT),
        name="adaln",
    )(c_pad, w, bias)


GRID_W = 64
HALO = GRID_W


def _prep_kernel(top_ref, main_ref, bot_ref, pn_ref, cw_ref, rpar_ref, wbf_ref, wbb_ref, ab_ref, gb_ref, bd_ref,
                 q_ref, k_ref, r_ref, km_ref, v_ref, kh_ref, abo_ref, lwf_ref, lwb_ref, gg_ref, bon_ref, xbuf):
    j = pl.program_id(1)
    nt = pl.num_programs(1)
    is_ctx = j == 0
    top_ok = j >= 2
    bot_ok = (j >= 1) & (j < nt - 1)
    l_idx = lax.broadcasted_iota(jnp.int32, (ROW_TILE, 1), 0)
    col = jnp.where(is_ctx, l_idx, l_idx % GRID_W)
    left_ok = col != 0
    right_ok = col != jnp.where(is_ctx, ROW_TILE - 1, GRID_W - 1)
    vert = jnp.where(is_ctx, 0.0, 1.0)
    sec = {}
    for s in range(5):
        sl = slice(s * W_M, (s + 1) * W_M)
        xbuf[0:HALO, :] = jnp.where(top_ok, top_ref[0, :, sl], 0.0)
        xbuf[HALO:HALO + ROW_TILE, :] = main_ref[0, :, sl]
        xbuf[HALO + ROW_TILE:2 * HALO + ROW_TILE, :] = jnp.where(bot_ok, bot_ref[0, :, sl], 0.0)
        cols = [None, None, None]
        for dr in range(3):
            base = HALO + GRID_W * (dr - 1)
            w3 = cw_ref[3 * dr:3 * dr + 3, sl] if dr == 1 else cw_ref[3 * dr:3 * dr + 3, sl] * vert
            xrow = xbuf[base:base + ROW_TILE, :]
            for dc in range(3):
                term = xrow * w3[dc:dc + 1]
                cols[dc] = term if cols[dc] is None else cols[dc] + term
        left = jnp.where(left_ok, pltpu.roll(cols[0], 1, 0), 0.0)
        right = jnp.where(right_ok, pltpu.roll(cols[2], ROW_TILE - 1, 0), 0.0)
        sec[s] = left + cols[1] + right
    q_ref[0] = jax.nn.silu(sec[0]).astype(q_ref.dtype)
    k_ref[0] = (jax.nn.silu(sec[1]) * (D_HM ** -0.5)).astype(k_ref.dtype)
    rr, rk, rv = sec[2], sec[3], sec[4]
    r_ref[0] = rr
    v_ref[0] = rv.astype(v_ref.dtype)
    lw = jnp.tanh(pn_ref[0, :, 0:128])
    lwf_ref[0] = -DECAY_SCALE * jax.nn.sigmoid(rpar_ref[0:1, :] + _dot3_pre(lw, wbf_ref[...]))
    lwb_ref[0] = -DECAY_SCALE * jax.nn.sigmoid(rpar_ref[1:2, :] + _dot3_pre(lw, wbb_ref[...]))
    a = jax.nn.sigmoid(rpar_ref[2:3, :] + _dot3_pre(pn_ref[0, :, 128:256], ab_ref[...]))
    gg_ref[0] = _dot3_pre(jax.nn.sigmoid(pn_ref[0, :, 256:384]), gb_ref[...])
    kap = rk * rpar_ref[3:4, :]
    norm = jnp.sqrt(_head_sums(kap * kap, bd_ref[...]))
    kh = kap / jnp.maximum(norm, 1e-12)
    kmod = rk * (1.0 + (a - 1.0) * rpar_ref[4:5, :])
    kh_ref[0] = kh
    abo_ref[0] = kh * a
    km_ref[0] = kmod
    bon_ref[0] = _head_sums(rr * kmod * rpar_ref[5:6, :], bd_ref[...]) * rv


def _prep(pc, pn, cw, rpar, wbf, wbb, ab, gb, bd):
    b, t, _ = pc.shape
    nt = t // ROW_TILE
    per = ROW_TILE // HALO
    nh = t // HALO
    w = W_M
    full = lambda shp: pl.BlockSpec(shp, lambda i, j: tuple(0 for _ in shp))
    in_specs = [
        pl.BlockSpec((1, HALO, CONV_CH), lambda i, j: (i, jnp.maximum(j * per - 1, 0), 0)),
        pl.BlockSpec((1, ROW_TILE, CONV_CH), lambda i, j: (i, j, 0)),
        pl.BlockSpec((1, HALO, CONV_CH), lambda i, j: (i, jnp.minimum(j * per + per, nh - 1), 0)),
        pl.BlockSpec((1, ROW_TILE, LORA_TAIL_W), lambda i, j: (i, j, NONCONV_W // LORA_TAIL_W - 1)),
        full((16, CONV_CH)), full((8, w)), full((384, w)), full((384, w)), full((384, w)), full((384, w)),
        full((2 * HEAD_BLK, HEAD_BLK)),
    ]
    out = pl.BlockSpec((1, ROW_TILE, w), lambda i, j: (i, j, 0))
    return pl.pallas_call(
        _prep_kernel,
        grid=(b, nt),
        in_specs=in_specs,
        out_specs=[out] * 11,
        out_shape=[jax.ShapeDtypeStruct((b, t, w), jnp.bfloat16 if name in ("q", "k", "v") else jnp.float32)
                   for name in ("q", "k", "r", "kmod", "v", "kh", "ab", "lwf", "lwb", "gg", "bonus")],
        scratch_shapes=[pltpu.VMEM((2 * HALO + ROW_TILE, w), jnp.float32)],
        compiler_params=pltpu.CompilerParams(
            dimension_semantics=("arbitrary", "arbitrary"), vmem_limit_bytes=VMEM_LIMIT),
        name="conv_prep",
    )(pc, pc, pc, pn, cw, rpar, wbf, wbb, ab, gb, bd)


def _round_robin(gens):
    out = [None] * len(gens)
    live = list(range(len(gens)))
    while live:
        nxt = []
        for i in live:
            try:
                next(gens[i])
                nxt.append(i)
            except StopIteration as stop:
                out[i] = stop.value
        live = nxt
    return out


def _rwkv_pair_chunk(r, k, v, kh, ab, lw, s_mat, tri_incl, strict, incl, last_row, m0, m1):
    cw = _dot_exact_lhs(tri_incl, lw)
    yield
    e_pos = jnp.exp(cw)
    e_neg = jnp.exp(-cw)
    e_prev = jnp.exp(cw - lw)
    stack = lambda a: jnp.concatenate([a * m0, a * m1], axis=0)
    kt = stack(kh * e_prev)
    bt = stack(ab * e_neg)
    kk = stack(k * e_neg)
    rt = stack(r * e_pos)
    vs = stack(v)
    zero = jnp.zeros((), jnp.float32)
    kr = jnp.concatenate([kt, rt], axis=0)
    bk = jnp.concatenate([bt, kk], axis=0)
    amat = _dot1(kr, bk, _NT)
    yield
    a_bk = jnp.where(strict, amat[:PAIR, :PAIR], zero)
    a_kk = jnp.where(strict, amat[:PAIR, PAIR:], zero)
    a_rb = jnp.where(incl, amat[PAIR:, :PAIR], zero)
    a_rk = jnp.where(incl, amat[PAIR:, PAIR:], zero)
    n_mat = -a_bk
    eye = (lax.broadcasted_iota(jnp.int32, (PAIR, PAIR), 0)
           == lax.broadcasted_iota(jnp.int32, (PAIR, PAIR), 1)).astype(jnp.float32)
    q = eye + n_mat
    p = _dot1(n_mat, n_mat)
    ks = _dot1(kr, s_mat, _NT)
    av = _dot1(jnp.concatenate([a_kk, a_rk], axis=0), vs)
    yield
    for _ in range(4):
        qp = _dot1(jnp.concatenate([q, p], axis=0), p)
        yield
        q = q + qp[:PAIR]
        p = qp[PAIR:]
    t_inv = q + _dot1(q, p)
    yield
    u = -_dot1(t_inv, ks[:PAIR] + av[:PAIR])
    yield
    y = ks[PAIR:] + av[PAIR:] + _dot1(a_rb, u)
    yield
    w_last = jnp.sum(jnp.where(last_row, e_pos, zero), axis=0, keepdims=True)
    s_new = (s_mat + _dot1(jnp.concatenate([u, vs], axis=0), bk, _TN)) * w_last
    return y[:CHUNK] + y[CHUNK:], s_new


def _rwkv_kernel(rf, kf, vf, khf, abf, lwf, rb, kb, vb, khb, abb, lwb, yf_ref, yb_ref, s_ref, *, nctx):
    j = pl.program_id(1)

    @pl.when(j == 0)
    def _():
        s_ref[...] = jnp.zeros_like(s_ref)

    row = lax.broadcasted_iota(jnp.int32, (PAIR, PAIR), 0)
    col = lax.broadcasted_iota(jnp.int32, (PAIR, PAIR), 1)
    same = (row // CHUNK) == (col // CHUNK)
    r64 = lax.broadcasted_iota(jnp.int32, (CHUNK, CHUNK), 0)
    c64 = lax.broadcasted_iota(jnp.int32, (CHUNK, CHUNK), 1)
    rowl = lax.broadcasted_iota(jnp.int32, (CHUNK, PAIR), 0)
    lane = lax.broadcasted_iota(jnp.int32, (1, PAIR), 1)
    m0 = (lane < D_HR).astype(jnp.float32)
    m1 = 1.0 - m0
    dirs = (
        (rf, kf, vf, khf, abf, lwf, yf_ref, (c64 <= r64), same & (col < row), same & (col <= row), rowl == CHUNK - 1),
        (rb, kb, vb, khb, abb, lwb, yb_ref, (c64 >= r64), same & (col > row), same & (col >= row), rowl == 0),
    )
    nb = rf.shape[0]
    states = [[[s_ref[bi, d, p] for p in range(NPAIR)] for d in range(2)] for bi in range(nb)]
    gens, meta = [], []
    for d, (r_ref, k_ref, v_ref, kh_ref, ab_ref, lw_ref, y_ref, tri, strict, incl, last_row) in enumerate(dirs):
        tri = tri.astype(jnp.float32).astype(jnp.bfloat16)
        for bi in range(nb):
            for p in range(NPAIR):
                sl = slice(p * PAIR, (p + 1) * PAIR)
                gens.append(_rwkv_pair_chunk(r_ref[bi, :, sl], k_ref[bi, :, sl], v_ref[bi, :, sl],
                                             kh_ref[bi, :, sl], ab_ref[bi, :, sl], lw_ref[bi, :, sl],
                                             states[bi][d][p], tri, strict, incl, last_row, m0, m1))
                meta.append((bi, d, p, sl, y_ref))
    results = [m + r for m, r in zip(meta, _round_robin(gens))]
    for bi, d, p, sl, y_ref, y, s_new in results:
        s_ref[bi, d, p] = s_new
        y_ref[bi, :, sl] = y


def _rwkv_scan(r, k, v, kh, ab, lw_f, lw_b, ctx_len):
    b, t, w = r.shape
    nc = t // CHUNK
    nctx = ctx_len // CHUNK
    nlat = nc - nctx

    def fwd_map(i, j):
        return (i, j, 0)

    def bwd_map(i, j):
        return (i, jnp.where(j < nctx, nctx - 1 - j, nc - 1 - (j - nctx)), 0)

    nb = math.gcd(b, SCAN_BATCH)
    blk = (nb, CHUNK, w)
    in_specs = [pl.BlockSpec(blk, fwd_map)] * 6 + [pl.BlockSpec(blk, bwd_map)] * 6
    out_specs = [
        pl.BlockSpec(blk, lambda i, j: (i, jnp.maximum(j - nctx, 0), 0)),
        pl.BlockSpec(blk, lambda i, j: (i, nlat - 1 - jnp.maximum(j - nctx, 0), 0)),
    ]
    return pl.pallas_call(
        functools.partial(_rwkv_kernel, nctx=nctx),
        grid=(b // nb, nc),
        in_specs=in_specs,
        out_specs=out_specs,
        out_shape=[jax.ShapeDtypeStruct((b, t - ctx_len, w), jnp.float32)] * 2,
        scratch_shapes=[pltpu.VMEM((nb, 2, NPAIR, PAIR, PAIR), jnp.float32)],
        compiler_params=pltpu.CompilerParams(
            dimension_semantics=("arbitrary", "arbitrary"), vmem_limit_bytes=VMEM_LIMIT),
        name="rwkv_scan",
    )(r, k, v, kh, ab, lw_f, r, k, v, kh, ab, lw_b)


def _scan_max(x, reverse):
    rows = lax.broadcasted_iota(jnp.int32, x.shape, 0)
    neg_inf = jnp.full((), -jnp.inf, jnp.float32)
    step = 1
    while step < CHUNK:
        if reverse:
            shifted = jnp.where(rows < CHUNK - step, pltpu.roll(x, CHUNK - step, 0), neg_inf)
        else:
            shifted = jnp.where(rows >= step, pltpu.roll(x, step, 0), neg_inf)
        x = jnp.maximum(x, shifted)
        step *= 2
    return x


def _mlstm_gates(gates, bias, tri, m_prev, is_f, reverse):
    gl = gates + bias
    act = jnp.where(is_f, jax.nn.log_sigmoid(gl), gl)
    b_all = _dot_exact_lhs(tri, act)
    li = pltpu.roll(act, H_M, 1)
    x = li - b_all
    m_inter = b_all + m_prev
    m_t = jnp.maximum(m_inter, b_all + _scan_max(x, reverse))
    b_last = b_all[0:1, :] if reverse else b_all[CHUNK - 1:CHUNK, :]
    g = b_last + x
    m_new = jnp.maximum(b_last + m_prev, jnp.max(g, axis=0, keepdims=True))
    wts = jnp.exp(g - m_new)
    decay = jnp.exp(b_last + m_prev - m_new)
    return b_all - m_t, jnp.exp(m_inter - m_t), jnp.exp(-m_t), x, wts, decay, m_new


def _mlstm_pair_chunk(q, k, v, pieces, cf, decay_g, c_mat, n_row, incl, m0, m1):
    stack = lambda a: jnp.concatenate([a * m0, a * m1], axis=0)
    q_st, k_st, v_st = stack(q), stack(k), stack(v)
    lane = lax.broadcasted_iota(jnp.int32, (1, PAIR), 1)
    zero_bf = jnp.zeros((), jnp.bfloat16)
    sel = lambda a: jnp.concatenate([jnp.where(lane == cf, a, zero_bf), jnp.where(lane == cf + 1, a, zero_bf)], 0)
    ones = jnp.ones((PAIR, PAIR), jnp.bfloat16)
    spread = [sel(a) for a in pieces[0]] + [sel(p[0]) for p in pieces[1:4]]
    bc = _dg(jnp.concatenate(spread, axis=0), ones, _NN)
    bm_bc = bc[:PAIR] + bc[PAIR:2 * PAIR] + bc[2 * PAIR:3 * PAIR]
    inter_bc, enm_bc, wts_bc = (bc[i * PAIR:(i + 1) * PAIR] for i in range(3, 6))
    r3 = _dg(ones, jnp.concatenate([sel(a) for a in pieces[4]], axis=0), _NT)
    r_mat = r3[:, :PAIR] + r3[:, PAIR:2 * PAIR] + r3[:, 2 * PAIR:]
    qk = _dot1(q_st, k_st, _NT)
    qc = _dot1(q_st, c_mat)
    yield
    neg_inf = jnp.full((), -jnp.inf, jnp.float32)
    s = qk * jnp.exp(jnp.where(incl, bm_bc + r_mat, neg_inf))
    wk = wts_bc * k_st
    sv = _dot1(s, v_st)
    kv = _dot1(wk, v_st, _TN)
    sums = _dot1(jnp.concatenate([q_st * n_row, s], axis=0), jnp.ones((PAIR, PAIR), jnp.float32))
    yield
    den = inter_bc * sums[:PAIR] + sums[PAIR:]
    h_st = (inter_bc * qc + sv) / jnp.maximum(jnp.abs(den), enm_bc)
    h = h_st[:CHUNK] + h_st[CHUNK:]
    decay_row = jnp.where(lane < D_HM, decay_g[:, cf:cf + 1], decay_g[:, cf + 1:cf + 2])
    c_new = decay_row * c_mat + kv
    n_new = decay_row * n_row + jnp.sum(wk, axis=0, keepdims=True)
    return h, c_new, n_new


def _mlstm_kernel(qf, kf, vf, gf, qb, kb, vb, gb, bias_ref, hf_ref, hb_ref, c_ref, n_ref, m_ref, *, nctx):
    del nctx
    j = pl.program_id(1)

    @pl.when(j == 0)
    def _():
        c_ref[...] = jnp.zeros_like(c_ref)
        n_ref[...] = jnp.zeros_like(n_ref)
        m_ref[...] = jnp.zeros_like(m_ref)

    row = lax.broadcasted_iota(jnp.int32, (PAIR, PAIR), 0)
    col = lax.broadcasted_iota(jnp.int32, (PAIR, PAIR), 1)
    same = (row // CHUNK) == (col // CHUNK)
    r64 = lax.broadcasted_iota(jnp.int32, (CHUNK, CHUNK), 0)
    c64 = lax.broadcasted_iota(jnp.int32, (CHUNK, CHUNK), 1)
    lane = lax.broadcasted_iota(jnp.int32, (1, PAIR), 1)
    m0 = (lane < D_HM).astype(jnp.float32)
    m1 = 1.0 - m0
    is_f = (lane % (2 * H_M)) >= H_M
    dirs = (
        (qf, kf, vf, gf, hf_ref, (c64 <= r64), same & (col <= row)),
        (qb, kb, vb, gb, hb_ref, (c64 >= r64), same & (col >= row)),
    )
    nb = qf.shape[0]
    states = [[[(c_ref[bi, d, p], n_ref[bi, d, p]) for p in range(NPAIR)] for d in range(2)] for bi in range(nb)]
    gens, meta, m_news = [], [], []
    lead = lambda a: (a.astype(jnp.bfloat16),)
    for d, (q_ref, k_ref, v_ref, g_ref, h_ref, tri, incl) in enumerate(dirs):
        tri = tri.astype(jnp.float32).astype(jnp.bfloat16)
        for bi in range(nb):
            bm, inter, enm, x, wts, decay_g, m_new = _mlstm_gates(g_ref[bi], bias_ref[...], tri, m_ref[bi, d], is_f,
                                                                  d == 1)
            m_news.append((bi, d, m_new))
            pieces = [_split3(bm), lead(inter), lead(enm), lead(wts), _split3(x)]
            for p in range(NPAIR):
                sl = slice(p * PAIR, (p + 1) * PAIR)
                cf = d * 2 * H_M + H_M + 2 * p
                gens.append(_mlstm_pair_chunk(q_ref[bi, :, sl], k_ref[bi, :, sl], v_ref[bi, :, sl], pieces, cf,
                                               decay_g, *states[bi][d][p], incl, m0, m1))
                meta.append((bi, d, p, sl, h_ref))
    for (bi, d, p, sl, h_ref), (h, c_new, n_new) in zip(meta, _round_robin(gens)):
        c_ref[bi, d, p] = c_new
        n_ref[bi, d, p] = n_new
        h_ref[bi, :, sl] = h
    for bi, d, m_new in m_news:
        m_ref[bi, d] = m_new


def _mlstm_scan(q, k, pn, bias_row, ctx_len):
    v = gates = pn
    b, t, w = q.shape
    nc = t // CHUNK
    nctx = ctx_len // CHUNK
    nlat = nc - nctx

    def fwd_map(i, j):
        return (i, j, 0)

    def bwd_map(i, j):
        return (i, jnp.where(j < nctx, nctx - 1 - j, nc - 1 - (j - nctx)), 0)

    nb = math.gcd(b, SCAN_BATCH)
    blk = (nb, CHUNK, w)
    gblk = (nb, CHUNK, 128)
    gate_blk = (2 * W_M) // 128
    gate_of = lambda m: (lambda i, j: m(i, j)[:2] + (gate_blk,))
    in_specs = ([pl.BlockSpec(blk, fwd_map)] * 3 + [pl.BlockSpec(gblk, gate_of(fwd_map))]
                + [pl.BlockSpec(blk, bwd_map)] * 3 + [pl.BlockSpec(gblk, gate_of(bwd_map))]
                + [pl.BlockSpec((1, 128), lambda i, j: (0, 0))])
    out_specs = [
        pl.BlockSpec(blk, lambda i, j: (i, jnp.maximum(j - nctx, 0), 0)),
        pl.BlockSpec(blk, lambda i, j: (i, nlat - 1 - jnp.maximum(j - nctx, 0), 0)),
    ]
    return pl.pallas_call(
        functools.partial(_mlstm_kernel, nctx=nctx),
        grid=(b // nb, nc),
        in_specs=in_specs,
        out_specs=out_specs,
        out_shape=[jax.ShapeDtypeStruct((b, t - ctx_len, w), jnp.float32)] * 2,
        scratch_shapes=[pltpu.VMEM((nb, 2, NPAIR, PAIR, PAIR), jnp.float32),
                        pltpu.VMEM((nb, 2, NPAIR, 1, PAIR), jnp.float32),
                        pltpu.VMEM((nb, 2, 1, 128), jnp.float32)],
        compiler_params=pltpu.CompilerParams(
            dimension_semantics=("arbitrary", "arbitrary"), vmem_limit_bytes=VMEM_LIMIT),
        name="mlstm_scan",
    )(q, k, v, gates, q, k, v, gates, bias_row)


ROUTE_W = 128
N_ROUTE = N_GROUPS + N_EXPERTS


def _ln_rows(z, eps):
    mu = jnp.mean(z, -1, keepdims=True)
    zc = z - mu
    var = jnp.mean(zc * zc, -1, keepdims=True)
    return zc * lax.rsqrt(var + eps)


def _headnorm_mxu(z, bd2, eps):
    mu = _head_sums(z, bd2) * (1.0 / D_HM)
    zc = z - mu
    var = _head_sums(zc * zc, bd2) * (1.0 / D_HM)
    return zc * lax.rsqrt(var + eps)


def _outproj_kernel(hf, hb, mo, yf, yb, gg, bonus, x_ref, mod_ref, wout_ref, hpar_ref, ln_ref, rtw_ref, rtb_ref,
                    bd_ref, x1_ref, tok_ref, route_ref, ridx_ref, cnt_ref, carry_ref):
    i, j = pl.program_id(0), pl.program_id(1)

    @pl.when((i == 0) & (j == 0))
    def _():
        carry_ref[...] = jnp.zeros_like(carry_ref)

    avg = bd_ref[...]
    out_m = jax.nn.sigmoid(mo[0]) * (_headnorm_mxu(hf[0] + hb[0], avg, LN_EPS) * hpar_ref[0:1, :])
    y = _headnorm_mxu(yf[0] + yb[0], avg, GN_EPS) * hpar_ref[1:2, :] + hpar_ref[2:3, :]
    out_r = (y + bonus[0]) * gg[0]
    proj = (jnp.dot(out_m.astype(jnp.bfloat16), wout_ref[:W_M, :], preferred_element_type=jnp.float32)
            + jnp.dot(out_r.astype(jnp.bfloat16), wout_ref[W_M:, :], preferred_element_type=jnp.float32))
    g1, sh2, sc2 = mod_ref[0, 2:3, :], mod_ref[0, 3:4, :], mod_ref[0, 4:5, :]
    x1 = _ln_rows(ALPHA * x_ref[0] + g1 * proj, LN_EPS) * ln_ref[0:1, :] + ln_ref[1:2, :]
    x1_ref[0] = x1
    tok = _ln_rows(x1, LN_EPS) * (1.0 + sc2) + sh2
    tok_ref[0] = tok

    logits = _dot3_pre(tok, rtw_ref[...]) + rtb_ref[...]
    lane = lax.broadcasted_iota(jnp.int32, (ROW_TILE, ROUTE_W), 1)
    neg_inf = jnp.full((), -jnp.inf, jnp.float32)
    big = jnp.int32(ROUTE_W)
    is_g = lane < N_GROUPS
    lg = jnp.where(is_g, logits, neg_inf)
    gmax = jnp.max(lg, axis=1, keepdims=True)
    grp = jnp.min(jnp.where(lg == gmax, lane, big), axis=1, keepdims=True)
    p_top = 1.0 / jnp.sum(jnp.where(is_g, jnp.exp(lg - gmax), 0.0), axis=1, keepdims=True)
    in_grp = (lane >= N_GROUPS) & (lane < N_ROUTE) & ((lane - N_GROUPS) // E_PER_GROUP == grp)
    le = jnp.where(in_grp, logits, neg_inf)
    v1 = jnp.max(le, axis=1, keepdims=True)
    i1 = jnp.min(jnp.where(le == v1, lane, big), axis=1, keepdims=True)
    le2 = jnp.where(lane == i1, neg_inf, le)
    v2 = jnp.max(le2, axis=1, keepdims=True)
    i2 = jnp.min(jnp.where(le2 == v2, lane, big), axis=1, keepdims=True)
    e21 = jnp.exp(v2 - v1)
    w1 = (1.0 / (1.0 + e21)) * p_top
    w2 = (e21 / (1.0 + e21)) * p_top

    sel1, sel2 = lane == i1, lane == i2
    onehot = jnp.where(sel1 | sel2, 1.0, 0.0)
    tr = lax.broadcasted_iota(jnp.int32, (ROW_TILE, ROW_TILE), 0)
    tc = lax.broadcasted_iota(jnp.int32, (ROW_TILE, ROW_TILE), 1)
    strict = jnp.where(tc < tr, 1.0, 0.0).astype(jnp.bfloat16)
    before = _dg(strict, onehot.astype(jnp.bfloat16), _NN) + carry_ref[...]
    r1 = jnp.sum(jnp.where(sel1, before, 0.0), axis=1, keepdims=True)
    r2 = jnp.sum(jnp.where(sel2, before, 0.0), axis=1, keepdims=True)
    carry_ref[...] = carry_ref[...] + jnp.sum(onehot, axis=0, keepdims=True)
    e1 = (i1 - N_GROUPS).astype(jnp.float32)
    e2 = (i2 - N_GROUPS).astype(jnp.float32)
    route = jnp.where(lane == 0, e1, jnp.where(lane == 1, e2, jnp.where(lane == 2, r1, jnp.where(
        lane == 3, r2, jnp.where(lane == 4, w1, jnp.where(lane == 5, w2, 0.0))))))
    route_ref[0] = route
    ridx_ref[0] = route.T[0:8, :].astype(jnp.int32)
    cnt_ref[...] = jnp.broadcast_to(carry_ref[...], cnt_ref.shape)


def _outproj(h_f, h_b, pn, y_f, y_b, gg, bonus, x, mod3, wout_bf16, hpar, lnpar, rtw, rtb, bd, ctx_len):
    b, seq, d = x.shape
    nt = seq // ROW_TILE
    off = ctx_len // ROW_TILE
    lat = lambda w: pl.BlockSpec((1, ROW_TILE, w), lambda i, j: (i, j, 0))
    full = lambda shp: pl.BlockSpec(shp, lambda i, j: tuple(0 for _ in shp))
    in_specs = [
        lat(W_M), lat(W_M),
        pl.BlockSpec((1, ROW_TILE, W_M), lambda i, j: (i, j + off, 1)),
        lat(W_R), lat(W_R),
        pl.BlockSpec((1, ROW_TILE, W_R), lambda i, j: (i, j + off, 0)),
        pl.BlockSpec((1, ROW_TILE, W_R), lambda i, j: (i, j + off, 0)),
        lat(d),
        pl.BlockSpec((1, 6, d), lambda i, j: (i, 0, 0)),
        full((MIX_W, d)), full((8, W_M)), full((8, d)), full((3 * d, ROUTE_W)), full((1, ROUTE_W)),
        full((2 * HEAD_BLK, HEAD_BLK)),
    ]
    out_specs = [
        lat(d), lat(d), lat(ROUTE_W),
        pl.BlockSpec((1, 8, ROW_TILE), lambda i, j: (i * nt + j, 0, 0)),
        full((8, ROUTE_W)),
    ]
    out_shape = [
        jax.ShapeDtypeStruct((b, seq, d), jnp.float32),
        jax.ShapeDtypeStruct((b, seq, d), jnp.float32),
        jax.ShapeDtypeStruct((b, seq, ROUTE_W), jnp.float32),
        jax.ShapeDtypeStruct((b * nt, 8, ROW_TILE), jnp.int32),
        jax.ShapeDtypeStruct((8, ROUTE_W), jnp.float32),
    ]
    return pl.pallas_call(
        _outproj_kernel,
        grid=(b, nt),
        in_specs=in_specs,
        out_specs=out_specs,
        out_shape=out_shape,
        scratch_shapes=[pltpu.VMEM((1, ROUTE_W), jnp.float32)],
        compiler_params=pltpu.CompilerParams(
            dimension_semantics=("arbitrary", "arbitrary"), vmem_limit_bytes=VMEM_LIMIT),
        name="outproj_router",
    )(h_f, h_b, pn, y_f, y_b, gg, bonus, x, mod3, wout_bf16, hpar, lnpar, rtw, rtb, bd)


DEST_TILES = 8


def _dest_kernel(starts_ref, ridx_ref, dest_ref):
    blk = ridx_ref[...]
    dest = pltpu.roll(blk, 8 - TOP_K_IN_GROUP, 1)
    for e in range(N_EXPERTS):
        dest = dest + jnp.where(blk == e, starts_ref[e], 0)
    dest_ref[...] = dest


def _moe_dest(starts, ridx):
    nt = ridx.shape[0]
    per_step = math.gcd(nt, DEST_TILES)
    blk = pl.BlockSpec((per_step, 8, ROW_TILE), lambda i, s: (i, 0, 0))
    return pl.pallas_call(
        _dest_kernel,
        grid_spec=pltpu.PrefetchScalarGridSpec(num_scalar_prefetch=1, grid=(nt // per_step,),
                                               in_specs=[blk], out_specs=blk),
        out_shape=jax.ShapeDtypeStruct(ridx.shape, jnp.int32),
        compiler_params=pltpu.CompilerParams(dimension_semantics=("arbitrary",), vmem_limit_bytes=VMEM_LIMIT),
        name="moe_dest",
    )(starts, ridx)


SUB = 8
SUB_SHIFT = 3
TILE_GROUPS = ROW_TILE // SUB


def _scatter_kernel(dest_ref, tok_ref, xs_in_ref, xs_ref, sem):
    del xs_in_ref

    def issue(g, c):
        for k in range(SUB):
            for slot in range(TOP_K_IN_GROUP):
                dst = dest_ref[slot * ROW_TILE + g * SUB + k]
                pltpu.make_async_copy(tok_ref.at[g, pl.ds(k, 1), :],
                                      xs_ref.at[dst >> SUB_SHIFT, pl.ds(dst & (SUB - 1), 1), :], sem).start()
        return c

    lax.fori_loop(0, TILE_GROUPS, issue, 0)
    for half in range(TOP_K_IN_GROUP):
        pltpu.make_async_copy(tok_ref, xs_ref.at[pl.ds(half * TILE_GROUPS, TILE_GROUPS)], sem).wait()


def _moe_scatter(dest_flat, tok2d, n_rows):
    n, d = tok2d.shape
    xs_zero = jnp.zeros((n_rows // SUB, SUB, d), tok2d.dtype)
    nt = n // ROW_TILE
    xs = pl.pallas_call(
        _scatter_kernel,
        grid=(nt,),
        in_specs=[
            pl.BlockSpec((SUB * ROW_TILE,), lambda i: (i,), memory_space=pltpu.SMEM),
            pl.BlockSpec((TILE_GROUPS, SUB, d), lambda i: (i, 0, 0)),
            pl.BlockSpec(memory_space=pl.ANY),
        ],
        out_specs=pl.BlockSpec(memory_space=pl.ANY),
        scratch_shapes=[pltpu.SemaphoreType.DMA(())],
        out_shape=jax.ShapeDtypeStruct(xs_zero.shape, tok2d.dtype),
        input_output_aliases={2: 0},
        compiler_params=pltpu.CompilerParams(dimension_semantics=("arbitrary",), vmem_limit_bytes=VMEM_LIMIT),
        name="moe_scatter",
    )(dest_flat, tok2d.reshape(n // SUB, SUB, d), xs_zero)
    return xs.reshape(n_rows, d)


def _expert_kernel(blk_exp_ref, valid_ref, xs_ref, wg_ref, wu_ref, wd_ref, ys_ref):
    i = pl.program_id(0)
    valid = valid_ref[i]

    @pl.when(valid > 0)
    def _():
        xb = xs_ref[...].astype(jnp.bfloat16)
        hg = jnp.dot(xb, wg_ref[0], preferred_element_type=jnp.float32)
        hu = jnp.dot(xb, wu_ref[0], preferred_element_type=jnp.float32)
        hb = (jax.nn.silu(hg) * hu).astype(jnp.bfloat16)
        ys_ref[...] = jnp.dot(hb, wd_ref[0], preferred_element_type=jnp.float32)

    @pl.when(valid == 0)
    def _():
        ys_ref[...] = jnp.zeros_like(ys_ref)


def _moe_experts(blk_exp, blk_valid, xs, wg, wu, wd):
    nrow, d = xs.shape
    de = wg.shape[2]
    return pl.pallas_call(
        _expert_kernel,
        grid_spec=pltpu.PrefetchScalarGridSpec(
            num_scalar_prefetch=2,
            grid=(nrow // MOE_BLOCK,),
            in_specs=[
                pl.BlockSpec((MOE_BLOCK, d), lambda i, be, nu: (i, 0)),
                pl.BlockSpec((1, d, de), lambda i, be, nu: (be[i], 0, 0)),
                pl.BlockSpec((1, d, de), lambda i, be, nu: (be[i], 0, 0)),
                pl.BlockSpec((1, de, d), lambda i, be, nu: (be[i], 0, 0)),
            ],
            out_specs=pl.BlockSpec((MOE_BLOCK, d), lambda i, be, nu: (i, 0)),
        ),
        out_shape=jax.ShapeDtypeStruct((nrow, d), jnp.float32),
        compiler_params=pltpu.CompilerParams(dimension_semantics=("arbitrary",), vmem_limit_bytes=VMEM_LIMIT),
        name="moe_experts",
    )(blk_exp, blk_valid, xs, wg, wu, wd)


def _combine_kernel(dest_ref, dest_next_ref, route_ref, x1_ref, mod_ref, ln_ref, ys_ref, out_ref,
                    ybuf, sems):
    i = pl.program_id(0)
    n = pl.num_programs(0)
    cur = i % 2

    def gather(idx_ref, buf):
        def issue(r, c):
            for slot in range(TOP_K_IN_GROUP):
                src = idx_ref[0, slot, r]
                pltpu.make_async_copy(ys_ref.at[pl.ds(src, 1), :],
                                      ybuf.at[buf, pl.ds(slot * ROW_TILE + r, 1), :], sems.at[buf]).start()
            return c

        lax.fori_loop(0, ROW_TILE, issue, 0, unroll=DMA_UNROLL)

    @pl.when(i == 0)
    def _():
        gather(dest_ref, 0)

    @pl.when(i + 1 < n)
    def _():
        gather(dest_next_ref, 1 - cur)

    pltpu.make_async_copy(ys_ref.at[pl.ds(0, TOP_K_IN_GROUP * ROW_TILE), :], ybuf.at[cur], sems.at[cur]).wait()
    route = route_ref[0]
    ffn = ybuf[cur, 0:ROW_TILE, :] * route[:, 4:5] + ybuf[cur, ROW_TILE:2 * ROW_TILE, :] * route[:, 5:6]
    g2 = mod_ref[0, 5:6, :]
    out_ref[0] = _ln_rows(ALPHA * x1_ref[0] + g2 * ffn, LN_EPS) * ln_ref[2:3, :] + ln_ref[3:4, :]


def _moe_combine(dest, route, x1, mod3, lnpar, ys):
    b, seq, d = x1.shape
    nt = seq // ROW_TILE
    n = b * nt
    tile = lambda w: pl.BlockSpec((1, ROW_TILE, w), lambda i: (i // nt, i % nt, 0))
    return pl.pallas_call(
        _combine_kernel,
        grid=(n,),
        in_specs=[
            pl.BlockSpec((1, 8, ROW_TILE), lambda i: (i, 0, 0), memory_space=pltpu.SMEM),
            pl.BlockSpec((1, 8, ROW_TILE), lambda i: (jnp.minimum(i + 1, n - 1), 0, 0), memory_space=pltpu.SMEM),
            tile(ROUTE_W),
            tile(d),
            pl.BlockSpec((1, 6, d), lambda i: (i // nt, 0, 0)),
            pl.BlockSpec((8, d), lambda i: (0, 0)),
            pl.BlockSpec(memory_space=pl.ANY),
        ],
        out_specs=tile(d),
        scratch_shapes=[pltpu.VMEM((2, TOP_K_IN_GROUP * ROW_TILE, d), jnp.float32), pltpu.SemaphoreType.DMA((2,))],
        out_shape=jax.ShapeDtypeStruct((b, seq, d), jnp.float32),
        compiler_params=pltpu.CompilerParams(dimension_semantics=("arbitrary",), vmem_limit_bytes=VMEM_LIMIT),
        name="moe_combine",
    )(dest, dest, route, x1, mod3, lnpar, ys)


def _moe_plan(cnt, n_pairs):
    counts = cnt[0, N_GROUPS:N_ROUTE].astype(jnp.int32)
    padded = (counts + MOE_BLOCK - 1) // MOE_BLOCK * MOE_BLOCK
    pad_ends = jnp.cumsum(padded)
    pad_starts = pad_ends - padded
    n_blk = -(-n_pairs // MOE_BLOCK) + N_EXPERTS
    blk_row0 = jnp.arange(n_blk, dtype=jnp.int32) * MOE_BLOCK
    blk_exp = jnp.minimum(jnp.sum((pad_ends[None, :] <= blk_row0[:, None]).astype(jnp.int32), axis=1), N_EXPERTS - 1)
    blk_valid = jnp.clip(counts[blk_exp] - (blk_row0 - pad_starts[blk_exp]), 0, MOE_BLOCK)
    blk_valid = jnp.where(blk_row0 < pad_ends[-1], blk_valid, 0).astype(jnp.int32)
    return pad_starts.astype(jnp.int32), blk_exp, blk_valid, n_blk


def kernel(x, c, ctx, c_ctx, w_ada, b_ada, w_in, conv_w, m_bias_i, m_bias_f, m_norm_w, r_w0, r_wB, r_a0, r_aB,
           r_gB, r_kk, r_ka, r_bonus, r_norm_w, r_norm_b, w_out, ln1_g, ln1_b, ln2_g, ln2_b, rt_g, rt_g_b, rt_e,
           rt_e_b, ex_gate, ex_up, ex_down):
    assert w_ada.shape[0] == DEPTH
    bsz, seq, d = x.shape
    ctx_len = ctx.shape[1]
    assert ctx_len == ROW_TILE and seq % ROW_TILE == 0
    assert seq % GRID_W == 0 and ROW_TILE % GRID_W == 0
    l = 0
    zrow = lambda n, w: jnp.zeros((n, w), jnp.float32)
    mrows = -(-(bsz + 1) // 8) * 8
    c_pad = jnp.concatenate([c, c_ctx[None, :], zrow(mrows - bsz - 1, d)], 0)
    mod_all = _ada(c_pad, w_ada[l], b_ada[l][None, :])[:bsz + 1]
    mod3 = mod_all.reshape(bsz + 1, 6, d)
    pc, pn = _inproj(ctx, x, mod3, _pad_w_in(w_in[l]))
    cw = jnp.concatenate([conv_w[l].reshape(CONV_K * CONV_K, CONV_CH), zrow(16 - CONV_K * CONV_K, CONV_CH)], 0)
    rpar = jnp.concatenate([r_w0[l], r_a0[l][None, :], r_kk[l][None, :], r_ka[l][None, :],
                            r_bonus[l].reshape(1, W_R), zrow(2, W_R)], 0)
    wbf = jnp.concatenate([r_wB[l][0], zrow(W_LORA, W_R)], 0)
    wbb = jnp.concatenate([zrow(W_LORA, W_R), r_wB[l][1]], 0)
    ab_mat = jnp.concatenate([r_aB[l], zrow(128 - A_LORA, W_R)], 0)
    bd = _head_sum_operator(D_HR)
    q, k, rr, kmod, rv, kh, kha, logw_f, logw_b, gg, bonus = _prep(
        pc, pn, cw, rpar, _stack3(wbf), _stack3(wbb), _stack3(ab_mat), _stack3(r_gB[l]), bd)
    bias_row = jnp.concatenate([m_bias_i[l][0], m_bias_f[l][0], m_bias_i[l][1], m_bias_f[l][1],
                                jnp.zeros((128 - 4 * H_M,), jnp.float32)])[None, :]
    h_f, h_b = _mlstm_scan(q, k, pn, bias_row, ctx_len)
    y_f, y_b = _rwkv_scan(rr, kmod, rv, kh, kha, logw_f, logw_b, ctx_len)
    hpar = jnp.zeros((8, W_M), jnp.float32).at[0].set(m_norm_w[l]).at[1].set(r_norm_w[l]).at[2].set(r_norm_b[l])
    lnpar = jnp.zeros((8, d), jnp.float32).at[0].set(ln1_g[l]).at[1].set(ln1_b[l]).at[2].set(ln2_g[l]).at[3].set(
        ln2_b[l])
    rtw = jnp.concatenate([rt_g[l], rt_e[l], jnp.zeros((d, ROUTE_W - N_ROUTE), jnp.float32)], axis=1)
    rtb = jnp.concatenate([rt_g_b[l], rt_e_b[l], jnp.zeros((ROUTE_W - N_ROUTE,), jnp.float32)])[None, :]
    x1, tok, route, ridx, cnt = _outproj(h_f, h_b, pn, y_f, y_b, gg, bonus, x, mod3, w_out[l].astype(jnp.bfloat16),
                                         hpar, lnpar, _stack3(rtw), rtb, bd, ctx_len)
    n_tok = bsz * seq
    starts, blk_exp, blk_valid, n_blk = _moe_plan(cnt, n_tok * TOP_K_IN_GROUP)
    dest = _moe_dest(starts, ridx)
    dest_flat = dest.reshape(-1)
    xs = _moe_scatter(dest_flat, tok.reshape(n_tok, d), n_blk * MOE_BLOCK)
    ys = _moe_experts(blk_exp, blk_valid, xs, ex_gate[l].astype(jnp.bfloat16), ex_up[l].astype(jnp.bfloat16),
                      ex_down[l].astype(jnp.bfloat16))
    return _moe_combine(dest, route, x1, mod3, lnpar, ys)
```

```python
import functools
import math

import jax
import jax.numpy as jnp
from jax import lax
import numpy as np
from jax.experimental import pallas as pl
from jax.experimental.pallas import tpu as pltpu

H_M = 8
D_HM = 64
W_M = H_M * D_HM
H_R = 8
D_HR = 64
W_R = H_R * D_HR
MIX_W = W_M + W_R
W_LORA = 64
A_LORA = 64
G_LORA = 128
CONV_K = 3
CONV_CH = 2 * W_M + 3 * W_R
SECTION_WIDTHS = (W_M, W_M, W_R, W_R, W_R, W_M, W_M, 4 * H_M, W_LORA, W_LORA, A_LORA, G_LORA)
IN_COLS = sum(SECTION_WIDTHS)
SPLIT_POINTS = tuple(int(v) for v in np.cumsum(SECTION_WIDTHS)[:-1])
CHUNK = 64
N_GROUPS = 4
E_PER_GROUP = 8
N_EXPERTS = N_GROUPS * E_PER_GROUP
TOP_K_IN_GROUP = 2
D_EXPERT = 512
MOE_BLOCK = 256
DEPTH = 1
ALPHA = (2.0 * DEPTH) ** 0.25
DECAY_SCALE = math.exp(-0.5)
LN_EPS = 1e-6
GN_EPS = 64e-5

ROW_TILE = 256
NONCONV_W = 1536
LORA_TAIL_W = 384
VMEM_LIMIT = 56 * 1024 * 1024
SCAN_BATCH = 4


def _inproj_kernel(ctx_ref, x_ref, mod_ref, w_ref, oc_ref, on_ref):
    j = pl.program_id(1)

    def body(src_ref):
        z = src_ref[0]
        mu = jnp.mean(z, -1, keepdims=True)
        zc = z - mu
        var = jnp.mean(zc * zc, -1, keepdims=True)
        y = zc * lax.rsqrt(var + LN_EPS)
        h = (y * (1.0 + mod_ref[0, 1:2, :]) + mod_ref[0, 0:1, :]).astype(jnp.bfloat16)
        oc_ref[0] = jnp.dot(h, w_ref[:, :CONV_CH], preferred_element_type=jnp.float32)
        on_ref[0] = jnp.dot(h, w_ref[:, CONV_CH:], preferred_element_type=jnp.float32)

    @pl.when(j == 0)
    def _():
        body(ctx_ref)

    @pl.when(j > 0)
    def _():
        body(x_ref)


def _inproj(ctx, x, mod_all, w_pad):
    b, seq, d = x.shape
    nt = (ctx.shape[1] + seq) // ROW_TILE
    t_all = ctx.shape[1] + seq
    return pl.pallas_call(
        _inproj_kernel,
        grid=(b, nt),
        in_specs=[
            pl.BlockSpec((1, ROW_TILE, d), lambda i, j: (i, 0, 0)),
            pl.BlockSpec((1, ROW_TILE, d), lambda i, j: (i, jnp.maximum(j - 1, 0), 0)),
            pl.BlockSpec((1, 6, d), lambda i, j: (jnp.where(j == 0, b, i), 0, 0)),
            pl.BlockSpec((d, CONV_CH + NONCONV_W), lambda i, j: (0, 0)),
        ],
        out_specs=[
            pl.BlockSpec((1, ROW_TILE, CONV_CH), lambda i, j: (i, j, 0)),
            pl.BlockSpec((1, ROW_TILE, NONCONV_W), lambda i, j: (i, j, 0)),
        ],
        out_shape=[
            jax.ShapeDtypeStruct((b, t_all, CONV_CH), jnp.float32),
            jax.ShapeDtypeStruct((b, t_all, NONCONV_W), jnp.float32),
        ],
        compiler_params=pltpu.CompilerParams(
            dimension_semantics=("arbitrary", "arbitrary"), vmem_limit_bytes=VMEM_LIMIT),
        name="inproj",
    )(ctx, x, mod_all, w_pad)


def _pad_w_in(w_in):
    d = w_in.shape[0]
    sp = SPLIT_POINTS
    z = lambda n: jnp.zeros((d, n), w_in.dtype)
    gates = w_in[:, sp[6]:sp[7]]
    lw = w_in[:, sp[7]:sp[9]]
    la = w_in[:, sp[9]:sp[10]]
    lg = w_in[:, sp[10]:]
    return jnp.concatenate([w_in[:, :sp[6]], gates, z(96), lw, la, z(64), lg], axis=1).astype(jnp.bfloat16)


PAIR = 2 * D_HR
NPAIR = H_R // 2
_NN = (((1,), (0,)), ((), ()))
_NT = (((1,), (1,)), ((), ()))
_TN = (((0,), (0,)), ((), ()))


def _split2(a):
    hi = a.astype(jnp.bfloat16)
    lo = (a - hi.astype(jnp.float32)).astype(jnp.bfloat16)
    return hi, lo


def _split3(a):
    a1 = a.astype(jnp.bfloat16)
    r1 = a - a1.astype(jnp.float32)
    a2 = r1.astype(jnp.bfloat16)
    a3 = (r1 - a2.astype(jnp.float32)).astype(jnp.bfloat16)
    return a1, a2, a3


def _dg(a, b, dims):
    return lax.dot_general(a, b, dims, preferred_element_type=jnp.float32)


def _dot3(a, b, dims=_NN):
    ah, al = _split2(a)
    bh, bl = _split2(b)
    return _dg(ah, bh, dims) + _dg(ah, bl, dims) + _dg(al, bh, dims)


def _dot1(a, b, dims=_NN):
    return _dg(a.astype(jnp.bfloat16), b.astype(jnp.bfloat16), dims)


def _stack3(w):
    hi = w.astype(jnp.bfloat16)
    lo = (w - hi.astype(jnp.float32)).astype(jnp.bfloat16)
    return jnp.concatenate([hi, lo, hi], axis=0)


def _dot3_pre(a, w3):
    ah, al = _split2(a)
    return _dg(jnp.concatenate([ah, ah, al], axis=1), w3, _NN)


HEAD_BLK = 256


def _head_sum_operator(head):
    r = np.arange(HEAD_BLK) // head
    bd = (r[:, None] == r[None, :]).astype(np.float32)
    return jnp.asarray(np.concatenate([bd, bd], axis=0), jnp.bfloat16)


def _head_sums(z, bd2):
    hi, lo = _split2(z)
    out = []
    for c in range(0, z.shape[1], HEAD_BLK):
        lhs = jnp.concatenate([hi[:, c:c + HEAD_BLK], lo[:, c:c + HEAD_BLK]], axis=1)
        out.append(_dg(lhs, bd2, _NN))
    return jnp.concatenate(out, axis=1)


def _dot_exact_lhs(a_bf16, b, dims=_NN):
    b1, b2, b3 = _split3(b)
    return _dg(a_bf16, b1, dims) + _dg(a_bf16, b2, dims) + _dg(a_bf16, b3, dims)


def _ada_kernel(c_ref, w_ref, b_ref, o_ref):
    o_ref[...] = _dot3(jax.nn.silu(c_ref[...]), w_ref[...]) + b_ref[...]


def _ada(c_pad, w, bias):
    m, d = c_pad.shape
    n = w.shape[1]
    return pl.pallas_call(
        _ada_kernel,
        grid=(n // d,),
        in_specs=[pl.BlockSpec((m, d), lambda j: (0, 0)),
                  pl.BlockSpec((d, d), lambda j: (0, j)),
                  pl.BlockSpec((1, d), lambda j: (0, j))],
        out_specs=pl.BlockSpec((m, d), lambda j: (0, j)),
        out_shape=jax.ShapeDtypeStruct((m, n), jnp.float32),
        compiler_params=pltpu.CompilerParams(dimension_semantics=("arbitrary",), vmem_limit_bytes=VMEM_LIMIT),
        name="adaln",
    )(c_pad, w, bias)


GRID_W = 64
HALO = GRID_W


def _prep_kernel(top_ref, main_ref, bot_ref, pn_ref, cw_ref, rpar_ref, wbf_ref, wbb_ref, ab_ref, gb_ref, bd_ref,
                 q_ref, k_ref, r_ref, km_ref, v_ref, kh_ref, abo_ref, lwf_ref, lwb_ref, gg_ref, bon_ref, xbuf):
    j = pl.program_id(1)
    nt = pl.num_programs(1)
    is_ctx = j == 0
    top_ok = j >= 2
    bot_ok = (j >= 1) & (j < nt - 1)
    l_idx = lax.broadcasted_iota(jnp.int32, (ROW_TILE, 1), 0)
    col = jnp.where(is_ctx, l_idx, l_idx % GRID_W)
    left_ok = col != 0
    right_ok = col != jnp.where(is_ctx, ROW_TILE - 1, GRID_W - 1)
    vert = jnp.where(is_ctx, 0.0, 1.0)
    sec = {}
    for s in range(5):
        sl = slice(s * W_M, (s + 1) * W_M)
        xbuf[0:HALO, :] = jnp.where(top_ok, top_ref[0, :, sl], 0.0)
        xbuf[HALO:HALO + ROW_TILE, :] = main_ref[0, :, sl]
        xbuf[HALO + ROW_TILE:2 * HALO + ROW_TILE, :] = jnp.where(bot_ok, bot_ref[0, :, sl], 0.0)
        cols = [None, None, None]
        for dr in range(3):
            base = HALO + GRID_W * (dr - 1)
            w3 = cw_ref[3 * dr:3 * dr + 3, sl] if dr == 1 else cw_ref[3 * dr:3 * dr + 3, sl] * vert
            xrow = xbuf[base:base + ROW_TILE, :]
            for dc in range(3):
                term = xrow * w3[dc:dc + 1]
                cols[dc] = term if cols[dc] is None else cols[dc] + term
        left = jnp.where(left_ok, pltpu.roll(cols[0], 1, 0), 0.0)
        right = jnp.where(right_ok, pltpu.roll(cols[2], ROW_TILE - 1, 0), 0.0)
        sec[s] = left + cols[1] + right
    q_ref[0] = jax.nn.silu(sec[0]).astype(q_ref.dtype)
    k_ref[0] = (jax.nn.silu(sec[1]) * (D_HM ** -0.5)).astype(k_ref.dtype)
    rr, rk, rv = sec[2], sec[3], sec[4]
    r_ref[0] = rr
    v_ref[0] = rv.astype(v_ref.dtype)
    lw = jnp.tanh(pn_ref[0, :, 0:128])
    lwf_ref[0] = -DECAY_SCALE * jax.nn.sigmoid(rpar_ref[0:1, :] + _dot3_pre(lw, wbf_ref[...]))
    lwb_ref[0] = -DECAY_SCALE * jax.nn.sigmoid(rpar_ref[1:2, :] + _dot3_pre(lw, wbb_ref[...]))
    a = jax.nn.sigmoid(rpar_ref[2:3, :] + _dot3_pre(pn_ref[0, :, 128:256], ab_ref[...]))
    gg_ref[0] = _dot3_pre(jax.nn.sigmoid(pn_ref[0, :, 256:384]), gb_ref[...])
    kap = rk * rpar_ref[3:4, :]
    norm = jnp.sqrt(_head_sums(kap * kap, bd_ref[...]))
    kh = kap / jnp.maximum(norm, 1e-12)
    kmod = rk * (1.0 + (a - 1.0) * rpar_ref[4:5, :])
    kh_ref[0] = kh
    abo_ref[0] = kh * a
    km_ref[0] = kmod
    bon_ref[0] = _head_sums(rr * kmod * rpar_ref[5:6, :], bd_ref[...]) * rv


def _prep(pc, pn, cw, rpar, wbf, wbb, ab, gb, bd):
    b, t, _ = pc.shape
    nt = t // ROW_TILE
    per = ROW_TILE // HALO
    nh = t // HALO
    w = W_M
    full = lambda shp: pl.BlockSpec(shp, lambda i, j: tuple(0 for _ in shp))
    in_specs = [
        pl.BlockSpec((1, HALO, CONV_CH), lambda i, j: (i, jnp.maximum(j * per - 1, 0), 0)),
        pl.BlockSpec((1, ROW_TILE, CONV_CH), lambda i, j: (i, j, 0)),
        pl.BlockSpec((1, HALO, CONV_CH), lambda i, j: (i, jnp.minimum(j * per + per, nh - 1), 0)),
        pl.BlockSpec((1, ROW_TILE, LORA_TAIL_W), lambda i, j: (i, j, NONCONV_W // LORA_TAIL_W - 1)),
        full((16, CONV_CH)), full((8, w)), full((384, w)), full((384, w)), full((384, w)), full((384, w)),
        full((2 * HEAD_BLK, HEAD_BLK)),
    ]
    out = pl.BlockSpec((1, ROW_TILE, w), lambda i, j: (i, j, 0))
    return pl.pallas_call(
        _prep_kernel,
        grid=(b, nt),
        in_specs=in_specs,
        out_specs=[out] * 11,
        out_shape=[jax.ShapeDtypeStruct((b, t, w), jnp.bfloat16 if name in ("q", "k", "v") else jnp.float32)
                   for name in ("q", "k", "r", "kmod", "v", "kh", "ab", "lwf", "lwb", "gg", "bonus")],
        scratch_shapes=[pltpu.VMEM((2 * HALO + ROW_TILE, w), jnp.float32)],
        compiler_params=pltpu.CompilerParams(
            dimension_semantics=("arbitrary", "arbitrary"), vmem_limit_bytes=VMEM_LIMIT),
        name="conv_prep",
    )(pc, pc, pc, pn, cw, rpar, wbf, wbb, ab, gb, bd)


def _round_robin(gens):
    out = [None] * len(gens)
    live = list(range(len(gens)))
    while live:
        nxt = []
        for i in live:
            try:
                next(gens[i])
                nxt.append(i)
            except StopIteration as stop:
                out[i] = stop.value
        live = nxt
    return out


def _rwkv_pair_chunk(r, k, v, kh, ab, lw, s_mat, tri_incl, strict, incl, last_row, m0, m1):
    cw = _dot_exact_lhs(tri_incl, lw)
    yield
    e_pos = jnp.exp(cw)
    e_neg = jnp.exp(-cw)
    e_prev = jnp.exp(cw - lw)
    stack = lambda a: jnp.concatenate([a * m0, a * m1], axis=0)
    kt = stack(kh * e_prev)
    bt = stack(ab * e_neg)
    kk = stack(k * e_neg)
    rt = stack(r * e_pos)
    vs = stack(v)
    zero = jnp.zeros((), jnp.float32)
    kr = jnp.concatenate([kt, rt], axis=0)
    bk = jnp.concatenate([bt, kk], axis=0)
    amat = _dot1(kr, bk, _NT)
    yield
    a_bk = jnp.where(strict, amat[:PAIR, :PAIR], zero)
    a_kk = jnp.where(strict, amat[:PAIR, PAIR:], zero)
    a_rb = jnp.where(incl, amat[PAIR:, :PAIR], zero)
    a_rk = jnp.where(incl, amat[PAIR:, PAIR:], zero)
    n_mat = -a_bk
    eye = (lax.broadcasted_iota(jnp.int32, (PAIR, PAIR), 0)
           == lax.broadcasted_iota(jnp.int32, (PAIR, PAIR), 1)).astype(jnp.float32)
    q = eye + n_mat
    p = _dot1(n_mat, n_mat)
    ks = _dot1(kr, s_mat, _NT)
    av = _dot1(jnp.concatenate([a_kk, a_rk], axis=0), vs)
    yield
    for _ in range(4):
        qp = _dot1(jnp.concatenate([q, p], axis=0), p)
        yield
        q = q + qp[:PAIR]
        p = qp[PAIR:]
    t_inv = q + _dot1(q, p)
    yield
    u = -_dot1(t_inv, ks[:PAIR] + av[:PAIR])
    yield
    y = ks[PAIR:] + av[PAIR:] + _dot1(a_rb, u)
    yield
    w_last = jnp.sum(jnp.where(last_row, e_pos, zero), axis=0, keepdims=True)
    s_new = (s_mat + _dot1(jnp.concatenate([u, vs], axis=0), bk, _TN)) * w_last
    return y[:CHUNK] + y[CHUNK:], s_new


def _rwkv_kernel(rf, kf, vf, khf, abf, lwf, rb, kb, vb, khb, abb, lwb, yf_ref, yb_ref, s_ref, *, nctx):
    j = pl.program_id(1)

    @pl.when(j == 0)
    def _():
        s_ref[...] = jnp.zeros_like(s_ref)

    row = lax.broadcasted_iota(jnp.int32, (PAIR, PAIR), 0)
    col = lax.broadcasted_iota(jnp.int32, (PAIR, PAIR), 1)
    same = (row // CHUNK) == (col // CHUNK)
    r64 = lax.broadcasted_iota(jnp.int32, (CHUNK, CHUNK), 0)
    c64 = lax.broadcasted_iota(jnp.int32, (CHUNK, CHUNK), 1)
    rowl = lax.broadcasted_iota(jnp.int32, (CHUNK, PAIR), 0)
    lane = lax.broadcasted_iota(jnp.int32, (1, PAIR), 1)
    m0 = (lane < D_HR).astype(jnp.float32)
    m1 = 1.0 - m0
    dirs = (
        (rf, kf, vf, khf, abf, lwf, yf_ref, (c64 <= r64), same & (col < row), same & (col <= row), rowl == CHUNK - 1),
        (rb, kb, vb, khb, abb, lwb, yb_ref, (c64 >= r64), same & (col > row), same & (col >= row), rowl == 0),
    )
    nb = rf.shape[0]
    states = [[[s_ref[bi, d, p] for p in range(NPAIR)] for d in range(2)] for bi in range(nb)]
    gens, meta = [], []
    for d, (r_ref, k_ref, v_ref, kh_ref, ab_ref, lw_ref, y_ref, tri, strict, incl, last_row) in enumerate(dirs):
        tri = tri.astype(jnp.float32).astype(jnp.bfloat16)
        for bi in range(nb):
            for p in range(NPAIR):
                sl = slice(p * PAIR, (p + 1) * PAIR)
                gens.append(_rwkv_pair_chunk(r_ref[bi, :, sl], k_ref[bi, :, sl], v_ref[bi, :, sl],
                                             kh_ref[bi, :, sl], ab_ref[bi, :, sl], lw_ref[bi, :, sl],
                                             states[bi][d][p], tri, strict, incl, last_row, m0, m1))
                meta.append((bi, d, p, sl, y_ref))
    results = [m + r for m, r in zip(meta, _round_robin(gens))]
    for bi, d, p, sl, y_ref, y, s_new in results:
        s_ref[bi, d, p] = s_new
        y_ref[bi, :, sl] = y


def _rwkv_scan(r, k, v, kh, ab, lw_f, lw_b, ctx_len):
    b, t, w = r.shape
    nc = t // CHUNK
    nctx = ctx_len // CHUNK
    nlat = nc - nctx

    def fwd_map(i, j):
        return (i, j, 0)

    def bwd_map(i, j):
        return (i, jnp.where(j < nctx, nctx - 1 - j, nc - 1 - (j - nctx)), 0)

    nb = math.gcd(b, SCAN_BATCH)
    blk = (nb, CHUNK, w)
    in_specs = [pl.BlockSpec(blk, fwd_map)] * 6 + [pl.BlockSpec(blk, bwd_map)] * 6
    out_specs = [
        pl.BlockSpec(blk, lambda i, j: (i, jnp.maximum(j - nctx, 0), 0)),
        pl.BlockSpec(blk, lambda i, j: (i, nlat - 1 - jnp.maximum(j - nctx, 0), 0)),
    ]
    return pl.pallas_call(
        functools.partial(_rwkv_kernel, nctx=nctx),
        grid=(b // nb, nc),
        in_specs=in_specs,
        out_specs=out_specs,
        out_shape=[jax.ShapeDtypeStruct((b, t - ctx_len, w), jnp.float32)] * 2,
        scratch_shapes=[pltpu.VMEM((nb, 2, NPAIR, PAIR, PAIR), jnp.float32)],
        compiler_params=pltpu.CompilerParams(
            dimension_semantics=("arbitrary", "arbitrary"), vmem_limit_bytes=VMEM_LIMIT),
        name="rwkv_scan",
    )(r, k, v, kh, ab, lw_f, r, k, v, kh, ab, lw_b)


def _scan_max(x, reverse):
    rows = lax.broadcasted_iota(jnp.int32, x.shape, 0)
    neg_inf = jnp.full((), -jnp.inf, jnp.float32)
    step = 1
    while step < CHUNK:
        if reverse:
            shifted = jnp.where(rows < CHUNK - step, pltpu.roll(x, CHUNK - step, 0), neg_inf)
        else:
            shifted = jnp.where(rows >= step, pltpu.roll(x, step, 0), neg_inf)
        x = jnp.maximum(x, shifted)
        step *= 2
    return x


def _mlstm_gates(gates, bias, tri, m_prev, is_f, reverse):
    gl = gates + bias
    act = jnp.where(is_f, jax.nn.log_sigmoid(gl), gl)
    b_all = _dot_exact_lhs(tri, act)
    li = pltpu.roll(act, H_M, 1)
    x = li - b_all
    m_inter = b_all + m_prev
    m_t = jnp.maximum(m_inter, b_all + _scan_max(x, reverse))
    b_last = b_all[0:1, :] if reverse else b_all[CHUNK - 1:CHUNK, :]
    g = b_last + x
    m_new = jnp.maximum(b_last + m_prev, jnp.max(g, axis=0, keepdims=True))
    wts = jnp.exp(g - m_new)
    decay = jnp.exp(b_last + m_prev - m_new)
    return b_all - m_t, jnp.exp(m_inter - m_t), jnp.exp(-m_t), x, wts, decay, m_new


def _mlstm_pair_chunk(q, k, v, pieces, cf, decay_g, c_mat, n_row, incl, m0, m1):
    stack = lambda a: jnp.concatenate([a * m0, a * m1], axis=0)
    q_st, k_st, v_st = stack(q), stack(k), stack(v)
    lane = lax.broadcasted_iota(jnp.int32, (1, PAIR), 1)
    zero_bf = jnp.zeros((), jnp.bfloat16)
    sel = lambda a: jnp.concatenate([jnp.where(lane == cf, a, zero_bf), jnp.where(lane == cf + 1, a, zero_bf)], 0)
    ones = jnp.ones((PAIR, PAIR), jnp.bfloat16)
    spread = [sel(a) for a in pieces[0]] + [sel(p[0]) for p in pieces[1:4]]
    bc = _dg(jnp.concatenate(spread, axis=0), ones, _NN)
    bm_bc = bc[:PAIR] + bc[PAIR:2 * PAIR] + bc[2 * PAIR:3 * PAIR]
    inter_bc, enm_bc, wts_bc = (bc[i * PAIR:(i + 1) * PAIR] for i in range(3, 6))
    r3 = _dg(ones, jnp.concatenate([sel(a) for a in pieces[4]], axis=0), _NT)
    r_mat = r3[:, :PAIR] + r3[:, PAIR:2 * PAIR] + r3[:, 2 * PAIR:]
    qk = _dot1(q_st, k_st, _NT)
    qc = _dot1(q_st, c_mat)
    yield
    neg_inf = jnp.full((), -jnp.inf, jnp.float32)
    s = qk * jnp.exp(jnp.where(incl, bm_bc + r_mat, neg_inf))
    wk = wts_bc * k_st
    sv = _dot1(s, v_st)
    kv = _dot1(wk, v_st, _TN)
    sums = _dot1(jnp.concatenate([q_st * n_row, s], axis=0), jnp.ones((PAIR, PAIR), jnp.float32))
    yield
    den = inter_bc * sums[:PAIR] + sums[PAIR:]
    h_st = (inter_bc * qc + sv) / jnp.maximum(jnp.abs(den), enm_bc)
    h = h_st[:CHUNK] + h_st[CHUNK:]
    decay_row = jnp.where(lane < D_HM, decay_g[:, cf:cf + 1], decay_g[:, cf + 1:cf + 2])
    c_new = decay_row * c_mat + kv
    n_new = decay_row * n_row + jnp.sum(wk, axis=0, keepdims=True)
    return h, c_new, n_new


def _mlstm_kernel(qf, kf, vf, gf, qb, kb, vb, gb, bias_ref, hf_ref, hb_ref, c_ref, n_ref, m_ref, *, nctx):
    del nctx
    j = pl.program_id(1)

    @pl.when(j == 0)
    def _():
        c_ref[...] = jnp.zeros_like(c_ref)
        n_ref[...] = jnp.zeros_like(n_ref)
        m_ref[...] = jnp.zeros_like(m_ref)

    row = lax.broadcasted_iota(jnp.int32, (PAIR, PAIR), 0)
    col = lax.broadcasted_iota(jnp.int32, (PAIR, PAIR), 1)
    same = (row // CHUNK) == (col // CHUNK)
    r64 = lax.broadcasted_iota(jnp.int32, (CHUNK, CHUNK), 0)
    c64 = lax.broadcasted_iota(jnp.int32, (CHUNK, CHUNK), 1)
    lane = lax.broadcasted_iota(jnp.int32, (1, PAIR), 1)
    m0 = (lane < D_HM).astype(jnp.float32)
    m1 = 1.0 - m0
    is_f = (lane % (2 * H_M)) >= H_M
    dirs = (
        (qf, kf, vf, gf, hf_ref, (c64 <= r64), same & (col <= row)),
        (qb, kb, vb, gb, hb_ref, (c64 >= r64), same & (col >= row)),
    )
    nb = qf.shape[0]
    states = [[[(c_ref[bi, d, p], n_ref[bi, d, p]) for p in range(NPAIR)] for d in range(2)] for bi in range(nb)]
    gens, meta, m_news = [], [], []
    lead = lambda a: (a.astype(jnp.bfloat16),)
    for d, (q_ref, k_ref, v_ref, g_ref, h_ref, tri, incl) in enumerate(dirs):
        tri = tri.astype(jnp.float32).astype(jnp.bfloat16)
        for bi in range(nb):
            bm, inter, enm, x, wts, decay_g, m_new = _mlstm_gates(g_ref[bi], bias_ref[...], tri, m_ref[bi, d], is_f,
                                                                  d == 1)
            m_news.append((bi, d, m_new))
            pieces = [_split3(bm), lead(inter), lead(enm), lead(wts), _split3(x)]
            for p in range(NPAIR):
                sl = slice(p * PAIR, (p + 1) * PAIR)
                cf = d * 2 * H_M + H_M + 2 * p
                gens.append(_mlstm_pair_chunk(q_ref[bi, :, sl], k_ref[bi, :, sl], v_ref[bi, :, sl], pieces, cf,
                                               decay_g, *states[bi][d][p], incl, m0, m1))
                meta.append((bi, d, p, sl, h_ref))
    for (bi, d, p, sl, h_ref), (h, c_new, n_new) in zip(meta, _round_robin(gens)):
        c_ref[bi, d, p] = c_new
        n_ref[bi, d, p] = n_new
        h_ref[bi, :, sl] = h
    for bi, d, m_new in m_news:
        m_ref[bi, d] = m_new


def _mlstm_scan(q, k, pn, bias_row, ctx_len):
    v = gates = pn
    b, t, w = q.shape
    nc = t // CHUNK
    nctx = ctx_len // CHUNK
    nlat = nc - nctx

    def fwd_map(i, j):
        return (i, j, 0)

    def bwd_map(i, j):
        return (i, jnp.where(j < nctx, nctx - 1 - j, nc - 1 - (j - nctx)), 0)

    nb = math.gcd(b, SCAN_BATCH)
    blk = (nb, CHUNK, w)
    gblk = (nb, CHUNK, 128)
    gate_blk = (2 * W_M) // 128
    gate_of = lambda m: (lambda i, j: m(i, j)[:2] + (gate_blk,))
    in_specs = ([pl.BlockSpec(blk, fwd_map)] * 3 + [pl.BlockSpec(gblk, gate_of(fwd_map))]
                + [pl.BlockSpec(blk, bwd_map)] * 3 + [pl.BlockSpec(gblk, gate_of(bwd_map))]
                + [pl.BlockSpec((1, 128), lambda i, j: (0, 0))])
    out_specs = [
        pl.BlockSpec(blk, lambda i, j: (i, jnp.maximum(j - nctx, 0), 0)),
        pl.BlockSpec(blk, lambda i, j: (i, nlat - 1 - jnp.maximum(j - nctx, 0), 0)),
    ]
    return pl.pallas_call(
        functools.partial(_mlstm_kernel, nctx=nctx),
        grid=(b // nb, nc),
        in_specs=in_specs,
        out_specs=out_specs,
        out_shape=[jax.ShapeDtypeStruct((b, t - ctx_len, w), jnp.float32)] * 2,
        scratch_shapes=[pltpu.VMEM((nb, 2, NPAIR, PAIR, PAIR), jnp.float32),
                        pltpu.VMEM((nb, 2, NPAIR, 1, PAIR), jnp.float32),
                        pltpu.VMEM((nb, 2, 1, 128), jnp.float32)],
        compiler_params=pltpu.CompilerParams(
            dimension_semantics=("arbitrary", "arbitrary"), vmem_limit_bytes=VMEM_LIMIT),
        name="mlstm_scan",
    )(q, k, v, gates, q, k, v, gates, bias_row)


ROUTE_W = 128
N_ROUTE = N_GROUPS + N_EXPERTS


def _ln_rows(z, eps):
    mu = jnp.mean(z, -1, keepdims=True)
    zc = z - mu
    var = jnp.mean(zc * zc, -1, keepdims=True)
    return zc * lax.rsqrt(var + eps)


def _headnorm_mxu(z, bd2, eps):
    mu = _head_sums(z, bd2) * (1.0 / D_HM)
    zc = z - mu
    var = _head_sums(zc * zc, bd2) * (1.0 / D_HM)
    return zc * lax.rsqrt(var + eps)


def _outproj_kernel(hf, hb, mo, yf, yb, gg, bonus, x_ref, mod_ref, wout_ref, hpar_ref, ln_ref, rtw_ref, rtb_ref,
                    bd_ref, x1_ref, tok_ref, route_ref, ridx_ref, cnt_ref, carry_ref):
    i, j = pl.program_id(0), pl.program_id(1)

    @pl.when((i == 0) & (j == 0))
    def _():
        carry_ref[...] = jnp.zeros_like(carry_ref)

    avg = bd_ref[...]
    out_m = jax.nn.sigmoid(mo[0]) * (_headnorm_mxu(hf[0] + hb[0], avg, LN_EPS) * hpar_ref[0:1, :])
    y = _headnorm_mxu(yf[0] + yb[0], avg, GN_EPS) * hpar_ref[1:2, :] + hpar_ref[2:3, :]
    out_r = (y + bonus[0]) * gg[0]
    proj = (jnp.dot(out_m.astype(jnp.bfloat16), wout_ref[:W_M, :], preferred_element_type=jnp.float32)
            + jnp.dot(out_r.astype(jnp.bfloat16), wout_ref[W_M:, :], preferred_element_type=jnp.float32))
    g1, sh2, sc2 = mod_ref[0, 2:3, :], mod_ref[0, 3:4, :], mod_ref[0, 4:5, :]
    x1 = _ln_rows(ALPHA * x_ref[0] + g1 * proj, LN_EPS) * ln_ref[0:1, :] + ln_ref[1:2, :]
    x1_ref[0] = x1
    tok = _ln_rows(x1, LN_EPS) * (1.0 + sc2) + sh2
    tok_ref[0] = tok

    logits = _dot3_pre(tok, rtw_ref[...]) + rtb_ref[...]
    lane = lax.broadcasted_iota(jnp.int32, (ROW_TILE, ROUTE_W), 1)
    neg_inf = jnp.full((), -jnp.inf, jnp.float32)
    big = jnp.int32(ROUTE_W)
    is_g = lane < N_GROUPS
    lg = jnp.where(is_g, logits, neg_inf)
    gmax = jnp.max(lg, axis=1, keepdims=True)
    grp = jnp.min(jnp.where(lg == gmax, lane, big), axis=1, keepdims=True)
    p_top = 1.0 / jnp.sum(jnp.where(is_g, jnp.exp(lg - gmax), 0.0), axis=1, keepdims=True)
    in_grp = (lane >= N_GROUPS) & (lane < N_ROUTE) & ((lane - N_GROUPS) // E_PER_GROUP == grp)
    le = jnp.where(in_grp, logits, neg_inf)
    v1 = jnp.max(le, axis=1, keepdims=True)
    i1 = jnp.min(jnp.where(le == v1, lane, big), axis=1, keepdims=True)
    le2 = jnp.where(lane == i1, neg_inf, le)
    v2 = jnp.max(le2, axis=1, keepdims=True)
    i2 = jnp.min(jnp.where(le2 == v2, lane, big), axis=1, keepdims=True)
    e21 = jnp.exp(v2 - v1)
    w1 = (1.0 / (1.0 + e21)) * p_top
    w2 = (e21 / (1.0 + e21)) * p_top

    sel1, sel2 = lane == i1, lane == i2
    onehot = jnp.where(sel1 | sel2, 1.0, 0.0)
    tr = lax.broadcasted_iota(jnp.int32, (ROW_TILE, ROW_TILE), 0)
    tc = lax.broadcasted_iota(jnp.int32, (ROW_TILE, ROW_TILE), 1)
    strict = jnp.where(tc < tr, 1.0, 0.0).astype(jnp.bfloat16)
    before = _dg(strict, onehot.astype(jnp.bfloat16), _NN) + carry_ref[...]
    r1 = jnp.sum(jnp.where(sel1, before, 0.0), axis=1, keepdims=True)
    r2 = jnp.sum(jnp.where(sel2, before, 0.0), axis=1, keepdims=True)
    carry_ref[...] = carry_ref[...] + jnp.sum(onehot, axis=0, keepdims=True)
    e1 = (i1 - N_GROUPS).astype(jnp.float32)
    e2 = (i2 - N_GROUPS).astype(jnp.float32)
    route = jnp.where(lane == 0, e1, jnp.where(lane == 1, e2, jnp.where(lane == 2, r1, jnp.where(
        lane == 3, r2, jnp.where(lane == 4, w1, jnp.where(lane == 5, w2, 0.0))))))
    route_ref[0] = route
    ridx_ref[0] = route.T[0:8, :].astype(jnp.int32)
    cnt_ref[...] = jnp.broadcast_to(carry_ref[...], cnt_ref.shape)


def _outproj(h_f, h_b, pn, y_f, y_b, gg, bonus, x, mod3, wout_bf16, hpar, lnpar, rtw, rtb, bd, ctx_len):
    b, seq, d = x.shape
    nt = seq // ROW_TILE
    off = ctx_len // ROW_TILE
    lat = lambda w: pl.BlockSpec((1, ROW_TILE, w), lambda i, j: (i, j, 0))
    full = lambda shp: pl.BlockSpec(shp, lambda i, j: tuple(0 for _ in shp))
    in_specs = [
        lat(W_M), lat(W_M),
        pl.BlockSpec((1, ROW_TILE, W_M), lambda i, j: (i, j + off, 1)),
        lat(W_R), lat(W_R),
        pl.BlockSpec((1, ROW_TILE, W_R), lambda i, j: (i, j + off, 0)),
        pl.BlockSpec((1, ROW_TILE, W_R), lambda i, j: (i, j + off, 0)),
        lat(d),
        pl.BlockSpec((1, 6, d), lambda i, j: (i, 0, 0)),
        full((MIX_W, d)), full((8, W_M)), full((8, d)), full((3 * d, ROUTE_W)), full((1, ROUTE_W)),
        full((2 * HEAD_BLK, HEAD_BLK)),
    ]
    out_specs = [
        lat(d), lat(d), lat(ROUTE_W),
        pl.BlockSpec((1, 8, ROW_TILE), lambda i, j: (i * nt + j, 0, 0)),
        full((8, ROUTE_W)),
    ]
    out_shape = [
        jax.ShapeDtypeStruct((b, seq, d), jnp.float32),
        jax.ShapeDtypeStruct((b, seq, d), jnp.float32),
        jax.ShapeDtypeStruct((b, seq, ROUTE_W), jnp.float32),
        jax.ShapeDtypeStruct((b * nt, 8, ROW_TILE), jnp.int32),
        jax.ShapeDtypeStruct((8, ROUTE_W), jnp.float32),
    ]
    return pl.pallas_call(
        _outproj_kernel,
        grid=(b, nt),
        in_specs=in_specs,
        out_specs=out_specs,
        out_shape=out_shape,
        scratch_shapes=[pltpu.VMEM((1, ROUTE_W), jnp.float32)],
        compiler_params=pltpu.CompilerParams(
            dimension_semantics=("arbitrary", "arbitrary"), vmem_limit_bytes=VMEM_LIMIT),
        name="outproj_router",
    )(h_f, h_b, pn, y_f, y_b, gg, bonus, x, mod3, wout_bf16, hpar, lnpar, rtw, rtb, bd)


DEST_TILES = 8


def _dest_kernel(starts_ref, ridx_ref, dest_ref):
    blk = ridx_ref[...]
    dest = pltpu.roll(blk, 8 - TOP_K_IN_GROUP, 1)
    for e in range(N_EXPERTS):
        dest = dest + jnp.where(blk == e, starts_ref[e], 0)
    dest_ref[...] = dest


def _moe_dest(starts, ridx):
    nt = ridx.shape[0]
    per_step = math.gcd(nt, DEST_TILES)
    blk = pl.BlockSpec((per_step, 8, ROW_TILE), lambda i, s: (i, 0, 0))
    return pl.pallas_call(
        _dest_kernel,
        grid_spec=pltpu.PrefetchScalarGridSpec(num_scalar_prefetch=1, grid=(nt // per_step,),
                                               in_specs=[blk], out_specs=blk),
        out_shape=jax.ShapeDtypeStruct(ridx.shape, jnp.int32),
        compiler_params=pltpu.CompilerParams(dimension_semantics=("arbitrary",), vmem_limit_bytes=VMEM_LIMIT),
        name="moe_dest",
    )(starts, ridx)


SUB = 8
SUB_SHIFT = 3
TILE_GROUPS = ROW_TILE // SUB


def _scatter_kernel(dest_ref, tok_ref, xs_in_ref, xs_ref, sem):
    del xs_in_ref

    def issue(g, c):
        for k in range(SUB):
            for slot in range(TOP_K_IN_GROUP):
                dst = dest_ref[slot * ROW_TILE + g * SUB + k]
                pltpu.make_async_copy(tok_ref.at[g, pl.ds(k, 1), :],
                                      xs_ref.at[dst >> SUB_SHIFT, pl.ds(dst & (SUB - 1), 1), :], sem).start()
        return c

    lax.fori_loop(0, TILE_GROUPS, issue, 0)
    for half in range(TOP_K_IN_GROUP):
        pltpu.make_async_copy(tok_ref, xs_ref.at[pl.ds(half * TILE_GROUPS, TILE_GROUPS)], sem).wait()


def _moe_scatter(dest_flat, tok2d, n_rows):
    n, d = tok2d.shape
    xs_zero = jnp.zeros((n_rows // SUB, SUB, d), tok2d.dtype)
    nt = n // ROW_TILE
    xs = pl.pallas_call(
        _scatter_kernel,
        grid=(nt,),
        in_specs=[
            pl.BlockSpec((SUB * ROW_TILE,), lambda i: (i,), memory_space=pltpu.SMEM),
            pl.BlockSpec((TILE_GROUPS, SUB, d), lambda i: (i, 0, 0)),
            pl.BlockSpec(memory_space=pl.ANY),
        ],
        out_specs=pl.BlockSpec(memory_space=pl.ANY),
        scratch_shapes=[pltpu.SemaphoreType.DMA(())],
        out_shape=jax.ShapeDtypeStruct(xs_zero.shape, tok2d.dtype),
        input_output_aliases={2: 0},
        compiler_params=pltpu.CompilerParams(dimension_semantics=("arbitrary",), vmem_limit_bytes=VMEM_LIMIT),
        name="moe_scatter",
    )(dest_flat, tok2d.reshape(n // SUB, SUB, d), xs_zero)
    return xs.reshape(n_rows, d)


def _expert_kernel(blk_exp_ref, valid_ref, xs_ref, wg_ref, wu_ref, wd_ref, ys_ref, wg_bf, wu_bf, wd_bf):
    i = pl.program_id(0)
    valid = valid_ref[i]

    @pl.when((i == 0) | (blk_exp_ref[i] != blk_exp_ref[jnp.maximum(i - 1, 0)]))
    def _():
        wg_bf[...] = wg_ref[0].astype(jnp.bfloat16)
        wu_bf[...] = wu_ref[0].astype(jnp.bfloat16)
        wd_bf[...] = wd_ref[0].astype(jnp.bfloat16)

    @pl.when(valid > 0)
    def _():
        xb = xs_ref[...].astype(jnp.bfloat16)
        hg = jnp.dot(xb, wg_bf[...], preferred_element_type=jnp.float32)
        hu = jnp.dot(xb, wu_bf[...], preferred_element_type=jnp.float32)
        hb = (jax.nn.silu(hg) * hu).astype(jnp.bfloat16)
        ys_ref[...] = jnp.dot(hb, wd_bf[...], preferred_element_type=jnp.float32)

    @pl.when(valid == 0)
    def _():
        ys_ref[...] = jnp.zeros_like(ys_ref)


def _moe_experts(blk_exp, blk_valid, xs, wg, wu, wd):
    nrow, d = xs.shape
    de = wg.shape[2]
    return pl.pallas_call(
        _expert_kernel,
        grid_spec=pltpu.PrefetchScalarGridSpec(
            num_scalar_prefetch=2,
            grid=(nrow // MOE_BLOCK,),
            in_specs=[
                pl.BlockSpec((MOE_BLOCK, d), lambda i, be, nu: (i, 0)),
                pl.BlockSpec((1, d, de), lambda i, be, nu: (be[i], 0, 0)),
                pl.BlockSpec((1, d, de), lambda i, be, nu: (be[i], 0, 0)),
                pl.BlockSpec((1, de, d), lambda i, be, nu: (be[i], 0, 0)),
            ],
            out_specs=pl.BlockSpec((MOE_BLOCK, d), lambda i, be, nu: (i, 0)),
            scratch_shapes=[pltpu.VMEM((d, de), jnp.bfloat16), pltpu.VMEM((d, de), jnp.bfloat16),
                            pltpu.VMEM((de, d), jnp.bfloat16)],
        ),
        out_shape=jax.ShapeDtypeStruct((nrow, d), jnp.float32),
        compiler_params=pltpu.CompilerParams(dimension_semantics=("arbitrary",), vmem_limit_bytes=VMEM_LIMIT),
        name="moe_experts",
    )(blk_exp, blk_valid, xs, wg, wu, wd)


def _combine_kernel(dest_ref, dest_next_ref, route_ref, x1_ref, mod_ref, ln_ref, ys_ref, out_ref,
                    ybuf, sems):
    i = pl.program_id(0)
    n = pl.num_programs(0)
    cur = i % 2

    def gather(idx_ref, buf):
        def issue(g, c):
            for k in range(SUB):
                for slot in range(TOP_K_IN_GROUP):
                    src = idx_ref[slot * ROW_TILE + g * SUB + k]
                    pltpu.make_async_copy(ys_ref.at[src >> SUB_SHIFT, pl.ds(src & (SUB - 1), 1), :],
                                          ybuf.at[buf, slot * TILE_GROUPS + g, pl.ds(k, 1), :],
                                          sems.at[buf]).start()
            return c

        lax.fori_loop(0, TILE_GROUPS, issue, 0)

    @pl.when(i == 0)
    def _():
        gather(dest_ref, 0)

    @pl.when(i + 1 < n)
    def _():
        gather(dest_next_ref, 1 - cur)

    pltpu.make_async_copy(ys_ref.at[pl.ds(0, TOP_K_IN_GROUP * TILE_GROUPS)], ybuf.at[cur], sems.at[cur]).wait()
    route = route_ref[0]
    d = out_ref.shape[-1]
    y0 = ybuf[cur, 0:TILE_GROUPS].reshape(ROW_TILE, d)
    y1 = ybuf[cur, TILE_GROUPS:2 * TILE_GROUPS].reshape(ROW_TILE, d)
    ffn = y0 * route[:, 4:5] + y1 * route[:, 5:6]
    g2 = mod_ref[0, 5:6, :]
    out_ref[0] = _ln_rows(ALPHA * x1_ref[0] + g2 * ffn, LN_EPS) * ln_ref[2:3, :] + ln_ref[3:4, :]


def _moe_combine(dest_flat, route, x1, mod3, lnpar, ys):
    b, seq, d = x1.shape
    nt = seq // ROW_TILE
    n = b * nt
    tile = lambda w: pl.BlockSpec((1, ROW_TILE, w), lambda i: (i // nt, i % nt, 0))
    idx_blk = lambda f: pl.BlockSpec((SUB * ROW_TILE,), f, memory_space=pltpu.SMEM)
    return pl.pallas_call(
        _combine_kernel,
        grid=(n,),
        in_specs=[
            idx_blk(lambda i: (i,)),
            idx_blk(lambda i: (jnp.minimum(i + 1, n - 1),)),
            tile(ROUTE_W),
            tile(d),
            pl.BlockSpec((1, 6, d), lambda i: (i // nt, 0, 0)),
            pl.BlockSpec((8, d), lambda i: (0, 0)),
            pl.BlockSpec(memory_space=pl.ANY),
        ],
        out_specs=tile(d),
        scratch_shapes=[pltpu.VMEM((2, TOP_K_IN_GROUP * TILE_GROUPS, SUB, d), jnp.float32),
                        pltpu.SemaphoreType.DMA((2,))],
        out_shape=jax.ShapeDtypeStruct((b, seq, d), jnp.float32),
        compiler_params=pltpu.CompilerParams(dimension_semantics=("arbitrary",), vmem_limit_bytes=VMEM_LIMIT),
        name="moe_combine",
    )(dest_flat, dest_flat, route, x1, mod3, lnpar, ys.reshape(ys.shape[0] // SUB, SUB, d))


def _moe_plan(cnt, n_pairs):
    counts = cnt[0, N_GROUPS:N_ROUTE].astype(jnp.int32)
    padded = (counts + MOE_BLOCK - 1) // MOE_BLOCK * MOE_BLOCK
    pad_ends = jnp.cumsum(padded)
    pad_starts = pad_ends - padded
    n_blk = -(-n_pairs // MOE_BLOCK) + N_EXPERTS
    blk_row0 = jnp.arange(n_blk, dtype=jnp.int32) * MOE_BLOCK
    blk_exp = jnp.minimum(jnp.sum((pad_ends[None, :] <= blk_row0[:, None]).astype(jnp.int32), axis=1), N_EXPERTS - 1)
    blk_valid = jnp.clip(counts[blk_exp] - (blk_row0 - pad_starts[blk_exp]), 0, MOE_BLOCK)
    blk_valid = jnp.where(blk_row0 < pad_ends[-1], blk_valid, 0).astype(jnp.int32)
    return pad_starts.astype(jnp.int32), blk_exp, blk_valid, n_blk


def kernel(x, c, ctx, c_ctx, w_ada, b_ada, w_in, conv_w, m_bias_i, m_bias_f, m_norm_w, r_w0, r_wB, r_a0, r_aB,
           r_gB, r_kk, r_ka, r_bonus, r_norm_w, r_norm_b, w_out, ln1_g, ln1_b, ln2_g, ln2_b, rt_g, rt_g_b, rt_e,
           rt_e_b, ex_gate, ex_up, ex_down):
    assert w_ada.shape[0] == DEPTH
    bsz, seq, d = x.shape
    ctx_len = ctx.shape[1]
    assert ctx_len == ROW_TILE and seq % ROW_TILE == 0
    assert seq % GRID_W == 0 and ROW_TILE % GRID_W == 0
    l = 0
    zrow = lambda n, w: jnp.zeros((n, w), jnp.float32)
    mrows = -(-(bsz + 1) // 8) * 8
    c_pad = jnp.concatenate([c, c_ctx[None, :], zrow(mrows - bsz - 1, d)], 0)
    mod_all = _ada(c_pad, w_ada[l], b_ada[l][None, :])[:bsz + 1]
    mod3 = mod_all.reshape(bsz + 1, 6, d)
    pc, pn = _inproj(ctx, x, mod3, _pad_w_in(w_in[l]))
    cw = jnp.concatenate([conv_w[l].reshape(CONV_K * CONV_K, CONV_CH), zrow(16 - CONV_K * CONV_K, CONV_CH)], 0)
    rpar = jnp.concatenate([r_w0[l], r_a0[l][None, :], r_kk[l][None, :], r_ka[l][None, :],
                            r_bonus[l].reshape(1, W_R), zrow(2, W_R)], 0)
    wbf = jnp.concatenate([r_wB[l][0], zrow(W_LORA, W_R)], 0)
    wbb = jnp.concatenate([zrow(W_LORA, W_R), r_wB[l][1]], 0)
    ab_mat = jnp.concatenate([r_aB[l], zrow(128 - A_LORA, W_R)], 0)
    bd = _head_sum_operator(D_HR)
    q, k, rr, kmod, rv, kh, kha, logw_f, logw_b, gg, bonus = _prep(
        pc, pn, cw, rpar, _stack3(wbf), _stack3(wbb), _stack3(ab_mat), _stack3(r_gB[l]), bd)
    bias_row = jnp.concatenate([m_bias_i[l][0], m_bias_f[l][0], m_bias_i[l][1], m_bias_f[l][1],
                                jnp.zeros((128 - 4 * H_M,), jnp.float32)])[None, :]
    h_f, h_b = _mlstm_scan(q, k, pn, bias_row, ctx_len)
    y_f, y_b = _rwkv_scan(rr, kmod, rv, kh, kha, logw_f, logw_b, ctx_len)
    hpar = jnp.zeros((8, W_M), jnp.float32).at[0].set(m_norm_w[l]).at[1].set(r_norm_w[l]).at[2].set(r_norm_b[l])
    lnpar = jnp.zeros((8, d), jnp.float32).at[0].set(ln1_g[l]).at[1].set(ln1_b[l]).at[2].set(ln2_g[l]).at[3].set(
        ln2_b[l])
    rtw = jnp.concatenate([rt_g[l], rt_e[l], jnp.zeros((d, ROUTE_W - N_ROUTE), jnp.float32)], axis=1)
    rtb = jnp.concatenate([rt_g_b[l], rt_e_b[l], jnp.zeros((ROUTE_W - N_ROUTE,), jnp.float32)])[None, :]
    x1, tok, route, ridx, cnt = _outproj(h_f, h_b, pn, y_f, y_b, gg, bonus, x, mod3, w_out[l].astype(jnp.bfloat16),
                                         hpar, lnpar, _stack3(rtw), rtb, bd, ctx_len)
    n_tok = bsz * seq
    starts, blk_exp, blk_valid, n_blk = _moe_plan(cnt, n_tok * TOP_K_IN_GROUP)
    dest = _moe_dest(starts, ridx)
    dest_flat = dest.reshape(-1)
    xs = _moe_scatter(dest_flat, tok.reshape(n_tok, d), n_blk * MOE_BLOCK)
    ys = _moe_experts(blk_exp, blk_valid, xs, ex_gate[l], ex_up[l], ex_down[l])
    return _moe_combine(dest_flat, route, x1, mod3, lnpar, ys)
```

```python
import functools
import math

import jax
import jax.numpy as jnp
from jax import lax
import numpy as np
from jax.experimental import pallas as pl
from jax.experimental.pallas import tpu as pltpu

H_M = 8
D_HM = 64
W_M = H_M * D_HM
H_R = 8
D_HR = 64
W_R = H_R * D_HR
MIX_W = W_M + W_R
W_LORA = 64
A_LORA = 64
G_LORA = 128
CONV_K = 3
CONV_CH = 2 * W_M + 3 * W_R
SECTION_WIDTHS = (W_M, W_M, W_R, W_R, W_R, W_M, W_M, 4 * H_M, W_LORA, W_LORA, A_LORA, G_LORA)
IN_COLS = sum(SECTION_WIDTHS)
SPLIT_POINTS = tuple(int(v) for v in np.cumsum(SECTION_WIDTHS)[:-1])
CHUNK = 64
N_GROUPS = 4
E_PER_GROUP = 8
N_EXPERTS = N_GROUPS * E_PER_GROUP
TOP_K_IN_GROUP = 2
D_EXPERT = 512
MOE_BLOCK = 256
DEPTH = 1
ALPHA = (2.0 * DEPTH) ** 0.25
DECAY_SCALE = math.exp(-0.5)
LN_EPS = 1e-6
GN_EPS = 64e-5

ROW_TILE = 256
NONCONV_W = 1536
LORA_TAIL_W = 384
VMEM_LIMIT = 56 * 1024 * 1024
SCAN_BATCH = 4


def _inproj_kernel(ctx_ref, x_ref, mod_ref, w_ref, oc_ref, on_ref):
    j = pl.program_id(1)

    def body(src_ref):
        z = src_ref[0]
        mu = jnp.mean(z, -1, keepdims=True)
        zc = z - mu
        var = jnp.mean(zc * zc, -1, keepdims=True)
        y = zc * lax.rsqrt(var + LN_EPS)
        h = (y * (1.0 + mod_ref[0, 1:2, :]) + mod_ref[0, 0:1, :]).astype(jnp.bfloat16)
        oc_ref[0] = jnp.dot(h, w_ref[:, :CONV_CH], preferred_element_type=jnp.float32)
        on_ref[0] = jnp.dot(h, w_ref[:, CONV_CH:], preferred_element_type=jnp.float32)

    @pl.when(j == 0)
    def _():
        body(ctx_ref)

    @pl.when(j > 0)
    def _():
        body(x_ref)


def _inproj(ctx, x, mod_all, w_pad):
    b, seq, d = x.shape
    nt = (ctx.shape[1] + seq) // ROW_TILE
    t_all = ctx.shape[1] + seq
    return pl.pallas_call(
        _inproj_kernel,
        grid=(b, nt),
        in_specs=[
            pl.BlockSpec((1, ROW_TILE, d), lambda i, j: (i, 0, 0)),
            pl.BlockSpec((1, ROW_TILE, d), lambda i, j: (i, jnp.maximum(j - 1, 0), 0)),
            pl.BlockSpec((1, 6, d), lambda i, j: (jnp.where(j == 0, b, i), 0, 0)),
            pl.BlockSpec((d, CONV_CH + NONCONV_W), lambda i, j: (0, 0)),
        ],
        out_specs=[
            pl.BlockSpec((1, ROW_TILE, CONV_CH), lambda i, j: (i, j, 0)),
            pl.BlockSpec((1, ROW_TILE, NONCONV_W), lambda i, j: (i, j, 0)),
        ],
        out_shape=[
            jax.ShapeDtypeStruct((b, t_all, CONV_CH), jnp.float32),
            jax.ShapeDtypeStruct((b, t_all, NONCONV_W), jnp.float32),
        ],
        compiler_params=pltpu.CompilerParams(
            dimension_semantics=("arbitrary", "arbitrary"), vmem_limit_bytes=VMEM_LIMIT),
        name="inproj",
    )(ctx, x, mod_all, w_pad)


def _pad_w_in(w_in):
    d = w_in.shape[0]
    sp = SPLIT_POINTS
    z = lambda n: jnp.zeros((d, n), w_in.dtype)
    gates = w_in[:, sp[6]:sp[7]]
    lw = w_in[:, sp[7]:sp[9]]
    la = w_in[:, sp[9]:sp[10]]
    lg = w_in[:, sp[10]:]
    return jnp.concatenate([w_in[:, :sp[6]], gates, z(96), lw, la, z(64), lg], axis=1).astype(jnp.bfloat16)


PAIR = 2 * D_HR
NPAIR = H_R // 2
_NN = (((1,), (0,)), ((), ()))
_NT = (((1,), (1,)), ((), ()))
_TN = (((0,), (0,)), ((), ()))


def _split2(a):
    hi = a.astype(jnp.bfloat16)
    lo = (a - hi.astype(jnp.float32)).astype(jnp.bfloat16)
    return hi, lo


def _split3(a):
    a1 = a.astype(jnp.bfloat16)
    r1 = a - a1.astype(jnp.float32)
    a2 = r1.astype(jnp.bfloat16)
    a3 = (r1 - a2.astype(jnp.float32)).astype(jnp.bfloat16)
    return a1, a2, a3


def _dg(a, b, dims):
    return lax.dot_general(a, b, dims, preferred_element_type=jnp.float32)


def _dot3(a, b, dims=_NN):
    ah, al = _split2(a)
    bh, bl = _split2(b)
    return _dg(ah, bh, dims) + _dg(ah, bl, dims) + _dg(al, bh, dims)


def _dot1(a, b, dims=_NN):
    return _dg(a.astype(jnp.bfloat16), b.astype(jnp.bfloat16), dims)


def _stack3(w):
    hi = w.astype(jnp.bfloat16)
    lo = (w - hi.astype(jnp.float32)).astype(jnp.bfloat16)
    return jnp.concatenate([hi, lo, hi], axis=0)


def _dot3_pre(a, w3):
    ah, al = _split2(a)
    return _dg(jnp.concatenate([ah, ah, al], axis=1), w3, _NN)


HEAD_BLK = 256


def _head_sum_operator(head):
    r = np.arange(HEAD_BLK) // head
    bd = (r[:, None] == r[None, :]).astype(np.float32)
    return jnp.asarray(np.concatenate([bd, bd], axis=0), jnp.bfloat16)


def _head_sums(z, bd2):
    hi, lo = _split2(z)
    out = []
    for c in range(0, z.shape[1], HEAD_BLK):
        lhs = jnp.concatenate([hi[:, c:c + HEAD_BLK], lo[:, c:c + HEAD_BLK]], axis=1)
        out.append(_dg(lhs, bd2, _NN))
    return jnp.concatenate(out, axis=1)


def _dot_exact_lhs(a_bf16, b, dims=_NN):
    b1, b2, b3 = _split3(b)
    return _dg(a_bf16, b1, dims) + _dg(a_bf16, b2, dims) + _dg(a_bf16, b3, dims)


def _ada_kernel(c_ref, w_ref, b_ref, o_ref):
    o_ref[...] = _dot3(jax.nn.silu(c_ref[...]), w_ref[...]) + b_ref[...]


def _ada(c_pad, w, bias):
    m, d = c_pad.shape
    n = w.shape[1]
    return pl.pallas_call(
        _ada_kernel,
        grid=(n // d,),
        in_specs=[pl.BlockSpec((m, d), lambda j: (0, 0)),
                  pl.BlockSpec((d, d), lambda j: (0, j)),
                  pl.BlockSpec((1, d), lambda j: (0, j))],
        out_specs=pl.BlockSpec((m, d), lambda j: (0, j)),
        out_shape=jax.ShapeDtypeStruct((m, n), jnp.float32),
        compiler_params=pltpu.CompilerParams(dimension_semantics=("arbitrary",), vmem_limit_bytes=VMEM_LIMIT),
        name="adaln",
    )(c_pad, w, bias)


GRID_W = 64
HALO = GRID_W


def _prep_kernel(top_ref, main_ref, bot_ref, pn_ref, cw_ref, rpar_ref, wbf_ref, wbb_ref, ab_ref, gb_ref, bd_ref,
                 q_ref, k_ref, r_ref, km_ref, v_ref, kh_ref, abo_ref, lwf_ref, lwb_ref, gg_ref, bon_ref, xbuf):
    j = pl.program_id(1)
    nt = pl.num_programs(1)
    is_ctx = j == 0
    top_ok = j >= 2
    bot_ok = (j >= 1) & (j < nt - 1)
    l_idx = lax.broadcasted_iota(jnp.int32, (ROW_TILE, 1), 0)
    col = jnp.where(is_ctx, l_idx, l_idx % GRID_W)
    left_ok = col != 0
    right_ok = col != jnp.where(is_ctx, ROW_TILE - 1, GRID_W - 1)
    vert = jnp.where(is_ctx, 0.0, 1.0)
    sec = {}
    for s in range(5):
        sl = slice(s * W_M, (s + 1) * W_M)
        xbuf[0:HALO, :] = jnp.where(top_ok, top_ref[0, :, sl], 0.0)
        xbuf[HALO:HALO + ROW_TILE, :] = main_ref[0, :, sl]
        xbuf[HALO + ROW_TILE:2 * HALO + ROW_TILE, :] = jnp.where(bot_ok, bot_ref[0, :, sl], 0.0)
        cols = [None, None, None]
        for dr in range(3):
            base = HALO + GRID_W * (dr - 1)
            w3 = cw_ref[3 * dr:3 * dr + 3, sl] if dr == 1 else cw_ref[3 * dr:3 * dr + 3, sl] * vert
            xrow = xbuf[base:base + ROW_TILE, :]
            for dc in range(3):
                term = xrow * w3[dc:dc + 1]
                cols[dc] = term if cols[dc] is None else cols[dc] + term
        left = jnp.where(left_ok, pltpu.roll(cols[0], 1, 0), 0.0)
        right = jnp.where(right_ok, pltpu.roll(cols[2], ROW_TILE - 1, 0), 0.0)
        sec[s] = left + cols[1] + right
    q_ref[0] = jax.nn.silu(sec[0]).astype(q_ref.dtype)
    k_ref[0] = (jax.nn.silu(sec[1]) * (D_HM ** -0.5)).astype(k_ref.dtype)
    rr, rk, rv = sec[2], sec[3], sec[4]
    r_ref[0] = rr
    v_ref[0] = rv.astype(v_ref.dtype)
    lw = jnp.tanh(pn_ref[0, :, 0:128])
    lwf_ref[0] = -DECAY_SCALE * jax.nn.sigmoid(rpar_ref[0:1, :] + _dot3_pre(lw, wbf_ref[...]))
    lwb_ref[0] = -DECAY_SCALE * jax.nn.sigmoid(rpar_ref[1:2, :] + _dot3_pre(lw, wbb_ref[...]))
    a = jax.nn.sigmoid(rpar_ref[2:3, :] + _dot3_pre(pn_ref[0, :, 128:256], ab_ref[...]))
    gg_ref[0] = _dot3_pre(jax.nn.sigmoid(pn_ref[0, :, 256:384]), gb_ref[...])
    kap = rk * rpar_ref[3:4, :]
    norm = jnp.sqrt(_head_sums(kap * kap, bd_ref[...]))
    kh = kap / jnp.maximum(norm, 1e-12)
    kmod = rk * (1.0 + (a - 1.0) * rpar_ref[4:5, :])
    kh_ref[0] = kh
    abo_ref[0] = kh * a
    km_ref[0] = kmod
    bon_ref[0] = _head_sums(rr * kmod * rpar_ref[5:6, :], bd_ref[...]) * rv


def _prep(pc, pn, cw, rpar, wbf, wbb, ab, gb, bd):
    b, t, _ = pc.shape
    nt = t // ROW_TILE
    per = ROW_TILE // HALO
    nh = t // HALO
    w = W_M
    full = lambda shp: pl.BlockSpec(shp, lambda i, j: tuple(0 for _ in shp))
    in_specs = [
        pl.BlockSpec((1, HALO, CONV_CH), lambda i, j: (i, jnp.maximum(j * per - 1, 0), 0)),
        pl.BlockSpec((1, ROW_TILE, CONV_CH), lambda i, j: (i, j, 0)),
        pl.BlockSpec((1, HALO, CONV_CH), lambda i, j: (i, jnp.minimum(j * per + per, nh - 1), 0)),
        pl.BlockSpec((1, ROW_TILE, LORA_TAIL_W), lambda i, j: (i, j, NONCONV_W // LORA_TAIL_W - 1)),
        full((16, CONV_CH)), full((8, w)), full((384, w)), full((384, w)), full((384, w)), full((384, w)),
        full((2 * HEAD_BLK, HEAD_BLK)),
    ]
    out = pl.BlockSpec((1, ROW_TILE, w), lambda i, j: (i, j, 0))
    return pl.pallas_call(
        _prep_kernel,
        grid=(b, nt),
        in_specs=in_specs,
        out_specs=[out] * 11,
        out_shape=[jax.ShapeDtypeStruct((b, t, w), jnp.bfloat16 if name in ("q", "k", "v") else jnp.float32)
                   for name in ("q", "k", "r", "kmod", "v", "kh", "ab", "lwf", "lwb", "gg", "bonus")],
        scratch_shapes=[pltpu.VMEM((2 * HALO + ROW_TILE, w), jnp.float32)],
        compiler_params=pltpu.CompilerParams(
            dimension_semantics=("arbitrary", "arbitrary"), vmem_limit_bytes=VMEM_LIMIT),
        name="conv_prep",
    )(pc, pc, pc, pn, cw, rpar, wbf, wbb, ab, gb, bd)


def _round_robin(gens):
    out = [None] * len(gens)
    live = list(range(len(gens)))
    while live:
        nxt = []
        for i in live:
            try:
                next(gens[i])
                nxt.append(i)
            except StopIteration as stop:
                out[i] = stop.value
        live = nxt
    return out


def _rwkv_pair_chunk(r, k, v, kh, ab, lw, s_mat, tri_incl, strict, incl, last_row, m0, m1):
    cw = _dot_exact_lhs(tri_incl, lw)
    yield
    e_pos = jnp.exp(cw)
    e_neg = jnp.exp(-cw)
    e_prev = jnp.exp(cw - lw)
    stack = lambda a: jnp.concatenate([a * m0, a * m1], axis=0)
    kt = stack(kh * e_prev)
    bt = stack(ab * e_neg)
    kk = stack(k * e_neg)
    rt = stack(r * e_pos)
    vs = stack(v)
    zero = jnp.zeros((), jnp.float32)
    kr = jnp.concatenate([kt, rt], axis=0)
    bk = jnp.concatenate([bt, kk], axis=0)
    amat = _dot1(kr, bk, _NT)
    yield
    a_bk = jnp.where(strict, amat[:PAIR, :PAIR], zero)
    a_kk = jnp.where(strict, amat[:PAIR, PAIR:], zero)
    a_rb = jnp.where(incl, amat[PAIR:, :PAIR], zero)
    a_rk = jnp.where(incl, amat[PAIR:, PAIR:], zero)
    n_mat = -a_bk
    eye = (lax.broadcasted_iota(jnp.int32, (PAIR, PAIR), 0)
           == lax.broadcasted_iota(jnp.int32, (PAIR, PAIR), 1)).astype(jnp.float32)
    q = eye + n_mat
    p = _dot1(n_mat, n_mat)
    ks = _dot1(kr, s_mat, _NT)
    av = _dot1(jnp.concatenate([a_kk, a_rk], axis=0), vs)
    yield
    for _ in range(4):
        qp = _dot1(jnp.concatenate([q, p], axis=0), p)
        yield
        q = q + qp[:PAIR]
        p = qp[PAIR:]
    t_inv = q + _dot1(q, p)
    yield
    u = -_dot1(t_inv, ks[:PAIR] + av[:PAIR])
    yield
    y = ks[PAIR:] + av[PAIR:] + _dot1(a_rb, u)
    yield
    w_last = jnp.sum(jnp.where(last_row, e_pos, zero), axis=0, keepdims=True)
    s_new = (s_mat + _dot1(jnp.concatenate([u, vs], axis=0), bk, _TN)) * w_last
    return y[:CHUNK] + y[CHUNK:], s_new


def _rwkv_kernel(rf, kf, vf, khf, abf, lwf, rb, kb, vb, khb, abb, lwb, yf_ref, yb_ref, s_ref, *, nctx):
    j = pl.program_id(1)

    @pl.when(j == 0)
    def _():
        s_ref[...] = jnp.zeros_like(s_ref)

    row = lax.broadcasted_iota(jnp.int32, (PAIR, PAIR), 0)
    col = lax.broadcasted_iota(jnp.int32, (PAIR, PAIR), 1)
    same = (row // CHUNK) == (col // CHUNK)
    r64 = lax.broadcasted_iota(jnp.int32, (CHUNK, CHUNK), 0)
    c64 = lax.broadcasted_iota(jnp.int32, (CHUNK, CHUNK), 1)
    rowl = lax.broadcasted_iota(jnp.int32, (CHUNK, PAIR), 0)
    lane = lax.broadcasted_iota(jnp.int32, (1, PAIR), 1)
    m0 = (lane < D_HR).astype(jnp.float32)
    m1 = 1.0 - m0
    dirs = (
        (rf, kf, vf, khf, abf, lwf, yf_ref, (c64 <= r64), same & (col < row), same & (col <= row), rowl == CHUNK - 1),
        (rb, kb, vb, khb, abb, lwb, yb_ref, (c64 >= r64), same & (col > row), same & (col >= row), rowl == 0),
    )
    nb = rf.shape[0]
    states = [[[s_ref[bi, d, p] for p in range(NPAIR)] for d in range(2)] for bi in range(nb)]
    gens, meta = [], []
    for d, (r_ref, k_ref, v_ref, kh_ref, ab_ref, lw_ref, y_ref, tri, strict, incl, last_row) in enumerate(dirs):
        tri = tri.astype(jnp.float32).astype(jnp.bfloat16)
        for bi in range(nb):
            for p in range(NPAIR):
                sl = slice(p * PAIR, (p + 1) * PAIR)
                gens.append(_rwkv_pair_chunk(r_ref[bi, :, sl], k_ref[bi, :, sl], v_ref[bi, :, sl],
                                             kh_ref[bi, :, sl], ab_ref[bi, :, sl], lw_ref[bi, :, sl],
                                             states[bi][d][p], tri, strict, incl, last_row, m0, m1))
                meta.append((bi, d, p, sl, y_ref))
    results = [m + r for m, r in zip(meta, _round_robin(gens))]
    for bi, d, p, sl, y_ref, y, s_new in results:
        s_ref[bi, d, p] = s_new
        y_ref[bi, :, sl] = y


def _rwkv_scan(r, k, v, kh, ab, lw_f, lw_b, ctx_len):
    b, t, w = r.shape
    nc = t // CHUNK
    nctx = ctx_len // CHUNK
    nlat = nc - nctx

    def fwd_map(i, j):
        return (i, j, 0)

    def bwd_map(i, j):
        return (i, jnp.where(j < nctx, nctx - 1 - j, nc - 1 - (j - nctx)), 0)

    nb = math.gcd(b, SCAN_BATCH)
    blk = (nb, CHUNK, w)
    in_specs = [pl.BlockSpec(blk, fwd_map)] * 6 + [pl.BlockSpec(blk, bwd_map)] * 6
    out_specs = [
        pl.BlockSpec(blk, lambda i, j: (i, jnp.maximum(j - nctx, 0), 0)),
        pl.BlockSpec(blk, lambda i, j: (i, nlat - 1 - jnp.maximum(j - nctx, 0), 0)),
    ]
    return pl.pallas_call(
        functools.partial(_rwkv_kernel, nctx=nctx),
        grid=(b // nb, nc),
        in_specs=in_specs,
        out_specs=out_specs,
        out_shape=[jax.ShapeDtypeStruct((b, t - ctx_len, w), jnp.float32)] * 2,
        scratch_shapes=[pltpu.VMEM((nb, 2, NPAIR, PAIR, PAIR), jnp.float32)],
        compiler_params=pltpu.CompilerParams(
            dimension_semantics=("arbitrary", "arbitrary"), vmem_limit_bytes=VMEM_LIMIT),
        name="rwkv_scan",
    )(r, k, v, kh, ab, lw_f, r, k, v, kh, ab, lw_b)


def _scan_max(x, reverse):
    rows = lax.broadcasted_iota(jnp.int32, x.shape, 0)
    neg_inf = jnp.full((), -jnp.inf, jnp.float32)
    step = 1
    while step < CHUNK:
        if reverse:
            shifted = jnp.where(rows < CHUNK - step, pltpu.roll(x, CHUNK - step, 0), neg_inf)
        else:
            shifted = jnp.where(rows >= step, pltpu.roll(x, step, 0), neg_inf)
        x = jnp.maximum(x, shifted)
        step *= 2
    return x


def _mlstm_gates(gates, bias, tri, m_prev, is_f, reverse):
    gl = gates + bias
    act = jnp.where(is_f, jax.nn.log_sigmoid(gl), gl)
    b_all = _dot_exact_lhs(tri, act)
    li = pltpu.roll(act, H_M, 1)
    x = li - b_all
    m_inter = b_all + m_prev
    m_t = jnp.maximum(m_inter, b_all + _scan_max(x, reverse))
    b_last = b_all[0:1, :] if reverse else b_all[CHUNK - 1:CHUNK, :]
    g = b_last + x
    m_new = jnp.maximum(b_last + m_prev, jnp.max(g, axis=0, keepdims=True))
    wts = jnp.exp(g - m_new)
    decay = jnp.exp(b_last + m_prev - m_new)
    return b_all - m_t, jnp.exp(m_inter - m_t), jnp.exp(-m_t), x, wts, decay, m_new


def _mlstm_pair_chunk(q, k, v, pieces, cf, decay_g, c_mat, n_row, incl, m0, m1):
    stack = lambda a: jnp.concatenate([a * m0, a * m1], axis=0)
    q_st, k_st, v_st = stack(q), stack(k), stack(v)
    lane = lax.broadcasted_iota(jnp.int32, (1, PAIR), 1)
    zero_bf = jnp.zeros((), jnp.bfloat16)
    sel = lambda a: jnp.concatenate([jnp.where(lane == cf, a, zero_bf), jnp.where(lane == cf + 1, a, zero_bf)], 0)
    ones = jnp.ones((PAIR, PAIR), jnp.bfloat16)
    spread = [sel(a) for a in pieces[0]] + [sel(p[0]) for p in pieces[1:4]]
    bc = _dg(jnp.concatenate(spread, axis=0), ones, _NN)
    bm_bc = bc[:PAIR] + bc[PAIR:2 * PAIR] + bc[2 * PAIR:3 * PAIR]
    inter_bc, enm_bc, wts_bc = (bc[i * PAIR:(i + 1) * PAIR] for i in range(3, 6))
    r3 = _dg(ones, jnp.concatenate([sel(a) for a in pieces[4]], axis=0), _NT)
    r_mat = r3[:, :PAIR] + r3[:, PAIR:2 * PAIR] + r3[:, 2 * PAIR:]
    qk = _dot1(q_st, k_st, _NT)
    qc = _dot1(q_st, c_mat)
    yield
    neg_inf = jnp.full((), -jnp.inf, jnp.float32)
    s = qk * jnp.exp(jnp.where(incl, bm_bc + r_mat, neg_inf))
    wk = wts_bc * k_st
    sv = _dot1(s, v_st)
    kv = _dot1(wk, v_st, _TN)
    sums = _dot1(jnp.concatenate([q_st * n_row, s], axis=0), jnp.ones((PAIR, PAIR), jnp.float32))
    yield
    den = inter_bc * sums[:PAIR] + sums[PAIR:]
    h_st = (inter_bc * qc + sv) / jnp.maximum(jnp.abs(den), enm_bc)
    h = h_st[:CHUNK] + h_st[CHUNK:]
    decay_row = jnp.where(lane < D_HM, decay_g[:, cf:cf + 1], decay_g[:, cf + 1:cf + 2])
    c_new = decay_row * c_mat + kv
    n_new = decay_row * n_row + jnp.sum(wk, axis=0, keepdims=True)
    return h, c_new, n_new


def _mlstm_kernel(qf, kf, vf, gf, qb, kb, vb, gb, bias_ref, hf_ref, hb_ref, c_ref, n_ref, m_ref, *, nctx):
    del nctx
    j = pl.program_id(1)

    @pl.when(j == 0)
    def _():
        c_ref[...] = jnp.zeros_like(c_ref)
        n_ref[...] = jnp.zeros_like(n_ref)
        m_ref[...] = jnp.zeros_like(m_ref)

    row = lax.broadcasted_iota(jnp.int32, (PAIR, PAIR), 0)
    col = lax.broadcasted_iota(jnp.int32, (PAIR, PAIR), 1)
    same = (row // CHUNK) == (col // CHUNK)
    r64 = lax.broadcasted_iota(jnp.int32, (CHUNK, CHUNK), 0)
    c64 = lax.broadcasted_iota(jnp.int32, (CHUNK, CHUNK), 1)
    lane = lax.broadcasted_iota(jnp.int32, (1, PAIR), 1)
    m0 = (lane < D_HM).astype(jnp.float32)
    m1 = 1.0 - m0
    is_f = (lane % (2 * H_M)) >= H_M
    dirs = (
        (qf, kf, vf, gf, hf_ref, (c64 <= r64), same & (col <= row)),
        (qb, kb, vb, gb, hb_ref, (c64 >= r64), same & (col >= row)),
    )
    nb = qf.shape[0]
    states = [[[(c_ref[bi, d, p], n_ref[bi, d, p]) for p in range(NPAIR)] for d in range(2)] for bi in range(nb)]
    gens, meta, m_news = [], [], []
    lead = lambda a: (a.astype(jnp.bfloat16),)
    for d, (q_ref, k_ref, v_ref, g_ref, h_ref, tri, incl) in enumerate(dirs):
        tri = tri.astype(jnp.float32).astype(jnp.bfloat16)
        for bi in range(nb):
            bm, inter, enm, x, wts, decay_g, m_new = _mlstm_gates(g_ref[bi], bias_ref[...], tri, m_ref[bi, d], is_f,
                                                                  d == 1)
            m_news.append((bi, d, m_new))
            pieces = [_split3(bm), lead(inter), lead(enm), lead(wts), _split3(x)]
            for p in range(NPAIR):
                sl = slice(p * PAIR, (p + 1) * PAIR)
                cf = d * 2 * H_M + H_M + 2 * p
                gens.append(_mlstm_pair_chunk(q_ref[bi, :, sl], k_ref[bi, :, sl], v_ref[bi, :, sl], pieces, cf,
                                               decay_g, *states[bi][d][p], incl, m0, m1))
                meta.append((bi, d, p, sl, h_ref))
    for (bi, d, p, sl, h_ref), (h, c_new, n_new) in zip(meta, _round_robin(gens)):
        c_ref[bi, d, p] = c_new
        n_ref[bi, d, p] = n_new
        h_ref[bi, :, sl] = h
    for bi, d, m_new in m_news:
        m_ref[bi, d] = m_new


def _mlstm_scan(q, k, pn, bias_row, ctx_len):
    v = gates = pn
    b, t, w = q.shape
    nc = t // CHUNK
    nctx = ctx_len // CHUNK
    nlat = nc - nctx

    def fwd_map(i, j):
        return (i, j, 0)

    def bwd_map(i, j):
        return (i, jnp.where(j < nctx, nctx - 1 - j, nc - 1 - (j - nctx)), 0)

    nb = math.gcd(b, SCAN_BATCH)
    blk = (nb, CHUNK, w)
    gblk = (nb, CHUNK, 128)
    gate_blk = (2 * W_M) // 128
    gate_of = lambda m: (lambda i, j: m(i, j)[:2] + (gate_blk,))
    in_specs = ([pl.BlockSpec(blk, fwd_map)] * 3 + [pl.BlockSpec(gblk, gate_of(fwd_map))]
                + [pl.BlockSpec(blk, bwd_map)] * 3 + [pl.BlockSpec(gblk, gate_of(bwd_map))]
                + [pl.BlockSpec((1, 128), lambda i, j: (0, 0))])
    out_specs = [
        pl.BlockSpec(blk, lambda i, j: (i, jnp.maximum(j - nctx, 0), 0)),
        pl.BlockSpec(blk, lambda i, j: (i, nlat - 1 - jnp.maximum(j - nctx, 0), 0)),
    ]
    return pl.pallas_call(
        functools.partial(_mlstm_kernel, nctx=nctx),
        grid=(b // nb, nc),
        in_specs=in_specs,
        out_specs=out_specs,
        out_shape=[jax.ShapeDtypeStruct((b, t - ctx_len, w), jnp.float32)] * 2,
        scratch_shapes=[pltpu.VMEM((nb, 2, NPAIR, PAIR, PAIR), jnp.float32),
                        pltpu.VMEM((nb, 2, NPAIR, 1, PAIR), jnp.float32),
                        pltpu.VMEM((nb, 2, 1, 128), jnp.float32)],
        compiler_params=pltpu.CompilerParams(
            dimension_semantics=("arbitrary", "arbitrary"), vmem_limit_bytes=VMEM_LIMIT),
        name="mlstm_scan",
    )(q, k, v, gates, q, k, v, gates, bias_row)


ROUTE_W = 128
N_ROUTE = N_GROUPS + N_EXPERTS


def _ln_rows(z, eps):
    mu = jnp.mean(z, -1, keepdims=True)
    zc = z - mu
    var = jnp.mean(zc * zc, -1, keepdims=True)
    return zc * lax.rsqrt(var + eps)


OUT_TILES = 2


def _outproj_tile(rows, hf, hb, mo, yf, yb, gg, bonus, x_ref, mod_ref, wout_ref, hpar_ref, ln_ref, rtw_ref, rtb_ref,
                  bd_ref, x1_ref, tok_ref):
    avg = bd_ref[...]
    zm = hf[0, rows, :] + hb[0, rows, :]
    zr = yf[0, rows, :] + yb[0, rows, :]
    mu_m = _head_sums(zm, avg) * (1.0 / D_HM)
    mu_r = _head_sums(zr, avg) * (1.0 / D_HR)
    yield
    cm, cr = zm - mu_m, zr - mu_r
    var_m = _head_sums(cm * cm, avg) * (1.0 / D_HM)
    var_r = _head_sums(cr * cr, avg) * (1.0 / D_HR)
    yield
    out_m = jax.nn.sigmoid(mo[0]) * (cm * lax.rsqrt(var_m + LN_EPS) * hpar_ref[0:1, :])
    y = cr * lax.rsqrt(var_r + GN_EPS) * hpar_ref[1:2, :] + hpar_ref[2:3, :]
    out_r = (y + bonus[0]) * gg[0]
    proj = (jnp.dot(out_m.astype(jnp.bfloat16), wout_ref[:W_M, :], preferred_element_type=jnp.float32)
            + jnp.dot(out_r.astype(jnp.bfloat16), wout_ref[W_M:, :], preferred_element_type=jnp.float32))
    yield
    g1, sh2, sc2 = mod_ref[0, 2:3, :], mod_ref[0, 3:4, :], mod_ref[0, 4:5, :]
    x1 = _ln_rows(ALPHA * x_ref[0, rows, :] + g1 * proj, LN_EPS) * ln_ref[0:1, :] + ln_ref[1:2, :]
    x1_ref[0, rows, :] = x1
    tok = _ln_rows(x1, LN_EPS) * (1.0 + sc2) + sh2
    tok_ref[0, rows, :] = tok
    logits = _dot3_pre(tok, rtw_ref[...]) + rtb_ref[...]
    yield
    lane = lax.broadcasted_iota(jnp.int32, (ROW_TILE, ROUTE_W), 1)
    neg_inf = jnp.full((), -jnp.inf, jnp.float32)
    big = jnp.int32(ROUTE_W)
    is_g = lane < N_GROUPS
    lg = jnp.where(is_g, logits, neg_inf)
    gmax = jnp.max(lg, axis=1, keepdims=True)
    grp = jnp.min(jnp.where(lg == gmax, lane, big), axis=1, keepdims=True)
    p_top = 1.0 / jnp.sum(jnp.where(is_g, jnp.exp(lg - gmax), 0.0), axis=1, keepdims=True)
    in_grp = (lane >= N_GROUPS) & (lane < N_ROUTE) & ((lane - N_GROUPS) // E_PER_GROUP == grp)
    le = jnp.where(in_grp, logits, neg_inf)
    v1 = jnp.max(le, axis=1, keepdims=True)
    i1 = jnp.min(jnp.where(le == v1, lane, big), axis=1, keepdims=True)
    le2 = jnp.where(lane == i1, neg_inf, le)
    v2 = jnp.max(le2, axis=1, keepdims=True)
    i2 = jnp.min(jnp.where(le2 == v2, lane, big), axis=1, keepdims=True)
    e21 = jnp.exp(v2 - v1)
    w1 = (1.0 / (1.0 + e21)) * p_top
    w2 = (e21 / (1.0 + e21)) * p_top

    sel1, sel2 = lane == i1, lane == i2
    onehot = jnp.where(sel1 | sel2, 1.0, 0.0)
    tr = lax.broadcasted_iota(jnp.int32, (ROW_TILE, ROW_TILE), 0)
    tc = lax.broadcasted_iota(jnp.int32, (ROW_TILE, ROW_TILE), 1)
    strict = jnp.where(tc < tr, 1.0, 0.0).astype(jnp.bfloat16)
    before = _dg(strict, onehot.astype(jnp.bfloat16), _NN)
    yield
    e1 = (i1 - N_GROUPS).astype(jnp.float32)
    e2 = (i2 - N_GROUPS).astype(jnp.float32)
    return lane, sel1, sel2, onehot, before, e1, e2, w1, w2


def _outproj_kernel(hf, hb, mo_a, mo_b, yf, yb, gg_a, gg_b, bon_a, bon_b, x_ref, mod_ref, wout_ref, hpar_ref, ln_ref,
                    rtw_ref, rtb_ref, bd_ref, x1_ref, tok_ref, route_ref, ridx_ref, cnt_ref, carry_ref):
    i, j = pl.program_id(0), pl.program_id(1)

    @pl.when((i == 0) & (j == 0))
    def _():
        carry_ref[...] = jnp.zeros_like(carry_ref)

    tiles = ((mo_a, gg_a, bon_a), (mo_b, gg_b, bon_b))
    gens = [_outproj_tile(slice(t * ROW_TILE, (t + 1) * ROW_TILE), hf, hb, mo, yf, yb, gg, bonus, x_ref, mod_ref,
                          wout_ref, hpar_ref, ln_ref, rtw_ref, rtb_ref, bd_ref, x1_ref, tok_ref)
            for t, (mo, gg, bonus) in enumerate(tiles)]
    carry = carry_ref[...]
    for t, (lane, sel1, sel2, onehot, before, e1, e2, w1, w2) in enumerate(_round_robin(gens)):
        before = before + carry
        r1 = jnp.sum(jnp.where(sel1, before, 0.0), axis=1, keepdims=True)
        r2 = jnp.sum(jnp.where(sel2, before, 0.0), axis=1, keepdims=True)
        carry = carry + jnp.sum(onehot, axis=0, keepdims=True)
        route = jnp.where(lane == 0, e1, jnp.where(lane == 1, e2, jnp.where(lane == 2, r1, jnp.where(
            lane == 3, r2, jnp.where(lane == 4, w1, jnp.where(lane == 5, w2, 0.0))))))
        route_ref[0, t * ROW_TILE:(t + 1) * ROW_TILE, :] = route
        ridx_ref[t] = route.T[0:8, :].astype(jnp.int32)
    carry_ref[...] = carry
    cnt_ref[...] = jnp.broadcast_to(carry, cnt_ref.shape)


def _outproj(h_f, h_b, pn, y_f, y_b, gg, bonus, x, mod3, wout_bf16, hpar, lnpar, rtw, rtb, bd, ctx_len):
    b, seq, d = x.shape
    nt = seq // ROW_TILE
    assert nt % OUT_TILES == 0
    off = ctx_len // ROW_TILE
    step_rows = OUT_TILES * ROW_TILE
    lat = lambda w: pl.BlockSpec((1, step_rows, w), lambda i, j: (i, j, 0))
    full = lambda shp: pl.BlockSpec(shp, lambda i, j: tuple(0 for _ in shp))
    shifted = lambda w, t, lane_blk: pl.BlockSpec((1, ROW_TILE, w),
                                                  lambda i, j: (i, OUT_TILES * j + t + off, lane_blk))
    in_specs = [
        lat(W_M), lat(W_M),
        shifted(W_M, 0, 1), shifted(W_M, 1, 1),
        lat(W_R), lat(W_R),
        shifted(W_R, 0, 0), shifted(W_R, 1, 0),
        shifted(W_R, 0, 0), shifted(W_R, 1, 0),
        lat(d),
        pl.BlockSpec((1, 6, d), lambda i, j: (i, 0, 0)),
        full((MIX_W, d)), full((8, W_M)), full((8, d)), full((3 * d, ROUTE_W)), full((1, ROUTE_W)),
        full((2 * HEAD_BLK, HEAD_BLK)),
    ]
    out_specs = [
        lat(d), lat(d), lat(ROUTE_W),
        pl.BlockSpec((OUT_TILES, 8, ROW_TILE), lambda i, j: (i * (nt // OUT_TILES) + j, 0, 0)),
        full((8, ROUTE_W)),
    ]
    out_shape = [
        jax.ShapeDtypeStruct((b, seq, d), jnp.float32),
        jax.ShapeDtypeStruct((b, seq, d), jnp.float32),
        jax.ShapeDtypeStruct((b, seq, ROUTE_W), jnp.float32),
        jax.ShapeDtypeStruct((b * nt, 8, ROW_TILE), jnp.int32),
        jax.ShapeDtypeStruct((8, ROUTE_W), jnp.float32),
    ]
    return pl.pallas_call(
        _outproj_kernel,
        grid=(b, nt // OUT_TILES),
        in_specs=in_specs,
        out_specs=out_specs,
        out_shape=out_shape,
        scratch_shapes=[pltpu.VMEM((1, ROUTE_W), jnp.float32)],
        compiler_params=pltpu.CompilerParams(
            dimension_semantics=("arbitrary", "arbitrary"), vmem_limit_bytes=VMEM_LIMIT),
        name="outproj_router",
    )(h_f, h_b, pn, pn, y_f, y_b, gg, gg, bonus, bonus, x, mod3, wout_bf16, hpar, lnpar, rtw, rtb, bd)


DEST_TILES = 8


def _dest_kernel(starts_ref, ridx_ref, dest_ref):
    blk = ridx_ref[...]
    dest = pltpu.roll(blk, 8 - TOP_K_IN_GROUP, 1)
    for e in range(N_EXPERTS):
        dest = dest + jnp.where(blk == e, starts_ref[e], 0)
    dest_ref[...] = dest


def _moe_dest(starts, ridx):
    nt = ridx.shape[0]
    per_step = math.gcd(nt, DEST_TILES)
    blk = pl.BlockSpec((per_step, 8, ROW_TILE), lambda i, s: (i, 0, 0))
    return pl.pallas_call(
        _dest_kernel,
        grid_spec=pltpu.PrefetchScalarGridSpec(num_scalar_prefetch=1, grid=(nt // per_step,),
                                               in_specs=[blk], out_specs=blk),
        out_shape=jax.ShapeDtypeStruct(ridx.shape, jnp.int32),
        compiler_params=pltpu.CompilerParams(dimension_semantics=("arbitrary",), vmem_limit_bytes=VMEM_LIMIT),
        name="moe_dest",
    )(starts, ridx)


SUB = 8
SUB_SHIFT = 3
TILE_GROUPS = ROW_TILE // SUB


def _scatter_kernel(dest_ref, tok_ref, xs_in_ref, xs_ref, sem):
    del xs_in_ref

    def issue(g, c):
        for k in range(SUB):
            for slot in range(TOP_K_IN_GROUP):
                dst = dest_ref[slot * ROW_TILE + g * SUB + k]
                pltpu.make_async_copy(tok_ref.at[g, pl.ds(k, 1), :],
                                      xs_ref.at[dst >> SUB_SHIFT, pl.ds(dst & (SUB - 1), 1), :], sem).start()
        return c

    lax.fori_loop(0, TILE_GROUPS, issue, 0)
    for half in range(TOP_K_IN_GROUP):
        pltpu.make_async_copy(tok_ref, xs_ref.at[pl.ds(half * TILE_GROUPS, TILE_GROUPS)], sem).wait()


def _moe_scatter(dest_flat, tok2d, n_rows):
    n, d = tok2d.shape
    xs_zero = jnp.zeros((n_rows // SUB, SUB, d), tok2d.dtype)
    nt = n // ROW_TILE
    xs = pl.pallas_call(
        _scatter_kernel,
        grid=(nt,),
        in_specs=[
            pl.BlockSpec((SUB * ROW_TILE,), lambda i: (i,), memory_space=pltpu.SMEM),
            pl.BlockSpec((TILE_GROUPS, SUB, d), lambda i: (i, 0, 0)),
            pl.BlockSpec(memory_space=pl.ANY),
        ],
        out_specs=pl.BlockSpec(memory_space=pl.ANY),
        scratch_shapes=[pltpu.SemaphoreType.DMA(())],
        out_shape=jax.ShapeDtypeStruct(xs_zero.shape, tok2d.dtype),
        input_output_aliases={2: 0},
        compiler_params=pltpu.CompilerParams(dimension_semantics=("arbitrary",), vmem_limit_bytes=VMEM_LIMIT),
        name="moe_scatter",
    )(dest_flat, tok2d.reshape(n // SUB, SUB, d), xs_zero)
    return xs.reshape(n_rows, d)


def _expert_kernel(blk_exp_ref, valid_ref, xs_ref, wg_ref, wu_ref, wd_ref, ys_ref, wg_bf, wu_bf, wd_bf):
    i = pl.program_id(0)
    valid = valid_ref[i]

    @pl.when((i == 0) | (blk_exp_ref[i] != blk_exp_ref[jnp.maximum(i - 1, 0)]))
    def _():
        wg_bf[...] = wg_ref[0].astype(jnp.bfloat16)
        wu_bf[...] = wu_ref[0].astype(jnp.bfloat16)
        wd_bf[...] = wd_ref[0].astype(jnp.bfloat16)

    @pl.when(valid > 0)
    def _():
        xb = xs_ref[...].astype(jnp.bfloat16)
        hg = jnp.dot(xb, wg_bf[...], preferred_element_type=jnp.float32)
        hu = jnp.dot(xb, wu_bf[...], preferred_element_type=jnp.float32)
        hb = (jax.nn.silu(hg) * hu).astype(jnp.bfloat16)
        ys_ref[...] = jnp.dot(hb, wd_bf[...], preferred_element_type=jnp.float32)

    @pl.when(valid == 0)
    def _():
        ys_ref[...] = jnp.zeros_like(ys_ref)


def _moe_experts(blk_exp, blk_valid, xs, wg, wu, wd):
    nrow, d = xs.shape
    de = wg.shape[2]
    return pl.pallas_call(
        _expert_kernel,
        grid_spec=pltpu.PrefetchScalarGridSpec(
            num_scalar_prefetch=2,
            grid=(nrow // MOE_BLOCK,),
            in_specs=[
                pl.BlockSpec((MOE_BLOCK, d), lambda i, be, nu: (i, 0)),
                pl.BlockSpec((1, d, de), lambda i, be, nu: (be[i], 0, 0)),
                pl.BlockSpec((1, d, de), lambda i, be, nu: (be[i], 0, 0)),
                pl.BlockSpec((1, de, d), lambda i, be, nu: (be[i], 0, 0)),
            ],
            out_specs=pl.BlockSpec((MOE_BLOCK, d), lambda i, be, nu: (i, 0)),
            scratch_shapes=[pltpu.VMEM((d, de), jnp.bfloat16), pltpu.VMEM((d, de), jnp.bfloat16),
                            pltpu.VMEM((de, d), jnp.bfloat16)],
        ),
        out_shape=jax.ShapeDtypeStruct((nrow, d), jnp.float32),
        compiler_params=pltpu.CompilerParams(dimension_semantics=("arbitrary",), vmem_limit_bytes=VMEM_LIMIT),
        name="moe_experts",
    )(blk_exp, blk_valid, xs, wg, wu, wd)


def _combine_kernel(dest_ref, dest_next_ref, route_ref, x1_ref, mod_ref, ln_ref, ys_ref, out_ref,
                    ybuf, sems):
    i = pl.program_id(0)
    n = pl.num_programs(0)
    cur = i % 2

    def gather(idx_ref, buf):
        def issue(g, c):
            for k in range(SUB):
                for slot in range(TOP_K_IN_GROUP):
                    src = idx_ref[slot * ROW_TILE + g * SUB + k]
                    pltpu.make_async_copy(ys_ref.at[src >> SUB_SHIFT, pl.ds(src & (SUB - 1), 1), :],
                                          ybuf.at[buf, slot * TILE_GROUPS + g, pl.ds(k, 1), :],
                                          sems.at[buf]).start()
            return c

        lax.fori_loop(0, TILE_GROUPS, issue, 0)

    @pl.when(i == 0)
    def _():
        gather(dest_ref, 0)

    @pl.when(i + 1 < n)
    def _():
        gather(dest_next_ref, 1 - cur)

    pltpu.make_async_copy(ys_ref.at[pl.ds(0, TOP_K_IN_GROUP * TILE_GROUPS)], ybuf.at[cur], sems.at[cur]).wait()
    route = route_ref[0]
    d = out_ref.shape[-1]
    y0 = ybuf[cur, 0:TILE_GROUPS].reshape(ROW_TILE, d)
    y1 = ybuf[cur, TILE_GROUPS:2 * TILE_GROUPS].reshape(ROW_TILE, d)
    ffn = y0 * route[:, 4:5] + y1 * route[:, 5:6]
    g2 = mod_ref[0, 5:6, :]
    out_ref[0] = _ln_rows(ALPHA * x1_ref[0] + g2 * ffn, LN_EPS) * ln_ref[2:3, :] + ln_ref[3:4, :]


def _moe_combine(dest_flat, route, x1, mod3, lnpar, ys):
    b, seq, d = x1.shape
    nt = seq // ROW_TILE
    n = b * nt
    tile = lambda w: pl.BlockSpec((1, ROW_TILE, w), lambda i: (i // nt, i % nt, 0))
    idx_blk = lambda f: pl.BlockSpec((SUB * ROW_TILE,), f, memory_space=pltpu.SMEM)
    return pl.pallas_call(
        _combine_kernel,
        grid=(n,),
        in_specs=[
            idx_blk(lambda i: (i,)),
            idx_blk(lambda i: (jnp.minimum(i + 1, n - 1),)),
            tile(ROUTE_W),
            tile(d),
            pl.BlockSpec((1, 6, d), lambda i: (i // nt, 0, 0)),
            pl.BlockSpec((8, d), lambda i: (0, 0)),
            pl.BlockSpec(memory_space=pl.ANY),
        ],
        out_specs=tile(d),
        scratch_shapes=[pltpu.VMEM((2, TOP_K_IN_GROUP * TILE_GROUPS, SUB, d), jnp.float32),
                        pltpu.SemaphoreType.DMA((2,))],
        out_shape=jax.ShapeDtypeStruct((b, seq, d), jnp.float32),
        compiler_params=pltpu.CompilerParams(dimension_semantics=("arbitrary",), vmem_limit_bytes=VMEM_LIMIT),
        name="moe_combine",
    )(dest_flat, dest_flat, route, x1, mod3, lnpar, ys.reshape(ys.shape[0] // SUB, SUB, d))


def _moe_plan(cnt, n_pairs):
    counts = cnt[0, N_GROUPS:N_ROUTE].astype(jnp.int32)
    padded = (counts + MOE_BLOCK - 1) // MOE_BLOCK * MOE_BLOCK
    pad_ends = jnp.cumsum(padded)
    pad_starts = pad_ends - padded
    n_blk = -(-n_pairs // MOE_BLOCK) + N_EXPERTS
    blk_row0 = jnp.arange(n_blk, dtype=jnp.int32) * MOE_BLOCK
    blk_exp = jnp.minimum(jnp.sum((pad_ends[None, :] <= blk_row0[:, None]).astype(jnp.int32), axis=1), N_EXPERTS - 1)
    blk_valid = jnp.clip(counts[blk_exp] - (blk_row0 - pad_starts[blk_exp]), 0, MOE_BLOCK)
    blk_valid = jnp.where(blk_row0 < pad_ends[-1], blk_valid, 0).astype(jnp.int32)
    return pad_starts.astype(jnp.int32), blk_exp, blk_valid, n_blk


def kernel(x, c, ctx, c_ctx, w_ada, b_ada, w_in, conv_w, m_bias_i, m_bias_f, m_norm_w, r_w0, r_wB, r_a0, r_aB,
           r_gB, r_kk, r_ka, r_bonus, r_norm_w, r_norm_b, w_out, ln1_g, ln1_b, ln2_g, ln2_b, rt_g, rt_g_b, rt_e,
           rt_e_b, ex_gate, ex_up, ex_down):
    assert w_ada.shape[0] == DEPTH
    bsz, seq, d = x.shape
    ctx_len = ctx.shape[1]
    assert ctx_len == ROW_TILE and seq % ROW_TILE == 0
    assert seq % GRID_W == 0 and ROW_TILE % GRID_W == 0
    l = 0
    zrow = lambda n, w: jnp.zeros((n, w), jnp.float32)
    mrows = -(-(bsz + 1) // 8) * 8
    c_pad = jnp.concatenate([c, c_ctx[None, :], zrow(mrows - bsz - 1, d)], 0)
    mod_all = _ada(c_pad, w_ada[l], b_ada[l][None, :])[:bsz + 1]
    mod3 = mod_all.reshape(bsz + 1, 6, d)
    pc, pn = _inproj(ctx, x, mod3, _pad_w_in(w_in[l]))
    cw = jnp.concatenate([conv_w[l].reshape(CONV_K * CONV_K, CONV_CH), zrow(16 - CONV_K * CONV_K, CONV_CH)], 0)
    rpar = jnp.concatenate([r_w0[l], r_a0[l][None, :], r_kk[l][None, :], r_ka[l][None, :],
                            r_bonus[l].reshape(1, W_R), zrow(2, W_R)], 0)
    wbf = jnp.concatenate([r_wB[l][0], zrow(W_LORA, W_R)], 0)
    wbb = jnp.concatenate([zrow(W_LORA, W_R), r_wB[l][1]], 0)
    ab_mat = jnp.concatenate([r_aB[l], zrow(128 - A_LORA, W_R)], 0)
    bd = _head_sum_operator(D_HR)
    q, k, rr, kmod, rv, kh, kha, logw_f, logw_b, gg, bonus = _prep(
        pc, pn, cw, rpar, _stack3(wbf), _stack3(wbb), _stack3(ab_mat), _stack3(r_gB[l]), bd)
    bias_row = jnp.concatenate([m_bias_i[l][0], m_bias_f[l][0], m_bias_i[l][1], m_bias_f[l][1],
                                jnp.zeros((128 - 4 * H_M,), jnp.float32)])[None, :]
    h_f, h_b = _mlstm_scan(q, k, pn, bias_row, ctx_len)
    y_f, y_b = _rwkv_scan(rr, kmod, rv, kh, kha, logw_f, logw_b, ctx_len)
    hpar = jnp.zeros((8, W_M), jnp.float32).at[0].set(m_norm_w[l]).at[1].set(r_norm_w[l]).at[2].set(r_norm_b[l])
    lnpar = jnp.zeros((8, d), jnp.float32).at[0].set(ln1_g[l]).at[1].set(ln1_b[l]).at[2].set(ln2_g[l]).at[3].set(
        ln2_b[l])
    rtw = jnp.concatenate([rt_g[l], rt_e[l], jnp.zeros((d, ROUTE_W - N_ROUTE), jnp.float32)], axis=1)
    rtb = jnp.concatenate([rt_g_b[l], rt_e_b[l], jnp.zeros((ROUTE_W - N_ROUTE,), jnp.float32)])[None, :]
    x1, tok, route, ridx, cnt = _outproj(h_f, h_b, pn, y_f, y_b, gg, bonus, x, mod3, w_out[l].astype(jnp.bfloat16),
                                         hpar, lnpar, _stack3(rtw), rtb, bd, ctx_len)
    n_tok = bsz * seq
    starts, blk_exp, blk_valid, n_blk = _moe_plan(cnt, n_tok * TOP_K_IN_GROUP)
    dest = _moe_dest(starts, ridx)
    dest_flat = dest.reshape(-1)
    xs = _moe_scatter(dest_flat, tok.reshape(n_tok, d), n_blk * MOE_BLOCK)
    ys = _moe_experts(blk_exp, blk_valid, xs, ex_gate[l], ex_up[l], ex_down[l])
    return _moe_combine(dest_flat, route, x1, mod3, lnpar, ys)
```

```python
import functools
import math

import jax
import jax.numpy as jnp
from jax import lax
import numpy as np
from jax.experimental import pallas as pl
from jax.experimental.pallas import tpu as pltpu

H_M = 8
D_HM = 64
W_M = H_M * D_HM
H_R = 8
D_HR = 64
W_R = H_R * D_HR
MIX_W = W_M + W_R
W_LORA = 64
A_LORA = 64
G_LORA = 128
CONV_K = 3
CONV_CH = 2 * W_M + 3 * W_R
SECTION_WIDTHS = (W_M, W_M, W_R, W_R, W_R, W_M, W_M, 4 * H_M, W_LORA, W_LORA, A_LORA, G_LORA)
IN_COLS = sum(SECTION_WIDTHS)
SPLIT_POINTS = tuple(int(v) for v in np.cumsum(SECTION_WIDTHS)[:-1])
CHUNK = 64
N_GROUPS = 4
E_PER_GROUP = 8
N_EXPERTS = N_GROUPS * E_PER_GROUP
TOP_K_IN_GROUP = 2
D_EXPERT = 512
MOE_BLOCK = 256
DEPTH = 1
ALPHA = (2.0 * DEPTH) ** 0.25
DECAY_SCALE = math.exp(-0.5)
LN_EPS = 1e-6
GN_EPS = 64e-5

ROW_TILE = 256
NONCONV_W = 1536
LORA_TAIL_W = 384
VMEM_LIMIT = 56 * 1024 * 1024
SCAN_BATCH = 4


def _inproj_kernel(ctx_ref, x_ref, mod_ref, w_ref, oc_ref, on_ref):
    j = pl.program_id(1)

    def body(src_ref):
        z = src_ref[0]
        mu = jnp.mean(z, -1, keepdims=True)
        zc = z - mu
        var = jnp.mean(zc * zc, -1, keepdims=True)
        y = zc * lax.rsqrt(var + LN_EPS)
        h = (y * (1.0 + mod_ref[0, 1:2, :]) + mod_ref[0, 0:1, :]).astype(jnp.bfloat16)
        oc_ref[0] = jnp.dot(h, w_ref[:, :CONV_CH], preferred_element_type=jnp.float32)
        on_ref[0] = jnp.dot(h, w_ref[:, CONV_CH:], preferred_element_type=jnp.float32)

    @pl.when(j == 0)
    def _():
        body(ctx_ref)

    @pl.when(j > 0)
    def _():
        body(x_ref)


def _inproj(ctx, x, mod_all, w_pad):
    b, seq, d = x.shape
    nt = (ctx.shape[1] + seq) // ROW_TILE
    t_all = ctx.shape[1] + seq
    return pl.pallas_call(
        _inproj_kernel,
        grid=(b, nt),
        in_specs=[
            pl.BlockSpec((1, ROW_TILE, d), lambda i, j: (i, 0, 0)),
            pl.BlockSpec((1, ROW_TILE, d), lambda i, j: (i, jnp.maximum(j - 1, 0), 0)),
            pl.BlockSpec((1, 6, d), lambda i, j: (jnp.where(j == 0, b, i), 0, 0)),
            pl.BlockSpec((d, CONV_CH + NONCONV_W), lambda i, j: (0, 0)),
        ],
        out_specs=[
            pl.BlockSpec((1, ROW_TILE, CONV_CH), lambda i, j: (i, j, 0)),
            pl.BlockSpec((1, ROW_TILE, NONCONV_W), lambda i, j: (i, j, 0)),
        ],
        out_shape=[
            jax.ShapeDtypeStruct((b, t_all, CONV_CH), jnp.float32),
            jax.ShapeDtypeStruct((b, t_all, NONCONV_W), jnp.float32),
        ],
        compiler_params=pltpu.CompilerParams(
            dimension_semantics=("arbitrary", "arbitrary"), vmem_limit_bytes=VMEM_LIMIT),
        name="inproj",
    )(ctx, x, mod_all, w_pad)


def _pad_w_in(w_in):
    d = w_in.shape[0]
    sp = SPLIT_POINTS
    z = lambda n: jnp.zeros((d, n), w_in.dtype)
    gates = w_in[:, sp[6]:sp[7]]
    lw = w_in[:, sp[7]:sp[9]]
    la = w_in[:, sp[9]:sp[10]]
    lg = w_in[:, sp[10]:]
    return jnp.concatenate([w_in[:, :sp[6]], gates, z(96), lw, la, z(64), lg], axis=1).astype(jnp.bfloat16)


PAIR = 2 * D_HR
NPAIR = H_R // 2
_NN = (((1,), (0,)), ((), ()))
_NT = (((1,), (1,)), ((), ()))
_TN = (((0,), (0,)), ((), ()))


def _split2(a):
    hi = a.astype(jnp.bfloat16)
    lo = (a - hi.astype(jnp.float32)).astype(jnp.bfloat16)
    return hi, lo


def _split3(a):
    a1 = a.astype(jnp.bfloat16)
    r1 = a - a1.astype(jnp.float32)
    a2 = r1.astype(jnp.bfloat16)
    a3 = (r1 - a2.astype(jnp.float32)).astype(jnp.bfloat16)
    return a1, a2, a3


def _dg(a, b, dims):
    return lax.dot_general(a, b, dims, preferred_element_type=jnp.float32)


def _dot3(a, b, dims=_NN):
    ah, al = _split2(a)
    bh, bl = _split2(b)
    return _dg(ah, bh, dims) + _dg(ah, bl, dims) + _dg(al, bh, dims)


def _dot1(a, b, dims=_NN):
    return _dg(a.astype(jnp.bfloat16), b.astype(jnp.bfloat16), dims)


def _stack3(w):
    hi = w.astype(jnp.bfloat16)
    lo = (w - hi.astype(jnp.float32)).astype(jnp.bfloat16)
    return jnp.concatenate([hi, lo, hi], axis=0)


def _dot3_pre(a, w3):
    ah, al = _split2(a)
    return _dg(jnp.concatenate([ah, ah, al], axis=1), w3, _NN)


HEAD_BLK = 256


def _head_sum_operator(head):
    r = np.arange(HEAD_BLK) // head
    bd = (r[:, None] == r[None, :]).astype(np.float32)
    return jnp.asarray(np.concatenate([bd, bd], axis=0), jnp.bfloat16)


def _head_sums(z, bd2):
    hi, lo = _split2(z)
    out = []
    for c in range(0, z.shape[1], HEAD_BLK):
        lhs = jnp.concatenate([hi[:, c:c + HEAD_BLK], lo[:, c:c + HEAD_BLK]], axis=1)
        out.append(_dg(lhs, bd2, _NN))
    return jnp.concatenate(out, axis=1)


def _dot_exact_lhs(a_bf16, b, dims=_NN):
    b1, b2, b3 = _split3(b)
    return _dg(a_bf16, b1, dims) + _dg(a_bf16, b2, dims) + _dg(a_bf16, b3, dims)


def _ada_kernel(c_ref, w_ref, b_ref, o_ref):
    o_ref[...] = _dot3(jax.nn.silu(c_ref[...]), w_ref[...]) + b_ref[...]


def _ada(c_pad, w, bias):
    m, d = c_pad.shape
    n = w.shape[1]
    return pl.pallas_call(
        _ada_kernel,
        grid=(n // d,),
        in_specs=[pl.BlockSpec((m, d), lambda j: (0, 0)),
                  pl.BlockSpec((d, d), lambda j: (0, j)),
                  pl.BlockSpec((1, d), lambda j: (0, j))],
        out_specs=pl.BlockSpec((m, d), lambda j: (0, j)),
        out_shape=jax.ShapeDtypeStruct((m, n), jnp.float32),
        compiler_params=pltpu.CompilerParams(dimension_semantics=("arbitrary",), vmem_limit_bytes=VMEM_LIMIT),
        name="adaln",
    )(c_pad, w, bias)


GRID_W = 64
HALO = GRID_W


def _prep_kernel(top_ref, main_ref, bot_ref, pn_ref, cw_ref, rpar_ref, wbf_ref, wbb_ref, ab_ref, gb_ref, bd_ref,
                 q_ref, k_ref, r_ref, km_ref, v_ref, kh_ref, abo_ref, lwf_ref, lwb_ref, gg_ref, bon_ref, xbuf):
    j = pl.program_id(1)
    nt = pl.num_programs(1)
    is_ctx = j == 0
    top_ok = j >= 2
    bot_ok = (j >= 1) & (j < nt - 1)
    l_idx = lax.broadcasted_iota(jnp.int32, (ROW_TILE, 1), 0)
    col = jnp.where(is_ctx, l_idx, l_idx % GRID_W)
    left_ok = col != 0
    right_ok = col != jnp.where(is_ctx, ROW_TILE - 1, GRID_W - 1)
    vert = jnp.where(is_ctx, 0.0, 1.0)
    sec = {}
    for s in range(5):
        sl = slice(s * W_M, (s + 1) * W_M)
        xbuf[0:HALO, :] = jnp.where(top_ok, top_ref[0, :, sl], 0.0)
        xbuf[HALO:HALO + ROW_TILE, :] = main_ref[0, :, sl]
        xbuf[HALO + ROW_TILE:2 * HALO + ROW_TILE, :] = jnp.where(bot_ok, bot_ref[0, :, sl], 0.0)
        cols = [None, None, None]
        for dr in range(3):
            base = HALO + GRID_W * (dr - 1)
            w3 = cw_ref[3 * dr:3 * dr + 3, sl] if dr == 1 else cw_ref[3 * dr:3 * dr + 3, sl] * vert
            xrow = xbuf[base:base + ROW_TILE, :]
            for dc in range(3):
                term = xrow * w3[dc:dc + 1]
                cols[dc] = term if cols[dc] is None else cols[dc] + term
        left = jnp.where(left_ok, pltpu.roll(cols[0], 1, 0), 0.0)
        right = jnp.where(right_ok, pltpu.roll(cols[2], ROW_TILE - 1, 0), 0.0)
        sec[s] = left + cols[1] + right
    q_ref[0] = jax.nn.silu(sec[0]).astype(q_ref.dtype)
    k_ref[0] = (jax.nn.silu(sec[1]) * (D_HM ** -0.5)).astype(k_ref.dtype)
    rr, rk, rv = sec[2], sec[3], sec[4]
    r_ref[0] = rr
    v_ref[0] = rv.astype(v_ref.dtype)
    lw = jnp.tanh(pn_ref[0, :, 0:128])
    lwf_ref[0] = -DECAY_SCALE * jax.nn.sigmoid(rpar_ref[0:1, :] + _dot3_pre(lw, wbf_ref[...]))
    lwb_ref[0] = -DECAY_SCALE * jax.nn.sigmoid(rpar_ref[1:2, :] + _dot3_pre(lw, wbb_ref[...]))
    a = jax.nn.sigmoid(rpar_ref[2:3, :] + _dot3_pre(pn_ref[0, :, 128:256], ab_ref[...]))
    gg_ref[0] = _dot3_pre(jax.nn.sigmoid(pn_ref[0, :, 256:384]), gb_ref[...])
    kap = rk * rpar_ref[3:4, :]
    norm = jnp.sqrt(_head_sums(kap * kap, bd_ref[...]))
    kh = kap / jnp.maximum(norm, 1e-12)
    kmod = rk * (1.0 + (a - 1.0) * rpar_ref[4:5, :])
    kh_ref[0] = kh
    abo_ref[0] = kh * a
    km_ref[0] = kmod
    bon_ref[0] = _head_sums(rr * kmod * rpar_ref[5:6, :], bd_ref[...]) * rv


def _prep(pc, pn, cw, rpar, wbf, wbb, ab, gb, bd):
    b, t, _ = pc.shape
    nt = t // ROW_TILE
    per = ROW_TILE // HALO
    nh = t // HALO
    w = W_M
    full = lambda shp: pl.BlockSpec(shp, lambda i, j: tuple(0 for _ in shp))
    in_specs = [
        pl.BlockSpec((1, HALO, CONV_CH), lambda i, j: (i, jnp.maximum(j * per - 1, 0), 0)),
        pl.BlockSpec((1, ROW_TILE, CONV_CH), lambda i, j: (i, j, 0)),
        pl.BlockSpec((1, HALO, CONV_CH), lambda i, j: (i, jnp.minimum(j * per + per, nh - 1), 0)),
        pl.BlockSpec((1, ROW_TILE, LORA_TAIL_W), lambda i, j: (i, j, NONCONV_W // LORA_TAIL_W - 1)),
        full((16, CONV_CH)), full((8, w)), full((384, w)), full((384, w)), full((384, w)), full((384, w)),
        full((2 * HEAD_BLK, HEAD_BLK)),
    ]
    out = pl.BlockSpec((1, ROW_TILE, w), lambda i, j: (i, j, 0))
    return pl.pallas_call(
        _prep_kernel,
        grid=(b, nt),
        in_specs=in_specs,
        out_specs=[out] * 11,
        out_shape=[jax.ShapeDtypeStruct((b, t, w), jnp.bfloat16 if name in ("q", "k", "v") else jnp.float32)
                   for name in ("q", "k", "r", "kmod", "v", "kh", "ab", "lwf", "lwb", "gg", "bonus")],
        scratch_shapes=[pltpu.VMEM((2 * HALO + ROW_TILE, w), jnp.float32)],
        compiler_params=pltpu.CompilerParams(
            dimension_semantics=("arbitrary", "arbitrary"), vmem_limit_bytes=VMEM_LIMIT),
        name="conv_prep",
    )(pc, pc, pc, pn, cw, rpar, wbf, wbb, ab, gb, bd)


def _round_robin(gens):
    out = [None] * len(gens)
    live = list(range(len(gens)))
    while live:
        nxt = []
        for i in live:
            try:
                next(gens[i])
                nxt.append(i)
            except StopIteration as stop:
                out[i] = stop.value
        live = nxt
    return out


def _rwkv_pair_chunk(r, k, v, kh, ab, lw, s_mat, tri_incl, strict, incl, last_row, m0, m1):
    cw = _dot_exact_lhs(tri_incl, lw)
    yield
    e_pos = jnp.exp(cw)
    e_neg = jnp.exp(-cw)
    e_prev = jnp.exp(cw - lw)
    stack = lambda a: jnp.concatenate([a * m0, a * m1], axis=0)
    kt = stack(kh * e_prev)
    bt = stack(ab * e_neg)
    kk = stack(k * e_neg)
    rt = stack(r * e_pos)
    vs = stack(v)
    zero = jnp.zeros((), jnp.float32)
    kr = jnp.concatenate([kt, rt], axis=0)
    bk = jnp.concatenate([bt, kk], axis=0)
    amat = _dot1(kr, bk, _NT)
    yield
    a_bk = jnp.where(strict, amat[:PAIR, :PAIR], zero)
    a_kk = jnp.where(strict, amat[:PAIR, PAIR:], zero)
    a_rb = jnp.where(incl, amat[PAIR:, :PAIR], zero)
    a_rk = jnp.where(incl, amat[PAIR:, PAIR:], zero)
    n_mat = -a_bk
    eye = (lax.broadcasted_iota(jnp.int32, (PAIR, PAIR), 0)
           == lax.broadcasted_iota(jnp.int32, (PAIR, PAIR), 1)).astype(jnp.float32)
    q = eye + n_mat
    p = _dot1(n_mat, n_mat)
    ks = _dot1(kr, s_mat, _NT)
    av = _dot1(jnp.concatenate([a_kk, a_rk], axis=0), vs)
    yield
    for _ in range(4):
        qp = _dot1(jnp.concatenate([q, p], axis=0), p)
        yield
        q = q + qp[:PAIR]
        p = qp[PAIR:]
    t_inv = q + _dot1(q, p)
    yield
    u = -_dot1(t_inv, ks[:PAIR] + av[:PAIR])
    yield
    y = ks[PAIR:] + av[PAIR:] + _dot1(a_rb, u)
    yield
    w_last = jnp.sum(jnp.where(last_row, e_pos, zero), axis=0, keepdims=True)
    s_new = (s_mat + _dot1(jnp.concatenate([u, vs], axis=0), bk, _TN)) * w_last
    return y[:CHUNK] + y[CHUNK:], s_new


def _rwkv_kernel(rf, kf, vf, khf, abf, lwf, rb, kb, vb, khb, abb, lwb, yf_ref, yb_ref, s_ref, *, nctx):
    j = pl.program_id(1)

    @pl.when(j == 0)
    def _():
        s_ref[...] = jnp.zeros_like(s_ref)

    row = lax.broadcasted_iota(jnp.int32, (PAIR, PAIR), 0)
    col = lax.broadcasted_iota(jnp.int32, (PAIR, PAIR), 1)
    same = (row // CHUNK) == (col // CHUNK)
    r64 = lax.broadcasted_iota(jnp.int32, (CHUNK, CHUNK), 0)
    c64 = lax.broadcasted_iota(jnp.int32, (CHUNK, CHUNK), 1)
    rowl = lax.broadcasted_iota(jnp.int32, (CHUNK, PAIR), 0)
    lane = lax.broadcasted_iota(jnp.int32, (1, PAIR), 1)
    m0 = (lane < D_HR).astype(jnp.float32)
    m1 = 1.0 - m0
    dirs = (
        (rf, kf, vf, khf, abf, lwf, yf_ref, (c64 <= r64), same & (col < row), same & (col <= row), rowl == CHUNK - 1),
        (rb, kb, vb, khb, abb, lwb, yb_ref, (c64 >= r64), same & (col > row), same & (col >= row), rowl == 0),
    )
    nb = rf.shape[0]
    states = [[[s_ref[bi, d, p] for p in range(NPAIR)] for d in range(2)] for bi in range(nb)]
    gens, meta = [], []
    for d, (r_ref, k_ref, v_ref, kh_ref, ab_ref, lw_ref, y_ref, tri, strict, incl, last_row) in enumerate(dirs):
        tri = tri.astype(jnp.float32).astype(jnp.bfloat16)
        for bi in range(nb):
            for p in range(NPAIR):
                sl = slice(p * PAIR, (p + 1) * PAIR)
                gens.append(_rwkv_pair_chunk(r_ref[bi, :, sl], k_ref[bi, :, sl], v_ref[bi, :, sl],
                                             kh_ref[bi, :, sl], ab_ref[bi, :, sl], lw_ref[bi, :, sl],
                                             states[bi][d][p], tri, strict, incl, last_row, m0, m1))
                meta.append((bi, d, p, sl, y_ref))
    results = [m + r for m, r in zip(meta, _round_robin(gens))]
    for bi, d, p, sl, y_ref, y, s_new in results:
        s_ref[bi, d, p] = s_new
        y_ref[bi, :, sl] = y


def _rwkv_scan(r, k, v, kh, ab, lw_f, lw_b, ctx_len):
    b, t, w = r.shape
    nc = t // CHUNK
    nctx = ctx_len // CHUNK
    nlat = nc - nctx

    def fwd_map(i, j):
        return (i, j, 0)

    def bwd_map(i, j):
        return (i, jnp.where(j < nctx, nctx - 1 - j, nc - 1 - (j - nctx)), 0)

    nb = math.gcd(b, SCAN_BATCH)
    blk = (nb, CHUNK, w)
    in_specs = [pl.BlockSpec(blk, fwd_map)] * 6 + [pl.BlockSpec(blk, bwd_map)] * 6
    out_specs = [
        pl.BlockSpec(blk, lambda i, j: (i, jnp.maximum(j - nctx, 0), 0)),
        pl.BlockSpec(blk, lambda i, j: (i, nlat - 1 - jnp.maximum(j - nctx, 0), 0)),
    ]
    return pl.pallas_call(
        functools.partial(_rwkv_kernel, nctx=nctx),
        grid=(b // nb, nc),
        in_specs=in_specs,
        out_specs=out_specs,
        out_shape=[jax.ShapeDtypeStruct((b, t - ctx_len, w), jnp.float32)] * 2,
        scratch_shapes=[pltpu.VMEM((nb, 2, NPAIR, PAIR, PAIR), jnp.float32)],
        compiler_params=pltpu.CompilerParams(
            dimension_semantics=("arbitrary", "arbitrary"), vmem_limit_bytes=VMEM_LIMIT),
        name="rwkv_scan",
    )(r, k, v, kh, ab, lw_f, r, k, v, kh, ab, lw_b)


def _scan_max(x, reverse):
    rows = lax.broadcasted_iota(jnp.int32, x.shape, 0)
    neg_inf = jnp.full((), -jnp.inf, jnp.float32)
    step = 1
    while step < CHUNK:
        if reverse:
            shifted = jnp.where(rows < CHUNK - step, pltpu.roll(x, CHUNK - step, 0), neg_inf)
        else:
            shifted = jnp.where(rows >= step, pltpu.roll(x, step, 0), neg_inf)
        x = jnp.maximum(x, shifted)
        step *= 2
    return x


def _mlstm_gates(gates, bias, tri, m_prev, is_f, reverse):
    gl = gates + bias
    act = jnp.where(is_f, jax.nn.log_sigmoid(gl), gl)
    b_all = _dot_exact_lhs(tri, act)
    li = pltpu.roll(act, H_M, 1)
    x = li - b_all
    m_inter = b_all + m_prev
    m_t = jnp.maximum(m_inter, b_all + _scan_max(x, reverse))
    b_last = b_all[0:1, :] if reverse else b_all[CHUNK - 1:CHUNK, :]
    g = b_last + x
    m_new = jnp.maximum(b_last + m_prev, jnp.max(g, axis=0, keepdims=True))
    wts = jnp.exp(g - m_new)
    decay = jnp.exp(b_last + m_prev - m_new)
    return b_all - m_t, jnp.exp(m_inter - m_t), jnp.exp(-m_t), x, wts, decay, m_new


def _mlstm_pair_chunk(q, k, v, pieces, cf, decay_g, c_mat, n_row, incl, m0, m1):
    stack = lambda a: jnp.concatenate([a * m0, a * m1], axis=0)
    q_st, k_st, v_st = stack(q), stack(k), stack(v)
    lane = lax.broadcasted_iota(jnp.int32, (1, PAIR), 1)
    zero_bf = jnp.zeros((), jnp.bfloat16)
    sel = lambda a: jnp.concatenate([jnp.where(lane == cf, a, zero_bf), jnp.where(lane == cf + 1, a, zero_bf)], 0)
    ones = jnp.ones((PAIR, PAIR), jnp.bfloat16)
    spread = [sel(a) for a in pieces[0]] + [sel(p[0]) for p in pieces[1:4]]
    bc = _dg(jnp.concatenate(spread, axis=0), ones, _NN)
    bm_bc = bc[:PAIR] + bc[PAIR:2 * PAIR] + bc[2 * PAIR:3 * PAIR]
    inter_bc, enm_bc, wts_bc = (bc[i * PAIR:(i + 1) * PAIR] for i in range(3, 6))
    r3 = _dg(ones, jnp.concatenate([sel(a) for a in pieces[4]], axis=0), _NT)
    r_mat = r3[:, :PAIR] + r3[:, PAIR:2 * PAIR] + r3[:, 2 * PAIR:]
    qk = _dot1(q_st, k_st, _NT)
    qc = _dot1(q_st, c_mat)
    yield
    neg_inf = jnp.full((), -jnp.inf, jnp.float32)
    s = qk * jnp.exp(jnp.where(incl, bm_bc + r_mat, neg_inf))
    wk = wts_bc * k_st
    sv = _dot1(s, v_st)
    kv = _dot1(wk, v_st, _TN)
    sums = _dot1(jnp.concatenate([q_st * n_row, s], axis=0), jnp.ones((PAIR, PAIR), jnp.float32))
    yield
    den = inter_bc * sums[:PAIR] + sums[PAIR:]
    h_st = (inter_bc * qc + sv) / jnp.maximum(jnp.abs(den), enm_bc)
    h = h_st[:CHUNK] + h_st[CHUNK:]
    decay_row = jnp.where(lane < D_HM, decay_g[:, cf:cf + 1], decay_g[:, cf + 1:cf + 2])
    c_new = decay_row * c_mat + kv
    n_new = decay_row * n_row + jnp.sum(wk, axis=0, keepdims=True)
    return h, c_new, n_new


def _mlstm_kernel(qf, kf, vf, gf, qb, kb, vb, gb, bias_ref, hf_ref, hb_ref, c_ref, n_ref, m_ref, *, nctx):
    del nctx
    j = pl.program_id(1)

    @pl.when(j == 0)
    def _():
        c_ref[...] = jnp.zeros_like(c_ref)
        n_ref[...] = jnp.zeros_like(n_ref)
        m_ref[...] = jnp.zeros_like(m_ref)

    row = lax.broadcasted_iota(jnp.int32, (PAIR, PAIR), 0)
    col = lax.broadcasted_iota(jnp.int32, (PAIR, PAIR), 1)
    same = (row // CHUNK) == (col // CHUNK)
    r64 = lax.broadcasted_iota(jnp.int32, (CHUNK, CHUNK), 0)
    c64 = lax.broadcasted_iota(jnp.int32, (CHUNK, CHUNK), 1)
    lane = lax.broadcasted_iota(jnp.int32, (1, PAIR), 1)
    m0 = (lane < D_HM).astype(jnp.float32)
    m1 = 1.0 - m0
    is_f = (lane % (2 * H_M)) >= H_M
    dirs = (
        (qf, kf, vf, gf, hf_ref, (c64 <= r64), same & (col <= row)),
        (qb, kb, vb, gb, hb_ref, (c64 >= r64), same & (col >= row)),
    )
    nb = qf.shape[0]
    states = [[[(c_ref[bi, d, p], n_ref[bi, d, p]) for p in range(NPAIR)] for d in range(2)] for bi in range(nb)]
    gens, meta, m_news = [], [], []
    lead = lambda a: (a.astype(jnp.bfloat16),)
    for d, (q_ref, k_ref, v_ref, g_ref, h_ref, tri, incl) in enumerate(dirs):
        tri = tri.astype(jnp.float32).astype(jnp.bfloat16)
        for bi in range(nb):
            bm, inter, enm, x, wts, decay_g, m_new = _mlstm_gates(g_ref[bi], bias_ref[...], tri, m_ref[bi, d], is_f,
                                                                  d == 1)
            m_news.append((bi, d, m_new))
            pieces = [_split3(bm), lead(inter), lead(enm), lead(wts), _split3(x)]
            for p in range(NPAIR):
                sl = slice(p * PAIR, (p + 1) * PAIR)
                cf = d * 2 * H_M + H_M + 2 * p
                gens.append(_mlstm_pair_chunk(q_ref[bi, :, sl], k_ref[bi, :, sl], v_ref[bi, :, sl], pieces, cf,
                                               decay_g, *states[bi][d][p], incl, m0, m1))
                meta.append((bi, d, p, sl, h_ref))
    for (bi, d, p, sl, h_ref), (h, c_new, n_new) in zip(meta, _round_robin(gens)):
        c_ref[bi, d, p] = c_new
        n_ref[bi, d, p] = n_new
        h_ref[bi, :, sl] = h
    for bi, d, m_new in m_news:
        m_ref[bi, d] = m_new


def _mlstm_scan(q, k, pn, bias_row, ctx_len):
    v = gates = pn
    b, t, w = q.shape
    nc = t // CHUNK
    nctx = ctx_len // CHUNK
    nlat = nc - nctx

    def fwd_map(i, j):
        return (i, j, 0)

    def bwd_map(i, j):
        return (i, jnp.where(j < nctx, nctx - 1 - j, nc - 1 - (j - nctx)), 0)

    nb = math.gcd(b, SCAN_BATCH)
    blk = (nb, CHUNK, w)
    gblk = (nb, CHUNK, 128)
    gate_blk = (2 * W_M) // 128
    gate_of = lambda m: (lambda i, j: m(i, j)[:2] + (gate_blk,))
    in_specs = ([pl.BlockSpec(blk, fwd_map)] * 3 + [pl.BlockSpec(gblk, gate_of(fwd_map))]
                + [pl.BlockSpec(blk, bwd_map)] * 3 + [pl.BlockSpec(gblk, gate_of(bwd_map))]
                + [pl.BlockSpec((1, 128), lambda i, j: (0, 0))])
    out_specs = [
        pl.BlockSpec(blk, lambda i, j: (i, jnp.maximum(j - nctx, 0), 0)),
        pl.BlockSpec(blk, lambda i, j: (i, nlat - 1 - jnp.maximum(j - nctx, 0), 0)),
    ]
    return pl.pallas_call(
        functools.partial(_mlstm_kernel, nctx=nctx),
        grid=(b // nb, nc),
        in_specs=in_specs,
        out_specs=out_specs,
        out_shape=[jax.ShapeDtypeStruct((b, t - ctx_len, w), jnp.float32)] * 2,
        scratch_shapes=[pltpu.VMEM((nb, 2, NPAIR, PAIR, PAIR), jnp.float32),
                        pltpu.VMEM((nb, 2, NPAIR, 1, PAIR), jnp.float32),
                        pltpu.VMEM((nb, 2, 1, 128), jnp.float32)],
        compiler_params=pltpu.CompilerParams(
            dimension_semantics=("arbitrary", "arbitrary"), vmem_limit_bytes=VMEM_LIMIT),
        name="mlstm_scan",
    )(q, k, v, gates, q, k, v, gates, bias_row)


ROUTE_W = 128
N_ROUTE = N_GROUPS + N_EXPERTS


def _ln_rows(z, eps):
    mu = jnp.mean(z, -1, keepdims=True)
    zc = z - mu
    var = jnp.mean(zc * zc, -1, keepdims=True)
    return zc * lax.rsqrt(var + eps)


OUT_TILES = 2


def _outproj_tile(rows, hf, hb, mo, yf, yb, gg, bonus, x_ref, mod_ref, wout_ref, hpar_ref, ln_ref, rtw_ref, rtb_ref,
                  bd_ref, x1_ref, tok_ref):
    avg = bd_ref[...]
    zm = hf[0, rows, :] + hb[0, rows, :]
    zr = yf[0, rows, :] + yb[0, rows, :]
    mu_m = _head_sums(zm, avg) * (1.0 / D_HM)
    mu_r = _head_sums(zr, avg) * (1.0 / D_HR)
    yield
    cm, cr = zm - mu_m, zr - mu_r
    var_m = _head_sums(cm * cm, avg) * (1.0 / D_HM)
    var_r = _head_sums(cr * cr, avg) * (1.0 / D_HR)
    yield
    out_m = jax.nn.sigmoid(mo[0]) * (cm * lax.rsqrt(var_m + LN_EPS) * hpar_ref[0:1, :])
    y = cr * lax.rsqrt(var_r + GN_EPS) * hpar_ref[1:2, :] + hpar_ref[2:3, :]
    out_r = (y + bonus[0]) * gg[0]
    proj = (jnp.dot(out_m.astype(jnp.bfloat16), wout_ref[:W_M, :], preferred_element_type=jnp.float32)
            + jnp.dot(out_r.astype(jnp.bfloat16), wout_ref[W_M:, :], preferred_element_type=jnp.float32))
    yield
    g1, sh2, sc2 = mod_ref[0, 2:3, :], mod_ref[0, 3:4, :], mod_ref[0, 4:5, :]
    x1 = _ln_rows(ALPHA * x_ref[0, rows, :] + g1 * proj, LN_EPS) * ln_ref[0:1, :] + ln_ref[1:2, :]
    x1_ref[0, rows, :] = x1
    tok = _ln_rows(x1, LN_EPS) * (1.0 + sc2) + sh2
    tok_ref[0, rows, :] = tok
    logits = _dot3_pre(tok, rtw_ref[...]) + rtb_ref[...]
    yield
    lane = lax.broadcasted_iota(jnp.int32, (ROW_TILE, ROUTE_W), 1)
    neg_inf = jnp.full((), -jnp.inf, jnp.float32)
    big = jnp.int32(ROUTE_W)
    is_g = lane < N_GROUPS
    lg = jnp.where(is_g, logits, neg_inf)
    gmax = jnp.max(lg, axis=1, keepdims=True)
    grp = jnp.min(jnp.where(lg == gmax, lane, big), axis=1, keepdims=True)
    p_top = 1.0 / jnp.sum(jnp.where(is_g, jnp.exp(lg - gmax), 0.0), axis=1, keepdims=True)
    in_grp = (lane >= N_GROUPS) & (lane < N_ROUTE) & ((lane - N_GROUPS) // E_PER_GROUP == grp)
    le = jnp.where(in_grp, logits, neg_inf)
    v1 = jnp.max(le, axis=1, keepdims=True)
    i1 = jnp.min(jnp.where(le == v1, lane, big), axis=1, keepdims=True)
    le2 = jnp.where(lane == i1, neg_inf, le)
    v2 = jnp.max(le2, axis=1, keepdims=True)
    i2 = jnp.min(jnp.where(le2 == v2, lane, big), axis=1, keepdims=True)
    e21 = jnp.exp(v2 - v1)
    w1 = (1.0 / (1.0 + e21)) * p_top
    w2 = (e21 / (1.0 + e21)) * p_top

    sel1, sel2 = lane == i1, lane == i2
    onehot = jnp.where(sel1 | sel2, 1.0, 0.0)
    tr = lax.broadcasted_iota(jnp.int32, (ROW_TILE, ROW_TILE), 0)
    tc = lax.broadcasted_iota(jnp.int32, (ROW_TILE, ROW_TILE), 1)
    strict = jnp.where(tc < tr, 1.0, 0.0).astype(jnp.bfloat16)
    before = _dg(strict, onehot.astype(jnp.bfloat16), _NN)
    yield
    e1 = (i1 - N_GROUPS).astype(jnp.float32)
    e2 = (i2 - N_GROUPS).astype(jnp.float32)
    return lane, sel1, sel2, onehot, before, e1, e2, w1, w2


def _outproj_kernel(hf, hb, mo_a, mo_b, yf, yb, gg_a, gg_b, bon_a, bon_b, x_ref, mod_ref, wout_ref, hpar_ref, ln_ref,
                    rtw_ref, rtb_ref, bd_ref, x1_ref, tok_ref, route_ref, ridx_ref, cnt_ref, carry_ref):
    i, j = pl.program_id(0), pl.program_id(1)

    @pl.when((i == 0) & (j == 0))
    def _():
        carry_ref[...] = jnp.zeros_like(carry_ref)

    tiles = ((mo_a, gg_a, bon_a), (mo_b, gg_b, bon_b))
    gens = [_outproj_tile(slice(t * ROW_TILE, (t + 1) * ROW_TILE), hf, hb, mo, yf, yb, gg, bonus, x_ref, mod_ref,
                          wout_ref, hpar_ref, ln_ref, rtw_ref, rtb_ref, bd_ref, x1_ref, tok_ref)
            for t, (mo, gg, bonus) in enumerate(tiles)]
    carry = carry_ref[...]
    for t, (lane, sel1, sel2, onehot, before, e1, e2, w1, w2) in enumerate(_round_robin(gens)):
        before = before + carry
        r1 = jnp.sum(jnp.where(sel1, before, 0.0), axis=1, keepdims=True)
        r2 = jnp.sum(jnp.where(sel2, before, 0.0), axis=1, keepdims=True)
        carry = carry + jnp.sum(onehot, axis=0, keepdims=True)
        route = jnp.where(lane == 0, e1, jnp.where(lane == 1, e2, jnp.where(lane == 2, r1, jnp.where(
            lane == 3, r2, jnp.where(lane == 4, w1, jnp.where(lane == 5, w2, 0.0))))))
        route_ref[0, t * ROW_TILE:(t + 1) * ROW_TILE, :] = route
        ridx_ref[t] = route.T[0:8, :].astype(jnp.int32)
    carry_ref[...] = carry
    cnt_ref[...] = jnp.broadcast_to(carry, cnt_ref.shape)


def _outproj(h_f, h_b, pn, y_f, y_b, gg, bonus, x, mod3, wout_bf16, hpar, lnpar, rtw, rtb, bd, ctx_len):
    b, seq, d = x.shape
    nt = seq // ROW_TILE
    assert nt % OUT_TILES == 0
    off = ctx_len // ROW_TILE
    step_rows = OUT_TILES * ROW_TILE
    lat = lambda w: pl.BlockSpec((1, step_rows, w), lambda i, j: (i, j, 0))
    full = lambda shp: pl.BlockSpec(shp, lambda i, j: tuple(0 for _ in shp))
    shifted = lambda w, t, lane_blk: pl.BlockSpec((1, ROW_TILE, w),
                                                  lambda i, j: (i, OUT_TILES * j + t + off, lane_blk))
    in_specs = [
        lat(W_M), lat(W_M),
        shifted(W_M, 0, 1), shifted(W_M, 1, 1),
        lat(W_R), lat(W_R),
        shifted(W_R, 0, 0), shifted(W_R, 1, 0),
        shifted(W_R, 0, 0), shifted(W_R, 1, 0),
        lat(d),
        pl.BlockSpec((1, 6, d), lambda i, j: (i, 0, 0)),
        full((MIX_W, d)), full((8, W_M)), full((8, d)), full((3 * d, ROUTE_W)), full((1, ROUTE_W)),
        full((2 * HEAD_BLK, HEAD_BLK)),
    ]
    out_specs = [
        lat(d), lat(d), lat(ROUTE_W),
        pl.BlockSpec((OUT_TILES, 8, ROW_TILE), lambda i, j: (i * (nt // OUT_TILES) + j, 0, 0)),
        full((8, ROUTE_W)),
    ]
    out_shape = [
        jax.ShapeDtypeStruct((b, seq, d), jnp.float32),
        jax.ShapeDtypeStruct((b, seq, d), jnp.float32),
        jax.ShapeDtypeStruct((b, seq, ROUTE_W), jnp.float32),
        jax.ShapeDtypeStruct((b * nt, 8, ROW_TILE), jnp.int32),
        jax.ShapeDtypeStruct((8, ROUTE_W), jnp.float32),
    ]
    return pl.pallas_call(
        _outproj_kernel,
        grid=(b, nt // OUT_TILES),
        in_specs=in_specs,
        out_specs=out_specs,
        out_shape=out_shape,
        scratch_shapes=[pltpu.VMEM((1, ROUTE_W), jnp.float32)],
        compiler_params=pltpu.CompilerParams(
            dimension_semantics=("arbitrary", "arbitrary"), vmem_limit_bytes=VMEM_LIMIT),
        name="outproj_router",
    )(h_f, h_b, pn, pn, y_f, y_b, gg, gg, bonus, bonus, x, mod3, wout_bf16, hpar, lnpar, rtw, rtb, bd)


DEST_TILES = 8


def _dest_kernel(starts_ref, ridx_ref, dest_ref):
    blk = ridx_ref[...]
    dest = pltpu.roll(blk, 8 - TOP_K_IN_GROUP, 1)
    for e in range(N_EXPERTS):
        dest = dest + jnp.where(blk == e, starts_ref[e], 0)
    dest_ref[...] = dest


def _moe_dest(starts, ridx):
    nt = ridx.shape[0]
    per_step = math.gcd(nt, DEST_TILES)
    blk = pl.BlockSpec((per_step, 8, ROW_TILE), lambda i, s: (i, 0, 0))
    return pl.pallas_call(
        _dest_kernel,
        grid_spec=pltpu.PrefetchScalarGridSpec(num_scalar_prefetch=1, grid=(nt // per_step,),
                                               in_specs=[blk], out_specs=blk),
        out_shape=jax.ShapeDtypeStruct(ridx.shape, jnp.int32),
        compiler_params=pltpu.CompilerParams(dimension_semantics=("arbitrary",), vmem_limit_bytes=VMEM_LIMIT),
        name="moe_dest",
    )(starts, ridx)


SUB = 8
SUB_SHIFT = 3
TILE_GROUPS = ROW_TILE // SUB


def _scatter_kernel(zinfo_ref, dest_ref, tok_ref, xs_ref, zbuf, sem, zsem):
    i = pl.program_id(0)

    @pl.when(i == 0)
    def _():
        zbuf[...] = jnp.zeros_like(zbuf)

    def zero_row(row):
        return pltpu.make_async_copy(zbuf.at[0, pl.ds(0, 1), :],
                                     xs_ref.at[row >> SUB_SHIFT, pl.ds(row & (SUB - 1), 1), :], zsem)

    e = jnp.minimum(i, N_EXPERTS - 1)
    lo = zinfo_ref[e]
    hi = jnp.where(i < N_EXPERTS, zinfo_ref[N_EXPERTS + e], lo)
    tail_blk = zinfo_ref[2 * N_EXPERTS] + i
    has_tail = tail_blk < zinfo_ref[2 * N_EXPERTS + 1]
    tail_copy = pltpu.make_async_copy(zbuf, xs_ref.at[pl.ds(tail_blk * TILE_GROUPS, TILE_GROUPS)], zsem)

    def start_zero(row, c):
        zero_row(row).start()
        return c

    lax.fori_loop(lo, hi, start_zero, 0)

    @pl.when(has_tail)
    def _():
        tail_copy.start()

    def issue(g, c):
        for k in range(SUB):
            for slot in range(TOP_K_IN_GROUP):
                dst = dest_ref[slot * ROW_TILE + g * SUB + k]
                pltpu.make_async_copy(tok_ref.at[g, pl.ds(k, 1), :],
                                      xs_ref.at[dst >> SUB_SHIFT, pl.ds(dst & (SUB - 1), 1), :], sem).start()
        return c

    lax.fori_loop(0, TILE_GROUPS, issue, 0)
    for half in range(TOP_K_IN_GROUP):
        pltpu.make_async_copy(tok_ref, xs_ref.at[pl.ds(half * TILE_GROUPS, TILE_GROUPS)], sem).wait()

    def wait_zero(row, c):
        zero_row(row).wait()
        return c

    lax.fori_loop(lo, hi, wait_zero, 0)

    @pl.when(has_tail)
    def _():
        tail_copy.wait()


def _moe_scatter(zinfo, dest_flat, tok2d, n_rows):
    n, d = tok2d.shape
    nt = n // ROW_TILE
    assert nt >= N_EXPERTS
    xs = pl.pallas_call(
        _scatter_kernel,
        grid_spec=pltpu.PrefetchScalarGridSpec(
            num_scalar_prefetch=1,
            grid=(nt,),
            in_specs=[
                pl.BlockSpec((SUB * ROW_TILE,), lambda i, z: (i,), memory_space=pltpu.SMEM),
                pl.BlockSpec((TILE_GROUPS, SUB, d), lambda i, z: (i, 0, 0)),
            ],
            out_specs=pl.BlockSpec(memory_space=pl.ANY),
            scratch_shapes=[pltpu.VMEM((TILE_GROUPS, SUB, d), tok2d.dtype), pltpu.SemaphoreType.DMA(()),
                            pltpu.SemaphoreType.DMA(())],
        ),
        out_shape=jax.ShapeDtypeStruct((n_rows // SUB, SUB, d), tok2d.dtype),
        compiler_params=pltpu.CompilerParams(dimension_semantics=("arbitrary",), vmem_limit_bytes=VMEM_LIMIT),
        name="moe_scatter",
    )(zinfo, dest_flat, tok2d.reshape(n // SUB, SUB, d))
    return xs.reshape(n_rows, d)


def _expert_kernel(blk_exp_ref, valid_ref, xs_ref, wg_ref, wu_ref, wd_ref, ys_ref, wg_bf, wu_bf, wd_bf):
    i = pl.program_id(0)
    valid = valid_ref[i]

    @pl.when((i == 0) | (blk_exp_ref[i] != blk_exp_ref[jnp.maximum(i - 1, 0)]))
    def _():
        wg_bf[...] = wg_ref[0].astype(jnp.bfloat16)
        wu_bf[...] = wu_ref[0].astype(jnp.bfloat16)
        wd_bf[...] = wd_ref[0].astype(jnp.bfloat16)

    @pl.when(valid > 0)
    def _():
        xb = xs_ref[...].astype(jnp.bfloat16)
        hg = jnp.dot(xb, wg_bf[...], preferred_element_type=jnp.float32)
        hu = jnp.dot(xb, wu_bf[...], preferred_element_type=jnp.float32)
        hb = (jax.nn.silu(hg) * hu).astype(jnp.bfloat16)
        ys_ref[...] = jnp.dot(hb, wd_bf[...], preferred_element_type=jnp.float32)

    @pl.when(valid == 0)
    def _():
        ys_ref[...] = jnp.zeros_like(ys_ref)


def _moe_experts(blk_exp, blk_valid, xs, wg, wu, wd):
    nrow, d = xs.shape
    de = wg.shape[2]
    return pl.pallas_call(
        _expert_kernel,
        grid_spec=pltpu.PrefetchScalarGridSpec(
            num_scalar_prefetch=2,
            grid=(nrow // MOE_BLOCK,),
            in_specs=[
                pl.BlockSpec((MOE_BLOCK, d), lambda i, be, nu: (i, 0)),
                pl.BlockSpec((1, d, de), lambda i, be, nu: (be[i], 0, 0)),
                pl.BlockSpec((1, d, de), lambda i, be, nu: (be[i], 0, 0)),
                pl.BlockSpec((1, de, d), lambda i, be, nu: (be[i], 0, 0)),
            ],
            out_specs=pl.BlockSpec((MOE_BLOCK, d), lambda i, be, nu: (i, 0)),
            scratch_shapes=[pltpu.VMEM((d, de), jnp.bfloat16), pltpu.VMEM((d, de), jnp.bfloat16),
                            pltpu.VMEM((de, d), jnp.bfloat16)],
        ),
        out_shape=jax.ShapeDtypeStruct((nrow, d), jnp.float32),
        compiler_params=pltpu.CompilerParams(dimension_semantics=("arbitrary",), vmem_limit_bytes=VMEM_LIMIT),
        name="moe_experts",
    )(blk_exp, blk_valid, xs, wg, wu, wd)


def _combine_kernel(dest_ref, dest_next_ref, route_ref, x1_ref, mod_ref, ln_ref, ys_ref, out_ref,
                    ybuf, sems):
    i = pl.program_id(0)
    n = pl.num_programs(0)
    cur = i % 2

    def gather(idx_ref, buf):
        def issue(g, c):
            for k in range(SUB):
                for slot in range(TOP_K_IN_GROUP):
                    src = idx_ref[slot * ROW_TILE + g * SUB + k]
                    pltpu.make_async_copy(ys_ref.at[src >> SUB_SHIFT, pl.ds(src & (SUB - 1), 1), :],
                                          ybuf.at[buf, slot * TILE_GROUPS + g, pl.ds(k, 1), :],
                                          sems.at[buf]).start()
            return c

        lax.fori_loop(0, TILE_GROUPS, issue, 0)

    @pl.when(i == 0)
    def _():
        gather(dest_ref, 0)

    @pl.when(i + 1 < n)
    def _():
        gather(dest_next_ref, 1 - cur)

    pltpu.make_async_copy(ys_ref.at[pl.ds(0, TOP_K_IN_GROUP * TILE_GROUPS)], ybuf.at[cur], sems.at[cur]).wait()
    route = route_ref[0]
    d = out_ref.shape[-1]
    y0 = ybuf[cur, 0:TILE_GROUPS].reshape(ROW_TILE, d)
    y1 = ybuf[cur, TILE_GROUPS:2 * TILE_GROUPS].reshape(ROW_TILE, d)
    ffn = y0 * route[:, 4:5] + y1 * route[:, 5:6]
    g2 = mod_ref[0, 5:6, :]
    out_ref[0] = _ln_rows(ALPHA * x1_ref[0] + g2 * ffn, LN_EPS) * ln_ref[2:3, :] + ln_ref[3:4, :]


def _moe_combine(dest_flat, route, x1, mod3, lnpar, ys):
    b, seq, d = x1.shape
    nt = seq // ROW_TILE
    n = b * nt
    tile = lambda w: pl.BlockSpec((1, ROW_TILE, w), lambda i: (i // nt, i % nt, 0))
    idx_blk = lambda f: pl.BlockSpec((SUB * ROW_TILE,), f, memory_space=pltpu.SMEM)
    return pl.pallas_call(
        _combine_kernel,
        grid=(n,),
        in_specs=[
            idx_blk(lambda i: (i,)),
            idx_blk(lambda i: (jnp.minimum(i + 1, n - 1),)),
            tile(ROUTE_W),
            tile(d),
            pl.BlockSpec((1, 6, d), lambda i: (i // nt, 0, 0)),
            pl.BlockSpec((8, d), lambda i: (0, 0)),
            pl.BlockSpec(memory_space=pl.ANY),
        ],
        out_specs=tile(d),
        scratch_shapes=[pltpu.VMEM((2, TOP_K_IN_GROUP * TILE_GROUPS, SUB, d), jnp.float32),
                        pltpu.SemaphoreType.DMA((2,))],
        out_shape=jax.ShapeDtypeStruct((b, seq, d), jnp.float32),
        compiler_params=pltpu.CompilerParams(dimension_semantics=("arbitrary",), vmem_limit_bytes=VMEM_LIMIT),
        name="moe_combine",
    )(dest_flat, dest_flat, route, x1, mod3, lnpar, ys.reshape(ys.shape[0] // SUB, SUB, d))


def _moe_plan(cnt, n_pairs):
    counts = cnt[0, N_GROUPS:N_ROUTE].astype(jnp.int32)
    padded = (counts + MOE_BLOCK - 1) // MOE_BLOCK * MOE_BLOCK
    pad_ends = jnp.cumsum(padded)
    pad_starts = pad_ends - padded
    n_blk = -(-n_pairs // MOE_BLOCK) + N_EXPERTS
    blk_row0 = jnp.arange(n_blk, dtype=jnp.int32) * MOE_BLOCK
    blk_exp = jnp.minimum(jnp.sum((pad_ends[None, :] <= blk_row0[:, None]).astype(jnp.int32), axis=1), N_EXPERTS - 1)
    blk_valid = jnp.clip(counts[blk_exp] - (blk_row0 - pad_starts[blk_exp]), 0, MOE_BLOCK)
    blk_valid = jnp.where(blk_row0 < pad_ends[-1], blk_valid, 0).astype(jnp.int32)
    zinfo = jnp.concatenate([pad_starts + counts, pad_ends, pad_ends[-1:] // MOE_BLOCK,
                             jnp.full((1,), n_blk, jnp.int32)]).astype(jnp.int32)
    return pad_starts.astype(jnp.int32), blk_exp, blk_valid, zinfo, n_blk


def kernel(x, c, ctx, c_ctx, w_ada, b_ada, w_in, conv_w, m_bias_i, m_bias_f, m_norm_w, r_w0, r_wB, r_a0, r_aB,
           r_gB, r_kk, r_ka, r_bonus, r_norm_w, r_norm_b, w_out, ln1_g, ln1_b, ln2_g, ln2_b, rt_g, rt_g_b, rt_e,
           rt_e_b, ex_gate, ex_up, ex_down):
    assert w_ada.shape[0] == DEPTH
    bsz, seq, d = x.shape
    ctx_len = ctx.shape[1]
    assert ctx_len == ROW_TILE and seq % ROW_TILE == 0
    assert seq % GRID_W == 0 and ROW_TILE % GRID_W == 0
    l = 0
    zrow = lambda n, w: jnp.zeros((n, w), jnp.float32)
    mrows = -(-(bsz + 1) // 8) * 8
    c_pad = jnp.concatenate([c, c_ctx[None, :], zrow(mrows - bsz - 1, d)], 0)
    mod_all = _ada(c_pad, w_ada[l], b_ada[l][None, :])[:bsz + 1]
    mod3 = mod_all.reshape(bsz + 1, 6, d)
    pc, pn = _inproj(ctx, x, mod3, _pad_w_in(w_in[l]))
    cw = jnp.concatenate([conv_w[l].reshape(CONV_K * CONV_K, CONV_CH), zrow(16 - CONV_K * CONV_K, CONV_CH)], 0)
    rpar = jnp.concatenate([r_w0[l], r_a0[l][None, :], r_kk[l][None, :], r_ka[l][None, :],
                            r_bonus[l].reshape(1, W_R), zrow(2, W_R)], 0)
    wbf = jnp.concatenate([r_wB[l][0], zrow(W_LORA, W_R)], 0)
    wbb = jnp.concatenate([zrow(W_LORA, W_R), r_wB[l][1]], 0)
    ab_mat = jnp.concatenate([r_aB[l], zrow(128 - A_LORA, W_R)], 0)
    bd = _head_sum_operator(D_HR)
    q, k, rr, kmod, rv, kh, kha, logw_f, logw_b, gg, bonus = _prep(
        pc, pn, cw, rpar, _stack3(wbf), _stack3(wbb), _stack3(ab_mat), _stack3(r_gB[l]), bd)
    bias_row = jnp.concatenate([m_bias_i[l][0], m_bias_f[l][0], m_bias_i[l][1], m_bias_f[l][1],
                                jnp.zeros((128 - 4 * H_M,), jnp.float32)])[None, :]
    h_f, h_b = _mlstm_scan(q, k, pn, bias_row, ctx_len)
    y_f, y_b = _rwkv_scan(rr, kmod, rv, kh, kha, logw_f, logw_b, ctx_len)
    hpar = jnp.zeros((8, W_M), jnp.float32).at[0].set(m_norm_w[l]).at[1].set(r_norm_w[l]).at[2].set(r_norm_b[l])
    lnpar = jnp.zeros((8, d), jnp.float32).at[0].set(ln1_g[l]).at[1].set(ln1_b[l]).at[2].set(ln2_g[l]).at[3].set(
        ln2_b[l])
    rtw = jnp.concatenate([rt_g[l], rt_e[l], jnp.zeros((d, ROUTE_W - N_ROUTE), jnp.float32)], axis=1)
    rtb = jnp.concatenate([rt_g_b[l], rt_e_b[l], jnp.zeros((ROUTE_W - N_ROUTE,), jnp.float32)])[None, :]
    x1, tok, route, ridx, cnt = _outproj(h_f, h_b, pn, y_f, y_b, gg, bonus, x, mod3, w_out[l].astype(jnp.bfloat16),
                                         hpar, lnpar, _stack3(rtw), rtb, bd, ctx_len)
    n_tok = bsz * seq
    starts, blk_exp, blk_valid, zinfo, n_blk = _moe_plan(cnt, n_tok * TOP_K_IN_GROUP)
    dest = _moe_dest(starts, ridx)
    dest_flat = dest.reshape(-1)
    xs = _moe_scatter(zinfo, dest_flat, tok.reshape(n_tok, d), n_blk * MOE_BLOCK)
    ys = _moe_experts(blk_exp, blk_valid, xs, ex_gate[l], ex_up[l], ex_down[l])
    return _moe_combine(dest_flat, route, x1, mod3, lnpar, ys)
```

```python
import functools
import math

import jax
import jax.numpy as jnp
from jax import lax
import numpy as np
from jax.experimental import pallas as pl
from jax.experimental.pallas import tpu as pltpu

H_M = 8
D_HM = 64
W_M = H_M * D_HM
H_R = 8
D_HR = 64
W_R = H_R * D_HR
MIX_W = W_M + W_R
W_LORA = 64
A_LORA = 64
G_LORA = 128
CONV_K = 3
CONV_CH = 2 * W_M + 3 * W_R
SECTION_WIDTHS = (W_M, W_M, W_R, W_R, W_R, W_M, W_M, 4 * H_M, W_LORA, W_LORA, A_LORA, G_LORA)
IN_COLS = sum(SECTION_WIDTHS)
SPLIT_POINTS = tuple(int(v) for v in np.cumsum(SECTION_WIDTHS)[:-1])
CHUNK = 64
N_GROUPS = 4
E_PER_GROUP = 8
N_EXPERTS = N_GROUPS * E_PER_GROUP
TOP_K_IN_GROUP = 2
D_EXPERT = 512
MOE_BLOCK = 256
DEPTH = 1
ALPHA = (2.0 * DEPTH) ** 0.25
DECAY_SCALE = math.exp(-0.5)
LN_EPS = 1e-6
GN_EPS = 64e-5

ROW_TILE = 256
NONCONV_W = 1536
LORA_TAIL_W = 384
VMEM_LIMIT = 56 * 1024 * 1024
SCAN_BATCH = 8


def _inproj_kernel(ctx_ref, x_ref, mod_ref, w_ref, oc_ref, on_ref):
    j = pl.program_id(1)

    def body(src_ref):
        z = src_ref[0]
        mu = jnp.mean(z, -1, keepdims=True)
        zc = z - mu
        var = jnp.mean(zc * zc, -1, keepdims=True)
        y = zc * lax.rsqrt(var + LN_EPS)
        h = (y * (1.0 + mod_ref[0, 1:2, :]) + mod_ref[0, 0:1, :]).astype(jnp.bfloat16)
        oc_ref[0] = jnp.dot(h, w_ref[:, :CONV_CH], preferred_element_type=jnp.float32)
        on_ref[0] = jnp.dot(h, w_ref[:, CONV_CH:], preferred_element_type=jnp.float32)

    @pl.when(j == 0)
    def _():
        body(ctx_ref)

    @pl.when(j > 0)
    def _():
        body(x_ref)


def _inproj(ctx, x, mod_all, w_pad):
    b, seq, d = x.shape
    nt = (ctx.shape[1] + seq) // ROW_TILE
    t_all = ctx.shape[1] + seq
    return pl.pallas_call(
        _inproj_kernel,
        grid=(b, nt),
        in_specs=[
            pl.BlockSpec((1, ROW_TILE, d), lambda i, j: (i, 0, 0)),
            pl.BlockSpec((1, ROW_TILE, d), lambda i, j: (i, jnp.maximum(j - 1, 0), 0)),
            pl.BlockSpec((1, 6, d), lambda i, j: (jnp.where(j == 0, b, i), 0, 0)),
            pl.BlockSpec((d, CONV_CH + NONCONV_W), lambda i, j: (0, 0)),
        ],
        out_specs=[
            pl.BlockSpec((1, ROW_TILE, CONV_CH), lambda i, j: (i, j, 0)),
            pl.BlockSpec((1, ROW_TILE, NONCONV_W), lambda i, j: (i, j, 0)),
        ],
        out_shape=[
            jax.ShapeDtypeStruct((b, t_all, CONV_CH), jnp.float32),
            jax.ShapeDtypeStruct((b, t_all, NONCONV_W), jnp.float32),
        ],
        compiler_params=pltpu.CompilerParams(
            dimension_semantics=("arbitrary", "arbitrary"), vmem_limit_bytes=VMEM_LIMIT),
        name="inproj",
    )(ctx, x, mod_all, w_pad)


def _pad_w_in(w_in):
    d = w_in.shape[0]
    sp = SPLIT_POINTS
    z = lambda n: jnp.zeros((d, n), w_in.dtype)
    gates = w_in[:, sp[6]:sp[7]]
    lw = w_in[:, sp[7]:sp[9]]
    la = w_in[:, sp[9]:sp[10]]
    lg = w_in[:, sp[10]:]
    return jnp.concatenate([w_in[:, :sp[6]], gates, z(96), lw, la, z(64), lg], axis=1).astype(jnp.bfloat16)


PAIR = 2 * D_HR
NPAIR = H_R // 2
_NN = (((1,), (0,)), ((), ()))
_NT = (((1,), (1,)), ((), ()))
_TN = (((0,), (0,)), ((), ()))


def _split2(a):
    hi = a.astype(jnp.bfloat16)
    lo = (a - hi.astype(jnp.float32)).astype(jnp.bfloat16)
    return hi, lo


def _split3(a):
    a1 = a.astype(jnp.bfloat16)
    r1 = a - a1.astype(jnp.float32)
    a2 = r1.astype(jnp.bfloat16)
    a3 = (r1 - a2.astype(jnp.float32)).astype(jnp.bfloat16)
    return a1, a2, a3


def _dg(a, b, dims):
    return lax.dot_general(a, b, dims, preferred_element_type=jnp.float32)


def _dot3(a, b, dims=_NN):
    ah, al = _split2(a)
    bh, bl = _split2(b)
    return _dg(ah, bh, dims) + _dg(ah, bl, dims) + _dg(al, bh, dims)


def _dot1(a, b, dims=_NN):
    return _dg(a.astype(jnp.bfloat16), b.astype(jnp.bfloat16), dims)


def _stack3(w):
    hi = w.astype(jnp.bfloat16)
    lo = (w - hi.astype(jnp.float32)).astype(jnp.bfloat16)
    return jnp.concatenate([hi, lo, hi], axis=0)


def _dot3_pre(a, w3):
    ah, al = _split2(a)
    return _dg(jnp.concatenate([ah, ah, al], axis=1), w3, _NN)


HEAD_BLK = 256


def _head_sum_operator(head):
    r = np.arange(HEAD_BLK) // head
    bd = (r[:, None] == r[None, :]).astype(np.float32)
    return jnp.asarray(np.concatenate([bd, bd], axis=0), jnp.bfloat16)


def _head_sums(z, bd2):
    hi, lo = _split2(z)
    out = []
    for c in range(0, z.shape[1], HEAD_BLK):
        lhs = jnp.concatenate([hi[:, c:c + HEAD_BLK], lo[:, c:c + HEAD_BLK]], axis=1)
        out.append(_dg(lhs, bd2, _NN))
    return jnp.concatenate(out, axis=1)


def _dot_exact_lhs(a_bf16, b, dims=_NN):
    b1, b2, b3 = _split3(b)
    return _dg(a_bf16, b1, dims) + _dg(a_bf16, b2, dims) + _dg(a_bf16, b3, dims)


def _ada_kernel(c_ref, w_ref, b_ref, o_ref):
    o_ref[...] = _dot3(jax.nn.silu(c_ref[...]), w_ref[...]) + b_ref[...]


def _ada(c_pad, w, bias):
    m, d = c_pad.shape
    n = w.shape[1]
    return pl.pallas_call(
        _ada_kernel,
        grid=(n // d,),
        in_specs=[pl.BlockSpec((m, d), lambda j: (0, 0)),
                  pl.BlockSpec((d, d), lambda j: (0, j)),
                  pl.BlockSpec((1, d), lambda j: (0, j))],
        out_specs=pl.BlockSpec((m, d), lambda j: (0, j)),
        out_shape=jax.ShapeDtypeStruct((m, n), jnp.float32),
        compiler_params=pltpu.CompilerParams(dimension_semantics=("arbitrary",), vmem_limit_bytes=VMEM_LIMIT),
        name="adaln",
    )(c_pad, w, bias)


GRID_W = 64
HALO = GRID_W


def _prep_kernel(top_ref, main_ref, bot_ref, pn_ref, cw_ref, rpar_ref, wbf_ref, wbb_ref, ab_ref, gb_ref, bd_ref,
                 q_ref, k_ref, r_ref, km_ref, v_ref, kh_ref, abo_ref, lwf_ref, lwb_ref, gg_ref, bon_ref, xbuf):
    j = pl.program_id(1)
    nt = pl.num_programs(1)
    is_ctx = j == 0
    top_ok = j >= 2
    bot_ok = (j >= 1) & (j < nt - 1)
    l_idx = lax.broadcasted_iota(jnp.int32, (ROW_TILE, 1), 0)
    col = jnp.where(is_ctx, l_idx, l_idx % GRID_W)
    left_ok = col != 0
    right_ok = col != jnp.where(is_ctx, ROW_TILE - 1, GRID_W - 1)
    vert = jnp.where(is_ctx, 0.0, 1.0)
    sec = {}
    for s in range(5):
        sl = slice(s * W_M, (s + 1) * W_M)
        xbuf[0:HALO, :] = jnp.where(top_ok, top_ref[0, :, sl], 0.0)
        xbuf[HALO:HALO + ROW_TILE, :] = main_ref[0, :, sl]
        xbuf[HALO + ROW_TILE:2 * HALO + ROW_TILE, :] = jnp.where(bot_ok, bot_ref[0, :, sl], 0.0)
        cols = [None, None, None]
        for dr in range(3):
            base = HALO + GRID_W * (dr - 1)
            w3 = cw_ref[3 * dr:3 * dr + 3, sl] if dr == 1 else cw_ref[3 * dr:3 * dr + 3, sl] * vert
            xrow = xbuf[base:base + ROW_TILE, :]
            for dc in range(3):
                term = xrow * w3[dc:dc + 1]
                cols[dc] = term if cols[dc] is None else cols[dc] + term
        left = jnp.where(left_ok, pltpu.roll(cols[0], 1, 0), 0.0)
        right = jnp.where(right_ok, pltpu.roll(cols[2], ROW_TILE - 1, 0), 0.0)
        sec[s] = left + cols[1] + right
    q_ref[0] = jax.nn.silu(sec[0]).astype(q_ref.dtype)
    k_ref[0] = (jax.nn.silu(sec[1]) * (D_HM ** -0.5)).astype(k_ref.dtype)
    rr, rk, rv = sec[2], sec[3], sec[4]
    r_ref[0] = rr
    v_ref[0] = rv.astype(v_ref.dtype)
    lw = jnp.tanh(pn_ref[0, :, 0:128])
    lwf_ref[0] = -DECAY_SCALE * jax.nn.sigmoid(rpar_ref[0:1, :] + _dot3_pre(lw, wbf_ref[...]))
    lwb_ref[0] = -DECAY_SCALE * jax.nn.sigmoid(rpar_ref[1:2, :] + _dot3_pre(lw, wbb_ref[...]))
    a = jax.nn.sigmoid(rpar_ref[2:3, :] + _dot3_pre(pn_ref[0, :, 128:256], ab_ref[...]))
    gg_ref[0] = _dot3_pre(jax.nn.sigmoid(pn_ref[0, :, 256:384]), gb_ref[...])
    kap = rk * rpar_ref[3:4, :]
    norm = jnp.sqrt(_head_sums(kap * kap, bd_ref[...]))
    kh = kap / jnp.maximum(norm, 1e-12)
    kmod = rk * (1.0 + (a - 1.0) * rpar_ref[4:5, :])
    kh_ref[0] = kh
    abo_ref[0] = kh * a
    km_ref[0] = kmod
    bon_ref[0] = _head_sums(rr * kmod * rpar_ref[5:6, :], bd_ref[...]) * rv


def _prep(pc, pn, cw, rpar, wbf, wbb, ab, gb, bd):
    b, t, _ = pc.shape
    nt = t // ROW_TILE
    per = ROW_TILE // HALO
    nh = t // HALO
    w = W_M
    full = lambda shp: pl.BlockSpec(shp, lambda i, j: tuple(0 for _ in shp))
    in_specs = [
        pl.BlockSpec((1, HALO, CONV_CH), lambda i, j: (i, jnp.maximum(j * per - 1, 0), 0)),
        pl.BlockSpec((1, ROW_TILE, CONV_CH), lambda i, j: (i, j, 0)),
        pl.BlockSpec((1, HALO, CONV_CH), lambda i, j: (i, jnp.minimum(j * per + per, nh - 1), 0)),
        pl.BlockSpec((1, ROW_TILE, LORA_TAIL_W), lambda i, j: (i, j, NONCONV_W // LORA_TAIL_W - 1)),
        full((16, CONV_CH)), full((8, w)), full((384, w)), full((384, w)), full((384, w)), full((384, w)),
        full((2 * HEAD_BLK, HEAD_BLK)),
    ]
    out = pl.BlockSpec((1, ROW_TILE, w), lambda i, j: (i, j, 0))
    return pl.pallas_call(
        _prep_kernel,
        grid=(b, nt),
        in_specs=in_specs,
        out_specs=[out] * 11,
        out_shape=[jax.ShapeDtypeStruct((b, t, w), jnp.bfloat16 if name in ("q", "k", "v") else jnp.float32)
                   for name in ("q", "k", "r", "kmod", "v", "kh", "ab", "lwf", "lwb", "gg", "bonus")],
        scratch_shapes=[pltpu.VMEM((2 * HALO + ROW_TILE, w), jnp.float32)],
        compiler_params=pltpu.CompilerParams(
            dimension_semantics=("arbitrary", "arbitrary"), vmem_limit_bytes=VMEM_LIMIT),
        name="conv_prep",
    )(pc, pc, pc, pn, cw, rpar, wbf, wbb, ab, gb, bd)


def _round_robin(gens):
    out = [None] * len(gens)
    live = list(range(len(gens)))
    while live:
        nxt = []
        for i in live:
            try:
                next(gens[i])
                nxt.append(i)
            except StopIteration as stop:
                out[i] = stop.value
        live = nxt
    return out


def _rwkv_pair_chunk(r, k, v, kh, ab, lw, s_mat, tri_incl, strict, incl, last_row, m0, m1):
    cw = _dot_exact_lhs(tri_incl, lw)
    yield
    e_pos = jnp.exp(cw)
    e_neg = jnp.exp(-cw)
    e_prev = jnp.exp(cw - lw)
    stack = lambda a: jnp.concatenate([a * m0, a * m1], axis=0)
    kt = stack(kh * e_prev)
    bt = stack(ab * e_neg)
    kk = stack(k * e_neg)
    rt = stack(r * e_pos)
    vs = stack(v)
    zero = jnp.zeros((), jnp.float32)
    kr = jnp.concatenate([kt, rt], axis=0)
    bk = jnp.concatenate([bt, kk], axis=0)
    amat = _dot1(kr, bk, _NT)
    yield
    a_bk = jnp.where(strict, amat[:PAIR, :PAIR], zero)
    a_kk = jnp.where(strict, amat[:PAIR, PAIR:], zero)
    a_rb = jnp.where(incl, amat[PAIR:, :PAIR], zero)
    a_rk = jnp.where(incl, amat[PAIR:, PAIR:], zero)
    n_mat = -a_bk
    eye = (lax.broadcasted_iota(jnp.int32, (PAIR, PAIR), 0)
           == lax.broadcasted_iota(jnp.int32, (PAIR, PAIR), 1)).astype(jnp.float32)
    q = eye + n_mat
    p = _dot1(n_mat, n_mat)
    ks = _dot1(kr, s_mat, _NT)
    av = _dot1(jnp.concatenate([a_kk, a_rk], axis=0), vs)
    yield
    for _ in range(4):
        qp = _dot1(jnp.concatenate([q, p], axis=0), p)
        yield
        q = q + qp[:PAIR]
        p = qp[PAIR:]
    t_inv = q + _dot1(q, p)
    yield
    u = -_dot1(t_inv, ks[:PAIR] + av[:PAIR])
    yield
    y = ks[PAIR:] + av[PAIR:] + _dot1(a_rb, u)
    yield
    w_last = jnp.sum(jnp.where(last_row, e_pos, zero), axis=0, keepdims=True)
    s_new = (s_mat + _dot1(jnp.concatenate([u, vs], axis=0), bk, _TN)) * w_last
    return y[:CHUNK] + y[CHUNK:], s_new


def _rwkv_kernel(rf, kf, vf, khf, abf, lwf, rb, kb, vb, khb, abb, lwb, yf_ref, yb_ref, s_ref, *, nctx):
    j = pl.program_id(1)

    @pl.when(j == 0)
    def _():
        s_ref[...] = jnp.zeros_like(s_ref)

    row = lax.broadcasted_iota(jnp.int32, (PAIR, PAIR), 0)
    col = lax.broadcasted_iota(jnp.int32, (PAIR, PAIR), 1)
    same = (row // CHUNK) == (col // CHUNK)
    r64 = lax.broadcasted_iota(jnp.int32, (CHUNK, CHUNK), 0)
    c64 = lax.broadcasted_iota(jnp.int32, (CHUNK, CHUNK), 1)
    rowl = lax.broadcasted_iota(jnp.int32, (CHUNK, PAIR), 0)
    lane = lax.broadcasted_iota(jnp.int32, (1, PAIR), 1)
    m0 = (lane < D_HR).astype(jnp.float32)
    m1 = 1.0 - m0
    dirs = (
        (rf, kf, vf, khf, abf, lwf, yf_ref, (c64 <= r64), same & (col < row), same & (col <= row), rowl == CHUNK - 1),
        (rb, kb, vb, khb, abb, lwb, yb_ref, (c64 >= r64), same & (col > row), same & (col >= row), rowl == 0),
    )
    nb = rf.shape[0]
    states = [[[s_ref[bi, d, p] for p in range(NPAIR)] for d in range(2)] for bi in range(nb)]
    gens, meta = [], []
    for d, (r_ref, k_ref, v_ref, kh_ref, ab_ref, lw_ref, y_ref, tri, strict, incl, last_row) in enumerate(dirs):
        tri = tri.astype(jnp.float32).astype(jnp.bfloat16)
        for bi in range(nb):
            for p in range(NPAIR):
                sl = slice(p * PAIR, (p + 1) * PAIR)
                gens.append(_rwkv_pair_chunk(r_ref[bi, :, sl], k_ref[bi, :, sl], v_ref[bi, :, sl],
                                             kh_ref[bi, :, sl], ab_ref[bi, :, sl], lw_ref[bi, :, sl],
                                             states[bi][d][p], tri, strict, incl, last_row, m0, m1))
                meta.append((bi, d, p, sl, y_ref))
    results = [m + r for m, r in zip(meta, _round_robin(gens))]
    for bi, d, p, sl, y_ref, y, s_new in results:
        s_ref[bi, d, p] = s_new
        y_ref[bi, :, sl] = y


def _rwkv_scan(r, k, v, kh, ab, lw_f, lw_b, ctx_len):
    b, t, w = r.shape
    nc = t // CHUNK
    nctx = ctx_len // CHUNK
    nlat = nc - nctx

    def fwd_map(i, j):
        return (i, j, 0)

    def bwd_map(i, j):
        return (i, jnp.where(j < nctx, nctx - 1 - j, nc - 1 - (j - nctx)), 0)

    nb = math.gcd(b, SCAN_BATCH)
    blk = (nb, CHUNK, w)
    in_specs = [pl.BlockSpec(blk, fwd_map)] * 6 + [pl.BlockSpec(blk, bwd_map)] * 6
    out_specs = [
        pl.BlockSpec(blk, lambda i, j: (i, jnp.maximum(j - nctx, 0), 0)),
        pl.BlockSpec(blk, lambda i, j: (i, nlat - 1 - jnp.maximum(j - nctx, 0), 0)),
    ]
    return pl.pallas_call(
        functools.partial(_rwkv_kernel, nctx=nctx),
        grid=(b // nb, nc),
        in_specs=in_specs,
        out_specs=out_specs,
        out_shape=[jax.ShapeDtypeStruct((b, t - ctx_len, w), jnp.float32)] * 2,
        scratch_shapes=[pltpu.VMEM((nb, 2, NPAIR, PAIR, PAIR), jnp.float32)],
        compiler_params=pltpu.CompilerParams(
            dimension_semantics=("arbitrary", "arbitrary"), vmem_limit_bytes=VMEM_LIMIT),
        name="rwkv_scan",
    )(r, k, v, kh, ab, lw_f, r, k, v, kh, ab, lw_b)


def _scan_max(x, reverse):
    rows = lax.broadcasted_iota(jnp.int32, x.shape, 0)
    neg_inf = jnp.full((), -jnp.inf, jnp.float32)
    step = 1
    while step < CHUNK:
        if reverse:
            shifted = jnp.where(rows < CHUNK - step, pltpu.roll(x, CHUNK - step, 0), neg_inf)
        else:
            shifted = jnp.where(rows >= step, pltpu.roll(x, step, 0), neg_inf)
        x = jnp.maximum(x, shifted)
        step *= 2
    return x


def _mlstm_gates(gates, bias, tri, m_prev, is_f, reverse):
    gl = gates + bias
    act = jnp.where(is_f, jax.nn.log_sigmoid(gl), gl)
    b_all = _dot_exact_lhs(tri, act)
    li = pltpu.roll(act, H_M, 1)
    x = li - b_all
    m_inter = b_all + m_prev
    m_t = jnp.maximum(m_inter, b_all + _scan_max(x, reverse))
    b_last = b_all[0:1, :] if reverse else b_all[CHUNK - 1:CHUNK, :]
    g = b_last + x
    m_new = jnp.maximum(b_last + m_prev, jnp.max(g, axis=0, keepdims=True))
    wts = jnp.exp(g - m_new)
    decay = jnp.exp(b_last + m_prev - m_new)
    return b_all - m_t, jnp.exp(m_inter - m_t), jnp.exp(-m_t), x, wts, decay, m_new


def _mlstm_pair_chunk(q, k, v, pieces, cf, decay_g, c_mat, n_row, incl, m0, m1):
    stack = lambda a: jnp.concatenate([a * m0, a * m1], axis=0)
    q_st, k_st, v_st = stack(q), stack(k), stack(v)
    lane = lax.broadcasted_iota(jnp.int32, (1, PAIR), 1)
    zero_bf = jnp.zeros((), jnp.bfloat16)
    sel = lambda a: jnp.concatenate([jnp.where(lane == cf, a, zero_bf), jnp.where(lane == cf + 1, a, zero_bf)], 0)
    ones = jnp.ones((PAIR, PAIR), jnp.bfloat16)
    spread = [sel(a) for a in pieces[0]] + [sel(p[0]) for p in pieces[1:4]]
    bc = _dg(jnp.concatenate(spread, axis=0), ones, _NN)
    bm_bc = bc[:PAIR] + bc[PAIR:2 * PAIR] + bc[2 * PAIR:3 * PAIR]
    inter_bc, enm_bc, wts_bc = (bc[i * PAIR:(i + 1) * PAIR] for i in range(3, 6))
    r3 = _dg(ones, jnp.concatenate([sel(a) for a in pieces[4]], axis=0), _NT)
    r_mat = r3[:, :PAIR] + r3[:, PAIR:2 * PAIR] + r3[:, 2 * PAIR:]
    qk = _dot1(q_st, k_st, _NT)
    qc = _dot1(q_st, c_mat)
    yield
    neg_inf = jnp.full((), -jnp.inf, jnp.float32)
    s = qk * jnp.exp(jnp.where(incl, bm_bc + r_mat, neg_inf))
    wk = wts_bc * k_st
    sv = _dot1(s, v_st)
    kv = _dot1(wk, v_st, _TN)
    sums = _dot1(jnp.concatenate([q_st * n_row, s], axis=0), jnp.ones((PAIR, PAIR), jnp.float32))
    yield
    den = inter_bc * sums[:PAIR] + sums[PAIR:]
    h_st = (inter_bc * qc + sv) / jnp.maximum(jnp.abs(den), enm_bc)
    h = h_st[:CHUNK] + h_st[CHUNK:]
    decay_row = jnp.where(lane < D_HM, decay_g[:, cf:cf + 1], decay_g[:, cf + 1:cf + 2])
    c_new = decay_row * c_mat + kv
    n_new = decay_row * n_row + jnp.sum(wk, axis=0, keepdims=True)
    return h, c_new, n_new


def _mlstm_kernel(qf, kf, vf, gf, qb, kb, vb, gb, bias_ref, hf_ref, hb_ref, c_ref, n_ref, m_ref, *, nctx):
    del nctx
    j = pl.program_id(1)

    @pl.when(j == 0)
    def _():
        c_ref[...] = jnp.zeros_like(c_ref)
        n_ref[...] = jnp.zeros_like(n_ref)
        m_ref[...] = jnp.zeros_like(m_ref)

    row = lax.broadcasted_iota(jnp.int32, (PAIR, PAIR), 0)
    col = lax.broadcasted_iota(jnp.int32, (PAIR, PAIR), 1)
    same = (row // CHUNK) == (col // CHUNK)
    r64 = lax.broadcasted_iota(jnp.int32, (CHUNK, CHUNK), 0)
    c64 = lax.broadcasted_iota(jnp.int32, (CHUNK, CHUNK), 1)
    lane = lax.broadcasted_iota(jnp.int32, (1, PAIR), 1)
    m0 = (lane < D_HM).astype(jnp.float32)
    m1 = 1.0 - m0
    is_f = (lane % (2 * H_M)) >= H_M
    dirs = (
        (qf, kf, vf, gf, hf_ref, (c64 <= r64), same & (col <= row)),
        (qb, kb, vb, gb, hb_ref, (c64 >= r64), same & (col >= row)),
    )
    nb = qf.shape[0]
    states = [[[(c_ref[bi, d, p], n_ref[bi, d, p]) for p in range(NPAIR)] for d in range(2)] for bi in range(nb)]
    gens, meta, m_news = [], [], []
    lead = lambda a: (a.astype(jnp.bfloat16),)
    for d, (q_ref, k_ref, v_ref, g_ref, h_ref, tri, incl) in enumerate(dirs):
        tri = tri.astype(jnp.float32).astype(jnp.bfloat16)
        for bi in range(nb):
            bm, inter, enm, x, wts, decay_g, m_new = _mlstm_gates(g_ref[bi], bias_ref[...], tri, m_ref[bi, d], is_f,
                                                                  d == 1)
            m_news.append((bi, d, m_new))
            pieces = [_split3(bm), lead(inter), lead(enm), lead(wts), _split3(x)]
            for p in range(NPAIR):
                sl = slice(p * PAIR, (p + 1) * PAIR)
                cf = d * 2 * H_M + H_M + 2 * p
                gens.append(_mlstm_pair_chunk(q_ref[bi, :, sl], k_ref[bi, :, sl], v_ref[bi, :, sl], pieces, cf,
                                               decay_g, *states[bi][d][p], incl, m0, m1))
                meta.append((bi, d, p, sl, h_ref))
    for (bi, d, p, sl, h_ref), (h, c_new, n_new) in zip(meta, _round_robin(gens)):
        c_ref[bi, d, p] = c_new
        n_ref[bi, d, p] = n_new
        h_ref[bi, :, sl] = h
    for bi, d, m_new in m_news:
        m_ref[bi, d] = m_new


def _mlstm_scan(q, k, pn, bias_row, ctx_len):
    v = gates = pn
    b, t, w = q.shape
    nc = t // CHUNK
    nctx = ctx_len // CHUNK
    nlat = nc - nctx

    def fwd_map(i, j):
        return (i, j, 0)

    def bwd_map(i, j):
        return (i, jnp.where(j < nctx, nctx - 1 - j, nc - 1 - (j - nctx)), 0)

    nb = math.gcd(b, SCAN_BATCH)
    blk = (nb, CHUNK, w)
    gblk = (nb, CHUNK, 128)
    gate_blk = (2 * W_M) // 128
    gate_of = lambda m: (lambda i, j: m(i, j)[:2] + (gate_blk,))
    in_specs = ([pl.BlockSpec(blk, fwd_map)] * 3 + [pl.BlockSpec(gblk, gate_of(fwd_map))]
                + [pl.BlockSpec(blk, bwd_map)] * 3 + [pl.BlockSpec(gblk, gate_of(bwd_map))]
                + [pl.BlockSpec((1, 128), lambda i, j: (0, 0))])
    out_specs = [
        pl.BlockSpec(blk, lambda i, j: (i, jnp.maximum(j - nctx, 0), 0)),
        pl.BlockSpec(blk, lambda i, j: (i, nlat - 1 - jnp.maximum(j - nctx, 0), 0)),
    ]
    return pl.pallas_call(
        functools.partial(_mlstm_kernel, nctx=nctx),
        grid=(b // nb, nc),
        in_specs=in_specs,
        out_specs=out_specs,
        out_shape=[jax.ShapeDtypeStruct((b, t - ctx_len, w), jnp.float32)] * 2,
        scratch_shapes=[pltpu.VMEM((nb, 2, NPAIR, PAIR, PAIR), jnp.float32),
                        pltpu.VMEM((nb, 2, NPAIR, 1, PAIR), jnp.float32),
                        pltpu.VMEM((nb, 2, 1, 128), jnp.float32)],
        compiler_params=pltpu.CompilerParams(
            dimension_semantics=("arbitrary", "arbitrary"), vmem_limit_bytes=VMEM_LIMIT),
        name="mlstm_scan",
    )(q, k, v, gates, q, k, v, gates, bias_row)


ROUTE_W = 128
N_ROUTE = N_GROUPS + N_EXPERTS


def _ln_rows(z, eps):
    mu = jnp.mean(z, -1, keepdims=True)
    zc = z - mu
    var = jnp.mean(zc * zc, -1, keepdims=True)
    return zc * lax.rsqrt(var + eps)


OUT_TILES = 2


def _outproj_tile(rows, hf, hb, mo, yf, yb, gg, bonus, x_ref, mod_ref, wout_ref, hpar_ref, ln_ref, rtw_ref, rtb_ref,
                  bd_ref, x1_ref, tok_ref):
    avg = bd_ref[...]
    zm = hf[0, rows, :] + hb[0, rows, :]
    zr = yf[0, rows, :] + yb[0, rows, :]
    mu_m = _head_sums(zm, avg) * (1.0 / D_HM)
    mu_r = _head_sums(zr, avg) * (1.0 / D_HR)
    yield
    cm, cr = zm - mu_m, zr - mu_r
    var_m = _head_sums(cm * cm, avg) * (1.0 / D_HM)
    var_r = _head_sums(cr * cr, avg) * (1.0 / D_HR)
    yield
    out_m = jax.nn.sigmoid(mo[0]) * (cm * lax.rsqrt(var_m + LN_EPS) * hpar_ref[0:1, :])
    y = cr * lax.rsqrt(var_r + GN_EPS) * hpar_ref[1:2, :] + hpar_ref[2:3, :]
    out_r = (y + bonus[0]) * gg[0]
    proj = (jnp.dot(out_m.astype(jnp.bfloat16), wout_ref[:W_M, :], preferred_element_type=jnp.float32)
            + jnp.dot(out_r.astype(jnp.bfloat16), wout_ref[W_M:, :], preferred_element_type=jnp.float32))
    yield
    g1, sh2, sc2 = mod_ref[0, 2:3, :], mod_ref[0, 3:4, :], mod_ref[0, 4:5, :]
    x1 = _ln_rows(ALPHA * x_ref[0, rows, :] + g1 * proj, LN_EPS) * ln_ref[0:1, :] + ln_ref[1:2, :]
    x1_ref[0, rows, :] = x1
    tok = _ln_rows(x1, LN_EPS) * (1.0 + sc2) + sh2
    tok_ref[0, rows, :] = tok
    logits = _dot3_pre(tok, rtw_ref[...]) + rtb_ref[...]
    yield
    lane = lax.broadcasted_iota(jnp.int32, (ROW_TILE, ROUTE_W), 1)
    neg_inf = jnp.full((), -jnp.inf, jnp.float32)
    big = jnp.int32(ROUTE_W)
    is_g = lane < N_GROUPS
    lg = jnp.where(is_g, logits, neg_inf)
    gmax = jnp.max(lg, axis=1, keepdims=True)
    grp = jnp.min(jnp.where(lg == gmax, lane, big), axis=1, keepdims=True)
    p_top = 1.0 / jnp.sum(jnp.where(is_g, jnp.exp(lg - gmax), 0.0), axis=1, keepdims=True)
    in_grp = (lane >= N_GROUPS) & (lane < N_ROUTE) & ((lane - N_GROUPS) // E_PER_GROUP == grp)
    le = jnp.where(in_grp, logits, neg_inf)
    v1 = jnp.max(le, axis=1, keepdims=True)
    i1 = jnp.min(jnp.where(le == v1, lane, big), axis=1, keepdims=True)
    le2 = jnp.where(lane == i1, neg_inf, le)
    v2 = jnp.max(le2, axis=1, keepdims=True)
    i2 = jnp.min(jnp.where(le2 == v2, lane, big), axis=1, keepdims=True)
    e21 = jnp.exp(v2 - v1)
    w1 = (1.0 / (1.0 + e21)) * p_top
    w2 = (e21 / (1.0 + e21)) * p_top

    sel1, sel2 = lane == i1, lane == i2
    onehot = jnp.where(sel1 | sel2, 1.0, 0.0)
    tr = lax.broadcasted_iota(jnp.int32, (ROW_TILE, ROW_TILE), 0)
    tc = lax.broadcasted_iota(jnp.int32, (ROW_TILE, ROW_TILE), 1)
    strict = jnp.where(tc < tr, 1.0, 0.0).astype(jnp.bfloat16)
    before = _dg(strict, onehot.astype(jnp.bfloat16), _NN)
    yield
    e1 = (i1 - N_GROUPS).astype(jnp.float32)
    e2 = (i2 - N_GROUPS).astype(jnp.float32)
    return lane, sel1, sel2, onehot, before, e1, e2, w1, w2


def _outproj_kernel(hf, hb, mo_a, mo_b, yf, yb, gg_a, gg_b, bon_a, bon_b, x_ref, mod_ref, wout_ref, hpar_ref, ln_ref,
                    rtw_ref, rtb_ref, bd_ref, x1_ref, tok_ref, route_ref, ridx_ref, cnt_ref, carry_ref):
    i, j = pl.program_id(0), pl.program_id(1)

    @pl.when((i == 0) & (j == 0))
    def _():
        carry_ref[...] = jnp.zeros_like(carry_ref)

    tiles = ((mo_a, gg_a, bon_a), (mo_b, gg_b, bon_b))
    gens = [_outproj_tile(slice(t * ROW_TILE, (t + 1) * ROW_TILE), hf, hb, mo, yf, yb, gg, bonus, x_ref, mod_ref,
                          wout_ref, hpar_ref, ln_ref, rtw_ref, rtb_ref, bd_ref, x1_ref, tok_ref)
            for t, (mo, gg, bonus) in enumerate(tiles)]
    carry = carry_ref[...]
    for t, (lane, sel1, sel2, onehot, before, e1, e2, w1, w2) in enumerate(_round_robin(gens)):
        before = before + carry
        r1 = jnp.sum(jnp.where(sel1, before, 0.0), axis=1, keepdims=True)
        r2 = jnp.sum(jnp.where(sel2, before, 0.0), axis=1, keepdims=True)
        carry = carry + jnp.sum(onehot, axis=0, keepdims=True)
        route = jnp.where(lane == 0, e1, jnp.where(lane == 1, e2, jnp.where(lane == 2, r1, jnp.where(
            lane == 3, r2, jnp.where(lane == 4, w1, jnp.where(lane == 5, w2, 0.0))))))
        route_ref[0, t * ROW_TILE:(t + 1) * ROW_TILE, :] = route
        ridx_ref[t] = route.T[0:8, :].astype(jnp.int32)
    carry_ref[...] = carry
    cnt_ref[...] = jnp.broadcast_to(carry, cnt_ref.shape)


def _outproj(h_f, h_b, pn, y_f, y_b, gg, bonus, x, mod3, wout_bf16, hpar, lnpar, rtw, rtb, bd, ctx_len):
    b, seq, d = x.shape
    nt = seq // ROW_TILE
    assert nt % OUT_TILES == 0
    off = ctx_len // ROW_TILE
    step_rows = OUT_TILES * ROW_TILE
    lat = lambda w: pl.BlockSpec((1, step_rows, w), lambda i, j: (i, j, 0))
    full = lambda shp: pl.BlockSpec(shp, lambda i, j: tuple(0 for _ in shp))
    shifted = lambda w, t, lane_blk: pl.BlockSpec((1, ROW_TILE, w),
                                                  lambda i, j: (i, OUT_TILES * j + t + off, lane_blk))
    in_specs = [
        lat(W_M), lat(W_M),
        shifted(W_M, 0, 1), shifted(W_M, 1, 1),
        lat(W_R), lat(W_R),
        shifted(W_R, 0, 0), shifted(W_R, 1, 0),
        shifted(W_R, 0, 0), shifted(W_R, 1, 0),
        lat(d),
        pl.BlockSpec((1, 6, d), lambda i, j: (i, 0, 0)),
        full((MIX_W, d)), full((8, W_M)), full((8, d)), full((3 * d, ROUTE_W)), full((1, ROUTE_W)),
        full((2 * HEAD_BLK, HEAD_BLK)),
    ]
    out_specs = [
        lat(d), lat(d), lat(ROUTE_W),
        pl.BlockSpec((OUT_TILES, 8, ROW_TILE), lambda i, j: (i * (nt // OUT_TILES) + j, 0, 0)),
        full((8, ROUTE_W)),
    ]
    out_shape = [
        jax.ShapeDtypeStruct((b, seq, d), jnp.float32),
        jax.ShapeDtypeStruct((b, seq, d), jnp.float32),
        jax.ShapeDtypeStruct((b, seq, ROUTE_W), jnp.float32),
        jax.ShapeDtypeStruct((b * nt, 8, ROW_TILE), jnp.int32),
        jax.ShapeDtypeStruct((8, ROUTE_W), jnp.float32),
    ]
    return pl.pallas_call(
        _outproj_kernel,
        grid=(b, nt // OUT_TILES),
        in_specs=in_specs,
        out_specs=out_specs,
        out_shape=out_shape,
        scratch_shapes=[pltpu.VMEM((1, ROUTE_W), jnp.float32)],
        compiler_params=pltpu.CompilerParams(
            dimension_semantics=("arbitrary", "arbitrary"), vmem_limit_bytes=VMEM_LIMIT),
        name="outproj_router",
    )(h_f, h_b, pn, pn, y_f, y_b, gg, gg, bonus, bonus, x, mod3, wout_bf16, hpar, lnpar, rtw, rtb, bd)


DEST_TILES = 8


def _dest_kernel(starts_ref, ridx_ref, dest_ref):
    blk = ridx_ref[...]
    dest = pltpu.roll(blk, 8 - TOP_K_IN_GROUP, 1)
    for e in range(N_EXPERTS):
        dest = dest + jnp.where(blk == e, starts_ref[e], 0)
    dest_ref[...] = dest


def _moe_dest(starts, ridx):
    nt = ridx.shape[0]
    per_step = math.gcd(nt, DEST_TILES)
    blk = pl.BlockSpec((per_step, 8, ROW_TILE), lambda i, s: (i, 0, 0))
    return pl.pallas_call(
        _dest_kernel,
        grid_spec=pltpu.PrefetchScalarGridSpec(num_scalar_prefetch=1, grid=(nt // per_step,),
                                               in_specs=[blk], out_specs=blk),
        out_shape=jax.ShapeDtypeStruct(ridx.shape, jnp.int32),
        compiler_params=pltpu.CompilerParams(dimension_semantics=("arbitrary",), vmem_limit_bytes=VMEM_LIMIT),
        name="moe_dest",
    )(starts, ridx)


SUB = 8
SUB_SHIFT = 3
TILE_GROUPS = ROW_TILE // SUB


def _scatter_kernel(zinfo_ref, dest_ref, tok_ref, xs_ref, zbuf, sem, zsem):
    i = pl.program_id(0)

    @pl.when(i == 0)
    def _():
        zbuf[...] = jnp.zeros_like(zbuf)

    def zero_row(row):
        return pltpu.make_async_copy(zbuf.at[0, pl.ds(0, 1), :],
                                     xs_ref.at[row >> SUB_SHIFT, pl.ds(row & (SUB - 1), 1), :], zsem)

    e = jnp.minimum(i, N_EXPERTS - 1)
    lo = zinfo_ref[e]
    hi = jnp.where(i < N_EXPERTS, zinfo_ref[N_EXPERTS + e], lo)
    tail_blk = zinfo_ref[2 * N_EXPERTS] + i
    has_tail = tail_blk < zinfo_ref[2 * N_EXPERTS + 1]
    tail_copy = pltpu.make_async_copy(zbuf, xs_ref.at[pl.ds(tail_blk * TILE_GROUPS, TILE_GROUPS)], zsem)

    def start_zero(row, c):
        zero_row(row).start()
        return c

    lax.fori_loop(lo, hi, start_zero, 0)

    @pl.when(has_tail)
    def _():
        tail_copy.start()

    def issue(g, c):
        for k in range(SUB):
            for slot in range(TOP_K_IN_GROUP):
                dst = dest_ref[slot * ROW_TILE + g * SUB + k]
                pltpu.make_async_copy(tok_ref.at[g, pl.ds(k, 1), :],
                                      xs_ref.at[dst >> SUB_SHIFT, pl.ds(dst & (SUB - 1), 1), :], sem).start()
        return c

    lax.fori_loop(0, TILE_GROUPS, issue, 0)
    for half in range(TOP_K_IN_GROUP):
        pltpu.make_async_copy(tok_ref, xs_ref.at[pl.ds(half * TILE_GROUPS, TILE_GROUPS)], sem).wait()

    def wait_zero(row, c):
        zero_row(row).wait()
        return c

    lax.fori_loop(lo, hi, wait_zero, 0)

    @pl.when(has_tail)
    def _():
        tail_copy.wait()


def _moe_scatter(zinfo, dest_flat, tok2d, n_rows):
    n, d = tok2d.shape
    nt = n // ROW_TILE
    assert nt >= N_EXPERTS
    xs = pl.pallas_call(
        _scatter_kernel,
        grid_spec=pltpu.PrefetchScalarGridSpec(
            num_scalar_prefetch=1,
            grid=(nt,),
            in_specs=[
                pl.BlockSpec((SUB * ROW_TILE,), lambda i, z: (i,), memory_space=pltpu.SMEM),
                pl.BlockSpec((TILE_GROUPS, SUB, d), lambda i, z: (i, 0, 0)),
            ],
            out_specs=pl.BlockSpec(memory_space=pl.ANY),
            scratch_shapes=[pltpu.VMEM((TILE_GROUPS, SUB, d), tok2d.dtype), pltpu.SemaphoreType.DMA(()),
                            pltpu.SemaphoreType.DMA(())],
        ),
        out_shape=jax.ShapeDtypeStruct((n_rows // SUB, SUB, d), tok2d.dtype),
        compiler_params=pltpu.CompilerParams(dimension_semantics=("arbitrary",), vmem_limit_bytes=VMEM_LIMIT),
        name="moe_scatter",
    )(zinfo, dest_flat, tok2d.reshape(n // SUB, SUB, d))
    return xs.reshape(n_rows, d)


def _expert_kernel(blk_exp_ref, valid_ref, xs_ref, wg_ref, wu_ref, wd_ref, ys_ref, wg_bf, wu_bf, wd_bf):
    i = pl.program_id(0)
    valid = valid_ref[i]

    @pl.when((i == 0) | (blk_exp_ref[i] != blk_exp_ref[jnp.maximum(i - 1, 0)]))
    def _():
        wg_bf[...] = wg_ref[0].astype(jnp.bfloat16)
        wu_bf[...] = wu_ref[0].astype(jnp.bfloat16)
        wd_bf[...] = wd_ref[0].astype(jnp.bfloat16)

    @pl.when(valid > 0)
    def _():
        xb = xs_ref[...].astype(jnp.bfloat16)
        hg = jnp.dot(xb, wg_bf[...], preferred_element_type=jnp.float32)
        hu = jnp.dot(xb, wu_bf[...], preferred_element_type=jnp.float32)
        hb = (jax.nn.silu(hg) * hu).astype(jnp.bfloat16)
        ys_ref[...] = jnp.dot(hb, wd_bf[...], preferred_element_type=jnp.float32)

    @pl.when(valid == 0)
    def _():
        ys_ref[...] = jnp.zeros_like(ys_ref)


def _moe_experts(blk_exp, blk_valid, xs, wg, wu, wd):
    nrow, d = xs.shape
    de = wg.shape[2]
    return pl.pallas_call(
        _expert_kernel,
        grid_spec=pltpu.PrefetchScalarGridSpec(
            num_scalar_prefetch=2,
            grid=(nrow // MOE_BLOCK,),
            in_specs=[
                pl.BlockSpec((MOE_BLOCK, d), lambda i, be, nu: (i, 0)),
                pl.BlockSpec((1, d, de), lambda i, be, nu: (be[i], 0, 0)),
                pl.BlockSpec((1, d, de), lambda i, be, nu: (be[i], 0, 0)),
                pl.BlockSpec((1, de, d), lambda i, be, nu: (be[i], 0, 0)),
            ],
            out_specs=pl.BlockSpec((MOE_BLOCK, d), lambda i, be, nu: (i, 0)),
            scratch_shapes=[pltpu.VMEM((d, de), jnp.bfloat16), pltpu.VMEM((d, de), jnp.bfloat16),
                            pltpu.VMEM((de, d), jnp.bfloat16)],
        ),
        out_shape=jax.ShapeDtypeStruct((nrow, d), jnp.float32),
        compiler_params=pltpu.CompilerParams(dimension_semantics=("arbitrary",), vmem_limit_bytes=VMEM_LIMIT),
        name="moe_experts",
    )(blk_exp, blk_valid, xs, wg, wu, wd)


def _combine_kernel(dest_ref, dest_next_ref, route_ref, x1_ref, mod_ref, ln_ref, ys_ref, out_ref,
                    ybuf, sems):
    i = pl.program_id(0)
    n = pl.num_programs(0)
    cur = i % 2

    def gather(idx_ref, buf):
        def issue(g, c):
            for k in range(SUB):
                for slot in range(TOP_K_IN_GROUP):
                    src = idx_ref[slot * ROW_TILE + g * SUB + k]
                    pltpu.make_async_copy(ys_ref.at[src >> SUB_SHIFT, pl.ds(src & (SUB - 1), 1), :],
                                          ybuf.at[buf, slot * TILE_GROUPS + g, pl.ds(k, 1), :],
                                          sems.at[buf]).start()
            return c

        lax.fori_loop(0, TILE_GROUPS, issue, 0)

    @pl.when(i == 0)
    def _():
        gather(dest_ref, 0)

    @pl.when(i + 1 < n)
    def _():
        gather(dest_next_ref, 1 - cur)

    pltpu.make_async_copy(ys_ref.at[pl.ds(0, TOP_K_IN_GROUP * TILE_GROUPS)], ybuf.at[cur], sems.at[cur]).wait()
    route = route_ref[0]
    d = out_ref.shape[-1]
    y0 = ybuf[cur, 0:TILE_GROUPS].reshape(ROW_TILE, d)
    y1 = ybuf[cur, TILE_GROUPS:2 * TILE_GROUPS].reshape(ROW_TILE, d)
    ffn = y0 * route[:, 4:5] + y1 * route[:, 5:6]
    g2 = mod_ref[0, 5:6, :]
    out_ref[0] = _ln_rows(ALPHA * x1_ref[0] + g2 * ffn, LN_EPS) * ln_ref[2:3, :] + ln_ref[3:4, :]


def _moe_combine(dest_flat, route, x1, mod3, lnpar, ys):
    b, seq, d = x1.shape
    nt = seq // ROW_TILE
    n = b * nt
    tile = lambda w: pl.BlockSpec((1, ROW_TILE, w), lambda i: (i // nt, i % nt, 0))
    idx_blk = lambda f: pl.BlockSpec((SUB * ROW_TILE,), f, memory_space=pltpu.SMEM)
    return pl.pallas_call(
        _combine_kernel,
        grid=(n,),
        in_specs=[
            idx_blk(lambda i: (i,)),
            idx_blk(lambda i: (jnp.minimum(i + 1, n - 1),)),
            tile(ROUTE_W),
            tile(d),
            pl.BlockSpec((1, 6, d), lambda i: (i // nt, 0, 0)),
            pl.BlockSpec((8, d), lambda i: (0, 0)),
            pl.BlockSpec(memory_space=pl.ANY),
        ],
        out_specs=tile(d),
        scratch_shapes=[pltpu.VMEM((2, TOP_K_IN_GROUP * TILE_GROUPS, SUB, d), jnp.float32),
                        pltpu.SemaphoreType.DMA((2,))],
        out_shape=jax.ShapeDtypeStruct((b, seq, d), jnp.float32),
        compiler_params=pltpu.CompilerParams(dimension_semantics=("arbitrary",), vmem_limit_bytes=VMEM_LIMIT),
        name="moe_combine",
    )(dest_flat, dest_flat, route, x1, mod3, lnpar, ys.reshape(ys.shape[0] // SUB, SUB, d))


def _moe_plan(cnt, n_pairs):
    counts = cnt[0, N_GROUPS:N_ROUTE].astype(jnp.int32)
    padded = (counts + MOE_BLOCK - 1) // MOE_BLOCK * MOE_BLOCK
    pad_ends = jnp.cumsum(padded)
    pad_starts = pad_ends - padded
    n_blk = -(-n_pairs // MOE_BLOCK) + N_EXPERTS
    blk_row0 = jnp.arange(n_blk, dtype=jnp.int32) * MOE_BLOCK
    blk_exp = jnp.minimum(jnp.sum((pad_ends[None, :] <= blk_row0[:, None]).astype(jnp.int32), axis=1), N_EXPERTS - 1)
    blk_valid = jnp.clip(counts[blk_exp] - (blk_row0 - pad_starts[blk_exp]), 0, MOE_BLOCK)
    blk_valid = jnp.where(blk_row0 < pad_ends[-1], blk_valid, 0).astype(jnp.int32)
    zinfo = jnp.concatenate([pad_starts + counts, pad_ends, pad_ends[-1:] // MOE_BLOCK,
                             jnp.full((1,), n_blk, jnp.int32)]).astype(jnp.int32)
    return pad_starts.astype(jnp.int32), blk_exp, blk_valid, zinfo, n_blk


def kernel(x, c, ctx, c_ctx, w_ada, b_ada, w_in, conv_w, m_bias_i, m_bias_f, m_norm_w, r_w0, r_wB, r_a0, r_aB,
           r_gB, r_kk, r_ka, r_bonus, r_norm_w, r_norm_b, w_out, ln1_g, ln1_b, ln2_g, ln2_b, rt_g, rt_g_b, rt_e,
           rt_e_b, ex_gate, ex_up, ex_down):
    assert w_ada.shape[0] == DEPTH
    bsz, seq, d = x.shape
    ctx_len = ctx.shape[1]
    assert ctx_len == ROW_TILE and seq % ROW_TILE == 0
    assert seq % GRID_W == 0 and ROW_TILE % GRID_W == 0
    l = 0
    zrow = lambda n, w: jnp.zeros((n, w), jnp.float32)
    mrows = -(-(bsz + 1) // 8) * 8
    c_pad = jnp.concatenate([c, c_ctx[None, :], zrow(mrows - bsz - 1, d)], 0)
    mod_all = _ada(c_pad, w_ada[l], b_ada[l][None, :])[:bsz + 1]
    mod3 = mod_all.reshape(bsz + 1, 6, d)
    pc, pn = _inproj(ctx, x, mod3, _pad_w_in(w_in[l]))
    cw = jnp.concatenate([conv_w[l].reshape(CONV_K * CONV_K, CONV_CH), zrow(16 - CONV_K * CONV_K, CONV_CH)], 0)
    rpar = jnp.concatenate([r_w0[l], r_a0[l][None, :], r_kk[l][None, :], r_ka[l][None, :],
                            r_bonus[l].reshape(1, W_R), zrow(2, W_R)], 0)
    wbf = jnp.concatenate([r_wB[l][0], zrow(W_LORA, W_R)], 0)
    wbb = jnp.concatenate([zrow(W_LORA, W_R), r_wB[l][1]], 0)
    ab_mat = jnp.concatenate([r_aB[l], zrow(128 - A_LORA, W_R)], 0)
    bd = _head_sum_operator(D_HR)
    q, k, rr, kmod, rv, kh, kha, logw_f, logw_b, gg, bonus = _prep(
        pc, pn, cw, rpar, _stack3(wbf), _stack3(wbb), _stack3(ab_mat), _stack3(r_gB[l]), bd)
    bias_row = jnp.concatenate([m_bias_i[l][0], m_bias_f[l][0], m_bias_i[l][1], m_bias_f[l][1],
                                jnp.zeros((128 - 4 * H_M,), jnp.float32)])[None, :]
    h_f, h_b = _mlstm_scan(q, k, pn, bias_row, ctx_len)
    y_f, y_b = _rwkv_scan(rr, kmod, rv, kh, kha, logw_f, logw_b, ctx_len)
    hpar = jnp.zeros((8, W_M), jnp.float32).at[0].set(m_norm_w[l]).at[1].set(r_norm_w[l]).at[2].set(r_norm_b[l])
    lnpar = jnp.zeros((8, d), jnp.float32).at[0].set(ln1_g[l]).at[1].set(ln1_b[l]).at[2].set(ln2_g[l]).at[3].set(
        ln2_b[l])
    rtw = jnp.concatenate([rt_g[l], rt_e[l], jnp.zeros((d, ROUTE_W - N_ROUTE), jnp.float32)], axis=1)
    rtb = jnp.concatenate([rt_g_b[l], rt_e_b[l], jnp.zeros((ROUTE_W - N_ROUTE,), jnp.float32)])[None, :]
    x1, tok, route, ridx, cnt = _outproj(h_f, h_b, pn, y_f, y_b, gg, bonus, x, mod3, w_out[l].astype(jnp.bfloat16),
                                         hpar, lnpar, _stack3(rtw), rtb, bd, ctx_len)
    n_tok = bsz * seq
    starts, blk_exp, blk_valid, zinfo, n_blk = _moe_plan(cnt, n_tok * TOP_K_IN_GROUP)
    dest = _moe_dest(starts, ridx)
    dest_flat = dest.reshape(-1)
    xs = _moe_scatter(zinfo, dest_flat, tok.reshape(n_tok, d), n_blk * MOE_BLOCK)
    ys = _moe_experts(blk_exp, blk_valid, xs, ex_gate[l], ex_up[l], ex_down[l])
    return _moe_combine(dest_flat, route, x1, mod3, lnpar, ys)
```

```python
import functools
import math

import jax
import jax.numpy as jnp
from jax import lax
import numpy as np
from jax.experimental import pallas as pl
from jax.experimental.pallas import tpu as pltpu

H_M = 8
D_HM = 64
W_M = H_M * D_HM
H_R = 8
D_HR = 64
W_R = H_R * D_HR
MIX_W = W_M + W_R
W_LORA = 64
A_LORA = 64
G_LORA = 128
CONV_K = 3
CONV_CH = 2 * W_M + 3 * W_R
SECTION_WIDTHS = (W_M, W_M, W_R, W_R, W_R, W_M, W_M, 4 * H_M, W_LORA, W_LORA, A_LORA, G_LORA)
IN_COLS = sum(SECTION_WIDTHS)
SPLIT_POINTS = tuple(int(v) for v in np.cumsum(SECTION_WIDTHS)[:-1])
CHUNK = 64
N_GROUPS = 4
E_PER_GROUP = 8
N_EXPERTS = N_GROUPS * E_PER_GROUP
TOP_K_IN_GROUP = 2
D_EXPERT = 512
MOE_BLOCK = 256
DEPTH = 1
ALPHA = (2.0 * DEPTH) ** 0.25
DECAY_SCALE = math.exp(-0.5)
LN_EPS = 1e-6
GN_EPS = 64e-5

ROW_TILE = 256
NONCONV_W = 1536
LORA_TAIL_W = 384
VMEM_LIMIT = 56 * 1024 * 1024
SCAN_BATCH = 8


def _inproj_kernel(ctx_ref, x_ref, mod_ref, w_ref, oc_ref, on_ref):
    j = pl.program_id(1)

    def body(src_ref):
        z = src_ref[0]
        mu = jnp.mean(z, -1, keepdims=True)
        zc = z - mu
        var = jnp.mean(zc * zc, -1, keepdims=True)
        y = zc * lax.rsqrt(var + LN_EPS)
        h = (y * (1.0 + mod_ref[0, 1:2, :]) + mod_ref[0, 0:1, :]).astype(jnp.bfloat16)
        oc_ref[0] = jnp.dot(h, w_ref[:, :CONV_CH], preferred_element_type=jnp.float32)
        on_ref[0] = jnp.dot(h, w_ref[:, CONV_CH:], preferred_element_type=jnp.float32)

    @pl.when(j == 0)
    def _():
        body(ctx_ref)

    @pl.when(j > 0)
    def _():
        body(x_ref)


def _inproj(ctx, x, mod_all, w_pad):
    b, seq, d = x.shape
    nt = (ctx.shape[1] + seq) // ROW_TILE
    t_all = ctx.shape[1] + seq
    return pl.pallas_call(
        _inproj_kernel,
        grid=(b, nt),
        in_specs=[
            pl.BlockSpec((1, ROW_TILE, d), lambda i, j: (i, 0, 0)),
            pl.BlockSpec((1, ROW_TILE, d), lambda i, j: (i, jnp.maximum(j - 1, 0), 0)),
            pl.BlockSpec((1, 6, d), lambda i, j: (jnp.where(j == 0, b, i), 0, 0)),
            pl.BlockSpec((d, CONV_CH + NONCONV_W), lambda i, j: (0, 0)),
        ],
        out_specs=[
            pl.BlockSpec((1, ROW_TILE, CONV_CH), lambda i, j: (i, j, 0)),
            pl.BlockSpec((1, ROW_TILE, NONCONV_W), lambda i, j: (i, j, 0)),
        ],
        out_shape=[
            jax.ShapeDtypeStruct((b, t_all, CONV_CH), jnp.float32),
            jax.ShapeDtypeStruct((b, t_all, NONCONV_W), jnp.float32),
        ],
        compiler_params=pltpu.CompilerParams(
            dimension_semantics=("arbitrary", "arbitrary"), vmem_limit_bytes=VMEM_LIMIT),
        name="inproj",
    )(ctx, x, mod_all, w_pad)


def _pad_w_in(w_in):
    d = w_in.shape[0]
    sp = SPLIT_POINTS
    z = lambda n: jnp.zeros((d, n), w_in.dtype)
    gates = w_in[:, sp[6]:sp[7]]
    lw = w_in[:, sp[7]:sp[9]]
    la = w_in[:, sp[9]:sp[10]]
    lg = w_in[:, sp[10]:]
    return jnp.concatenate([w_in[:, :sp[6]], gates, z(96), lw, la, z(64), lg], axis=1).astype(jnp.bfloat16)


PAIR = 2 * D_HR
NPAIR = H_R // 2
_NN = (((1,), (0,)), ((), ()))
_NT = (((1,), (1,)), ((), ()))
_TN = (((0,), (0,)), ((), ()))


def _split2(a):
    hi = a.astype(jnp.bfloat16)
    lo = (a - hi.astype(jnp.float32)).astype(jnp.bfloat16)
    return hi, lo


def _split3(a):
    a1 = a.astype(jnp.bfloat16)
    r1 = a - a1.astype(jnp.float32)
    a2 = r1.astype(jnp.bfloat16)
    a3 = (r1 - a2.astype(jnp.float32)).astype(jnp.bfloat16)
    return a1, a2, a3


def _dg(a, b, dims):
    return lax.dot_general(a, b, dims, preferred_element_type=jnp.float32)


def _dot3(a, b, dims=_NN):
    ah, al = _split2(a)
    bh, bl = _split2(b)
    return _dg(ah, bh, dims) + _dg(ah, bl, dims) + _dg(al, bh, dims)


def _dot1(a, b, dims=_NN):
    return _dg(a.astype(jnp.bfloat16), b.astype(jnp.bfloat16), dims)


def _stack3(w):
    hi = w.astype(jnp.bfloat16)
    lo = (w - hi.astype(jnp.float32)).astype(jnp.bfloat16)
    return jnp.concatenate([hi, lo, hi], axis=0)


def _dot3_pre(a, w3):
    ah, al = _split2(a)
    return _dg(jnp.concatenate([ah, ah, al], axis=1), w3, _NN)


HEAD_BLK = 256


def _head_sum_operator(head):
    r = np.arange(HEAD_BLK) // head
    bd = (r[:, None] == r[None, :]).astype(np.float32)
    return jnp.asarray(np.concatenate([bd, bd], axis=0), jnp.bfloat16)


def _head_sums(z, bd2):
    hi, lo = _split2(z)
    out = []
    for c in range(0, z.shape[1], HEAD_BLK):
        lhs = jnp.concatenate([hi[:, c:c + HEAD_BLK], lo[:, c:c + HEAD_BLK]], axis=1)
        out.append(_dg(lhs, bd2, _NN))
    return jnp.concatenate(out, axis=1)


def _dot_exact_lhs(a_bf16, b, dims=_NN):
    b1, b2, b3 = _split3(b)
    return _dg(a_bf16, b1, dims) + _dg(a_bf16, b2, dims) + _dg(a_bf16, b3, dims)


def _ada_kernel(c_ref, w_ref, b_ref, o_ref):
    o_ref[...] = _dot3(jax.nn.silu(c_ref[...]), w_ref[...]) + b_ref[...]


def _ada(c_pad, w, bias):
    m, d = c_pad.shape
    n = w.shape[1]
    return pl.pallas_call(
        _ada_kernel,
        grid=(n // d,),
        in_specs=[pl.BlockSpec((m, d), lambda j: (0, 0)),
                  pl.BlockSpec((d, d), lambda j: (0, j)),
                  pl.BlockSpec((1, d), lambda j: (0, j))],
        out_specs=pl.BlockSpec((m, d), lambda j: (0, j)),
        out_shape=jax.ShapeDtypeStruct((m, n), jnp.float32),
        compiler_params=pltpu.CompilerParams(dimension_semantics=("arbitrary",), vmem_limit_bytes=VMEM_LIMIT),
        name="adaln",
    )(c_pad, w, bias)


GRID_W = 64
HALO = GRID_W


def _prep_kernel(top_ref, main_ref, bot_ref, pn_ref, cw_ref, rpar_ref, wbf_ref, wbb_ref, ab_ref, gb_ref, bd_ref,
                 q_ref, k_ref, r_ref, km_ref, v_ref, kh_ref, abo_ref, lwf_ref, lwb_ref, gg_ref, bon_ref, xbuf):
    j = pl.program_id(1)
    nt = pl.num_programs(1)
    is_ctx = j == 0
    top_ok = j >= 2
    bot_ok = (j >= 1) & (j < nt - 1)
    l_idx = lax.broadcasted_iota(jnp.int32, (ROW_TILE, 1), 0)
    col = jnp.where(is_ctx, l_idx, l_idx % GRID_W)
    left_ok = col != 0
    right_ok = col != jnp.where(is_ctx, ROW_TILE - 1, GRID_W - 1)
    vert = jnp.where(is_ctx, 0.0, 1.0)
    sec = {}
    for s in range(5):
        sl = slice(s * W_M, (s + 1) * W_M)
        xbuf[0:HALO, :] = jnp.where(top_ok, top_ref[0, :, sl], 0.0)
        xbuf[HALO:HALO + ROW_TILE, :] = main_ref[0, :, sl]
        xbuf[HALO + ROW_TILE:2 * HALO + ROW_TILE, :] = jnp.where(bot_ok, bot_ref[0, :, sl], 0.0)
        cols = [None, None, None]
        for dr in range(3):
            base = HALO + GRID_W * (dr - 1)
            w3 = cw_ref[3 * dr:3 * dr + 3, sl] if dr == 1 else cw_ref[3 * dr:3 * dr + 3, sl] * vert
            xrow = xbuf[base:base + ROW_TILE, :]
            for dc in range(3):
                term = xrow * w3[dc:dc + 1]
                cols[dc] = term if cols[dc] is None else cols[dc] + term
        left = jnp.where(left_ok, pltpu.roll(cols[0], 1, 0), 0.0)
        right = jnp.where(right_ok, pltpu.roll(cols[2], ROW_TILE - 1, 0), 0.0)
        sec[s] = left + cols[1] + right
    q_ref[0] = jax.nn.silu(sec[0]).astype(q_ref.dtype)
    k_ref[0] = (jax.nn.silu(sec[1]) * (D_HM ** -0.5)).astype(k_ref.dtype)
    rr, rk, rv = sec[2], sec[3], sec[4]
    r_ref[0] = rr
    v_ref[0] = rv.astype(v_ref.dtype)
    lw = jnp.tanh(pn_ref[0, :, 0:128])
    lwf_ref[0] = -DECAY_SCALE * jax.nn.sigmoid(rpar_ref[0:1, :] + _dot3_pre(lw, wbf_ref[...]))
    lwb_ref[0] = -DECAY_SCALE * jax.nn.sigmoid(rpar_ref[1:2, :] + _dot3_pre(lw, wbb_ref[...]))
    a = jax.nn.sigmoid(rpar_ref[2:3, :] + _dot3_pre(pn_ref[0, :, 128:256], ab_ref[...]))
    gg_ref[0] = _dot3_pre(jax.nn.sigmoid(pn_ref[0, :, 256:384]), gb_ref[...])
    kap = rk * rpar_ref[3:4, :]
    norm = jnp.sqrt(_head_sums(kap * kap, bd_ref[...]))
    kh = kap / jnp.maximum(norm, 1e-12)
    kmod = rk * (1.0 + (a - 1.0) * rpar_ref[4:5, :])
    kh_ref[0] = kh
    abo_ref[0] = kh * a
    km_ref[0] = kmod
    bon_ref[0] = _head_sums(rr * kmod * rpar_ref[5:6, :], bd_ref[...]) * rv


def _prep(pc, pn, cw, rpar, wbf, wbb, ab, gb, bd):
    b, t, _ = pc.shape
    nt = t // ROW_TILE
    per = ROW_TILE // HALO
    nh = t // HALO
    w = W_M
    full = lambda shp: pl.BlockSpec(shp, lambda i, j: tuple(0 for _ in shp))
    in_specs = [
        pl.BlockSpec((1, HALO, CONV_CH), lambda i, j: (i, jnp.maximum(j * per - 1, 0), 0)),
        pl.BlockSpec((1, ROW_TILE, CONV_CH), lambda i, j: (i, j, 0)),
        pl.BlockSpec((1, HALO, CONV_CH), lambda i, j: (i, jnp.minimum(j * per + per, nh - 1), 0)),
        pl.BlockSpec((1, ROW_TILE, LORA_TAIL_W), lambda i, j: (i, j, NONCONV_W // LORA_TAIL_W - 1)),
        full((16, CONV_CH)), full((8, w)), full((384, w)), full((384, w)), full((384, w)), full((384, w)),
        full((2 * HEAD_BLK, HEAD_BLK)),
    ]
    out = pl.BlockSpec((1, ROW_TILE, w), lambda i, j: (i, j, 0))
    return pl.pallas_call(
        _prep_kernel,
        grid=(b, nt),
        in_specs=in_specs,
        out_specs=[out] * 11,
        out_shape=[jax.ShapeDtypeStruct((b, t, w), jnp.bfloat16 if name in ("q", "k", "v") else jnp.float32)
                   for name in ("q", "k", "r", "kmod", "v", "kh", "ab", "lwf", "lwb", "gg", "bonus")],
        scratch_shapes=[pltpu.VMEM((2 * HALO + ROW_TILE, w), jnp.float32)],
        compiler_params=pltpu.CompilerParams(
            dimension_semantics=("arbitrary", "arbitrary"), vmem_limit_bytes=VMEM_LIMIT),
        name="conv_prep",
    )(pc, pc, pc, pn, cw, rpar, wbf, wbb, ab, gb, bd)


def _round_robin(gens):
    out = [None] * len(gens)
    live = list(range(len(gens)))
    while live:
        nxt = []
        for i in live:
            try:
                next(gens[i])
                nxt.append(i)
            except StopIteration as stop:
                out[i] = stop.value
        live = nxt
    return out


def _rwkv_pair_chunk(r, k, v, kh, ab, lw, s_mat, tri_incl, strict, incl, last_row, m0, m1):
    cw = _dot_exact_lhs(tri_incl, lw)
    yield
    e_pos = jnp.exp(cw)
    e_neg = jnp.exp(-cw)
    e_prev = jnp.exp(cw - lw)
    stack = lambda a: jnp.concatenate([a * m0, a * m1], axis=0)
    kt = stack(kh * e_prev)
    bt = stack(ab * e_neg)
    kk = stack(k * e_neg)
    rt = stack(r * e_pos)
    vs = stack(v)
    zero = jnp.zeros((), jnp.float32)
    kr = jnp.concatenate([kt, rt], axis=0)
    bk = jnp.concatenate([bt, kk], axis=0)
    amat = _dot1(kr, bk, _NT)
    yield
    a_bk = jnp.where(strict, amat[:PAIR, :PAIR], zero)
    a_kk = jnp.where(strict, amat[:PAIR, PAIR:], zero)
    a_rb = jnp.where(incl, amat[PAIR:, :PAIR], zero)
    a_rk = jnp.where(incl, amat[PAIR:, PAIR:], zero)
    n_mat = -a_bk
    eye = (lax.broadcasted_iota(jnp.int32, (PAIR, PAIR), 0)
           == lax.broadcasted_iota(jnp.int32, (PAIR, PAIR), 1)).astype(jnp.float32)
    q = eye + n_mat
    p = _dot1(n_mat, n_mat)
    ks = _dot1(kr, s_mat, _NT)
    av = _dot1(jnp.concatenate([a_kk, a_rk], axis=0), vs)
    yield
    for _ in range(4):
        qp = _dot1(jnp.concatenate([q, p], axis=0), p)
        yield
        q = q + qp[:PAIR]
        p = qp[PAIR:]
    t_inv = q + _dot1(q, p)
    yield
    u = -_dot1(t_inv, ks[:PAIR] + av[:PAIR])
    yield
    y = ks[PAIR:] + av[PAIR:] + _dot1(a_rb, u)
    yield
    w_last = jnp.sum(jnp.where(last_row, e_pos, zero), axis=0, keepdims=True)
    s_new = (s_mat + _dot1(jnp.concatenate([u, vs], axis=0), bk, _TN)) * w_last
    return y[:CHUNK] + y[CHUNK:], s_new


def _rwkv_kernel(rf, kf, vf, khf, abf, lwf, rb, kb, vb, khb, abb, lwb, yf_ref, yb_ref, s_ref, *, nctx):
    j = pl.program_id(1)

    @pl.when(j == 0)
    def _():
        s_ref[...] = jnp.zeros_like(s_ref)

    row = lax.broadcasted_iota(jnp.int32, (PAIR, PAIR), 0)
    col = lax.broadcasted_iota(jnp.int32, (PAIR, PAIR), 1)
    same = (row // CHUNK) == (col // CHUNK)
    r64 = lax.broadcasted_iota(jnp.int32, (CHUNK, CHUNK), 0)
    c64 = lax.broadcasted_iota(jnp.int32, (CHUNK, CHUNK), 1)
    rowl = lax.broadcasted_iota(jnp.int32, (CHUNK, PAIR), 0)
    lane = lax.broadcasted_iota(jnp.int32, (1, PAIR), 1)
    m0 = (lane < D_HR).astype(jnp.float32)
    m1 = 1.0 - m0
    dirs = (
        (rf, kf, vf, khf, abf, lwf, yf_ref, (c64 <= r64), same & (col < row), same & (col <= row), rowl == CHUNK - 1),
        (rb, kb, vb, khb, abb, lwb, yb_ref, (c64 >= r64), same & (col > row), same & (col >= row), rowl == 0),
    )
    nb = rf.shape[0]
    states = [[[s_ref[bi, d, p] for p in range(NPAIR)] for d in range(2)] for bi in range(nb)]
    gens, meta = [], []
    for d, (r_ref, k_ref, v_ref, kh_ref, ab_ref, lw_ref, y_ref, tri, strict, incl, last_row) in enumerate(dirs):
        tri = tri.astype(jnp.float32).astype(jnp.bfloat16)
        for bi in range(nb):
            for p in range(NPAIR):
                sl = slice(p * PAIR, (p + 1) * PAIR)
                gens.append(_rwkv_pair_chunk(r_ref[bi, :, sl], k_ref[bi, :, sl], v_ref[bi, :, sl],
                                             kh_ref[bi, :, sl], ab_ref[bi, :, sl], lw_ref[bi, :, sl],
                                             states[bi][d][p], tri, strict, incl, last_row, m0, m1))
                meta.append((bi, d, p, sl, y_ref))
    results = [m + r for m, r in zip(meta, _round_robin(gens))]
    for bi, d, p, sl, y_ref, y, s_new in results:
        s_ref[bi, d, p] = s_new
        y_ref[bi, :, sl] = y


def _rwkv_scan(r, k, v, kh, ab, lw_f, lw_b, ctx_len):
    b, t, w = r.shape
    nc = t // CHUNK
    nctx = ctx_len // CHUNK
    nlat = nc - nctx

    def fwd_map(i, j):
        return (i, j, 0)

    def bwd_map(i, j):
        return (i, jnp.where(j < nctx, nctx - 1 - j, nc - 1 - (j - nctx)), 0)

    nb = math.gcd(b, SCAN_BATCH)
    blk = (nb, CHUNK, w)
    in_specs = [pl.BlockSpec(blk, fwd_map)] * 6 + [pl.BlockSpec(blk, bwd_map)] * 6
    out_specs = [
        pl.BlockSpec(blk, lambda i, j: (i, jnp.maximum(j - nctx, 0), 0)),
        pl.BlockSpec(blk, lambda i, j: (i, nlat - 1 - jnp.maximum(j - nctx, 0), 0)),
    ]
    return pl.pallas_call(
        functools.partial(_rwkv_kernel, nctx=nctx),
        grid=(b // nb, nc),
        in_specs=in_specs,
        out_specs=out_specs,
        out_shape=[jax.ShapeDtypeStruct((b, t - ctx_len, w), jnp.float32)] * 2,
        scratch_shapes=[pltpu.VMEM((nb, 2, NPAIR, PAIR, PAIR), jnp.float32)],
        compiler_params=pltpu.CompilerParams(
            dimension_semantics=("arbitrary", "arbitrary"), vmem_limit_bytes=VMEM_LIMIT),
        name="rwkv_scan",
    )(r, k, v, kh, ab, lw_f, r, k, v, kh, ab, lw_b)


def _scan_max(x, reverse):
    rows = lax.broadcasted_iota(jnp.int32, x.shape, 0)
    neg_inf = jnp.full((), -jnp.inf, jnp.float32)
    step = 1
    while step < CHUNK:
        if reverse:
            shifted = jnp.where(rows < CHUNK - step, pltpu.roll(x, CHUNK - step, 0), neg_inf)
        else:
            shifted = jnp.where(rows >= step, pltpu.roll(x, step, 0), neg_inf)
        x = jnp.maximum(x, shifted)
        step *= 2
    return x


def _mlstm_gates(gates, bias, tri, m_prev, is_f, reverse):
    gl = gates + bias
    act = jnp.where(is_f, jax.nn.log_sigmoid(gl), gl)
    b_all = _dot_exact_lhs(tri, act)
    li = pltpu.roll(act, H_M, 1)
    x = li - b_all
    m_inter = b_all + m_prev
    m_t = jnp.maximum(m_inter, b_all + _scan_max(x, reverse))
    b_last = b_all[0:1, :] if reverse else b_all[CHUNK - 1:CHUNK, :]
    g = b_last + x
    m_new = jnp.maximum(b_last + m_prev, jnp.max(g, axis=0, keepdims=True))
    wts = jnp.exp(g - m_new)
    decay = jnp.exp(b_last + m_prev - m_new)
    return b_all - m_t, jnp.exp(m_inter - m_t), jnp.exp(-m_t), x, wts, decay, m_new


def _mlstm_pair_chunk(q, k, v, pieces, cf, decay_g, c_mat, n_row, incl, m0, m1):
    stack = lambda a: jnp.concatenate([a * m0, a * m1], axis=0)
    q_st, k_st, v_st = stack(q), stack(k), stack(v)
    lane = lax.broadcasted_iota(jnp.int32, (1, PAIR), 1)
    zero_bf = jnp.zeros((), jnp.bfloat16)
    sel = lambda a: jnp.concatenate([jnp.where(lane == cf, a, zero_bf), jnp.where(lane == cf + 1, a, zero_bf)], 0)
    ones = jnp.ones((PAIR, PAIR), jnp.bfloat16)
    spread = [sel(a) for a in pieces[0]] + [sel(p[0]) for p in pieces[1:4]]
    bc = _dg(jnp.concatenate(spread, axis=0), ones, _NN)
    bm_bc = bc[:PAIR] + bc[PAIR:2 * PAIR] + bc[2 * PAIR:3 * PAIR]
    inter_bc, enm_bc, wts_bc = (bc[i * PAIR:(i + 1) * PAIR] for i in range(3, 6))
    r3 = _dg(ones, jnp.concatenate([sel(a) for a in pieces[4]], axis=0), _NT)
    r_mat = r3[:, :PAIR] + r3[:, PAIR:2 * PAIR] + r3[:, 2 * PAIR:]
    qk = _dot1(q_st, k_st, _NT)
    qc = _dot1(q_st, c_mat)
    yield
    neg_inf = jnp.full((), -jnp.inf, jnp.float32)
    s = qk * jnp.exp(jnp.where(incl, bm_bc + r_mat, neg_inf))
    wk = wts_bc * k_st
    sv = _dot1(s, v_st)
    kv = _dot1(wk, v_st, _TN)
    sums = _dot1(jnp.concatenate([q_st * n_row, s], axis=0), jnp.ones((PAIR, PAIR), jnp.float32))
    yield
    den = inter_bc * sums[:PAIR] + sums[PAIR:]
    h_st = (inter_bc * qc + sv) / jnp.maximum(jnp.abs(den), enm_bc)
    h = h_st[:CHUNK] + h_st[CHUNK:]
    decay_row = jnp.where(lane < D_HM, decay_g[:, cf:cf + 1], decay_g[:, cf + 1:cf + 2])
    c_new = decay_row * c_mat + kv
    n_new = decay_row * n_row + jnp.sum(wk, axis=0, keepdims=True)
    return h, c_new, n_new


def _mlstm_kernel(qf, kf, vf, gf, qb, kb, vb, gb, bias_ref, hf_ref, hb_ref, c_ref, n_ref, m_ref, *, nctx):
    del nctx
    j = pl.program_id(1)

    @pl.when(j == 0)
    def _():
        c_ref[...] = jnp.zeros_like(c_ref)
        n_ref[...] = jnp.zeros_like(n_ref)
        m_ref[...] = jnp.zeros_like(m_ref)

    row = lax.broadcasted_iota(jnp.int32, (PAIR, PAIR), 0)
    col = lax.broadcasted_iota(jnp.int32, (PAIR, PAIR), 1)
    same = (row // CHUNK) == (col // CHUNK)
    r64 = lax.broadcasted_iota(jnp.int32, (CHUNK, CHUNK), 0)
    c64 = lax.broadcasted_iota(jnp.int32, (CHUNK, CHUNK), 1)
    lane = lax.broadcasted_iota(jnp.int32, (1, PAIR), 1)
    m0 = (lane < D_HM).astype(jnp.float32)
    m1 = 1.0 - m0
    is_f = (lane % (2 * H_M)) >= H_M
    dirs = (
        (qf, kf, vf, gf, hf_ref, (c64 <= r64), same & (col <= row)),
        (qb, kb, vb, gb, hb_ref, (c64 >= r64), same & (col >= row)),
    )
    nb = qf.shape[0]
    states = [[[(c_ref[bi, d, p], n_ref[bi, d, p]) for p in range(NPAIR)] for d in range(2)] for bi in range(nb)]
    gens, meta, m_news = [], [], []
    lead = lambda a: (a.astype(jnp.bfloat16),)
    for d, (q_ref, k_ref, v_ref, g_ref, h_ref, tri, incl) in enumerate(dirs):
        tri = tri.astype(jnp.float32).astype(jnp.bfloat16)
        for bi in range(nb):
            bm, inter, enm, x, wts, decay_g, m_new = _mlstm_gates(g_ref[bi], bias_ref[...], tri, m_ref[bi, d], is_f,
                                                                  d == 1)
            m_news.append((bi, d, m_new))
            pieces = [_split3(bm), lead(inter), lead(enm), lead(wts), _split3(x)]
            for p in range(NPAIR):
                sl = slice(p * PAIR, (p + 1) * PAIR)
                cf = d * 2 * H_M + H_M + 2 * p
                gens.append(_mlstm_pair_chunk(q_ref[bi, :, sl], k_ref[bi, :, sl], v_ref[bi, :, sl], pieces, cf,
                                               decay_g, *states[bi][d][p], incl, m0, m1))
                meta.append((bi, d, p, sl, h_ref))
    for (bi, d, p, sl, h_ref), (h, c_new, n_new) in zip(meta, _round_robin(gens)):
        c_ref[bi, d, p] = c_new
        n_ref[bi, d, p] = n_new
        h_ref[bi, :, sl] = h
    for bi, d, m_new in m_news:
        m_ref[bi, d] = m_new


def _mlstm_scan(q, k, pn, bias_row, ctx_len):
    v = gates = pn
    b, t, w = q.shape
    nc = t // CHUNK
    nctx = ctx_len // CHUNK
    nlat = nc - nctx

    def fwd_map(i, j):
        return (i, j, 0)

    def bwd_map(i, j):
        return (i, jnp.where(j < nctx, nctx - 1 - j, nc - 1 - (j - nctx)), 0)

    nb = math.gcd(b, SCAN_BATCH)
    blk = (nb, CHUNK, w)
    gblk = (nb, CHUNK, 128)
    gate_blk = (2 * W_M) // 128
    gate_of = lambda m: (lambda i, j: m(i, j)[:2] + (gate_blk,))
    in_specs = ([pl.BlockSpec(blk, fwd_map)] * 3 + [pl.BlockSpec(gblk, gate_of(fwd_map))]
                + [pl.BlockSpec(blk, bwd_map)] * 3 + [pl.BlockSpec(gblk, gate_of(bwd_map))]
                + [pl.BlockSpec((1, 128), lambda i, j: (0, 0))])
    out_specs = [
        pl.BlockSpec(blk, lambda i, j: (i, jnp.maximum(j - nctx, 0), 0)),
        pl.BlockSpec(blk, lambda i, j: (i, nlat - 1 - jnp.maximum(j - nctx, 0), 0)),
    ]
    return pl.pallas_call(
        functools.partial(_mlstm_kernel, nctx=nctx),
        grid=(b // nb, nc),
        in_specs=in_specs,
        out_specs=out_specs,
        out_shape=[jax.ShapeDtypeStruct((b, t - ctx_len, w), jnp.float32)] * 2,
        scratch_shapes=[pltpu.VMEM((nb, 2, NPAIR, PAIR, PAIR), jnp.float32),
                        pltpu.VMEM((nb, 2, NPAIR, 1, PAIR), jnp.float32),
                        pltpu.VMEM((nb, 2, 1, 128), jnp.float32)],
        compiler_params=pltpu.CompilerParams(
            dimension_semantics=("arbitrary", "arbitrary"), vmem_limit_bytes=VMEM_LIMIT),
        name="mlstm_scan",
    )(q, k, v, gates, q, k, v, gates, bias_row)


ROUTE_W = 128
N_ROUTE = N_GROUPS + N_EXPERTS


def _ln_rows(z, eps):
    mu = jnp.mean(z, -1, keepdims=True)
    zc = z - mu
    var = jnp.mean(zc * zc, -1, keepdims=True)
    return zc * lax.rsqrt(var + eps)


OUT_TILES = 2


def _outproj_tile(rows, hf, hb, mo, yf, yb, gg, bonus, x_ref, mod_ref, wout_ref, hpar_ref, ln_ref, rtw_ref, rtb_ref,
                  bd_ref, x1_ref, tok_ref):
    avg = bd_ref[...]
    zm = hf[0, rows, :] + hb[0, rows, :]
    zr = yf[0, rows, :] + yb[0, rows, :]
    mu_m = _head_sums(zm, avg) * (1.0 / D_HM)
    mu_r = _head_sums(zr, avg) * (1.0 / D_HR)
    yield
    cm, cr = zm - mu_m, zr - mu_r
    var_m = _head_sums(cm * cm, avg) * (1.0 / D_HM)
    var_r = _head_sums(cr * cr, avg) * (1.0 / D_HR)
    yield
    out_m = jax.nn.sigmoid(mo[0]) * (cm * lax.rsqrt(var_m + LN_EPS) * hpar_ref[0:1, :])
    y = cr * lax.rsqrt(var_r + GN_EPS) * hpar_ref[1:2, :] + hpar_ref[2:3, :]
    out_r = (y + bonus[0]) * gg[0]
    proj = (jnp.dot(out_m.astype(jnp.bfloat16), wout_ref[:W_M, :], preferred_element_type=jnp.float32)
            + jnp.dot(out_r.astype(jnp.bfloat16), wout_ref[W_M:, :], preferred_element_type=jnp.float32))
    yield
    g1, sh2, sc2 = mod_ref[0, 2:3, :], mod_ref[0, 3:4, :], mod_ref[0, 4:5, :]
    x1 = _ln_rows(ALPHA * x_ref[0, rows, :] + g1 * proj, LN_EPS) * ln_ref[0:1, :] + ln_ref[1:2, :]
    x1_ref[0, rows, :] = x1
    tok = _ln_rows(x1, LN_EPS) * (1.0 + sc2) + sh2
    tok_ref[0, rows, :] = tok
    logits = _dot3_pre(tok, rtw_ref[...]) + rtb_ref[...]
    yield
    lane = lax.broadcasted_iota(jnp.int32, (ROW_TILE, ROUTE_W), 1)
    neg_inf = jnp.full((), -jnp.inf, jnp.float32)
    big = jnp.int32(ROUTE_W)
    is_g = lane < N_GROUPS
    lg = jnp.where(is_g, logits, neg_inf)
    gmax = jnp.max(lg, axis=1, keepdims=True)
    grp = jnp.min(jnp.where(lg == gmax, lane, big), axis=1, keepdims=True)
    p_top = 1.0 / jnp.sum(jnp.where(is_g, jnp.exp(lg - gmax), 0.0), axis=1, keepdims=True)
    in_grp = (lane >= N_GROUPS) & (lane < N_ROUTE) & ((lane - N_GROUPS) // E_PER_GROUP == grp)
    le = jnp.where(in_grp, logits, neg_inf)
    v1 = jnp.max(le, axis=1, keepdims=True)
    i1 = jnp.min(jnp.where(le == v1, lane, big), axis=1, keepdims=True)
    le2 = jnp.where(lane == i1, neg_inf, le)
    v2 = jnp.max(le2, axis=1, keepdims=True)
    i2 = jnp.min(jnp.where(le2 == v2, lane, big), axis=1, keepdims=True)
    e21 = jnp.exp(v2 - v1)
    w1 = (1.0 / (1.0 + e21)) * p_top
    w2 = (e21 / (1.0 + e21)) * p_top

    sel1, sel2 = lane == i1, lane == i2
    onehot = jnp.where(sel1 | sel2, 1.0, 0.0)
    tr = lax.broadcasted_iota(jnp.int32, (ROW_TILE, ROW_TILE), 0)
    tc = lax.broadcasted_iota(jnp.int32, (ROW_TILE, ROW_TILE), 1)
    strict = jnp.where(tc < tr, 1.0, 0.0).astype(jnp.bfloat16)
    before = _dg(strict, onehot.astype(jnp.bfloat16), _NN)
    yield
    e1 = (i1 - N_GROUPS).astype(jnp.float32)
    e2 = (i2 - N_GROUPS).astype(jnp.float32)
    return lane, sel1, sel2, onehot, before, e1, e2, w1, w2


def _outproj_kernel(hf, hb, mo_a, mo_b, yf, yb, gg_a, gg_b, bon_a, bon_b, x_ref, mod_ref, wout_ref, hpar_ref, ln_ref,
                    rtw_ref, rtb_ref, bd_ref, x1_ref, tok_ref, route_ref, ridx_ref, cnt_ref, carry_ref):
    i, j = pl.program_id(0), pl.program_id(1)

    @pl.when((i == 0) & (j == 0))
    def _():
        carry_ref[...] = jnp.zeros_like(carry_ref)

    tiles = ((mo_a, gg_a, bon_a), (mo_b, gg_b, bon_b))
    gens = [_outproj_tile(slice(t * ROW_TILE, (t + 1) * ROW_TILE), hf, hb, mo, yf, yb, gg, bonus, x_ref, mod_ref,
                          wout_ref, hpar_ref, ln_ref, rtw_ref, rtb_ref, bd_ref, x1_ref, tok_ref)
            for t, (mo, gg, bonus) in enumerate(tiles)]
    carry = carry_ref[...]
    for t, (lane, sel1, sel2, onehot, before, e1, e2, w1, w2) in enumerate(_round_robin(gens)):
        before = before + carry
        r1 = jnp.sum(jnp.where(sel1, before, 0.0), axis=1, keepdims=True)
        r2 = jnp.sum(jnp.where(sel2, before, 0.0), axis=1, keepdims=True)
        carry = carry + jnp.sum(onehot, axis=0, keepdims=True)
        route = jnp.where(lane == 0, e1, jnp.where(lane == 1, e2, jnp.where(lane == 2, r1, jnp.where(
            lane == 3, r2, jnp.where(lane == 4, w1, jnp.where(lane == 5, w2, 0.0))))))
        route_ref[0, t * ROW_TILE:(t + 1) * ROW_TILE, :] = route
        ridx_ref[t] = route.T[0:8, :].astype(jnp.int32)
    carry_ref[...] = carry
    cnt_ref[...] = jnp.broadcast_to(carry, cnt_ref.shape)


def _outproj(h_f, h_b, pn, y_f, y_b, gg, bonus, x, mod3, wout_bf16, hpar, lnpar, rtw, rtb, bd, ctx_len):
    b, seq, d = x.shape
    nt = seq // ROW_TILE
    assert nt % OUT_TILES == 0
    off = ctx_len // ROW_TILE
    step_rows = OUT_TILES * ROW_TILE
    lat = lambda w: pl.BlockSpec((1, step_rows, w), lambda i, j: (i, j, 0))
    full = lambda shp: pl.BlockSpec(shp, lambda i, j: tuple(0 for _ in shp))
    shifted = lambda w, t, lane_blk: pl.BlockSpec((1, ROW_TILE, w),
                                                  lambda i, j: (i, OUT_TILES * j + t + off, lane_blk))
    in_specs = [
        lat(W_M), lat(W_M),
        shifted(W_M, 0, 1), shifted(W_M, 1, 1),
        lat(W_R), lat(W_R),
        shifted(W_R, 0, 0), shifted(W_R, 1, 0),
        shifted(W_R, 0, 0), shifted(W_R, 1, 0),
        lat(d),
        pl.BlockSpec((1, 6, d), lambda i, j: (i, 0, 0)),
        full((MIX_W, d)), full((8, W_M)), full((8, d)), full((3 * d, ROUTE_W)), full((1, ROUTE_W)),
        full((2 * HEAD_BLK, HEAD_BLK)),
    ]
    out_specs = [
        lat(d), lat(d), lat(ROUTE_W),
        pl.BlockSpec((OUT_TILES, 8, ROW_TILE), lambda i, j: (i * (nt // OUT_TILES) + j, 0, 0)),
        full((8, ROUTE_W)),
    ]
    out_shape = [
        jax.ShapeDtypeStruct((b, seq, d), jnp.float32),
        jax.ShapeDtypeStruct((b, seq, d), jnp.float32),
        jax.ShapeDtypeStruct((b, seq, ROUTE_W), jnp.float32),
        jax.ShapeDtypeStruct((b * nt, 8, ROW_TILE), jnp.int32),
        jax.ShapeDtypeStruct((8, ROUTE_W), jnp.float32),
    ]
    return pl.pallas_call(
        _outproj_kernel,
        grid=(b, nt // OUT_TILES),
        in_specs=in_specs,
        out_specs=out_specs,
        out_shape=out_shape,
        scratch_shapes=[pltpu.VMEM((1, ROUTE_W), jnp.float32)],
        compiler_params=pltpu.CompilerParams(
            dimension_semantics=("arbitrary", "arbitrary"), vmem_limit_bytes=VMEM_LIMIT),
        name="outproj_router",
    )(h_f, h_b, pn, pn, y_f, y_b, gg, gg, bonus, bonus, x, mod3, wout_bf16, hpar, lnpar, rtw, rtb, bd)


DEST_TILES = 8


def _dest_kernel(starts_ref, ridx_ref, dest_ref):
    blk = ridx_ref[...]
    dest = pltpu.roll(blk, 8 - TOP_K_IN_GROUP, 1)
    for e in range(N_EXPERTS):
        dest = dest + jnp.where(blk == e, starts_ref[e], 0)
    dest_ref[...] = dest


def _moe_dest(starts, ridx):
    nt = ridx.shape[0]
    per_step = math.gcd(nt, DEST_TILES)
    blk = pl.BlockSpec((per_step, 8, ROW_TILE), lambda i, s: (i, 0, 0))
    return pl.pallas_call(
        _dest_kernel,
        grid_spec=pltpu.PrefetchScalarGridSpec(num_scalar_prefetch=1, grid=(nt // per_step,),
                                               in_specs=[blk], out_specs=blk),
        out_shape=jax.ShapeDtypeStruct(ridx.shape, jnp.int32),
        compiler_params=pltpu.CompilerParams(dimension_semantics=("arbitrary",), vmem_limit_bytes=VMEM_LIMIT),
        name="moe_dest",
    )(starts, ridx)


SUB = 8
SUB_SHIFT = 3
TILE_GROUPS = ROW_TILE // SUB


def _scatter_kernel(zinfo_ref, dest_ref, tok_ref, xs_ref, zbuf, sem, zsem):
    i = pl.program_id(0)

    @pl.when(i == 0)
    def _():
        zbuf[...] = jnp.zeros_like(zbuf)

    def zero_row(row):
        return pltpu.make_async_copy(zbuf.at[0, pl.ds(0, 1), :],
                                     xs_ref.at[row >> SUB_SHIFT, pl.ds(row & (SUB - 1), 1), :], zsem)

    e = jnp.minimum(i, N_EXPERTS - 1)
    lo = zinfo_ref[e]
    hi = jnp.where(i < N_EXPERTS, zinfo_ref[N_EXPERTS + e], lo)
    tail_blk = zinfo_ref[2 * N_EXPERTS] + i
    has_tail = tail_blk < zinfo_ref[2 * N_EXPERTS + 1]
    tail_copy = pltpu.make_async_copy(zbuf, xs_ref.at[pl.ds(tail_blk * TILE_GROUPS, TILE_GROUPS)], zsem)

    def start_zero(row, c):
        zero_row(row).start()
        return c

    lax.fori_loop(lo, hi, start_zero, 0)

    @pl.when(has_tail)
    def _():
        tail_copy.start()

    def issue(g, c):
        for k in range(SUB):
            for slot in range(TOP_K_IN_GROUP):
                dst = dest_ref[slot * ROW_TILE + g * SUB + k]
                pltpu.make_async_copy(tok_ref.at[g, pl.ds(k, 1), :],
                                      xs_ref.at[dst >> SUB_SHIFT, pl.ds(dst & (SUB - 1), 1), :],
                                      sem).start(priority=slot)
        return c

    lax.fori_loop(0, TILE_GROUPS, issue, 0)
    for half in range(TOP_K_IN_GROUP):
        pltpu.make_async_copy(tok_ref, xs_ref.at[pl.ds(half * TILE_GROUPS, TILE_GROUPS)], sem).wait()

    def wait_zero(row, c):
        zero_row(row).wait()
        return c

    lax.fori_loop(lo, hi, wait_zero, 0)

    @pl.when(has_tail)
    def _():
        tail_copy.wait()


def _moe_scatter(zinfo, dest_flat, tok2d, n_rows):
    n, d = tok2d.shape
    nt = n // ROW_TILE
    assert nt >= N_EXPERTS
    xs = pl.pallas_call(
        _scatter_kernel,
        grid_spec=pltpu.PrefetchScalarGridSpec(
            num_scalar_prefetch=1,
            grid=(nt,),
            in_specs=[
                pl.BlockSpec((SUB * ROW_TILE,), lambda i, z: (i,), memory_space=pltpu.SMEM),
                pl.BlockSpec((TILE_GROUPS, SUB, d), lambda i, z: (i, 0, 0)),
            ],
            out_specs=pl.BlockSpec(memory_space=pl.ANY),
            scratch_shapes=[pltpu.VMEM((TILE_GROUPS, SUB, d), tok2d.dtype), pltpu.SemaphoreType.DMA(()),
                            pltpu.SemaphoreType.DMA(())],
        ),
        out_shape=jax.ShapeDtypeStruct((n_rows // SUB, SUB, d), tok2d.dtype),
        compiler_params=pltpu.CompilerParams(dimension_semantics=("arbitrary",), vmem_limit_bytes=VMEM_LIMIT),
        name="moe_scatter",
    )(zinfo, dest_flat, tok2d.reshape(n // SUB, SUB, d))
    return xs.reshape(n_rows, d)


def _expert_kernel(blk_exp_ref, valid_ref, xs_ref, wg_ref, wu_ref, wd_ref, ys_ref, wg_bf, wu_bf, wd_bf):
    i = pl.program_id(0)
    valid = valid_ref[i]

    @pl.when((i == 0) | (blk_exp_ref[i] != blk_exp_ref[jnp.maximum(i - 1, 0)]))
    def _():
        wg_bf[...] = wg_ref[0].astype(jnp.bfloat16)
        wu_bf[...] = wu_ref[0].astype(jnp.bfloat16)
        wd_bf[...] = wd_ref[0].astype(jnp.bfloat16)

    @pl.when(valid > 0)
    def _():
        xb = xs_ref[...].astype(jnp.bfloat16)
        hg = jnp.dot(xb, wg_bf[...], preferred_element_type=jnp.float32)
        hu = jnp.dot(xb, wu_bf[...], preferred_element_type=jnp.float32)
        hb = (jax.nn.silu(hg) * hu).astype(jnp.bfloat16)
        ys_ref[...] = jnp.dot(hb, wd_bf[...], preferred_element_type=jnp.float32)

    @pl.when(valid == 0)
    def _():
        ys_ref[...] = jnp.zeros_like(ys_ref)


def _moe_experts(blk_exp, blk_valid, xs, wg, wu, wd):
    nrow, d = xs.shape
    de = wg.shape[2]
    return pl.pallas_call(
        _expert_kernel,
        grid_spec=pltpu.PrefetchScalarGridSpec(
            num_scalar_prefetch=2,
            grid=(nrow // MOE_BLOCK,),
            in_specs=[
                pl.BlockSpec((MOE_BLOCK, d), lambda i, be, nu: (i, 0)),
                pl.BlockSpec((1, d, de), lambda i, be, nu: (be[i], 0, 0)),
                pl.BlockSpec((1, d, de), lambda i, be, nu: (be[i], 0, 0)),
                pl.BlockSpec((1, de, d), lambda i, be, nu: (be[i], 0, 0)),
            ],
            out_specs=pl.BlockSpec((MOE_BLOCK, d), lambda i, be, nu: (i, 0)),
            scratch_shapes=[pltpu.VMEM((d, de), jnp.bfloat16), pltpu.VMEM((d, de), jnp.bfloat16),
                            pltpu.VMEM((de, d), jnp.bfloat16)],
        ),
        out_shape=jax.ShapeDtypeStruct((nrow, d), jnp.float32),
        compiler_params=pltpu.CompilerParams(dimension_semantics=("arbitrary",), vmem_limit_bytes=VMEM_LIMIT),
        name="moe_experts",
    )(blk_exp, blk_valid, xs, wg, wu, wd)


def _combine_kernel(dest_ref, dest_next_ref, route_ref, x1_ref, mod_ref, ln_ref, ys_ref, out_ref,
                    ybuf, sems):
    i = pl.program_id(0)
    n = pl.num_programs(0)
    cur = i % 2

    def gather(idx_ref, buf):
        def issue(g, c):
            for k in range(SUB):
                for slot in range(TOP_K_IN_GROUP):
                    src = idx_ref[slot * ROW_TILE + g * SUB + k]
                    pltpu.make_async_copy(ys_ref.at[src >> SUB_SHIFT, pl.ds(src & (SUB - 1), 1), :],
                                          ybuf.at[buf, slot * TILE_GROUPS + g, pl.ds(k, 1), :],
                                          sems.at[buf]).start(priority=slot)
            return c

        lax.fori_loop(0, TILE_GROUPS, issue, 0)

    @pl.when(i == 0)
    def _():
        gather(dest_ref, 0)

    @pl.when(i + 1 < n)
    def _():
        gather(dest_next_ref, 1 - cur)

    pltpu.make_async_copy(ys_ref.at[pl.ds(0, TOP_K_IN_GROUP * TILE_GROUPS)], ybuf.at[cur], sems.at[cur]).wait()
    route = route_ref[0]
    d = out_ref.shape[-1]
    y0 = ybuf[cur, 0:TILE_GROUPS].reshape(ROW_TILE, d)
    y1 = ybuf[cur, TILE_GROUPS:2 * TILE_GROUPS].reshape(ROW_TILE, d)
    ffn = y0 * route[:, 4:5] + y1 * route[:, 5:6]
    g2 = mod_ref[0, 5:6, :]
    out_ref[0] = _ln_rows(ALPHA * x1_ref[0] + g2 * ffn, LN_EPS) * ln_ref[2:3, :] + ln_ref[3:4, :]


def _moe_combine(dest_flat, route, x1, mod3, lnpar, ys):
    b, seq, d = x1.shape
    nt = seq // ROW_TILE
    n = b * nt
    tile = lambda w: pl.BlockSpec((1, ROW_TILE, w), lambda i: (i // nt, i % nt, 0))
    idx_blk = lambda f: pl.BlockSpec((SUB * ROW_TILE,), f, memory_space=pltpu.SMEM)
    return pl.pallas_call(
        _combine_kernel,
        grid=(n,),
        in_specs=[
            idx_blk(lambda i: (i,)),
            idx_blk(lambda i: (jnp.minimum(i + 1, n - 1),)),
            tile(ROUTE_W),
            tile(d),
            pl.BlockSpec((1, 6, d), lambda i: (i // nt, 0, 0)),
            pl.BlockSpec((8, d), lambda i: (0, 0)),
            pl.BlockSpec(memory_space=pl.ANY),
        ],
        out_specs=tile(d),
        scratch_shapes=[pltpu.VMEM((2, TOP_K_IN_GROUP * TILE_GROUPS, SUB, d), jnp.float32),
                        pltpu.SemaphoreType.DMA((2,))],
        out_shape=jax.ShapeDtypeStruct((b, seq, d), jnp.float32),
        compiler_params=pltpu.CompilerParams(dimension_semantics=("arbitrary",), vmem_limit_bytes=VMEM_LIMIT),
        name="moe_combine",
    )(dest_flat, dest_flat, route, x1, mod3, lnpar, ys.reshape(ys.shape[0] // SUB, SUB, d))


def _moe_plan(cnt, n_pairs):
    counts = cnt[0, N_GROUPS:N_ROUTE].astype(jnp.int32)
    padded = (counts + MOE_BLOCK - 1) // MOE_BLOCK * MOE_BLOCK
    pad_ends = jnp.cumsum(padded)
    pad_starts = pad_ends - padded
    n_blk = -(-n_pairs // MOE_BLOCK) + N_EXPERTS
    blk_row0 = jnp.arange(n_blk, dtype=jnp.int32) * MOE_BLOCK
    blk_exp = jnp.minimum(jnp.sum((pad_ends[None, :] <= blk_row0[:, None]).astype(jnp.int32), axis=1), N_EXPERTS - 1)
    blk_valid = jnp.clip(counts[blk_exp] - (blk_row0 - pad_starts[blk_exp]), 0, MOE_BLOCK)
    blk_valid = jnp.where(blk_row0 < pad_ends[-1], blk_valid, 0).astype(jnp.int32)
    zinfo = jnp.concatenate([pad_starts + counts, pad_ends, pad_ends[-1:] // MOE_BLOCK,
                             jnp.full((1,), n_blk, jnp.int32)]).astype(jnp.int32)
    return pad_starts.astype(jnp.int32), blk_exp, blk_valid, zinfo, n_blk


def kernel(x, c, ctx, c_ctx, w_ada, b_ada, w_in, conv_w, m_bias_i, m_bias_f, m_norm_w, r_w0, r_wB, r_a0, r_aB,
           r_gB, r_kk, r_ka, r_bonus, r_norm_w, r_norm_b, w_out, ln1_g, ln1_b, ln2_g, ln2_b, rt_g, rt_g_b, rt_e,
           rt_e_b, ex_gate, ex_up, ex_down):
    assert w_ada.shape[0] == DEPTH
    bsz, seq, d = x.shape
    ctx_len = ctx.shape[1]
    assert ctx_len == ROW_TILE and seq % ROW_TILE == 0
    assert seq % GRID_W == 0 and ROW_TILE % GRID_W == 0
    l = 0
    zrow = lambda n, w: jnp.zeros((n, w), jnp.float32)
    mrows = -(-(bsz + 1) // 8) * 8
    c_pad = jnp.concatenate([c, c_ctx[None, :], zrow(mrows - bsz - 1, d)], 0)
    mod_all = _ada(c_pad, w_ada[l], b_ada[l][None, :])[:bsz + 1]
    mod3 = mod_all.reshape(bsz + 1, 6, d)
    pc, pn = _inproj(ctx, x, mod3, _pad_w_in(w_in[l]))
    cw = jnp.concatenate([conv_w[l].reshape(CONV_K * CONV_K, CONV_CH), zrow(16 - CONV_K * CONV_K, CONV_CH)], 0)
    rpar = jnp.concatenate([r_w0[l], r_a0[l][None, :], r_kk[l][None, :], r_ka[l][None, :],
                            r_bonus[l].reshape(1, W_R), zrow(2, W_R)], 0)
    wbf = jnp.concatenate([r_wB[l][0], zrow(W_LORA, W_R)], 0)
    wbb = jnp.concatenate([zrow(W_LORA, W_R), r_wB[l][1]], 0)
    ab_mat = jnp.concatenate([r_aB[l], zrow(128 - A_LORA, W_R)], 0)
    bd = _head_sum_operator(D_HR)
    q, k, rr, kmod, rv, kh, kha, logw_f, logw_b, gg, bonus = _prep(
        pc, pn, cw, rpar, _stack3(wbf), _stack3(wbb), _stack3(ab_mat), _stack3(r_gB[l]), bd)
    bias_row = jnp.concatenate([m_bias_i[l][0], m_bias_f[l][0], m_bias_i[l][1], m_bias_f[l][1],
                                jnp.zeros((128 - 4 * H_M,), jnp.float32)])[None, :]
    h_f, h_b = _mlstm_scan(q, k, pn, bias_row, ctx_len)
    y_f, y_b = _rwkv_scan(rr, kmod, rv, kh, kha, logw_f, logw_b, ctx_len)
    hpar = jnp.zeros((8, W_M), jnp.float32).at[0].set(m_norm_w[l]).at[1].set(r_norm_w[l]).at[2].set(r_norm_b[l])
    lnpar = jnp.zeros((8, d), jnp.float32).at[0].set(ln1_g[l]).at[1].set(ln1_b[l]).at[2].set(ln2_g[l]).at[3].set(
        ln2_b[l])
    rtw = jnp.concatenate([rt_g[l], rt_e[l], jnp.zeros((d, ROUTE_W - N_ROUTE), jnp.float32)], axis=1)
    rtb = jnp.concatenate([rt_g_b[l], rt_e_b[l], jnp.zeros((ROUTE_W - N_ROUTE,), jnp.float32)])[None, :]
    x1, tok, route, ridx, cnt = _outproj(h_f, h_b, pn, y_f, y_b, gg, bonus, x, mod3, w_out[l].astype(jnp.bfloat16),
                                         hpar, lnpar, _stack3(rtw), rtb, bd, ctx_len)
    n_tok = bsz * seq
    starts, blk_exp, blk_valid, zinfo, n_blk = _moe_plan(cnt, n_tok * TOP_K_IN_GROUP)
    dest = _moe_dest(starts, ridx)
    dest_flat = dest.reshape(-1)
    xs = _moe_scatter(zinfo, dest_flat, tok.reshape(n_tok, d), n_blk * MOE_BLOCK)
    ys = _moe_experts(blk_exp, blk_valid, xs, ex_gate[l], ex_up[l], ex_down[l])
    return _moe_combine(dest_flat, route, x1, mod3, lnpar, ys)
```
